```python
import math
import jax, jax.numpy as jnp
from jax import lax
import numpy as np

D_MODEL = 1024
BATCH = 4
SEQ = 8192
DEPTH = 1

SSD_HEADS = 24
SSD_HEAD_DIM = 64
SSD_INNER = SSD_HEADS * SSD_HEAD_DIM
SSD_GROUPS = 4
SSD_STATE = 128
SSD_CONV = 5
SSD_CHUNK = 128
XBC_WIDTH = SSD_INNER + 2 * SSD_GROUPS * SSD_STATE
FOURIER_GROUPS = 4
FOURIER_GROUP_DIM = 128
FOURIER_WIDTH = FOURIER_GROUPS * FOURIER_GROUP_DIM
IN_WIDTH = SSD_INNER + XBC_WIDTH + 2 * SSD_HEADS + FOURIER_WIDTH + 2 * D_MODEL
N_EXPERTS = 256
TOP_K = 8
N_EXPERT_GROUPS = 8
TOPK_GROUPS = 4
EXPERT_HIDDEN = 256
SHARED_HIDDEN = 256
ROUTED_SCALE = 2.5
EXPERT_BLOCK = 128
RMS_EPS = 1e-6

kernel_name = "bidir_ssd_fourier_moe_sandwich_block"

F32 = jnp.float32


def rmsnorm(x, g):
    xf = x.astype(F32)
    y = xf * lax.rsqrt(jnp.mean(xf * xf, axis=-1, keepdims=True) + RMS_EPS)
    return (y * g.astype(F32)).astype(x.dtype)


def centred_depthwise_conv(u, w, b):
    pad = (SSD_CONV - 1) // 2
    out = lax.conv_general_dilated(u, w[:, None, :].astype(u.dtype), window_strides=(1,),
                                   padding=((pad, pad),), dimension_numbers=('NWC', 'WIO', 'NWC'),
                                   feature_group_count=u.shape[-1])
    return out + b.astype(u.dtype)


def ssd_chunked(xh, dt, A, Bm, Cm):
    b, l, h, p = xh.shape
    g, n = Bm.shape[-2:]
    r = h // g
    L = SSD_CHUNK
    c = l // L
    X = (xh * dt[..., None]).reshape(b, c, L, g, r, p)
    a = (dt * A).reshape(b, c, L, g, r)
    acs = jnp.cumsum(jnp.moveaxis(a, 2, -1), axis=-1)
    Bc = Bm.reshape(b, c, L, g, n)
    Cc = Cm.reshape(b, c, L, g, n)
    lower = jnp.tril(jnp.ones((L, L), bool))
    decay = jnp.exp(jnp.where(lower, acs[..., :, None] - acs[..., None, :], -jnp.inf))
    cb = jnp.einsum('bclgn,bcsgn->bcgls', Cc, Bc)
    y_diag = jnp.einsum('bcgrls,bcsgrp->bclgrp', cb[:, :, :, None] * decay, X)
    state_decay = jnp.exp(acs[..., -1:] - acs)
    states = jnp.einsum('bclgn,bcgrl,bclgrp->bcgrpn', Bc, state_decay, X)
    chunk_decay = jnp.exp(acs[..., -1])

    def carry_state(s, inp):
        s_c, d_c = inp
        return s * d_c[..., None, None] + s_c, s

    _, prev = lax.scan(carry_state, jnp.zeros((b, g, r, p, n), F32),
                       (jnp.moveaxis(states, 1, 0), jnp.moveaxis(chunk_decay, 1, 0)))
    prev = jnp.moveaxis(prev, 0, 1)
    y_off = jnp.einsum('bclgn,bcgrpn,bcgrl->bclgrp', Cc, prev, jnp.exp(acs))
    return (y_diag + y_off).reshape(b, l, h, p)


def ssd_bidirectional(xh, dt_raw_f, dt_raw_b, Bm, Cm, dt_bias_f, dt_bias_b, a_log_f, a_log_b):
    dt_f = jax.nn.softplus(dt_raw_f + dt_bias_f.astype(F32))
    dt_b = jax.nn.softplus(dt_raw_b + dt_bias_b.astype(F32))
    A_f = -jnp.exp(a_log_f.astype(F32))
    A_b = -jnp.exp(a_log_b.astype(F32))
    rev = lambda t: jnp.flip(t, axis=1)
    y_f = ssd_chunked(xh, dt_f, A_f, Bm, Cm)
    y_b = rev(ssd_chunked(rev(xh), rev(dt_b), A_b, rev(Bm), rev(Cm)))
    return y_f + y_b


def gated_group_rmsnorm(y, z, w):
    v = y * jax.nn.silu(z.astype(F32))
    v = v.reshape(*v.shape[:-1], SSD_GROUPS, -1)
    v = v * lax.rsqrt(jnp.mean(v * v, axis=-1, keepdims=True) + RMS_EPS)
    return v.reshape(y.shape) * w.astype(F32)


def fourier_mix(u):
    bsz, seq, _ = u.shape
    ug = u.astype(F32).reshape(bsz, seq, FOURIER_GROUPS, FOURIER_GROUP_DIM)
    f = jnp.fft.fft2(ug, axes=(1, 3), norm='ortho').real
    return f.reshape(bsz, seq, FOURIER_WIDTH).astype(u.dtype)


def swiglu(x, w13, w2):
    a, g = jnp.split(x @ w13, 2, axis=-1)
    return (jax.nn.silu(a) * g) @ w2


def moe_ffn(xf, w_router, router_bias, w13_experts, w2_experts, w13_shared, w2_shared):
    T, D = xf.shape
    scores = jax.nn.sigmoid(xf.astype(F32) @ w_router.astype(F32))
    biased = scores + router_bias.astype(F32)
    per_group = N_EXPERTS // N_EXPERT_GROUPS
    group_score = lax.top_k(biased.reshape(T, N_EXPERT_GROUPS, per_group), 2)[0].sum(-1)
    _, top_groups = lax.top_k(group_score, TOPK_GROUPS)
    group_mask = jax.nn.one_hot(top_groups, N_EXPERT_GROUPS, dtype=F32).sum(1)
    expert_mask = jnp.repeat(group_mask, per_group, axis=1) > 0
    _, top_idx = lax.top_k(jnp.where(expert_mask, biased, -jnp.inf), TOP_K)
    top_w = jnp.take_along_axis(scores, top_idx, axis=1)
    top_w = top_w / jnp.sum(top_w, axis=-1, keepdims=True) * ROUTED_SCALE

    n_assign = T * TOP_K
    flat_e = top_idx.reshape(-1)
    order = jnp.argsort(flat_e)
    sorted_e = flat_e[order]
    sorted_tok = (order // TOP_K).astype(jnp.int32)
    sorted_w = top_w.reshape(-1)[order]
    counts = jnp.bincount(flat_e, length=N_EXPERTS)
    padded = (counts + EXPERT_BLOCK - 1) // EXPERT_BLOCK * EXPERT_BLOCK
    pad_end = jnp.cumsum(padded)
    pad_start = pad_end - padded
    start = jnp.cumsum(counts) - counts
    dest = pad_start[sorted_e] + jnp.arange(n_assign) - start[sorted_e]
    n_rows = -(-n_assign // EXPERT_BLOCK) * EXPERT_BLOCK + N_EXPERTS * EXPERT_BLOCK
    n_blocks = n_rows // EXPERT_BLOCK
    row_tok = jnp.full((n_rows,), T, jnp.int32).at[dest].set(sorted_tok)
    row_w = jnp.zeros((n_rows,), F32).at[dest].set(sorted_w)
    block_e = jnp.minimum(jnp.searchsorted(pad_end, jnp.arange(n_blocks) * EXPERT_BLOCK, side='right'),
                          N_EXPERTS - 1)
    x_pad = jnp.concatenate([xf, jnp.zeros((1, D), xf.dtype)], axis=0)

    def expert_block(acc, blk):
        tok, wgt, e = blk
        yb = swiglu(x_pad[tok], w13_experts[e], w2_experts[e])
        return acc.at[tok].add(yb.astype(F32) * wgt[:, None]), None

    acc, _ = lax.scan(expert_block, jnp.zeros((T + 1, D), F32),
                      (row_tok.reshape(n_blocks, EXPERT_BLOCK), row_w.reshape(n_blocks, EXPERT_BLOCK), block_e))
    routed = acc[:T].astype(xf.dtype)
    return routed + swiglu(xf, w13_shared, w2_shared)


def hybrid_layer(x, c, w_ada, b_ada, pre_norm_mix, post_norm_mix, pre_norm_ffn, post_norm_ffn, w_in,
                 conv_w, conv_b, dt_bias_fwd, dt_bias_bwd, a_log_fwd, a_log_bwd, d_skip, ssd_norm,
                 w_branch_ssd, w_branch_fourier, w_out, w_router, router_bias, w13_experts, w2_experts,
                 w13_shared, w2_shared):
    bsz, seq, d = x.shape
    mod = jax.nn.silu(c) @ w_ada + b_ada
    shift_m, scale_m, gate_m, shift_f, scale_f, gate_f = jnp.split(mod[:, None, :], 6, axis=-1)

    h = rmsnorm(x, pre_norm_mix) * (1 + scale_m) + shift_m
    proj = h @ w_in
    i1 = SSD_INNER
    i2 = i1 + XBC_WIDTH
    i3 = i2 + 2 * SSD_HEADS
    i4 = i3 + FOURIER_WIDTH
    z, xbc, dt_raw, u_f, gates = jnp.split(proj, [i1, i2, i3, i4], axis=-1)

    xbc = jax.nn.silu(centred_depthwise_conv(xbc, conv_w, conv_b))
    gn = SSD_GROUPS * SSD_STATE
    xs, Bm, Cm = jnp.split(xbc, [SSD_INNER, SSD_INNER + gn], axis=-1)
    xh = xs.astype(F32).reshape(bsz, seq, SSD_HEADS, SSD_HEAD_DIM)
    Bm = Bm.astype(F32).reshape(bsz, seq, SSD_GROUPS, SSD_STATE)
    Cm = Cm.astype(F32).reshape(bsz, seq, SSD_GROUPS, SSD_STATE)
    dt_f, dt_b = jnp.split(dt_raw.astype(F32), 2, axis=-1)
    y = ssd_bidirectional(xh, dt_f, dt_b, Bm, Cm, dt_bias_fwd, dt_bias_bwd, a_log_fwd, a_log_bwd)
    y = (y + d_skip.astype(F32)[:, None] * xh).reshape(bsz, seq, SSD_INNER)
    y = gated_group_rmsnorm(y, z, ssd_norm).astype(x.dtype)
    y_ssd = y @ w_branch_ssd

    y_fourier = fourier_mix(u_f) @ w_branch_fourier

    g_ssd, g_fourier = jnp.split(jax.nn.sigmoid(gates), 2, axis=-1)
    mixed = (g_ssd * y_ssd + g_fourier * y_fourier) @ w_out
    x = x + gate_m * rmsnorm(mixed, post_norm_mix)

    h = rmsnorm(x, pre_norm_ffn) * (1 + scale_f) + shift_f
    ffn = moe_ffn(h.reshape(bsz * seq, d), w_router, router_bias, w13_experts, w2_experts,
                  w13_shared, w2_shared).reshape(bsz, seq, d)
    return x + gate_f * rmsnorm(ffn, post_norm_ffn)


def setup_inputs(seed: int = 0) -> dict:
    key = jax.random.key(seed)
    ks = jax.random.split(key, 24)
    D = D_MODEL
    Ld = DEPTH

    def normal(k, shape, scale):
        return jax.random.normal(k, shape, F32) * scale

    dt0 = jnp.exp(jax.random.uniform(ks[11], (Ld, 2, SSD_HEADS), F32, math.log(1e-3), math.log(1e-1)))
    dt_bias = dt0 + jnp.log(-jnp.expm1(-dt0))
    a_log = jnp.log(jax.random.uniform(ks[12], (Ld, 2, SSD_HEADS), F32, 1.0, 16.0))
    return {
        "x": normal(ks[0], (BATCH, SEQ, D), 1.0),
        "c": normal(ks[1], (BATCH, D), 1.0),
        "w_ada": normal(ks[2], (Ld, D, 6 * D), 0.5 * D ** -0.5),
        "b_ada": normal(ks[3], (Ld, 6 * D), 0.02),
        "pre_norm_mix": 1.0 + normal(ks[4], (Ld, D), 0.05),
        "post_norm_mix": 1.0 + normal(ks[5], (Ld, D), 0.05),
        "pre_norm_ffn": 1.0 + normal(ks[6], (Ld, D), 0.05),
        "post_norm_ffn": 1.0 + normal(ks[7], (Ld, D), 0.05),
        "w_in": normal(ks[8], (Ld, D, IN_WIDTH), D ** -0.5),
        "conv_w": normal(ks[9], (Ld, SSD_CONV, XBC_WIDTH), SSD_CONV ** -0.5),
        "conv_b": normal(ks[10], (Ld, XBC_WIDTH), 0.02),
        "dt_bias_fwd": dt_bias[:, 0],
        "dt_bias_bwd": dt_bias[:, 1],
        "a_log_fwd": a_log[:, 0],
        "a_log_bwd": a_log[:, 1],
        "d_skip": 1.0 + normal(ks[13], (Ld, SSD_HEADS), 0.1),
        "ssd_norm": 1.0 + normal(ks[14], (Ld, SSD_INNER), 0.05),
        "w_branch_ssd": normal(ks[15], (Ld, SSD_INNER, D), SSD_INNER ** -0.5),
        "w_branch_fourier": normal(ks[16], (Ld, FOURIER_WIDTH, D), FOURIER_WIDTH ** -0.5),
        "w_out": normal(ks[17], (Ld, D, D), D ** -0.5),
        "w_router": normal(ks[18], (Ld, D, N_EXPERTS), D ** -0.5),
        "router_bias": normal(ks[19], (Ld, N_EXPERTS), 0.01),
        "w13_experts": normal(ks[20], (Ld, N_EXPERTS, D, 2 * EXPERT_HIDDEN), D ** -0.5),
        "w2_experts": normal(ks[21], (Ld, N_EXPERTS, EXPERT_HIDDEN, D), EXPERT_HIDDEN ** -0.5),
        "w13_shared": normal(ks[22], (Ld, D, 2 * SHARED_HIDDEN), D ** -0.5),
        "w2_shared": normal(ks[23], (Ld, SHARED_HIDDEN, D), SHARED_HIDDEN ** -0.5),
    }


def reference(x, c, w_ada, b_ada, pre_norm_mix, post_norm_mix, pre_norm_ffn, post_norm_ffn, w_in,
              conv_w, conv_b, dt_bias_fwd, dt_bias_bwd, a_log_fwd, a_log_bwd, d_skip, ssd_norm,
              w_branch_ssd, w_branch_fourier, w_out, w_router, router_bias, w13_experts, w2_experts,
              w13_shared, w2_shared):
    for layer in range(DEPTH):
        x = hybrid_layer(x, c, w_ada[layer], b_ada[layer], pre_norm_mix[layer], post_norm_mix[layer],
                         pre_norm_ffn[layer], post_norm_ffn[layer], w_in[layer], conv_w[layer], conv_b[layer],
                         dt_bias_fwd[layer], dt_bias_bwd[layer], a_log_fwd[layer], a_log_bwd[layer],
                         d_skip[layer], ssd_norm[layer], w_branch_ssd[layer], w_branch_fourier[layer],
                         w_out[layer], w_router[layer], router_bias[layer], w13_experts[layer],
                         w2_experts[layer], w13_shared[layer], w2_shared[layer])
    return x
```

```python
import functools
import math

import numpy as np
import jax
import jax.numpy as jnp
from jax import lax
from jax.experimental import pallas as pl
from jax.experimental.pallas import tpu as pltpu

F32 = jnp.float32
BF16 = jnp.bfloat16
HIGHEST = lax.Precision.HIGHEST

D_MODEL = 1024
SSD_HEADS = 24
SSD_HEAD_DIM = 64
SSD_INNER = SSD_HEADS * SSD_HEAD_DIM
SSD_GROUPS = 4
HEADS_PER_GROUP = SSD_HEADS // SSD_GROUPS
HEADS_PADDED = 8
SSD_STATE = 128
SSD_CONV = 5
SSD_CHUNK = 128
XBC_WIDTH = SSD_INNER + 2 * SSD_GROUPS * SSD_STATE
GROUP_X = HEADS_PER_GROUP * SSD_HEAD_DIM
FOURIER_WIDTH = 512
FOURIER_GROUP_DIM = 128
N_EXPERTS = 256
TOP_K = 8
N_EXPERT_GROUPS = 8
TOPK_GROUPS = 4
EXPERT_HIDDEN = 256
SHARED_HIDDEN = 256
ROUTED_SCALE = 2.5
RMS_EPS = 1e-6

LANES = 128
VMEM_LIMIT = 56 * 1024 * 1024
TOKEN_TILE = 256
EXPERT_ROWS = 256
COMBINE_TILE = 128
CONV_ROWS = 256


def _cparams(sem):
    return pltpu.CompilerParams(dimension_semantics=sem, vmem_limit_bytes=VMEM_LIMIT)


def _dot(a, b, precision=None):
    return jnp.dot(a, b, preferred_element_type=F32, precision=precision)


def _dot_nt(a, b, precision=None):
    return lax.dot_general(a, b, (((1,), (1,)), ((), ())), preferred_element_type=F32,
                           precision=precision)


def _dot_tn(a, b):
    return lax.dot_general(a, b, (((0,), (0,)), ((), ())), preferred_element_type=F32)


def _sigmoid(x):
    return 1.0 / (1.0 + jnp.exp(-x))


def _silu(x):
    return x * _sigmoid(x)


def _softplus(x):
    return jnp.maximum(x, 0.0) + jnp.log1p(jnp.exp(-jnp.abs(x)))


def _rms(x, g):
    return x * lax.rsqrt(jnp.mean(x * x, axis=-1, keepdims=True) + RMS_EPS) * g


def _ada_kernel(c_ref, w_ref, b_ref, o_ref):
    o_ref[...] = _dot(_silu(c_ref[...]), w_ref[...], HIGHEST) + b_ref[...]


def _ada(c, w_ada, b_ada):
    bsz, d = c.shape
    rows = 8
    cp = jnp.zeros((rows, d), F32).at[:bsz].set(c)
    n = w_ada.shape[1]
    tn = 1536
    out = pl.pallas_call(
        _ada_kernel,
        grid=(n // tn,),
        in_specs=[pl.BlockSpec((rows, d), lambda j: (0, 0)),
                  pl.BlockSpec((d, tn), lambda j: (0, j)),
                  pl.BlockSpec((1, tn), lambda j: (0, j))],
        out_specs=pl.BlockSpec((rows, tn), lambda j: (0, j)),
        out_shape=jax.ShapeDtypeStruct((rows, n), F32),
        compiler_params=_cparams(("arbitrary",)),
        name="adaln",
    )(cp, w_ada, b_ada.reshape(1, n))
    return out[:bsz].reshape(bsz, 6, d)


def _inproj_kernel(x_ref, mod_ref, g_ref, wz_ref, wxbc_ref, wdtc_ref, wdtT_ref, wuf_ref, wg_ref,
                   z_ref, xbc_ref, dtc_ref, dtT_ref, uf_ref, gates_ref):
    m = mod_ref[0]
    h = _rms(x_ref[...], g_ref[...]) * (1.0 + m[1:2]) + m[0:1]
    hb = h.astype(BF16)
    z_ref[...] = _dot(hb, wz_ref[...])
    xbc_ref[...] = _dot(hb, wxbc_ref[...])
    dtc_ref[...] = _dot(hb, wdtc_ref[...])
    dtT_ref[...] = _dot_nt(wdtT_ref[...], hb)
    uf_ref[...] = _dot(hb, wuf_ref[...])
    gates_ref[...] = _dot(hb, wg_ref[...])


def _inproj(x2, mod, g, wz, wxbc, wdtc, wdtT, wuf, wg, seq):
    t, d = x2.shape
    tm = TOKEN_TILE
    per_b = seq // tm
    full = lambda a: pl.BlockSpec(a.shape, lambda i: (0,) * a.ndim)
    row = lambda n: pl.BlockSpec((tm, n), lambda i: (i, 0))
    nd = wdtT.shape[0]
    return pl.pallas_call(
        _inproj_kernel,
        grid=(t // tm,),
        in_specs=[row(d), pl.BlockSpec((1, 6, d), lambda i: (i // per_b, 0, 0)), full(g),
                  full(wz), full(wxbc), full(wdtc), full(wdtT), full(wuf), full(wg)],
        out_specs=[row(wz.shape[1]), row(wxbc.shape[1]), row(wdtc.shape[1]),
                   pl.BlockSpec((nd, tm), lambda i: (0, i)), row(wuf.shape[1]), row(wg.shape[1])],
        out_shape=[jax.ShapeDtypeStruct((t, wz.shape[1]), F32),
                   jax.ShapeDtypeStruct((t, wxbc.shape[1]), F32),
                   jax.ShapeDtypeStruct((t, wdtc.shape[1]), F32),
                   jax.ShapeDtypeStruct((nd, t), F32),
                   jax.ShapeDtypeStruct((t, wuf.shape[1]), F32),
                   jax.ShapeDtypeStruct((t, wg.shape[1]), F32)],
        compiler_params=_cparams(("arbitrary",)),
        name="inproj",
    )(x2, mod, g, wz, wxbc, wdtc, wdtT, wuf, wg)


def _conv_kernel(u_ref, w_ref, b_ref, o_ref, pad_ref):
    s = u_ref.shape[1]
    halo = 8
    pad_ref[0:halo, :] = jnp.zeros((halo, LANES), F32)
    pad_ref[halo + s:2 * halo + s, :] = jnp.zeros((halo, LANES), F32)
    pad_ref[halo:halo + s, :] = u_ref[0]
    w = w_ref[...]
    b = b_ref[...]
    half = (SSD_CONV - 1) // 2
    for r in range(s // CONV_ROWS):
        base = r * CONV_ROWS
        acc = b
        for k in range(SSD_CONV):
            lo = base + halo + k - half
            acc = acc + w[k:k + 1, :] * pad_ref[lo:lo + CONV_ROWS, :]
        o_ref[0, base:base + CONV_ROWS, :] = _silu(acc)


def _conv(xbc3, conv_w, conv_b):
    bsz, s, c = xbc3.shape
    return pl.pallas_call(
        _conv_kernel,
        grid=(bsz, c // LANES),
        in_specs=[pl.BlockSpec((1, s, LANES), lambda b, j: (b, 0, j)),
                  pl.BlockSpec((SSD_CONV, LANES), lambda b, j: (0, j)),
                  pl.BlockSpec((1, LANES), lambda b, j: (0, j))],
        out_specs=pl.BlockSpec((1, s, LANES), lambda b, j: (b, 0, j)),
        out_shape=jax.ShapeDtypeStruct((bsz, s, c), F32),
        scratch_shapes=[pltpu.VMEM((s + 16, LANES), F32)],
        compiler_params=_cparams(("arbitrary", "arbitrary")),
        name="conv",
    )(xbc3, conv_w, conv_b.reshape(1, c))


def _expand_heads(v):
    rows = v.shape[0]
    return jnp.concatenate(
        [jnp.broadcast_to(v[:, j:j + 1], (rows, SSD_HEAD_DIM)) for j in range(HEADS_PER_GROUP)],
        axis=1)


def _ssd_direction(x, bm, cm, dt_t, dtc_all, bias_t, a_t, bias_row, a_row, lane_off, s_ref, reverse):
    L = x.shape[0]
    dt_r = _softplus(dt_t + bias_t)
    a_r = dt_r * a_t
    dt_c_all = _softplus(dtc_all + bias_row)
    a_c_all = dt_c_all * a_row
    ii = lax.broadcasted_iota(jnp.int32, (L, L), 0)
    jj = lax.broadcasted_iota(jnp.int32, (L, L), 1)
    mask = (jj >= ii) if reverse else (jj <= ii)
    tri = mask.astype(F32)
    acs_c_all = _dot(tri, a_c_all, HIGHEST)
    acs_r = _dot_nt(a_r, tri, HIGHEST)
    shift = jnp.where(lane_off == 0, 0, LANES - lane_off)
    acs_c = pltpu.roll(acs_c_all, shift, 1)[:, :HEADS_PADDED]
    dt_c = pltpu.roll(dt_c_all, shift, 1)[:, :HEADS_PADDED]
    last = 0 if reverse else L - 1
    tot_c = acs_c[last:last + 1, :]

    xb = x.astype(BF16)
    bb = bm.astype(BF16)
    cb = cm.astype(BF16)
    cbt = _dot_nt(cb, bb)
    s_prev = s_ref[...]
    cs = _dot(cb, s_prev.astype(BF16))
    ys = []
    for j in range(HEADS_PER_GROUP):
        col = acs_c[:, j:j + 1]
        row = acs_r[j:j + 1, :]
        decay = jnp.exp(jnp.where(mask, col - row, -jnp.inf))
        m = (cbt * decay * dt_r[j:j + 1, :]).astype(BF16)
        ys.append(_dot(m, xb[:, j * SSD_HEAD_DIM:(j + 1) * SSD_HEAD_DIM]))
    y = jnp.concatenate(ys, axis=1) + cs * _expand_heads(jnp.exp(acs_c))
    sd = jnp.exp(tot_c - acs_c) * dt_c
    xs = (x * _expand_heads(sd)).astype(BF16)
    s_ref[...] = s_prev * _expand_heads(jnp.exp(tot_c)) + _dot_tn(bb, xs)
    return y


def _ssd_kernel(xf_ref, bf_ref, cf_ref, dtTf_ref, dtcf_ref, xb_ref, bb_ref, cb_ref, dtTb_ref,
                dtcb_ref, biasTf_ref, aTf_ref, biasTb_ref, aTb_ref, bias_row_ref, a_row_ref,
                yf_ref, yb_ref, sf_ref, sb_ref):
    g = pl.program_id(1)

    @pl.when(pl.program_id(2) == 0)
    def _():
        sf_ref[...] = jnp.zeros_like(sf_ref)
        sb_ref[...] = jnp.zeros_like(sb_ref)

    bias_row = bias_row_ref[...]
    a_row = a_row_ref[...]
    yf_ref[0] = _ssd_direction(xf_ref[0], bf_ref[0], cf_ref[0], dtTf_ref[...], dtcf_ref[...],
                               biasTf_ref[...], aTf_ref[...], bias_row, a_row,
                               g * HEADS_PADDED, sf_ref, False)
    yb_ref[0] = _ssd_direction(xb_ref[0], bb_ref[0], cb_ref[0], dtTb_ref[...], dtcb_ref[...],
                               biasTb_ref[...], aTb_ref[...], bias_row, a_row,
                               SSD_GROUPS * HEADS_PADDED + g * HEADS_PADDED, sb_ref, True)


def _ssd(xbc3, dtT, dtc, bias_col, a_col, bias_row, a_row):
    bsz, s, _ = xbc3.shape
    L = SSD_CHUNK
    nc = s // L
    G = SSD_GROUPS
    nb = SSD_INNER // SSD_STATE
    ncb = nb + G
    fwd = lambda c: c
    bwd = lambda c: nc - 1 - c

    def specs(cidx, dirn):
        return [
            pl.BlockSpec((1, L, GROUP_X), lambda b, g, c: (b, cidx(c), g)),
            pl.BlockSpec((1, L, SSD_STATE), lambda b, g, c: (b, cidx(c), nb + g)),
            pl.BlockSpec((1, L, SSD_STATE), lambda b, g, c: (b, cidx(c), ncb + g)),
            pl.BlockSpec((HEADS_PADDED, L), lambda b, g, c: (dirn * G + g, b * nc + cidx(c))),
            pl.BlockSpec((L, LANES), lambda b, g, c: (b * nc + cidx(c), 0)),
        ]

    def colspec(dirn):
        return pl.BlockSpec((HEADS_PADDED, 1), lambda b, g, c: (dirn * G + g, 0))

    rowspec = pl.BlockSpec((1, LANES), lambda b, g, c: (0, 0))
    in_specs = (specs(fwd, 0) + specs(bwd, 1)
                + [colspec(0), colspec(0), colspec(1), colspec(1), rowspec, rowspec])
    out_specs = [pl.BlockSpec((1, L, GROUP_X), lambda b, g, c: (b, c, g)),
                 pl.BlockSpec((1, L, GROUP_X), lambda b, g, c: (b, nc - 1 - c, g))]
    return pl.pallas_call(
        _ssd_kernel,
        grid=(bsz, G, nc),
        in_specs=in_specs,
        out_specs=out_specs,
        out_shape=[jax.ShapeDtypeStruct((bsz, s, SSD_INNER), F32)] * 2,
        scratch_shapes=[pltpu.VMEM((SSD_STATE, GROUP_X), F32)] * 2,
        compiler_params=_cparams(("arbitrary", "arbitrary", "arbitrary")),
        name="ssd",
    )(xbc3, xbc3, xbc3, dtT, dtc, xbc3, xbc3, xbc3, dtT, dtc,
      bias_col, a_col, bias_col, a_col, bias_row, a_row)


def _dft_tables(seq):
    n2n = LANES
    n1n = seq // n2n
    n1 = np.arange(n1n)
    k1 = np.arange(n1n)
    n2 = np.arange(n2n)
    ang = -2.0 * np.pi * (n2[:, None, None] * k1[None, :, None] / seq
                          + n1[None, None, :] * k1[None, :, None] / n1n)
    f1 = np.concatenate([np.cos(ang), np.sin(ang)], axis=1)
    k2 = np.arange(n2n)
    a2 = 2.0 * np.pi * np.outer(k2, n2) / n2n
    c2, s2 = np.cos(a2), np.sin(a2)
    g = np.block([[c2, s2], [-s2, c2]])
    ch = np.arange(FOURIER_GROUP_DIM)
    ac = 2.0 * np.pi * np.outer(ch, ch) / FOURIER_GROUP_DIM
    scale = 1.0 / math.sqrt(seq * FOURIER_GROUP_DIM)
    fc = np.concatenate([np.cos(ac), np.sin(ac)], axis=0) * scale
    return (jnp.asarray(f1, BF16), jnp.asarray(g, BF16), jnp.asarray(fc, BF16))


def _fourier_kernel(u_ref, f1_ref, g_ref, fc_ref, o_ref, a_ref):
    n2n = LANES
    n1n = u_ref.shape[1] // n2n

    def stage1(n2, carry):
        xs = u_ref[0, pl.ds(n2, n1n, stride=n2n), :].astype(BF16)
        a_ref[pl.ds(n2, 2 * n1n, stride=n2n), :] = _dot(f1_ref[n2], xs)
        return carry

    lax.fori_loop(0, n2n, stage1, 0)
    gm = g_ref[...]
    fc = fc_ref[...]

    def stage2(k1, carry):
        re = a_ref[pl.ds(pl.multiple_of(k1 * n2n, n2n), n2n), :]
        im = a_ref[pl.ds(pl.multiple_of((n1n + k1) * n2n, n2n), n2n), :]
        a = jnp.concatenate([re, im], axis=0).astype(BF16)
        z = _dot(gm, a)
        zz = jnp.concatenate([z[:n2n], z[n2n:]], axis=1).astype(BF16)
        o_ref[0, pl.ds(k1, n2n, stride=n1n), :] = _dot(zz, fc)
        return carry

    lax.fori_loop(0, n1n, stage2, 0)


def _fourier(uf3):
    bsz, s, w = uf3.shape
    f1, g, fc = _dft_tables(s)
    n1n = s // LANES
    return pl.pallas_call(
        _fourier_kernel,
        grid=(bsz, w // FOURIER_GROUP_DIM),
        in_specs=[pl.BlockSpec((1, s, LANES), lambda b, j: (b, 0, j)),
                  pl.BlockSpec(f1.shape, lambda b, j: (0, 0, 0)),
                  pl.BlockSpec(g.shape, lambda b, j: (0, 0)),
                  pl.BlockSpec(fc.shape, lambda b, j: (0, 0))],
        out_specs=pl.BlockSpec((1, s, LANES), lambda b, j: (b, 0, j)),
        out_shape=jax.ShapeDtypeStruct((bsz, s, w), F32),
        scratch_shapes=[pltpu.VMEM((2 * n1n * LANES, LANES), F32)],
        compiler_params=_cparams(("arbitrary", "arbitrary")),
        name="fourier",
    )(uf3, f1, g, fc)


def _mix_kernel(yf_ref, yb_ref, xs_ref, z_ref, fm_ref, gates_ref, x_ref, mod_ref, dskip_ref,
                ssdn_ref, wbs_ref, wbf_ref, wout_ref, gpost_ref, o_ref):
    m = mod_ref[0]
    y = yf_ref[...] + yb_ref[...] + dskip_ref[...] * xs_ref[...]
    v = y * _silu(z_ref[...])
    parts = []
    for g in range(SSD_GROUPS):
        vg = v[:, g * GROUP_X:(g + 1) * GROUP_X]
        parts.append(vg * lax.rsqrt(jnp.mean(vg * vg, axis=-1, keepdims=True) + RMS_EPS))
    vn = jnp.concatenate(parts, axis=1) * ssdn_ref[...]
    y_ssd = _dot(vn.astype(BF16), wbs_ref[...])
    y_fou = _dot(fm_ref[...].astype(BF16), wbf_ref[...])
    gt = _sigmoid(gates_ref[...])
    d = y_ssd.shape[1]
    mixed = gt[:, :d] * y_ssd + gt[:, d:] * y_fou
    mo = _dot(mixed.astype(BF16), wout_ref[...])
    o_ref[...] = x_ref[...] + m[2:3] * _rms(mo, gpost_ref[...])


def _mix(yf, yb, xbc, z, fm, gates, x2, mod, dskip, ssdn, wbs, wbf, wout, gpost, seq):
    t, d = x2.shape
    tm = TOKEN_TILE
    per_b = seq // tm
    full = lambda a: pl.BlockSpec(a.shape, lambda i: (0,) * a.ndim)
    row = lambda n: pl.BlockSpec((tm, n), lambda i: (i, 0))
    return pl.pallas_call(
        _mix_kernel,
        grid=(t // tm,),
        in_specs=[row(SSD_INNER), row(SSD_INNER), row(SSD_INNER), row(SSD_INNER),
                  row(FOURIER_WIDTH), row(2 * d), row(d),
                  pl.BlockSpec((1, 6, d), lambda i: (i // per_b, 0, 0)),
                  full(dskip), full(ssdn), full(wbs), full(wbf), full(wout), full(gpost)],
        out_specs=row(d),
        out_shape=jax.ShapeDtypeStruct((t, d), F32),
        compiler_params=_cparams(("arbitrary",)),
        name="mix",
    )(yf, yb, xbc, z, fm, gates, x2, mod, dskip, ssdn, wbs, wbf, wout, gpost)


def _route_kernel(x_ref, mod_ref, gpre_ref, wr_ref, rb_ref, hp_ref, idx_ref, w_ref):
    m = mod_ref[0]
    h = _rms(x_ref[...], gpre_ref[...]) * (1.0 + m[4:5]) + m[3:4]
    tm, d = h.shape
    half = d // 2
    bits = pltpu.bitcast(h.astype(BF16).astype(F32), jnp.uint32)
    hp_ref[...] = (bits[:, :half] >> 16) | (bits[:, half:] & jnp.uint32(0xFFFF0000))

    scores = _sigmoid(_dot(h, wr_ref[...], HIGHEST))
    biased = scores + rb_ref[...]
    ne = scores.shape[1]
    lane = lax.broadcasted_iota(jnp.int32, (tm, ne), 1)
    grp = lane // (ne // N_EXPERT_GROUPS)
    neg = -jnp.inf
    big = jnp.int32(ne)

    def argmax_first(v):
        mx = jnp.max(v, axis=-1, keepdims=True)
        ix = jnp.min(jnp.where(v == mx, lane, big), axis=-1, keepdims=True)
        return mx, ix

    gs = []
    for g in range(N_EXPERT_GROUPS):
        vg = jnp.where(grp == g, biased, neg)
        m1, i1 = argmax_first(vg)
        m2 = jnp.max(jnp.where(lane == i1, neg, vg), axis=-1, keepdims=True)
        gs.append(m1 + m2)
    allowed = jnp.zeros((tm, ne), jnp.bool_)
    for g in range(N_EXPERT_GROUPS):
        ahead = jnp.zeros((tm, 1), jnp.int32)
        for o in range(N_EXPERT_GROUPS):
            if o == g:
                continue
            beats = (gs[o] > gs[g]) | ((gs[o] == gs[g]) & (o < g))
            ahead = ahead + beats.astype(jnp.int32)
        allowed = allowed | ((grp == g) & (ahead < TOPK_GROUPS))
    masked = jnp.where(allowed, biased, neg)

    out_lane = lax.broadcasted_iota(jnp.int32, (tm, LANES), 1)
    idx_out = jnp.zeros((tm, LANES), jnp.int32)
    w_out = jnp.zeros((tm, LANES), F32)
    w_sum = jnp.zeros((tm, 1), F32)
    for k in range(TOP_K):
        _, ik = argmax_first(masked)
        hit = lane == ik
        wk = jnp.sum(jnp.where(hit, scores, 0.0), axis=-1, keepdims=True)
        masked = jnp.where(hit, neg, masked)
        idx_out = jnp.where(out_lane == k, ik, idx_out)
        w_out = jnp.where(out_lane == k, wk, w_out)
        w_sum = w_sum + wk
    idx_ref[...] = idx_out
    w_ref[...] = w_out / w_sum * ROUTED_SCALE


def _route(x1, mod, gpre, w_router, router_bias, seq):
    t, d = x1.shape
    tm = TOKEN_TILE
    per_b = seq // tm
    full = lambda a: pl.BlockSpec(a.shape, lambda i: (0,) * a.ndim)
    row = lambda n: pl.BlockSpec((tm, n), lambda i: (i, 0))
    rb = router_bias.reshape(1, -1)
    return pl.pallas_call(
        _route_kernel,
        grid=(t // tm,),
        in_specs=[row(d), pl.BlockSpec((1, 6, d), lambda i: (i // per_b, 0, 0)), full(gpre),
                  full(w_router), full(rb)],
        out_specs=[row(d // 2), row(LANES), row(LANES)],
        out_shape=[jax.ShapeDtypeStruct((t, d // 2), jnp.uint32),
                   jax.ShapeDtypeStruct((t, LANES), jnp.int32),
                   jax.ShapeDtypeStruct((t, LANES), F32)],
        compiler_params=_cparams(("arbitrary",)),
        name="route",
    )(x1, mod, gpre, w_router, rb)


GATHER_UNROLL = 8


def _gather_rows(idx_ref, n_rows, src_hbm, dst_of, sem):
    def body(q, carry):
        for u in range(GATHER_UNROLL):
            src = idx_ref[0, 0, q * GATHER_UNROLL + u]
            pltpu.make_async_copy(src_hbm.at[pl.ds(src, 1), :], dst_of(q, u), sem).start()
        return carry

    lax.fori_loop(0, n_rows // GATHER_UNROLL, body, 0)


def _expert_kernel(be_ref, nu_ref, tokc_ref, tokn_ref, hp_hbm, w13_ref, w2_ref, o_ref,
                   xbuf, w13b, w2b, sem):
    i = pl.program_id(0)
    n_used = nu_ref[0]
    slot = i % 2
    bm = xbuf.shape[1]

    def issue(tok_ref, s):
        _gather_rows(tok_ref, bm, hp_hbm,
                     lambda q, u: xbuf.at[s, pl.ds(q * GATHER_UNROLL + u, 1), :], sem.at[s])

    @pl.when(i == 0)
    def _():
        issue(tokc_ref, 0)

    @pl.when(i + 1 < n_used)
    def _():
        issue(tokn_ref, 1 - slot)

    @pl.when((i == 0) | (be_ref[i] != be_ref[jnp.maximum(i - 1, 0)]))
    def _():
        w13b[...] = w13_ref[0].astype(BF16)
        w2b[...] = w2_ref[0].astype(BF16)

    @pl.when(i < n_used)
    def _():
        pltpu.make_async_copy(hp_hbm.at[pl.ds(0, bm), :], xbuf.at[slot], sem.at[slot]).wait()
        w = xbuf[slot]
        lo = pltpu.bitcast(w << 16, F32).astype(BF16)
        hi = pltpu.bitcast(w & jnp.uint32(0xFFFF0000), F32).astype(BF16)
        half = lo.shape[1]
        ag = _dot(lo, w13b[:half, :]) + _dot(hi, w13b[half:, :])
        hh = ag.shape[1] // 2
        act = (_silu(ag[:, :hh]) * ag[:, hh:]).astype(BF16)
        o_ref[...] = _dot(act, w2b[...])

    @pl.when(i >= n_used)
    def _():
        o_ref[...] = jnp.zeros_like(o_ref)


def _experts(hp, row_tok, block_e, n_used, w13, w2):
    t, half = hp.shape
    ne, d, h2 = w13.shape
    bm = EXPERT_ROWS
    n_blocks = row_tok.shape[0] // bm
    tok3 = row_tok.reshape(n_blocks, 1, bm)
    grid_spec = pltpu.PrefetchScalarGridSpec(
        num_scalar_prefetch=2,
        grid=(n_blocks,),
        in_specs=[
            pl.BlockSpec((1, 1, bm), lambda i, be, nu: (i, 0, 0), memory_space=pltpu.SMEM),
            pl.BlockSpec((1, 1, bm), lambda i, be, nu: (jnp.minimum(i + 1, n_blocks - 1), 0, 0),
                         memory_space=pltpu.SMEM),
            pl.BlockSpec(memory_space=pl.ANY),
            pl.BlockSpec((1, d, h2), lambda i, be, nu: (be[i], 0, 0)),
            pl.BlockSpec((1, h2 // 2, d), lambda i, be, nu: (be[i], 0, 0)),
        ],
        out_specs=pl.BlockSpec((bm, d), lambda i, be, nu: (i, 0)),
        scratch_shapes=[pltpu.VMEM((2, bm, half), jnp.uint32),
                        pltpu.VMEM((d, h2), BF16),
                        pltpu.VMEM((h2 // 2, d), BF16),
                        pltpu.SemaphoreType.DMA((2,))],
    )
    return pl.pallas_call(
        _expert_kernel,
        grid_spec=grid_spec,
        out_shape=jax.ShapeDtypeStruct((n_blocks * bm, d), F32),
        compiler_params=_cparams(("arbitrary",)),
        name="experts",
    )(block_e, n_used, tok3, tok3, hp, w13, w2)


def _final_kernel(dc_ref, dn_ref, y_hbm, w_ref, x_ref, mod_ref, gpre_ref, gpost_ref, w13s_ref,
                  w2s_ref, o_ref, buf, sem):
    i = pl.program_id(0)
    slot = i % 2
    tm = x_ref.shape[0]

    def issue(d_ref, s):
        _gather_rows(d_ref, tm * TOP_K, y_hbm,
                     lambda q, u: buf.at[s, u, pl.ds(q, 1), :], sem.at[s])

    @pl.when(i == 0)
    def _():
        issue(dc_ref, 0)

    @pl.when(i + 1 < pl.num_programs(0))
    def _():
        issue(dn_ref, 1 - slot)

    for k in range(TOP_K):
        pltpu.make_async_copy(y_hbm.at[pl.ds(0, tm), :], buf.at[slot, k], sem.at[slot]).wait()

    w = w_ref[...]
    routed = buf[slot, 0] * w[:, 0:1]
    for k in range(1, TOP_K):
        routed = routed + buf[slot, k] * w[:, k:k + 1]

    m = mod_ref[0]
    x1 = x_ref[...]
    h = (_rms(x1, gpre_ref[...]) * (1.0 + m[4:5]) + m[3:4]).astype(BF16)
    ag = _dot(h, w13s_ref[...])
    hh = ag.shape[1] // 2
    act = (_silu(ag[:, :hh]) * ag[:, hh:]).astype(BF16)
    ffn = routed + _dot(act, w2s_ref[...])
    o_ref[...] = x1 + m[5:6] * _rms(ffn, gpost_ref[...])


def _final(dest, y_sorted, top_w, x1, mod, gpre, gpost, w13s, w2s, seq):
    t, d = x1.shape
    tm = COMBINE_TILE
    per_b = seq // tm
    n_steps = t // tm
    dest3 = dest.reshape(n_steps, 1, tm * TOP_K)
    full = lambda a: pl.BlockSpec(a.shape, lambda i: (0,) * a.ndim)
    row = lambda n: pl.BlockSpec((tm, n), lambda i: (i, 0))
    return pl.pallas_call(
        _final_kernel,
        grid=(n_steps,),
        in_specs=[
            pl.BlockSpec((1, 1, tm * TOP_K), lambda i: (i, 0, 0), memory_space=pltpu.SMEM),
            pl.BlockSpec((1, 1, tm * TOP_K), lambda i: (jnp.minimum(i + 1, n_steps - 1), 0, 0),
                         memory_space=pltpu.SMEM),
            pl.BlockSpec(memory_space=pl.ANY),
            row(LANES), row(d), pl.BlockSpec((1, 6, d), lambda i: (i // per_b, 0, 0)),
            full(gpre), full(gpost), full(w13s), full(w2s)],
        out_specs=row(d),
        out_shape=jax.ShapeDtypeStruct((t, d), F32),
        scratch_shapes=[pltpu.VMEM((2, TOP_K, tm, d), F32), pltpu.SemaphoreType.DMA((2,))],
        compiler_params=_cparams(("arbitrary",)),
        name="combine_final",
    )(dest3, dest3, y_sorted, top_w, x1, mod, gpre, gpost, w13s, w2s)


def _pad_heads(v):
    lead = v.shape[:-1]
    v = v.reshape(lead + (2, SSD_GROUPS, HEADS_PER_GROUP))
    v = jnp.pad(v, [(0, 0)] * (len(lead) + 2) + [(0, HEADS_PADDED - HEADS_PER_GROUP)])
    return v.reshape(lead + (2 * SSD_GROUPS * HEADS_PADDED,))


def _dispatch_plan(top_idx, n_tokens):
    bm = EXPERT_ROWS
    n_assign = n_tokens * TOP_K
    flat_e = top_idx.reshape(-1)
    order = jnp.argsort(flat_e, stable=True).astype(jnp.int32)
    sorted_e = flat_e[order]
    counts = jnp.zeros((N_EXPERTS,), jnp.int32).at[flat_e].add(1)
    padded = (counts + bm - 1) // bm * bm
    pad_end = jnp.cumsum(padded)
    pad_start = pad_end - padded
    start = jnp.cumsum(counts) - counts
    dest_sorted = pad_start[sorted_e] + jnp.arange(n_assign, dtype=jnp.int32) - start[sorted_e]
    n_rows = -(-n_assign // bm) * bm + N_EXPERTS * bm
    n_blocks = n_rows // bm
    row_tok = jnp.zeros((n_rows,), jnp.int32).at[dest_sorted].set(order // TOP_K)
    dest = jnp.zeros((n_assign,), jnp.int32).at[order].set(dest_sorted)
    block_start = jnp.arange(n_blocks, dtype=jnp.int32) * bm
    block_e = jnp.minimum(jnp.searchsorted(pad_end, block_start, side="right"),
                          N_EXPERTS - 1).astype(jnp.int32)
    n_used = (pad_end[-1] // bm).astype(jnp.int32).reshape(1)
    return row_tok, dest, block_e, n_used


def _layer(x, c, w_ada, b_ada, pre_norm_mix, post_norm_mix, pre_norm_ffn, post_norm_ffn, w_in,
           conv_w, conv_b, dt_bias_fwd, dt_bias_bwd, a_log_fwd, a_log_bwd, d_skip, ssd_norm,
           w_branch_ssd, w_branch_fourier, w_out, w_router, router_bias, w13_experts, w2_experts,
           w13_shared, w2_shared):
    bsz, seq, d = x.shape
    t = bsz * seq
    x2 = x.reshape(t, d)
    row = lambda v: v.reshape(1, -1).astype(F32)

    mod = _ada(c, w_ada, b_ada)

    i1 = SSD_INNER
    i2 = i1 + XBC_WIDTH
    i3 = i2 + 2 * SSD_HEADS
    i4 = i3 + FOURIER_WIDTH
    n_dt = 2 * SSD_GROUPS * HEADS_PADDED
    w_dtp = _pad_heads(w_in[:, i2:i3])
    wdtc = jnp.pad(w_dtp, ((0, 0), (0, LANES - n_dt))).astype(BF16)
    wdtT = w_dtp.T.astype(BF16)
    z, xbc, dtc, dtT, uf, gates = _inproj(
        x2, mod, row(pre_norm_mix), w_in[:, :i1].astype(BF16), w_in[:, i1:i2].astype(BF16),
        wdtc, wdtT, w_in[:, i3:i4].astype(BF16), w_in[:, i4:].astype(BF16), seq)

    xbc3 = _conv(xbc.reshape(bsz, seq, XBC_WIDTH), conv_w, conv_b)

    bias_p = _pad_heads(jnp.concatenate([dt_bias_fwd, dt_bias_bwd]).astype(F32))
    a_p = _pad_heads(-jnp.exp(jnp.concatenate([a_log_fwd, a_log_bwd]).astype(F32)))
    pad_row = lambda v: jnp.pad(v, (0, LANES - n_dt)).reshape(1, LANES)
    yf, yb = _ssd(xbc3, dtT, dtc, bias_p.reshape(n_dt, 1), a_p.reshape(n_dt, 1),
                  pad_row(bias_p), pad_row(a_p))

    fm = _fourier(uf.reshape(bsz, seq, FOURIER_WIDTH))

    x1 = _mix(yf.reshape(t, SSD_INNER), yb.reshape(t, SSD_INNER), xbc3.reshape(t, XBC_WIDTH), z,
              fm.reshape(t, FOURIER_WIDTH), gates, x2, mod,
              row(jnp.repeat(d_skip, SSD_HEAD_DIM)), row(ssd_norm), w_branch_ssd.astype(BF16),
              w_branch_fourier.astype(BF16), w_out.astype(BF16), row(post_norm_mix), seq)

    hp, idx_pad, w_pad = _route(x1, mod, row(pre_norm_ffn), w_router.astype(F32), router_bias, seq)
    row_tok, dest, block_e, n_used = _dispatch_plan(idx_pad[:, :TOP_K], t)
    y_sorted = _experts(hp, row_tok, block_e, n_used, w13_experts, w2_experts)
    out = _final(dest, y_sorted, w_pad, x1, mod, row(pre_norm_ffn), row(post_norm_ffn),
                 w13_shared.astype(BF16), w2_shared.astype(BF16), seq)
    return out.reshape(bsz, seq, d)


def kernel(x, c, w_ada, b_ada, pre_norm_mix, post_norm_mix, pre_norm_ffn, post_norm_ffn, w_in,
           conv_w, conv_b, dt_bias_fwd, dt_bias_bwd, a_log_fwd, a_log_bwd, d_skip, ssd_norm,
           w_branch_ssd, w_branch_fourier, w_out, w_router, router_bias, w13_experts, w2_experts,
           w13_shared, w2_shared):
    for layer in range(w_ada.shape[0]):
        x = _layer(x, c, w_ada[layer], b_ada[layer], pre_norm_mix[layer], post_norm_mix[layer],
                   pre_norm_ffn[layer], post_norm_ffn[layer], w_in[layer], conv_w[layer],
                   conv_b[layer], dt_bias_fwd[layer], dt_bias_bwd[layer], a_log_fwd[layer],
                   a_log_bwd[layer], d_skip[layer], ssd_norm[layer], w_branch_ssd[layer],
                   w_branch_fourier[layer], w_out[layer], w_router[layer], router_bias[layer],
                   w13_experts[layer], w2_experts[layer], w13_shared[layer], w2_shared[layer])
    return x
```

```python
import functools
import math

import numpy as np
import jax
import jax.numpy as jnp
from jax import lax
from jax.experimental import pallas as pl
from jax.experimental.pallas import tpu as pltpu

F32 = jnp.float32
BF16 = jnp.bfloat16
HIGHEST = lax.Precision.HIGHEST

D_MODEL = 1024
SSD_HEADS = 24
SSD_HEAD_DIM = 64
SSD_INNER = SSD_HEADS * SSD_HEAD_DIM
SSD_GROUPS = 4
HEADS_PER_GROUP = SSD_HEADS // SSD_GROUPS
HEADS_PADDED = 8
SSD_STATE = 128
SSD_CONV = 5
SSD_CHUNK = 128
XBC_WIDTH = SSD_INNER + 2 * SSD_GROUPS * SSD_STATE
GROUP_X = HEADS_PER_GROUP * SSD_HEAD_DIM
FOURIER_WIDTH = 512
FOURIER_GROUP_DIM = 128
N_EXPERTS = 256
TOP_K = 8
N_EXPERT_GROUPS = 8
TOPK_GROUPS = 4
EXPERT_HIDDEN = 256
SHARED_HIDDEN = 256
ROUTED_SCALE = 2.5
RMS_EPS = 1e-6

LANES = 128
VMEM_LIMIT = 56 * 1024 * 1024
TOKEN_TILE = 256
EXPERT_ROWS = 256
COMBINE_TILE = 128
SCATTER_TILE = 512
CONV_ROWS = 256


def _cparams(sem):
    return pltpu.CompilerParams(dimension_semantics=sem, vmem_limit_bytes=VMEM_LIMIT)


def _dot(a, b, precision=None):
    return jnp.dot(a, b, preferred_element_type=F32, precision=precision)


def _dot_nt(a, b, precision=None):
    return lax.dot_general(a, b, (((1,), (1,)), ((), ())), preferred_element_type=F32,
                           precision=precision)


def _dot_tn(a, b):
    return lax.dot_general(a, b, (((0,), (0,)), ((), ())), preferred_element_type=F32)


def _sigmoid(x):
    return 1.0 / (1.0 + jnp.exp(-x))


def _silu(x):
    return x * _sigmoid(x)


def _softplus(x):
    return jnp.maximum(x, 0.0) + jnp.log1p(jnp.exp(-jnp.abs(x)))


def _rms(x, g):
    return x * lax.rsqrt(jnp.mean(x * x, axis=-1, keepdims=True) + RMS_EPS) * g


def _ada_kernel(c_ref, w_ref, b_ref, o_ref):
    o_ref[...] = _dot(_silu(c_ref[...]), w_ref[...], HIGHEST) + b_ref[...]


def _ada(c, w_ada, b_ada):
    bsz, d = c.shape
    rows = 8
    cp = jnp.zeros((rows, d), F32).at[:bsz].set(c)
    n = w_ada.shape[1]
    tn = 1536
    out = pl.pallas_call(
        _ada_kernel,
        grid=(n // tn,),
        in_specs=[pl.BlockSpec((rows, d), lambda j: (0, 0)),
                  pl.BlockSpec((d, tn), lambda j: (0, j)),
                  pl.BlockSpec((1, tn), lambda j: (0, j))],
        out_specs=pl.BlockSpec((rows, tn), lambda j: (0, j)),
        out_shape=jax.ShapeDtypeStruct((rows, n), F32),
        compiler_params=_cparams(("arbitrary",)),
        name="adaln",
    )(cp, w_ada, b_ada.reshape(1, n))
    return out[:bsz].reshape(bsz, 6, d)


def _inproj_kernel(x_ref, mod_ref, g_ref, wz_ref, wxbc_ref, wdtc_ref, wdtT_ref, wuf_ref, wg_ref,
                   z_ref, xbc_ref, dtc_ref, dtT_ref, uf_ref, gates_ref):
    m = mod_ref[0]
    h = _rms(x_ref[...], g_ref[...]) * (1.0 + m[1:2]) + m[0:1]
    hb = h.astype(BF16)
    z_ref[...] = _dot(hb, wz_ref[...])
    xbc_ref[...] = _dot(hb, wxbc_ref[...])
    dtc_ref[...] = _dot(hb, wdtc_ref[...])
    dtT_ref[...] = _dot_nt(wdtT_ref[...], hb)
    uf_ref[...] = _dot(hb, wuf_ref[...])
    gates_ref[...] = _dot(hb, wg_ref[...])


def _inproj(x2, mod, g, wz, wxbc, wdtc, wdtT, wuf, wg, seq):
    t, d = x2.shape
    tm = TOKEN_TILE
    per_b = seq // tm
    full = lambda a: pl.BlockSpec(a.shape, lambda i: (0,) * a.ndim)
    row = lambda n: pl.BlockSpec((tm, n), lambda i: (i, 0))
    nd = wdtT.shape[0]
    return pl.pallas_call(
        _inproj_kernel,
        grid=(t // tm,),
        in_specs=[row(d), pl.BlockSpec((1, 6, d), lambda i: (i // per_b, 0, 0)), full(g),
                  full(wz), full(wxbc), full(wdtc), full(wdtT), full(wuf), full(wg)],
        out_specs=[row(wz.shape[1]), row(wxbc.shape[1]), row(wdtc.shape[1]),
                   pl.BlockSpec((nd, tm), lambda i: (0, i)), row(wuf.shape[1]), row(wg.shape[1])],
        out_shape=[jax.ShapeDtypeStruct((t, wz.shape[1]), F32),
                   jax.ShapeDtypeStruct((t, wxbc.shape[1]), F32),
                   jax.ShapeDtypeStruct((t, wdtc.shape[1]), F32),
                   jax.ShapeDtypeStruct((nd, t), F32),
                   jax.ShapeDtypeStruct((t, wuf.shape[1]), F32),
                   jax.ShapeDtypeStruct((t, wg.shape[1]), F32)],
        compiler_params=_cparams(("arbitrary",)),
        name="inproj",
    )(x2, mod, g, wz, wxbc, wdtc, wdtT, wuf, wg)


def _conv_kernel(u_ref, w_ref, b_ref, o_ref, pad_ref):
    s = u_ref.shape[1]
    halo = 8
    pad_ref[0:halo, :] = jnp.zeros((halo, LANES), F32)
    pad_ref[halo + s:2 * halo + s, :] = jnp.zeros((halo, LANES), F32)
    pad_ref[halo:halo + s, :] = u_ref[0]
    w = w_ref[...]
    b = b_ref[...]
    half = (SSD_CONV - 1) // 2
    for r in range(s // CONV_ROWS):
        base = r * CONV_ROWS
        acc = b
        for k in range(SSD_CONV):
            lo = base + halo + k - half
            acc = acc + w[k:k + 1, :] * pad_ref[lo:lo + CONV_ROWS, :]
        o_ref[0, base:base + CONV_ROWS, :] = _silu(acc)


def _conv(xbc3, conv_w, conv_b):
    bsz, s, c = xbc3.shape
    return pl.pallas_call(
        _conv_kernel,
        grid=(bsz, c // LANES),
        in_specs=[pl.BlockSpec((1, s, LANES), lambda b, j: (b, 0, j)),
                  pl.BlockSpec((SSD_CONV, LANES), lambda b, j: (0, j)),
                  pl.BlockSpec((1, LANES), lambda b, j: (0, j))],
        out_specs=pl.BlockSpec((1, s, LANES), lambda b, j: (b, 0, j)),
        out_shape=jax.ShapeDtypeStruct((bsz, s, c), F32),
        scratch_shapes=[pltpu.VMEM((s + 16, LANES), F32)],
        compiler_params=_cparams(("arbitrary", "arbitrary")),
        name="conv",
    )(xbc3, conv_w, conv_b.reshape(1, c))


def _expand_heads(v):
    rows = v.shape[0]
    return jnp.concatenate(
        [jnp.broadcast_to(v[:, j:j + 1], (rows, SSD_HEAD_DIM)) for j in range(HEADS_PER_GROUP)],
        axis=1)


def _ssd_direction(x, bm, cm, dt_t, dtc_all, bias_t, a_t, bias_row, a_row, lane_off, s_ref, reverse):
    L = x.shape[0]
    dt_r = _softplus(dt_t + bias_t)
    a_r = dt_r * a_t
    dt_c_all = _softplus(dtc_all + bias_row)
    a_c_all = dt_c_all * a_row
    ii = lax.broadcasted_iota(jnp.int32, (L, L), 0)
    jj = lax.broadcasted_iota(jnp.int32, (L, L), 1)
    mask = (jj >= ii) if reverse else (jj <= ii)
    tri = mask.astype(F32)
    acs_c_all = _dot(tri, a_c_all, HIGHEST)
    acs_r = _dot_nt(a_r, tri, HIGHEST)
    shift = jnp.where(lane_off == 0, 0, LANES - lane_off)
    acs_c = pltpu.roll(acs_c_all, shift, 1)[:, :HEADS_PADDED]
    dt_c = pltpu.roll(dt_c_all, shift, 1)[:, :HEADS_PADDED]
    last = 0 if reverse else L - 1
    tot_c = acs_c[last:last + 1, :]

    xb = x.astype(BF16)
    bb = bm.astype(BF16)
    cb = cm.astype(BF16)
    cbt = _dot_nt(cb, bb)
    s_prev = s_ref[...]
    cs = _dot(cb, s_prev.astype(BF16))
    ys = []
    for j in range(HEADS_PER_GROUP):
        col = acs_c[:, j:j + 1]
        row = acs_r[j:j + 1, :]
        decay = jnp.exp(jnp.where(mask, col - row, -jnp.inf))
        m = (cbt * decay * dt_r[j:j + 1, :]).astype(BF16)
        ys.append(_dot(m, xb[:, j * SSD_HEAD_DIM:(j + 1) * SSD_HEAD_DIM]))
    y = jnp.concatenate(ys, axis=1) + cs * _expand_heads(jnp.exp(acs_c))
    sd = jnp.exp(tot_c - acs_c) * dt_c
    xs = (x * _expand_heads(sd)).astype(BF16)
    s_ref[...] = s_prev * _expand_heads(jnp.exp(tot_c)) + _dot_tn(bb, xs)
    return y


def _ssd_kernel(xf_ref, bf_ref, cf_ref, dtTf_ref, dtcf_ref, xb_ref, bb_ref, cb_ref, dtTb_ref,
                dtcb_ref, biasTf_ref, aTf_ref, biasTb_ref, aTb_ref, bias_row_ref, a_row_ref,
                yf_ref, yb_ref, sf_ref, sb_ref):
    g = pl.program_id(1)

    @pl.when(pl.program_id(2) == 0)
    def _():
        sf_ref[...] = jnp.zeros_like(sf_ref)
        sb_ref[...] = jnp.zeros_like(sb_ref)

    bias_row = bias_row_ref[...]
    a_row = a_row_ref[...]
    yf_ref[0] = _ssd_direction(xf_ref[0], bf_ref[0], cf_ref[0], dtTf_ref[...], dtcf_ref[...],
                               biasTf_ref[...], aTf_ref[...], bias_row, a_row,
                               g * HEADS_PADDED, sf_ref, False)
    yb_ref[0] = _ssd_direction(xb_ref[0], bb_ref[0], cb_ref[0], dtTb_ref[...], dtcb_ref[...],
                               biasTb_ref[...], aTb_ref[...], bias_row, a_row,
                               SSD_GROUPS * HEADS_PADDED + g * HEADS_PADDED, sb_ref, True)


def _ssd(xbc3, dtT, dtc, bias_col, a_col, bias_row, a_row):
    bsz, s, _ = xbc3.shape
    L = SSD_CHUNK
    nc = s // L
    G = SSD_GROUPS
    nb = SSD_INNER // SSD_STATE
    ncb = nb + G
    fwd = lambda c: c
    bwd = lambda c: nc - 1 - c

    def specs(cidx, dirn):
        return [
            pl.BlockSpec((1, L, GROUP_X), lambda b, g, c: (b, cidx(c), g)),
            pl.BlockSpec((1, L, SSD_STATE), lambda b, g, c: (b, cidx(c), nb + g)),
            pl.BlockSpec((1, L, SSD_STATE), lambda b, g, c: (b, cidx(c), ncb + g)),
            pl.BlockSpec((HEADS_PADDED, L), lambda b, g, c: (dirn * G + g, b * nc + cidx(c))),
            pl.BlockSpec((L, LANES), lambda b, g, c: (b * nc + cidx(c), 0)),
        ]

    def colspec(dirn):
        return pl.BlockSpec((HEADS_PADDED, 1), lambda b, g, c: (dirn * G + g, 0))

    rowspec = pl.BlockSpec((1, LANES), lambda b, g, c: (0, 0))
    in_specs = (specs(fwd, 0) + specs(bwd, 1)
                + [colspec(0), colspec(0), colspec(1), colspec(1), rowspec, rowspec])
    out_specs = [pl.BlockSpec((1, L, GROUP_X), lambda b, g, c: (b, c, g)),
                 pl.BlockSpec((1, L, GROUP_X), lambda b, g, c: (b, nc - 1 - c, g))]
    return pl.pallas_call(
        _ssd_kernel,
        grid=(bsz, G, nc),
        in_specs=in_specs,
        out_specs=out_specs,
        out_shape=[jax.ShapeDtypeStruct((bsz, s, SSD_INNER), F32)] * 2,
        scratch_shapes=[pltpu.VMEM((SSD_STATE, GROUP_X), F32)] * 2,
        compiler_params=_cparams(("arbitrary", "arbitrary", "arbitrary")),
        name="ssd",
    )(xbc3, xbc3, xbc3, dtT, dtc, xbc3, xbc3, xbc3, dtT, dtc,
      bias_col, a_col, bias_col, a_col, bias_row, a_row)


def _dft_tables(seq):
    n2n = LANES
    n1n = seq // n2n
    n1 = np.arange(n1n)
    k1 = np.arange(n1n)
    n2 = np.arange(n2n)
    ang = -2.0 * np.pi * (n2[:, None, None] * k1[None, :, None] / seq
                          + n1[None, None, :] * k1[None, :, None] / n1n)
    f1 = np.concatenate([np.cos(ang), np.sin(ang)], axis=1)
    k2 = np.arange(n2n)
    a2 = 2.0 * np.pi * np.outer(k2, n2) / n2n
    c2, s2 = np.cos(a2), np.sin(a2)
    g = np.block([[c2, s2], [-s2, c2]])
    ch = np.arange(FOURIER_GROUP_DIM)
    ac = 2.0 * np.pi * np.outer(ch, ch) / FOURIER_GROUP_DIM
    scale = 1.0 / math.sqrt(seq * FOURIER_GROUP_DIM)
    fc = np.concatenate([np.cos(ac), np.sin(ac)], axis=0) * scale
    return (jnp.asarray(f1, BF16), jnp.asarray(g, BF16), jnp.asarray(fc, BF16))


def _fourier_kernel(u_ref, f1_ref, g_ref, fc_ref, o_ref, a_ref):
    n2n = LANES
    n1n = u_ref.shape[1] // n2n

    def stage1(n2, carry):
        xs = u_ref[0, pl.ds(n2, n1n, stride=n2n), :].astype(BF16)
        a_ref[pl.ds(n2, 2 * n1n, stride=n2n), :] = _dot(f1_ref[n2], xs)
        return carry

    lax.fori_loop(0, n2n, stage1, 0)
    gm = g_ref[...]
    fc = fc_ref[...]

    def stage2(k1, carry):
        re = a_ref[pl.ds(pl.multiple_of(k1 * n2n, n2n), n2n), :]
        im = a_ref[pl.ds(pl.multiple_of((n1n + k1) * n2n, n2n), n2n), :]
        a = jnp.concatenate([re, im], axis=0).astype(BF16)
        z = _dot(gm, a)
        zz = jnp.concatenate([z[:n2n], z[n2n:]], axis=1).astype(BF16)
        o_ref[0, pl.ds(k1, n2n, stride=n1n), :] = _dot(zz, fc)
        return carry

    lax.fori_loop(0, n1n, stage2, 0)


def _fourier(uf3):
    bsz, s, w = uf3.shape
    f1, g, fc = _dft_tables(s)
    n1n = s // LANES
    return pl.pallas_call(
        _fourier_kernel,
        grid=(bsz, w // FOURIER_GROUP_DIM),
        in_specs=[pl.BlockSpec((1, s, LANES), lambda b, j: (b, 0, j)),
                  pl.BlockSpec(f1.shape, lambda b, j: (0, 0, 0)),
                  pl.BlockSpec(g.shape, lambda b, j: (0, 0)),
                  pl.BlockSpec(fc.shape, lambda b, j: (0, 0))],
        out_specs=pl.BlockSpec((1, s, LANES), lambda b, j: (b, 0, j)),
        out_shape=jax.ShapeDtypeStruct((bsz, s, w), F32),
        scratch_shapes=[pltpu.VMEM((2 * n1n * LANES, LANES), F32)],
        compiler_params=_cparams(("arbitrary", "arbitrary")),
        name="fourier",
    )(uf3, f1, g, fc)


def _mix_kernel(yf_ref, yb_ref, xs_ref, z_ref, fm_ref, gates_ref, x_ref, mod_ref, dskip_ref,
                ssdn_ref, wbs_ref, wbf_ref, wout_ref, gpost_ref, o_ref):
    m = mod_ref[0]
    y = yf_ref[...] + yb_ref[...] + dskip_ref[...] * xs_ref[...]
    v = y * _silu(z_ref[...])
    parts = []
    for g in range(SSD_GROUPS):
        vg = v[:, g * GROUP_X:(g + 1) * GROUP_X]
        parts.append(vg * lax.rsqrt(jnp.mean(vg * vg, axis=-1, keepdims=True) + RMS_EPS))
    vn = jnp.concatenate(parts, axis=1) * ssdn_ref[...]
    y_ssd = _dot(vn.astype(BF16), wbs_ref[...])
    y_fou = _dot(fm_ref[...].astype(BF16), wbf_ref[...])
    gt = _sigmoid(gates_ref[...])
    d = y_ssd.shape[1]
    mixed = gt[:, :d] * y_ssd + gt[:, d:] * y_fou
    mo = _dot(mixed.astype(BF16), wout_ref[...])
    o_ref[...] = x_ref[...] + m[2:3] * _rms(mo, gpost_ref[...])


def _mix(yf, yb, xbc, z, fm, gates, x2, mod, dskip, ssdn, wbs, wbf, wout, gpost, seq):
    t, d = x2.shape
    tm = TOKEN_TILE
    per_b = seq // tm
    full = lambda a: pl.BlockSpec(a.shape, lambda i: (0,) * a.ndim)
    row = lambda n: pl.BlockSpec((tm, n), lambda i: (i, 0))
    return pl.pallas_call(
        _mix_kernel,
        grid=(t // tm,),
        in_specs=[row(SSD_INNER), row(SSD_INNER), row(SSD_INNER), row(SSD_INNER),
                  row(FOURIER_WIDTH), row(2 * d), row(d),
                  pl.BlockSpec((1, 6, d), lambda i: (i // per_b, 0, 0)),
                  full(dskip), full(ssdn), full(wbs), full(wbf), full(wout), full(gpost)],
        out_specs=row(d),
        out_shape=jax.ShapeDtypeStruct((t, d), F32),
        compiler_params=_cparams(("arbitrary",)),
        name="mix",
    )(yf, yb, xbc, z, fm, gates, x2, mod, dskip, ssdn, wbs, wbf, wout, gpost)


def _route_kernel(x_ref, mod_ref, gpre_ref, wr_ref, rb_ref, hp_ref, idx_ref, w_ref, rank_ref,
                  counts_ref, run_ref):
    @pl.when(pl.program_id(0) == 0)
    def _():
        run_ref[...] = jnp.zeros_like(run_ref)

    m = mod_ref[0]
    h = _rms(x_ref[...], gpre_ref[...]) * (1.0 + m[4:5]) + m[3:4]
    tm, d = h.shape
    half = d // 2
    bits = pltpu.bitcast(h.astype(BF16).astype(F32), jnp.uint32)
    hp_ref[...] = (bits[:, :half] >> 16) | (bits[:, half:] & jnp.uint32(0xFFFF0000))

    scores = _sigmoid(_dot(h, wr_ref[...], HIGHEST))
    biased = scores + rb_ref[...]
    ne = scores.shape[1]
    lane = lax.broadcasted_iota(jnp.int32, (tm, ne), 1)
    grp = lane // (ne // N_EXPERT_GROUPS)
    neg = -jnp.inf
    big = jnp.int32(ne)

    def argmax_first(v):
        mx = jnp.max(v, axis=-1, keepdims=True)
        ix = jnp.min(jnp.where(v == mx, lane, big), axis=-1, keepdims=True)
        return mx, ix

    gs = []
    for g in range(N_EXPERT_GROUPS):
        vg = jnp.where(grp == g, biased, neg)
        m1, i1 = argmax_first(vg)
        m2 = jnp.max(jnp.where(lane == i1, neg, vg), axis=-1, keepdims=True)
        gs.append(m1 + m2)
    allowed = jnp.zeros((tm, ne), jnp.bool_)
    for g in range(N_EXPERT_GROUPS):
        ahead = jnp.zeros((tm, 1), jnp.int32)
        for o in range(N_EXPERT_GROUPS):
            if o == g:
                continue
            beats = (gs[o] > gs[g]) | ((gs[o] == gs[g]) & (o < g))
            ahead = ahead + beats.astype(jnp.int32)
        allowed = allowed | ((grp == g) & (ahead < TOPK_GROUPS))
    masked = jnp.where(allowed, biased, neg)

    out_lane = lax.broadcasted_iota(jnp.int32, (tm, LANES), 1)
    idx_out = jnp.zeros((tm, LANES), jnp.int32)
    w_out = jnp.zeros((tm, LANES), F32)
    w_sum = jnp.zeros((tm, 1), F32)
    hits = []
    for k in range(TOP_K):
        _, ik = argmax_first(masked)
        hit = lane == ik
        hits.append(hit)
        wk = jnp.sum(jnp.where(hit, scores, 0.0), axis=-1, keepdims=True)
        masked = jnp.where(hit, neg, masked)
        idx_out = jnp.where(out_lane == k, ik, idx_out)
        w_out = jnp.where(out_lane == k, wk, w_out)
        w_sum = w_sum + wk
    idx_ref[...] = idx_out
    w_ref[...] = w_out / w_sum * ROUTED_SCALE

    chosen = functools.reduce(jnp.logical_or, hits).astype(F32)
    ti = lax.broadcasted_iota(jnp.int32, (tm, tm), 0)
    tj = lax.broadcasted_iota(jnp.int32, (tm, tm), 1)
    before = _dot((tj < ti).astype(BF16), chosen.astype(BF16)) + run_ref[...]
    rank_out = jnp.zeros((tm, LANES), jnp.int32)
    for k in range(TOP_K):
        rk = jnp.sum(jnp.where(hits[k], before, 0.0), axis=-1, keepdims=True)
        rank_out = jnp.where(out_lane == k, rk.astype(jnp.int32), rank_out)
    rank_ref[...] = rank_out
    run_ref[...] = run_ref[...] + jnp.sum(chosen, axis=0, keepdims=True)
    counts_ref[...] = run_ref[...]


def _route(x1, mod, gpre, w_router, router_bias, seq):
    t, d = x1.shape
    tm = TOKEN_TILE
    per_b = seq // tm
    full = lambda a: pl.BlockSpec(a.shape, lambda i: (0,) * a.ndim)
    row = lambda n: pl.BlockSpec((tm, n), lambda i: (i, 0))
    rb = router_bias.reshape(1, -1)
    return pl.pallas_call(
        _route_kernel,
        grid=(t // tm,),
        in_specs=[row(d), pl.BlockSpec((1, 6, d), lambda i: (i // per_b, 0, 0)), full(gpre),
                  full(w_router), full(rb)],
        out_specs=[row(d // 2), row(LANES), row(LANES), row(LANES),
                   pl.BlockSpec((1, N_EXPERTS), lambda i: (0, 0))],
        out_shape=[jax.ShapeDtypeStruct((t, d // 2), jnp.uint32),
                   jax.ShapeDtypeStruct((t, LANES), jnp.int32),
                   jax.ShapeDtypeStruct((t, LANES), F32),
                   jax.ShapeDtypeStruct((t, LANES), jnp.int32),
                   jax.ShapeDtypeStruct((1, N_EXPERTS), F32)],
        scratch_shapes=[pltpu.VMEM((1, N_EXPERTS), F32)],
        compiler_params=_cparams(("arbitrary",)),
        name="route",
    )(x1, mod, gpre, w_router, rb)


GATHER_UNROLL = 8


def _gather_rows(idx_ref, n_rows, src_hbm, dst_of, sem):
    def body(q, carry):
        for u in range(GATHER_UNROLL):
            src = idx_ref[0, 0, q * GATHER_UNROLL + u]
            pltpu.make_async_copy(src_hbm.at[pl.ds(src, 1), :], dst_of(q, u), sem).start()
        return carry

    lax.fori_loop(0, n_rows // GATHER_UNROLL, body, 0)


def _dest_kernel(idx_ref, rank_ref, start_ref, o_ref):
    idx = idx_ref[...]
    rank = rank_ref[...]
    start = start_ref[...]
    tm = idx.shape[0]
    lane = lax.broadcasted_iota(jnp.int32, (tm, start.shape[1]), 1)
    out_lane = lax.broadcasted_iota(jnp.int32, (tm, LANES), 1)
    out = jnp.zeros((tm, LANES), jnp.int32)
    for k in range(TOP_K):
        base = jnp.sum(jnp.where(lane == idx[:, k:k + 1], start, 0.0), axis=-1, keepdims=True)
        out = jnp.where(out_lane == k, base.astype(jnp.int32) + rank[:, k:k + 1], out)
    o_ref[...] = out


def _dest(idx_pad, rank_pad, pad_start):
    t = idx_pad.shape[0]
    tm = TOKEN_TILE
    row = pl.BlockSpec((tm, LANES), lambda i: (i, 0))
    return pl.pallas_call(
        _dest_kernel,
        grid=(t // tm,),
        in_specs=[row, row, pl.BlockSpec((1, N_EXPERTS), lambda i: (0, 0))],
        out_specs=row,
        out_shape=jax.ShapeDtypeStruct((t, LANES), jnp.int32),
        compiler_params=_cparams(("arbitrary",)),
        name="dest",
    )(idx_pad, rank_pad, pad_start.astype(F32).reshape(1, N_EXPERTS))


def _scatter_kernel(dest_ref, hp_ref, init_hbm, xs_hbm, sem):
    del init_hbm
    ts = hp_ref.shape[0]

    def body(t, carry):
        for k in range(TOP_K):
            dst = dest_ref[0, 0, t * TOP_K + k]
            pltpu.make_async_copy(hp_ref.at[pl.ds(t, 1), :], xs_hbm.at[pl.ds(dst, 1), :], sem).start()
        return carry

    lax.fori_loop(0, ts, body, 0)
    for k in range(TOP_K):
        pltpu.make_async_copy(hp_ref, xs_hbm.at[pl.ds(0, ts), :], sem).wait()


def _scatter(dest, hp, n_rows):
    t, half = hp.shape
    ts = SCATTER_TILE
    n_steps = t // ts
    dest3 = dest.reshape(n_steps, 1, ts * TOP_K)
    return pl.pallas_call(
        _scatter_kernel,
        grid=(n_steps,),
        in_specs=[pl.BlockSpec((1, 1, ts * TOP_K), lambda i: (i, 0, 0), memory_space=pltpu.SMEM),
                  pl.BlockSpec((ts, half), lambda i: (i, 0)),
                  pl.BlockSpec(memory_space=pl.ANY)],
        out_specs=pl.BlockSpec(memory_space=pl.ANY),
        out_shape=jax.ShapeDtypeStruct((n_rows, half), jnp.uint32),
        scratch_shapes=[pltpu.SemaphoreType.DMA(())],
        input_output_aliases={2: 0},
        compiler_params=_cparams(("arbitrary",)),
        name="scatter",
    )(dest3, hp, jnp.zeros((n_rows, half), jnp.uint32))


def _expert_kernel(be_ref, nu_ref, x_ref, w13_ref, w2_ref, o_ref, w13b, w2b):
    i = pl.program_id(0)
    n_used = nu_ref[0]

    @pl.when((i == 0) | (be_ref[i] != be_ref[jnp.maximum(i - 1, 0)]))
    def _():
        w13b[...] = w13_ref[0].astype(BF16)
        w2b[...] = w2_ref[0].astype(BF16)

    @pl.when(i < n_used)
    def _():
        w = x_ref[...]
        lo = pltpu.bitcast(w << 16, F32).astype(BF16)
        hi = pltpu.bitcast(w & jnp.uint32(0xFFFF0000), F32).astype(BF16)
        half = lo.shape[1]
        ag = _dot(lo, w13b[:half, :]) + _dot(hi, w13b[half:, :])
        hh = ag.shape[1] // 2
        act = (_silu(ag[:, :hh]) * ag[:, hh:]).astype(BF16)
        o_ref[...] = _dot(act, w2b[...])

    @pl.when(i >= n_used)
    def _():
        o_ref[...] = jnp.zeros_like(o_ref)


def _experts(xs, block_e, n_used, w13, w2):
    n_rows, half = xs.shape
    ne, d, h2 = w13.shape
    bm = EXPERT_ROWS
    n_blocks = n_rows // bm
    grid_spec = pltpu.PrefetchScalarGridSpec(
        num_scalar_prefetch=2,
        grid=(n_blocks,),
        in_specs=[
            pl.BlockSpec((bm, half), lambda i, be, nu: (jnp.minimum(i, nu[0] - 1), 0)),
            pl.BlockSpec((1, d, h2), lambda i, be, nu: (be[i], 0, 0)),
            pl.BlockSpec((1, h2 // 2, d), lambda i, be, nu: (be[i], 0, 0)),
        ],
        out_specs=pl.BlockSpec((bm, d), lambda i, be, nu: (i, 0)),
        scratch_shapes=[pltpu.VMEM((d, h2), BF16),
                        pltpu.VMEM((h2 // 2, d), BF16)],
    )
    return pl.pallas_call(
        _expert_kernel,
        grid_spec=grid_spec,
        out_shape=jax.ShapeDtypeStruct((n_rows, d), F32),
        compiler_params=_cparams(("arbitrary",)),
        name="experts",
    )(block_e, n_used, xs, w13, w2)


def _final_kernel(dc_ref, dn_ref, y_hbm, w_ref, x_ref, mod_ref, gpre_ref, gpost_ref, w13s_ref,
                  w2s_ref, o_ref, buf, sem):
    i = pl.program_id(0)
    slot = i % 2
    tm = x_ref.shape[0]

    def issue(d_ref, s):
        _gather_rows(d_ref, tm * TOP_K, y_hbm,
                     lambda q, u: buf.at[s, u, pl.ds(q, 1), :], sem.at[s])

    @pl.when(i == 0)
    def _():
        issue(dc_ref, 0)

    @pl.when(i + 1 < pl.num_programs(0))
    def _():
        issue(dn_ref, 1 - slot)

    for k in range(TOP_K):
        pltpu.make_async_copy(y_hbm.at[pl.ds(0, tm), :], buf.at[slot, k], sem.at[slot]).wait()

    w = w_ref[...]
    routed = buf[slot, 0] * w[:, 0:1]
    for k in range(1, TOP_K):
        routed = routed + buf[slot, k] * w[:, k:k + 1]

    m = mod_ref[0]
    x1 = x_ref[...]
    h = (_rms(x1, gpre_ref[...]) * (1.0 + m[4:5]) + m[3:4]).astype(BF16)
    ag = _dot(h, w13s_ref[...])
    hh = ag.shape[1] // 2
    act = (_silu(ag[:, :hh]) * ag[:, hh:]).astype(BF16)
    ffn = routed + _dot(act, w2s_ref[...])
    o_ref[...] = x1 + m[5:6] * _rms(ffn, gpost_ref[...])


def _final(dest, y_sorted, top_w, x1, mod, gpre, gpost, w13s, w2s, seq):
    t, d = x1.shape
    tm = COMBINE_TILE
    per_b = seq // tm
    n_steps = t // tm
    dest3 = dest.reshape(n_steps, 1, tm * TOP_K)
    full = lambda a: pl.BlockSpec(a.shape, lambda i: (0,) * a.ndim)
    row = lambda n: pl.BlockSpec((tm, n), lambda i: (i, 0))
    return pl.pallas_call(
        _final_kernel,
        grid=(n_steps,),
        in_specs=[
            pl.BlockSpec((1, 1, tm * TOP_K), lambda i: (i, 0, 0), memory_space=pltpu.SMEM),
            pl.BlockSpec((1, 1, tm * TOP_K), lambda i: (jnp.minimum(i + 1, n_steps - 1), 0, 0),
                         memory_space=pltpu.SMEM),
            pl.BlockSpec(memory_space=pl.ANY),
            row(LANES), row(d), pl.BlockSpec((1, 6, d), lambda i: (i // per_b, 0, 0)),
            full(gpre), full(gpost), full(w13s), full(w2s)],
        out_specs=row(d),
        out_shape=jax.ShapeDtypeStruct((t, d), F32),
        scratch_shapes=[pltpu.VMEM((2, TOP_K, tm, d), F32), pltpu.SemaphoreType.DMA((2,))],
        compiler_params=_cparams(("arbitrary",)),
        name="combine_final",
    )(dest3, dest3, y_sorted, top_w, x1, mod, gpre, gpost, w13s, w2s)


def _pad_heads(v):
    lead = v.shape[:-1]
    v = v.reshape(lead + (2, SSD_GROUPS, HEADS_PER_GROUP))
    v = jnp.pad(v, [(0, 0)] * (len(lead) + 2) + [(0, HEADS_PADDED - HEADS_PER_GROUP)])
    return v.reshape(lead + (2 * SSD_GROUPS * HEADS_PADDED,))


def _dispatch_plan(counts, n_tokens):
    bm = EXPERT_ROWS
    counts = counts.reshape(N_EXPERTS).astype(jnp.int32)
    padded = (counts + bm - 1) // bm * bm
    pad_end = jnp.cumsum(padded)
    pad_start = pad_end - padded
    n_rows = -(-n_tokens * TOP_K // bm) * bm + N_EXPERTS * bm
    n_blocks = n_rows // bm
    block_start = jnp.arange(n_blocks, dtype=jnp.int32) * bm
    block_e = jnp.minimum(jnp.sum(block_start[:, None] >= pad_end[None, :], axis=1),
                          N_EXPERTS - 1).astype(jnp.int32)
    n_used = (pad_end[-1] // bm).astype(jnp.int32).reshape(1)
    return pad_start, block_e, n_used, n_rows


def _layer(x, c, w_ada, b_ada, pre_norm_mix, post_norm_mix, pre_norm_ffn, post_norm_ffn, w_in,
           conv_w, conv_b, dt_bias_fwd, dt_bias_bwd, a_log_fwd, a_log_bwd, d_skip, ssd_norm,
           w_branch_ssd, w_branch_fourier, w_out, w_router, router_bias, w13_experts, w2_experts,
           w13_shared, w2_shared):
    bsz, seq, d = x.shape
    t = bsz * seq
    x2 = x.reshape(t, d)
    row = lambda v: v.reshape(1, -1).astype(F32)

    mod = _ada(c, w_ada, b_ada)

    i1 = SSD_INNER
    i2 = i1 + XBC_WIDTH
    i3 = i2 + 2 * SSD_HEADS
    i4 = i3 + FOURIER_WIDTH
    n_dt = 2 * SSD_GROUPS * HEADS_PADDED
    w_dtp = _pad_heads(w_in[:, i2:i3])
    wdtc = jnp.pad(w_dtp, ((0, 0), (0, LANES - n_dt))).astype(BF16)
    wdtT = w_dtp.T.astype(BF16)
    z, xbc, dtc, dtT, uf, gates = _inproj(
        x2, mod, row(pre_norm_mix), w_in[:, :i1].astype(BF16), w_in[:, i1:i2].astype(BF16),
        wdtc, wdtT, w_in[:, i3:i4].astype(BF16), w_in[:, i4:].astype(BF16), seq)

    xbc3 = _conv(xbc.reshape(bsz, seq, XBC_WIDTH), conv_w, conv_b)

    bias_p = _pad_heads(jnp.concatenate([dt_bias_fwd, dt_bias_bwd]).astype(F32))
    a_p = _pad_heads(-jnp.exp(jnp.concatenate([a_log_fwd, a_log_bwd]).astype(F32)))
    pad_row = lambda v: jnp.pad(v, (0, LANES - n_dt)).reshape(1, LANES)
    yf, yb = _ssd(xbc3, dtT, dtc, bias_p.reshape(n_dt, 1), a_p.reshape(n_dt, 1),
                  pad_row(bias_p), pad_row(a_p))

    fm = _fourier(uf.reshape(bsz, seq, FOURIER_WIDTH))

    x1 = _mix(yf.reshape(t, SSD_INNER), yb.reshape(t, SSD_INNER), xbc3.reshape(t, XBC_WIDTH), z,
              fm.reshape(t, FOURIER_WIDTH), gates, x2, mod,
              row(jnp.repeat(d_skip, SSD_HEAD_DIM)), row(ssd_norm), w_branch_ssd.astype(BF16),
              w_branch_fourier.astype(BF16), w_out.astype(BF16), row(post_norm_mix), seq)

    hp, idx_pad, w_pad, rank_pad, counts = _route(x1, mod, row(pre_norm_ffn),
                                                  w_router.astype(F32), router_bias, seq)
    pad_start, block_e, n_used, n_rows = _dispatch_plan(counts, t)
    dest = _dest(idx_pad, rank_pad, pad_start)[:, :TOP_K]
    xs = _scatter(dest, hp, n_rows)
    y_sorted = _experts(xs, block_e, n_used, w13_experts, w2_experts)
    out = _final(dest, y_sorted, w_pad, x1, mod, row(pre_norm_ffn), row(post_norm_ffn),
                 w13_shared.astype(BF16), w2_shared.astype(BF16), seq)
    return out.reshape(bsz, seq, d)


def kernel(x, c, w_ada, b_ada, pre_norm_mix, post_norm_mix, pre_norm_ffn, post_norm_ffn, w_in,
           conv_w, conv_b, dt_bias_fwd, dt_bias_bwd, a_log_fwd, a_log_bwd, d_skip, ssd_norm,
           w_branch_ssd, w_branch_fourier, w_out, w_router, router_bias, w13_experts, w2_experts,
           w13_shared, w2_shared):
    for layer in range(w_ada.shape[0]):
        x = _layer(x, c, w_ada[layer], b_ada[layer], pre_norm_mix[layer], post_norm_mix[layer],
                   pre_norm_ffn[layer], post_norm_ffn[layer], w_in[layer], conv_w[layer],
                   conv_b[layer], dt_bias_fwd[layer], dt_bias_bwd[layer], a_log_fwd[layer],
                   a_log_bwd[layer], d_skip[layer], ssd_norm[layer], w_branch_ssd[layer],
                   w_branch_fourier[layer], w_out[layer], w_router[layer], router_bias[layer],
                   w13_experts[layer], w2_experts[layer], w13_shared[layer], w2_shared[layer])
    return x
```

```python
import functools
import math

import numpy as np
import jax
import jax.numpy as jnp
from jax import lax
from jax.experimental import pallas as pl
from jax.experimental.pallas import tpu as pltpu

F32 = jnp.float32
BF16 = jnp.bfloat16
HIGHEST = lax.Precision.HIGHEST

D_MODEL = 1024
SSD_HEADS = 24
SSD_HEAD_DIM = 64
SSD_INNER = SSD_HEADS * SSD_HEAD_DIM
SSD_GROUPS = 4
HEADS_PER_GROUP = SSD_HEADS // SSD_GROUPS
HEADS_PADDED = 8
SSD_STATE = 128
SSD_CONV = 5
SSD_CHUNK = 128
XBC_WIDTH = SSD_INNER + 2 * SSD_GROUPS * SSD_STATE
GROUP_X = HEADS_PER_GROUP * SSD_HEAD_DIM
FOURIER_WIDTH = 512
FOURIER_GROUP_DIM = 128
N_EXPERTS = 256
TOP_K = 8
N_EXPERT_GROUPS = 8
TOPK_GROUPS = 4
EXPERT_HIDDEN = 256
SHARED_HIDDEN = 256
ROUTED_SCALE = 2.5
RMS_EPS = 1e-6

LANES = 128
VMEM_LIMIT = 56 * 1024 * 1024
TOKEN_TILE = 256
EXPERT_ROWS = 256
COMBINE_TILE = 128
SCATTER_TILE = 512
CONV_ROWS = 256


def _cparams(sem):
    return pltpu.CompilerParams(dimension_semantics=sem, vmem_limit_bytes=VMEM_LIMIT)


def _dot(a, b, precision=None):
    return jnp.dot(a, b, preferred_element_type=F32, precision=precision)


def _dot_nt(a, b, precision=None):
    return lax.dot_general(a, b, (((1,), (1,)), ((), ())), preferred_element_type=F32,
                           precision=precision)


def _dot_tn(a, b):
    return lax.dot_general(a, b, (((0,), (0,)), ((), ())), preferred_element_type=F32)


def _sigmoid(x):
    return 1.0 / (1.0 + jnp.exp(-x))


def _silu(x):
    return x * _sigmoid(x)


def _softplus(x):
    return jnp.maximum(x, 0.0) + jnp.log1p(jnp.exp(-jnp.abs(x)))


def _rms(x, g):
    return x * lax.rsqrt(jnp.mean(x * x, axis=-1, keepdims=True) + RMS_EPS) * g


def _ada_kernel(c_ref, w_ref, b_ref, o_ref):
    o_ref[...] = _dot(_silu(c_ref[...]), w_ref[...], HIGHEST) + b_ref[...]


def _ada(c, w_ada, b_ada):
    bsz, d = c.shape
    rows = 8
    cp = jnp.zeros((rows, d), F32).at[:bsz].set(c)
    n = w_ada.shape[1]
    tn = 1536
    out = pl.pallas_call(
        _ada_kernel,
        grid=(n // tn,),
        in_specs=[pl.BlockSpec((rows, d), lambda j: (0, 0)),
                  pl.BlockSpec((d, tn), lambda j: (0, j)),
                  pl.BlockSpec((1, tn), lambda j: (0, j))],
        out_specs=pl.BlockSpec((rows, tn), lambda j: (0, j)),
        out_shape=jax.ShapeDtypeStruct((rows, n), F32),
        compiler_params=_cparams(("arbitrary",)),
        name="adaln",
    )(cp, w_ada, b_ada.reshape(1, n))
    return out[:bsz].reshape(bsz, 6, d)


def _split3(a):
    a1 = a.astype(BF16)
    r1 = a - a1.astype(F32)
    a2 = r1.astype(BF16)
    a3 = (r1 - a2.astype(F32)).astype(BF16)
    return a1, a2, a3


def _inproj_kernel(x_ref, mod_ref, g_ref, wz_ref, wxbc_ref, wdtc_ref, wdtT_ref, wuf_ref, wg_ref,
                   bias_row_ref, a_row_ref, bias_col_ref, a_col_ref,
                   z_ref, xbc_ref, acsc_ref, dtT_ref, acsT_ref, uf_ref, gates_ref):
    m = mod_ref[0]
    h = _rms(x_ref[...], g_ref[...]) * (1.0 + m[1:2]) + m[0:1]
    hb = h.astype(BF16)
    z_ref[...] = _dot(hb, wz_ref[...])
    xbc_ref[...] = _dot(hb, wxbc_ref[...])
    uf_ref[...] = _dot(hb, wuf_ref[...])
    gates_ref[...] = _dot(hb, wg_ref[...])

    tm = hb.shape[0]
    n_fwd = SSD_GROUPS * HEADS_PADDED
    ii = lax.broadcasted_iota(jnp.int32, (tm, tm), 0)
    jj = lax.broadcasted_iota(jnp.int32, (tm, tm), 1)
    same = (ii // SSD_CHUNK) == (jj // SSD_CHUNK)
    tri_f = (same & (jj <= ii)).astype(BF16)
    tri_b = (same & (jj >= ii)).astype(BF16)

    dt_c = _softplus(_dot(hb, wdtc_ref[...]) + bias_row_ref[...])
    pieces = _split3(dt_c * a_row_ref[...])
    acs_f = sum(_dot(tri_f, p) for p in pieces)
    acs_b = sum(_dot(tri_b, p) for p in pieces)
    lane = lax.broadcasted_iota(jnp.int32, acs_f.shape, 1)
    acsc_ref[...] = jnp.where(lane < n_fwd, acs_f, acs_b)

    dt_t = _softplus(_dot_nt(wdtT_ref[...], hb) + bias_col_ref[...])
    pieces = _split3(dt_t * a_col_ref[...])
    acs_f = sum(_dot_nt(p, tri_f) for p in pieces)
    acs_b = sum(_dot_nt(p, tri_b) for p in pieces)
    sub = lax.broadcasted_iota(jnp.int32, acs_f.shape, 0)
    dtT_ref[...] = dt_t
    acsT_ref[...] = jnp.where(sub < n_fwd, acs_f, acs_b)


def _inproj(x2, mod, g, wz, wxbc, wdtc, wdtT, wuf, wg, bias_row, a_row, bias_col, a_col, seq):
    t, d = x2.shape
    tm = TOKEN_TILE
    per_b = seq // tm
    full = lambda a: pl.BlockSpec(a.shape, lambda i: (0,) * a.ndim)
    row = lambda n: pl.BlockSpec((tm, n), lambda i: (i, 0))
    nd = wdtT.shape[0]
    colspec = pl.BlockSpec((nd, tm), lambda i: (0, i))
    return pl.pallas_call(
        _inproj_kernel,
        grid=(t // tm,),
        in_specs=[row(d), pl.BlockSpec((1, 6, d), lambda i: (i // per_b, 0, 0)), full(g),
                  full(wz), full(wxbc), full(wdtc), full(wdtT), full(wuf), full(wg),
                  full(bias_row), full(a_row), full(bias_col), full(a_col)],
        out_specs=[row(wz.shape[1]), row(wxbc.shape[1]), row(LANES), colspec, colspec,
                   row(wuf.shape[1]), row(wg.shape[1])],
        out_shape=[jax.ShapeDtypeStruct((t, wz.shape[1]), F32),
                   jax.ShapeDtypeStruct((t, wxbc.shape[1]), F32),
                   jax.ShapeDtypeStruct((t, LANES), F32),
                   jax.ShapeDtypeStruct((nd, t), F32),
                   jax.ShapeDtypeStruct((nd, t), F32),
                   jax.ShapeDtypeStruct((t, wuf.shape[1]), F32),
                   jax.ShapeDtypeStruct((t, wg.shape[1]), F32)],
        compiler_params=_cparams(("arbitrary",)),
        name="inproj",
    )(x2, mod, g, wz, wxbc, wdtc, wdtT, wuf, wg, bias_row, a_row, bias_col, a_col)


def _conv_kernel(u_ref, w_ref, b_ref, o_ref, pad_ref):
    s = u_ref.shape[1]
    halo = 8
    pad_ref[0:halo, :] = jnp.zeros((halo, LANES), F32)
    pad_ref[halo + s:2 * halo + s, :] = jnp.zeros((halo, LANES), F32)
    pad_ref[halo:halo + s, :] = u_ref[0]
    w = w_ref[...]
    b = b_ref[...]
    half = (SSD_CONV - 1) // 2
    for r in range(s // CONV_ROWS):
        base = r * CONV_ROWS
        acc = b
        for k in range(SSD_CONV):
            lo = base + halo + k - half
            acc = acc + w[k:k + 1, :] * pad_ref[lo:lo + CONV_ROWS, :]
        o_ref[0, base:base + CONV_ROWS, :] = _silu(acc)


def _conv(xbc3, conv_w, conv_b):
    bsz, s, c = xbc3.shape
    return pl.pallas_call(
        _conv_kernel,
        grid=(bsz, c // LANES),
        in_specs=[pl.BlockSpec((1, s, LANES), lambda b, j: (b, 0, j)),
                  pl.BlockSpec((SSD_CONV, LANES), lambda b, j: (0, j)),
                  pl.BlockSpec((1, LANES), lambda b, j: (0, j))],
        out_specs=pl.BlockSpec((1, s, LANES), lambda b, j: (b, 0, j)),
        out_shape=jax.ShapeDtypeStruct((bsz, s, c), F32),
        scratch_shapes=[pltpu.VMEM((s + 16, LANES), F32)],
        compiler_params=_cparams(("arbitrary", "arbitrary")),
        name="conv",
    )(xbc3, conv_w, conv_b.reshape(1, c))


HEAD_PAIRS = HEADS_PER_GROUP // 2


def _ssd_direction(x, bm, cm, dt_r, acs_r, acsc_all, lane_off, s_ref, reverse):
    L, N = bm.shape
    assert L == N == LANES
    ii = lax.broadcasted_iota(jnp.int32, (L, L), 0)
    jj = lax.broadcasted_iota(jnp.int32, (L, L), 1)
    mask = (jj >= ii) if reverse else (jj <= ii)
    lo_half = jj < SSD_HEAD_DIM
    shift = jnp.where(lane_off == 0, 0, LANES - lane_off)
    acs_c = pltpu.roll(acsc_all, shift, 1)
    last = 0 if reverse else L - 1
    tot_r = acs_r[:, last:last + 1]
    w_r = jnp.exp(tot_r - acs_r) * dt_r
    etot = jnp.broadcast_to(jnp.exp(tot_r), (HEADS_PADDED, LANES))

    cbt = _dot_nt(cm.astype(BF16), bm.astype(BF16))
    bt = bm.T
    ys = []
    for q in range(HEAD_PAIRS):
        xq = x[:, q * LANES:(q + 1) * LANES]
        sq = s_ref[q]
        x_a = jnp.where(lo_half, xq, 0.0).astype(BF16)
        x_b = jnp.where(lo_half, 0.0, xq).astype(BF16)
        s_a = jnp.where(lo_half, sq, 0.0).astype(BF16)
        s_b = jnp.where(lo_half, 0.0, sq).astype(BF16)
        m_parts, c_parts, b_parts = [], [], []
        for h in (2 * q, 2 * q + 1):
            col = jnp.broadcast_to(acs_c[:, h:h + 1], (L, L))
            decay = jnp.exp(jnp.where(mask, col - acs_r[h:h + 1, :], -jnp.inf))
            m_parts.append((cbt * decay * dt_r[h:h + 1, :]).astype(BF16))
            c_parts.append((cm * jnp.exp(col)).astype(BF16))
            b_parts.append((bt * w_r[h:h + 1, :]).astype(BF16))
        x_diag = jnp.concatenate([x_a, x_b], axis=0)
        lhs = jnp.concatenate(m_parts + c_parts, axis=1)
        rhs = jnp.concatenate([x_diag, s_a, s_b], axis=0)
        ys.append(_dot(lhs, rhs))
        dec = jnp.where(lo_half[0:1], etot[2 * q:2 * q + 1], etot[2 * q + 1:2 * q + 2])
        s_ref[q] = sq * dec + _dot(jnp.concatenate(b_parts, axis=1), x_diag)
    return jnp.concatenate(ys, axis=1)


def _ssd_kernel(xf_ref, bf_ref, cf_ref, dtTf_ref, acsTf_ref, acscf_ref,
                xb_ref, bb_ref, cb_ref, dtTb_ref, acsTb_ref, acscb_ref,
                yf_ref, yb_ref, sf_ref, sb_ref):
    g = pl.program_id(1)

    @pl.when(pl.program_id(2) == 0)
    def _():
        sf_ref[...] = jnp.zeros_like(sf_ref)
        sb_ref[...] = jnp.zeros_like(sb_ref)

    yf_ref[0] = _ssd_direction(xf_ref[0], bf_ref[0], cf_ref[0], dtTf_ref[...], acsTf_ref[...],
                               acscf_ref[...], g * HEADS_PADDED, sf_ref, False)
    yb_ref[0] = _ssd_direction(xb_ref[0], bb_ref[0], cb_ref[0], dtTb_ref[...], acsTb_ref[...],
                               acscb_ref[...], (SSD_GROUPS + g) * HEADS_PADDED, sb_ref, True)


def _ssd(xbc3, dtT, acsT, acsc):
    bsz, s, _ = xbc3.shape
    L = SSD_CHUNK
    nc = s // L
    G = SSD_GROUPS
    nb = SSD_INNER // SSD_STATE
    ncb = nb + G
    fwd = lambda c: c
    bwd = lambda c: nc - 1 - c

    def specs(cidx, dirn):
        rowspec = pl.BlockSpec((HEADS_PADDED, L), lambda b, g, c: (dirn * G + g, b * nc + cidx(c)))
        return [
            pl.BlockSpec((1, L, GROUP_X), lambda b, g, c: (b, cidx(c), g)),
            pl.BlockSpec((1, L, SSD_STATE), lambda b, g, c: (b, cidx(c), nb + g)),
            pl.BlockSpec((1, L, SSD_STATE), lambda b, g, c: (b, cidx(c), ncb + g)),
            rowspec, rowspec,
            pl.BlockSpec((L, LANES), lambda b, g, c: (b * nc + cidx(c), 0)),
        ]

    out_specs = [pl.BlockSpec((1, L, GROUP_X), lambda b, g, c: (b, c, g)),
                 pl.BlockSpec((1, L, GROUP_X), lambda b, g, c: (b, nc - 1 - c, g))]
    return pl.pallas_call(
        _ssd_kernel,
        grid=(bsz, G, nc),
        in_specs=specs(fwd, 0) + specs(bwd, 1),
        out_specs=out_specs,
        out_shape=[jax.ShapeDtypeStruct((bsz, s, SSD_INNER), F32)] * 2,
        scratch_shapes=[pltpu.VMEM((HEAD_PAIRS, SSD_STATE, LANES), F32)] * 2,
        compiler_params=_cparams(("arbitrary", "arbitrary", "arbitrary")),
        name="ssd",
    )(xbc3, xbc3, xbc3, dtT, acsT, acsc, xbc3, xbc3, xbc3, dtT, acsT, acsc)


def _dft_tables(seq):
    n2n = LANES
    n1n = seq // n2n
    n1 = np.arange(n1n)
    k1 = np.arange(n1n)
    n2 = np.arange(n2n)
    ang = -2.0 * np.pi * (n2[:, None, None] * k1[None, :, None] / seq
                          + n1[None, None, :] * k1[None, :, None] / n1n)
    f1 = np.concatenate([np.cos(ang), np.sin(ang)], axis=1)
    k2 = np.arange(n2n)
    a2 = 2.0 * np.pi * np.outer(k2, n2) / n2n
    c2, s2 = np.cos(a2), np.sin(a2)
    g = np.block([[c2, s2], [-s2, c2]])
    ch = np.arange(FOURIER_GROUP_DIM)
    ac = 2.0 * np.pi * np.outer(ch, ch) / FOURIER_GROUP_DIM
    scale = 1.0 / math.sqrt(seq * FOURIER_GROUP_DIM)
    fc = np.concatenate([np.cos(ac), np.sin(ac)], axis=0) * scale
    return (jnp.asarray(f1, BF16), jnp.asarray(g, BF16), jnp.asarray(fc, BF16))


DFT_UNROLL = 4


def _dft_pitch(n1n):
    return 2 * n1n + 8


def _fourier_kernel(u_ref, f1_ref, g_ref, fc_ref, o_ref, a_ref):
    n2n = LANES
    n1n = u_ref.shape[2]
    pitch = _dft_pitch(n1n)

    def stage1(i, carry):
        for u in range(DFT_UNROLL):
            n2 = i * DFT_UNROLL + u
            xs = u_ref[0, 0, :, pl.ds(pl.multiple_of(n2 * LANES, LANES), LANES)].astype(BF16)
            a_ref[pl.ds(pl.multiple_of(n2 * pitch, 8), 2 * n1n), :] = _dot(f1_ref[n2], xs)
        return carry

    lax.fori_loop(0, n2n // DFT_UNROLL, stage1, 0)
    gm = g_ref[...]
    fc = fc_ref[...]

    def stage2(i, carry):
        for u in range(DFT_UNROLL):
            k1 = i * DFT_UNROLL + u
            re = a_ref[pl.ds(k1, n2n, stride=pitch), :]
            im = a_ref[pl.ds(n1n + k1, n2n, stride=pitch), :]
            a = jnp.concatenate([re, im], axis=0).astype(BF16)
            z = _dot(gm, a)
            zz = jnp.concatenate([z[:n2n], z[n2n:]], axis=1).astype(BF16)
            o_ref[0, pl.ds(k1, n2n, stride=n1n), :] = _dot(zz, fc)
        return carry

    lax.fori_loop(0, n1n // DFT_UNROLL, stage2, 0)


def _fourier(uf3):
    bsz, s, w = uf3.shape
    f1, g, fc = _dft_tables(s)
    n1n = s // LANES
    ng = w // FOURIER_GROUP_DIM
    u = uf3.reshape(bsz, n1n, LANES, ng, FOURIER_GROUP_DIM).transpose(0, 3, 1, 2, 4)
    u = u.reshape(bsz, ng, n1n, LANES * FOURIER_GROUP_DIM)
    return pl.pallas_call(
        _fourier_kernel,
        grid=(bsz, ng),
        in_specs=[pl.BlockSpec((1, 1, n1n, LANES * FOURIER_GROUP_DIM), lambda b, j: (b, j, 0, 0)),
                  pl.BlockSpec(f1.shape, lambda b, j: (0, 0, 0)),
                  pl.BlockSpec(g.shape, lambda b, j: (0, 0)),
                  pl.BlockSpec(fc.shape, lambda b, j: (0, 0))],
        out_specs=pl.BlockSpec((1, s, LANES), lambda b, j: (b, 0, j)),
        out_shape=jax.ShapeDtypeStruct((bsz, s, w), F32),
        scratch_shapes=[pltpu.VMEM((LANES * _dft_pitch(n1n), LANES), F32)],
        compiler_params=_cparams(("arbitrary", "arbitrary")),
        name="fourier",
    )(u, f1, g, fc)


def _mix_kernel(yf_ref, yb_ref, xs_ref, z_ref, fm_ref, gates_ref, x_ref, mod_ref, dskip_ref,
                ssdn_ref, wbs_ref, wbf_ref, wout_ref, gpost_ref, o_ref):
    m = mod_ref[0]
    y = yf_ref[...] + yb_ref[...] + dskip_ref[...] * xs_ref[...]
    v = y * _silu(z_ref[...])
    parts = []
    for g in range(SSD_GROUPS):
        vg = v[:, g * GROUP_X:(g + 1) * GROUP_X]
        parts.append(vg * lax.rsqrt(jnp.mean(vg * vg, axis=-1, keepdims=True) + RMS_EPS))
    vn = jnp.concatenate(parts, axis=1) * ssdn_ref[...]
    y_ssd = _dot(vn.astype(BF16), wbs_ref[...])
    y_fou = _dot(fm_ref[...].astype(BF16), wbf_ref[...])
    gt = _sigmoid(gates_ref[...])
    d = y_ssd.shape[1]
    mixed = gt[:, :d] * y_ssd + gt[:, d:] * y_fou
    mo = _dot(mixed.astype(BF16), wout_ref[...])
    o_ref[...] = x_ref[...] + m[2:3] * _rms(mo, gpost_ref[...])


def _mix(yf, yb, xbc, z, fm, gates, x2, mod, dskip, ssdn, wbs, wbf, wout, gpost, seq):
    t, d = x2.shape
    tm = TOKEN_TILE
    per_b = seq // tm
    full = lambda a: pl.BlockSpec(a.shape, lambda i: (0,) * a.ndim)
    row = lambda n: pl.BlockSpec((tm, n), lambda i: (i, 0))
    return pl.pallas_call(
        _mix_kernel,
        grid=(t // tm,),
        in_specs=[row(SSD_INNER), row(SSD_INNER), row(SSD_INNER), row(SSD_INNER),
                  row(FOURIER_WIDTH), row(2 * d), row(d),
                  pl.BlockSpec((1, 6, d), lambda i: (i // per_b, 0, 0)),
                  full(dskip), full(ssdn), full(wbs), full(wbf), full(wout), full(gpost)],
        out_specs=row(d),
        out_shape=jax.ShapeDtypeStruct((t, d), F32),
        compiler_params=_cparams(("arbitrary",)),
        name="mix",
    )(yf, yb, xbc, z, fm, gates, x2, mod, dskip, ssdn, wbs, wbf, wout, gpost)


def _route_kernel(x_ref, mod_ref, gpre_ref, wr_ref, rb_ref, hp_ref, idx_ref, w_ref, rank_ref,
                  counts_ref, run_ref):
    @pl.when(pl.program_id(0) == 0)
    def _():
        run_ref[...] = jnp.zeros_like(run_ref)

    m = mod_ref[0]
    h = _rms(x_ref[...], gpre_ref[...]) * (1.0 + m[4:5]) + m[3:4]
    tm, d = h.shape
    half = d // 2
    bits = pltpu.bitcast(h.astype(BF16).astype(F32), jnp.uint32)
    hp_ref[...] = (bits[:, :half] >> 16) | (bits[:, half:] & jnp.uint32(0xFFFF0000))

    h_hi = h.astype(BF16)
    h_lo = (h - h_hi.astype(F32)).astype(BF16)
    w_hi = wr_ref[0]
    logits = _dot(h_hi, w_hi) + (_dot(h_hi, wr_ref[1]) + _dot(h_lo, w_hi))
    scores = _sigmoid(logits)
    biased = scores + rb_ref[...]
    ne = scores.shape[1]
    per_group = ne // N_EXPERT_GROUPS
    lane = lax.broadcasted_iota(jnp.int32, (tm, ne), 1)
    lane_f = lane.astype(F32)
    grp = lane // per_group
    out_lane = lax.broadcasted_iota(jnp.int32, (tm, LANES), 1)
    neg = -jnp.inf

    def argmax_first(v):
        mx = jnp.max(v, axis=-1, keepdims=True)
        ix = jnp.min(jnp.where(v == mx, lane_f, float(ne)), axis=-1, keepdims=True)
        return mx, ix

    gs = []
    gmat = jnp.full((tm, LANES), neg, F32)
    for g in range(N_EXPERT_GROUPS):
        vg = jnp.where(grp == g, biased, neg)
        m1, i1 = argmax_first(vg)
        m2 = jnp.max(jnp.where(lane_f == i1, neg, vg), axis=-1, keepdims=True)
        gs.append(m1 + m2)
        gmat = jnp.where(out_lane == g, gs[g], gmat)
    ahead = jnp.zeros((tm, LANES), F32)
    for o in range(N_EXPERT_GROUPS):
        beats = (gs[o] > gmat) | ((gs[o] == gmat) & (out_lane > o))
        ahead = ahead + beats.astype(F32)
    sel = ((ahead < TOPK_GROUPS) & (out_lane < N_EXPERT_GROUPS)).astype(BF16)
    eg = lax.broadcasted_iota(jnp.int32, (LANES, ne), 0)
    ee = lax.broadcasted_iota(jnp.int32, (LANES, ne), 1)
    allowed = _dot(sel, (eg == ee // per_group).astype(BF16)) > 0.5
    masked = jnp.where(allowed, biased, neg)

    idx_out = jnp.zeros((tm, LANES), F32)
    w_out = jnp.zeros((tm, LANES), F32)
    w_sum = jnp.zeros((tm, 1), F32)
    hits = []
    for k in range(TOP_K):
        _, ik = argmax_first(masked)
        hit = lane_f == ik
        hits.append(hit)
        wk = jnp.sum(jnp.where(hit, scores, 0.0), axis=-1, keepdims=True)
        masked = jnp.where(hit, neg, masked)
        idx_out = jnp.where(out_lane == k, ik, idx_out)
        w_out = jnp.where(out_lane == k, wk, w_out)
        w_sum = w_sum + wk
    idx_ref[...] = idx_out.astype(jnp.int32)
    w_ref[...] = w_out / w_sum * ROUTED_SCALE

    chosen = functools.reduce(jnp.logical_or, hits).astype(F32)
    ti = lax.broadcasted_iota(jnp.int32, (tm, tm), 0)
    tj = lax.broadcasted_iota(jnp.int32, (tm, tm), 1)
    before = _dot((tj < ti).astype(BF16), chosen.astype(BF16)) + run_ref[...]
    rank_out = jnp.zeros((tm, LANES), jnp.int32)
    for k in range(TOP_K):
        rk = jnp.sum(jnp.where(hits[k], before, 0.0), axis=-1, keepdims=True)
        rank_out = jnp.where(out_lane == k, rk.astype(jnp.int32), rank_out)
    rank_ref[...] = rank_out
    run_ref[...] = run_ref[...] + jnp.sum(chosen, axis=0, keepdims=True)
    counts_ref[...] = run_ref[...]


def _route(x1, mod, gpre, w_router, router_bias, seq):
    t, d = x1.shape
    tm = TOKEN_TILE
    per_b = seq // tm
    full = lambda a: pl.BlockSpec(a.shape, lambda i: (0,) * a.ndim)
    row = lambda n: pl.BlockSpec((tm, n), lambda i: (i, 0))
    rb = router_bias.reshape(1, -1)
    w_hi = w_router.astype(BF16)
    w_router = jnp.stack([w_hi, (w_router - w_hi.astype(F32)).astype(BF16)])
    return pl.pallas_call(
        _route_kernel,
        grid=(t // tm,),
        in_specs=[row(d), pl.BlockSpec((1, 6, d), lambda i: (i // per_b, 0, 0)), full(gpre),
                  full(w_router), full(rb)],
        out_specs=[row(d // 2), row(LANES), row(LANES), row(LANES),
                   pl.BlockSpec((1, N_EXPERTS), lambda i: (0, 0))],
        out_shape=[jax.ShapeDtypeStruct((t, d // 2), jnp.uint32),
                   jax.ShapeDtypeStruct((t, LANES), jnp.int32),
                   jax.ShapeDtypeStruct((t, LANES), F32),
                   jax.ShapeDtypeStruct((t, LANES), jnp.int32),
                   jax.ShapeDtypeStruct((1, N_EXPERTS), F32)],
        scratch_shapes=[pltpu.VMEM((1, N_EXPERTS), F32)],
        compiler_params=_cparams(("arbitrary",)),
        name="route",
    )(x1, mod, gpre, w_router, rb)


GATHER_UNROLL = 8


def _gather_rows(idx_ref, n_rows, src_hbm, dst_of, sem):
    def body(q, carry):
        for u in range(GATHER_UNROLL):
            src = idx_ref[0, 0, q * GATHER_UNROLL + u]
            pltpu.make_async_copy(src_hbm.at[pl.ds(src, 1), :], dst_of(q, u), sem).start()
        return carry

    lax.fori_loop(0, n_rows // GATHER_UNROLL, body, 0)


def _dest_kernel(idx_ref, rank_ref, start_ref, o_ref):
    idx = idx_ref[...]
    rank = rank_ref[...]
    start = start_ref[...]
    tm = idx.shape[0]
    lane = lax.broadcasted_iota(jnp.int32, (tm, start.shape[1]), 1)
    out_lane = lax.broadcasted_iota(jnp.int32, (tm, LANES), 1)
    out = jnp.zeros((tm, LANES), jnp.int32)
    for k in range(TOP_K):
        base = jnp.sum(jnp.where(lane == idx[:, k:k + 1], start, 0.0), axis=-1, keepdims=True)
        out = jnp.where(out_lane == k, base.astype(jnp.int32) + rank[:, k:k + 1], out)
    o_ref[...] = out


def _dest(idx_pad, rank_pad, pad_start):
    t = idx_pad.shape[0]
    tm = TOKEN_TILE
    row = pl.BlockSpec((tm, LANES), lambda i: (i, 0))
    return pl.pallas_call(
        _dest_kernel,
        grid=(t // tm,),
        in_specs=[row, row, pl.BlockSpec((1, N_EXPERTS), lambda i: (0, 0))],
        out_specs=row,
        out_shape=jax.ShapeDtypeStruct((t, LANES), jnp.int32),
        compiler_params=_cparams(("arbitrary",)),
        name="dest",
    )(idx_pad, rank_pad, pad_start.astype(F32).reshape(1, N_EXPERTS))


def _scatter_kernel(dest_ref, hp_ref, init_hbm, xs_hbm, sem):
    del init_hbm
    ts = hp_ref.shape[0]

    def body(t, carry):
        for k in range(TOP_K):
            dst = dest_ref[0, 0, t * TOP_K + k]
            pltpu.make_async_copy(hp_ref.at[pl.ds(t, 1), :], xs_hbm.at[pl.ds(dst, 1), :], sem).start()
        return carry

    lax.fori_loop(0, ts, body, 0)
    for k in range(TOP_K):
        pltpu.make_async_copy(hp_ref, xs_hbm.at[pl.ds(0, ts), :], sem).wait()


def _scatter(dest, hp, n_rows):
    t, half = hp.shape
    ts = SCATTER_TILE
    n_steps = t // ts
    dest3 = dest.reshape(n_steps, 1, ts * TOP_K)
    return pl.pallas_call(
        _scatter_kernel,
        grid=(n_steps,),
        in_specs=[pl.BlockSpec((1, 1, ts * TOP_K), lambda i: (i, 0, 0), memory_space=pltpu.SMEM),
                  pl.BlockSpec((ts, half), lambda i: (i, 0)),
                  pl.BlockSpec(memory_space=pl.ANY)],
        out_specs=pl.BlockSpec(memory_space=pl.ANY),
        out_shape=jax.ShapeDtypeStruct((n_rows, half), jnp.uint32),
        scratch_shapes=[pltpu.SemaphoreType.DMA(())],
        input_output_aliases={2: 0},
        compiler_params=_cparams(("arbitrary",)),
        name="scatter",
    )(dest3, hp, jnp.zeros((n_rows, half), jnp.uint32))


def _expert_kernel(be_ref, nu_ref, x_ref, w13_ref, w2_ref, o_ref, w13b, w2b):
    i = pl.program_id(0)
    n_used = nu_ref[0]

    @pl.when((i == 0) | (be_ref[i] != be_ref[jnp.maximum(i - 1, 0)]))
    def _():
        w13b[...] = w13_ref[0].astype(BF16)
        w2b[...] = w2_ref[0].astype(BF16)

    @pl.when(i < n_used)
    def _():
        w = x_ref[...]
        lo = pltpu.bitcast(w << 16, F32).astype(BF16)
        hi = pltpu.bitcast(w & jnp.uint32(0xFFFF0000), F32).astype(BF16)
        half = lo.shape[1]
        ag = _dot(lo, w13b[:half, :]) + _dot(hi, w13b[half:, :])
        hh = ag.shape[1] // 2
        act = (_silu(ag[:, :hh]) * ag[:, hh:]).astype(BF16)
        o_ref[...] = _dot(act, w2b[...])

    @pl.when(i >= n_used)
    def _():
        o_ref[...] = jnp.zeros_like(o_ref)


def _experts(xs, block_e, n_used, w13, w2):
    n_rows, half = xs.shape
    ne, d, h2 = w13.shape
    bm = EXPERT_ROWS
    n_blocks = n_rows // bm
    grid_spec = pltpu.PrefetchScalarGridSpec(
        num_scalar_prefetch=2,
        grid=(n_blocks,),
        in_specs=[
            pl.BlockSpec((bm, half), lambda i, be, nu: (jnp.minimum(i, nu[0] - 1), 0)),
            pl.BlockSpec((1, d, h2), lambda i, be, nu: (be[i], 0, 0)),
            pl.BlockSpec((1, h2 // 2, d), lambda i, be, nu: (be[i], 0, 0)),
        ],
        out_specs=pl.BlockSpec((bm, d), lambda i, be, nu: (i, 0)),
        scratch_shapes=[pltpu.VMEM((d, h2), BF16),
                        pltpu.VMEM((h2 // 2, d), BF16)],
    )
    return pl.pallas_call(
        _expert_kernel,
        grid_spec=grid_spec,
        out_shape=jax.ShapeDtypeStruct((n_rows, d), F32),
        compiler_params=_cparams(("arbitrary",)),
        name="experts",
    )(block_e, n_used, xs, w13, w2)


def _final_kernel(dc_ref, dn_ref, y_hbm, w_ref, x_ref, mod_ref, gpre_ref, gpost_ref, w13s_ref,
                  w2s_ref, o_ref, buf, sem):
    i = pl.program_id(0)
    slot = i % 2
    tm = x_ref.shape[0]

    def issue(d_ref, s):
        _gather_rows(d_ref, tm * TOP_K, y_hbm,
                     lambda q, u: buf.at[s, u, pl.ds(q, 1), :], sem.at[s])

    @pl.when(i == 0)
    def _():
        issue(dc_ref, 0)

    @pl.when(i + 1 < pl.num_programs(0))
    def _():
        issue(dn_ref, 1 - slot)

    for k in range(TOP_K):
        pltpu.make_async_copy(y_hbm.at[pl.ds(0, tm), :], buf.at[slot, k], sem.at[slot]).wait()

    w = w_ref[...]
    routed = buf[slot, 0] * w[:, 0:1]
    for k in range(1, TOP_K):
        routed = routed + buf[slot, k] * w[:, k:k + 1]

    m = mod_ref[0]
    x1 = x_ref[...]
    h = (_rms(x1, gpre_ref[...]) * (1.0 + m[4:5]) + m[3:4]).astype(BF16)
    ag = _dot(h, w13s_ref[...])
    hh = ag.shape[1] // 2
    act = (_silu(ag[:, :hh]) * ag[:, hh:]).astype(BF16)
    ffn = routed + _dot(act, w2s_ref[...])
    o_ref[...] = x1 + m[5:6] * _rms(ffn, gpost_ref[...])


def _final(dest, y_sorted, top_w, x1, mod, gpre, gpost, w13s, w2s, seq):
    t, d = x1.shape
    tm = COMBINE_TILE
    per_b = seq // tm
    n_steps = t // tm
    dest3 = dest.reshape(n_steps, 1, tm * TOP_K)
    full = lambda a: pl.BlockSpec(a.shape, lambda i: (0,) * a.ndim)
    row = lambda n: pl.BlockSpec((tm, n), lambda i: (i, 0))
    return pl.pallas_call(
        _final_kernel,
        grid=(n_steps,),
        in_specs=[
            pl.BlockSpec((1, 1, tm * TOP_K), lambda i: (i, 0, 0), memory_space=pltpu.SMEM),
            pl.BlockSpec((1, 1, tm * TOP_K), lambda i: (jnp.minimum(i + 1, n_steps - 1), 0, 0),
                         memory_space=pltpu.SMEM),
            pl.BlockSpec(memory_space=pl.ANY),
            row(LANES), row(d), pl.BlockSpec((1, 6, d), lambda i: (i // per_b, 0, 0)),
            full(gpre), full(gpost), full(w13s), full(w2s)],
        out_specs=row(d),
        out_shape=jax.ShapeDtypeStruct((t, d), F32),
        scratch_shapes=[pltpu.VMEM((2, TOP_K, tm, d), F32), pltpu.SemaphoreType.DMA((2,))],
        compiler_params=_cparams(("arbitrary",)),
        name="combine_final",
    )(dest3, dest3, y_sorted, top_w, x1, mod, gpre, gpost, w13s, w2s)


def _pad_heads(v):
    lead = v.shape[:-1]
    v = v.reshape(lead + (2, SSD_GROUPS, HEADS_PER_GROUP))
    v = jnp.pad(v, [(0, 0)] * (len(lead) + 2) + [(0, HEADS_PADDED - HEADS_PER_GROUP)])
    return v.reshape(lead + (2 * SSD_GROUPS * HEADS_PADDED,))


def _dispatch_plan(counts, n_tokens):
    bm = EXPERT_ROWS
    counts = counts.reshape(N_EXPERTS).astype(jnp.int32)
    padded = (counts + bm - 1) // bm * bm
    pad_end = jnp.cumsum(padded)
    pad_start = pad_end - padded
    n_rows = -(-n_tokens * TOP_K // bm) * bm + N_EXPERTS * bm
    n_blocks = n_rows // bm
    block_start = jnp.arange(n_blocks, dtype=jnp.int32) * bm
    block_e = jnp.minimum(jnp.sum(block_start[:, None] >= pad_end[None, :], axis=1),
                          N_EXPERTS - 1).astype(jnp.int32)
    n_used = (pad_end[-1] // bm).astype(jnp.int32).reshape(1)
    return pad_start, block_e, n_used, n_rows


def _layer(x, c, w_ada, b_ada, pre_norm_mix, post_norm_mix, pre_norm_ffn, post_norm_ffn, w_in,
           conv_w, conv_b, dt_bias_fwd, dt_bias_bwd, a_log_fwd, a_log_bwd, d_skip, ssd_norm,
           w_branch_ssd, w_branch_fourier, w_out, w_router, router_bias, w13_experts, w2_experts,
           w13_shared, w2_shared):
    bsz, seq, d = x.shape
    t = bsz * seq
    x2 = x.reshape(t, d)
    row = lambda v: v.reshape(1, -1).astype(F32)

    mod = _ada(c, w_ada, b_ada)

    i1 = SSD_INNER
    i2 = i1 + XBC_WIDTH
    i3 = i2 + 2 * SSD_HEADS
    i4 = i3 + FOURIER_WIDTH
    n_dt = 2 * SSD_GROUPS * HEADS_PADDED
    w_dtp = _pad_heads(w_in[:, i2:i3])
    wdtc = jnp.pad(w_dtp, ((0, 0), (0, LANES - n_dt))).astype(BF16)
    wdtT = w_dtp.T.astype(BF16)
    bias_p = _pad_heads(jnp.concatenate([dt_bias_fwd, dt_bias_bwd]).astype(F32))
    a_p = _pad_heads(-jnp.exp(jnp.concatenate([a_log_fwd, a_log_bwd]).astype(F32)))
    pad_row = lambda v: jnp.pad(v, (0, LANES - n_dt)).reshape(1, LANES)
    z, xbc, acsc, dtT, acsT, uf, gates = _inproj(
        x2, mod, row(pre_norm_mix), w_in[:, :i1].astype(BF16), w_in[:, i1:i2].astype(BF16),
        wdtc, wdtT, w_in[:, i3:i4].astype(BF16), w_in[:, i4:].astype(BF16),
        pad_row(bias_p), pad_row(a_p), bias_p.reshape(n_dt, 1), a_p.reshape(n_dt, 1), seq)

    xbc3 = _conv(xbc.reshape(bsz, seq, XBC_WIDTH), conv_w, conv_b)

    yf, yb = _ssd(xbc3, dtT, acsT, acsc)

    fm = _fourier(uf.reshape(bsz, seq, FOURIER_WIDTH))

    x1 = _mix(yf.reshape(t, SSD_INNER), yb.reshape(t, SSD_INNER), xbc3.reshape(t, XBC_WIDTH), z,
              fm.reshape(t, FOURIER_WIDTH), gates, x2, mod,
              row(jnp.repeat(d_skip, SSD_HEAD_DIM)), row(ssd_norm), w_branch_ssd.astype(BF16),
              w_branch_fourier.astype(BF16), w_out.astype(BF16), row(post_norm_mix), seq)

    hp, idx_pad, w_pad, rank_pad, counts = _route(x1, mod, row(pre_norm_ffn),
                                                  w_router.astype(F32), router_bias, seq)
    pad_start, block_e, n_used, n_rows = _dispatch_plan(counts, t)
    dest = _dest(idx_pad, rank_pad, pad_start)[:, :TOP_K]
    xs = _scatter(dest, hp, n_rows)
    y_sorted = _experts(xs, block_e, n_used, w13_experts, w2_experts)
    out = _final(dest, y_sorted, w_pad, x1, mod, row(pre_norm_ffn), row(post_norm_ffn),
                 w13_shared.astype(BF16), w2_shared.astype(BF16), seq)
    return out.reshape(bsz, seq, d)


def kernel(x, c, w_ada, b_ada, pre_norm_mix, post_norm_mix, pre_norm_ffn, post_norm_ffn, w_in,
           conv_w, conv_b, dt_bias_fwd, dt_bias_bwd, a_log_fwd, a_log_bwd, d_skip, ssd_norm,
           w_branch_ssd, w_branch_fourier, w_out, w_router, router_bias, w13_experts, w2_experts,
           w13_shared, w2_shared):
    for layer in range(w_ada.shape[0]):
        x = _layer(x, c, w_ada[layer], b_ada[layer], pre_norm_mix[layer], post_norm_mix[layer],
                   pre_norm_ffn[layer], post_norm_ffn[layer], w_in[layer], conv_w[layer],
                   conv_b[layer], dt_bias_fwd[layer], dt_bias_bwd[layer], a_log_fwd[layer],
                   a_log_bwd[layer], d_skip[layer], ssd_norm[layer], w_branch_ssd[layer],
                   w_branch_fourier[layer], w_out[layer], w_router[layer], router_bias[layer],
                   w13_experts[layer], w2_experts[layer], w13_shared[layer], w2_shared[layer])
    return x
```

```python
import functools
import math

import numpy as np
import jax
import jax.numpy as jnp
from jax import lax
from jax.experimental import pallas as pl
from jax.experimental.pallas import tpu as pltpu

F32 = jnp.float32
BF16 = jnp.bfloat16
HIGHEST = lax.Precision.HIGHEST

D_MODEL = 1024
SSD_HEADS = 24
SSD_HEAD_DIM = 64
SSD_INNER = SSD_HEADS * SSD_HEAD_DIM
SSD_GROUPS = 4
HEADS_PER_GROUP = SSD_HEADS // SSD_GROUPS
HEADS_PADDED = 8
SSD_STATE = 128
SSD_CONV = 5
SSD_CHUNK = 128
XBC_WIDTH = SSD_INNER + 2 * SSD_GROUPS * SSD_STATE
GROUP_X = HEADS_PER_GROUP * SSD_HEAD_DIM
FOURIER_WIDTH = 512
FOURIER_GROUP_DIM = 128
N_EXPERTS = 256
TOP_K = 8
N_EXPERT_GROUPS = 8
TOPK_GROUPS = 4
EXPERT_HIDDEN = 256
SHARED_HIDDEN = 256
ROUTED_SCALE = 2.5
RMS_EPS = 1e-6

LANES = 128
VMEM_LIMIT = 56 * 1024 * 1024
TOKEN_TILE = 256
EXPERT_ROWS = 256
COMBINE_TILE = 128
SCATTER_TILE = 512
CONV_ROWS = 256


def _cparams(sem):
    return pltpu.CompilerParams(dimension_semantics=sem, vmem_limit_bytes=VMEM_LIMIT)


def _dot(a, b, precision=None):
    return jnp.dot(a, b, preferred_element_type=F32, precision=precision)


def _dot_nt(a, b, precision=None):
    return lax.dot_general(a, b, (((1,), (1,)), ((), ())), preferred_element_type=F32,
                           precision=precision)


def _dot_tn(a, b):
    return lax.dot_general(a, b, (((0,), (0,)), ((), ())), preferred_element_type=F32)


def _sigmoid(x):
    return 1.0 / (1.0 + jnp.exp(-x))


def _silu(x):
    return x * _sigmoid(x)


def _softplus(x):
    return jnp.maximum(x, 0.0) + jnp.log1p(jnp.exp(-jnp.abs(x)))


def _rms(x, g):
    return x * lax.rsqrt(jnp.mean(x * x, axis=-1, keepdims=True) + RMS_EPS) * g


def _ada_kernel(c_ref, w_ref, b_ref, o_ref):
    o_ref[...] = _dot(_silu(c_ref[...]), w_ref[...], HIGHEST) + b_ref[...]


def _ada(c, w_ada, b_ada):
    bsz, d = c.shape
    rows = 8
    cp = jnp.zeros((rows, d), F32).at[:bsz].set(c)
    n = w_ada.shape[1]
    tn = 1536
    out = pl.pallas_call(
        _ada_kernel,
        grid=(n // tn,),
        in_specs=[pl.BlockSpec((rows, d), lambda j: (0, 0)),
                  pl.BlockSpec((d, tn), lambda j: (0, j)),
                  pl.BlockSpec((1, tn), lambda j: (0, j))],
        out_specs=pl.BlockSpec((rows, tn), lambda j: (0, j)),
        out_shape=jax.ShapeDtypeStruct((rows, n), F32),
        compiler_params=_cparams(("arbitrary",)),
        name="adaln",
    )(cp, w_ada, b_ada.reshape(1, n))
    return out[:bsz].reshape(bsz, 6, d)


def _split3(a):
    a1 = a.astype(BF16)
    r1 = a - a1.astype(F32)
    a2 = r1.astype(BF16)
    a3 = (r1 - a2.astype(F32)).astype(BF16)
    return a1, a2, a3


def _inproj_kernel(x_ref, mod_ref, g_ref, wz_ref, wxbc_ref, wdtc_ref, wdtT_ref, wuf_ref, wg_ref,
                   bias_row_ref, a_row_ref, bias_col_ref, a_col_ref,
                   z_ref, xbc_ref, acsc_ref, dtT_ref, acsT_ref, uf_ref, gates_ref):
    m = mod_ref[0]
    h = _rms(x_ref[...], g_ref[...]) * (1.0 + m[1:2]) + m[0:1]
    hb = h.astype(BF16)
    z_ref[...] = _dot(hb, wz_ref[...])
    xbc_ref[...] = _dot(hb, wxbc_ref[...])
    uf_ref[...] = _dot(hb, wuf_ref[...])
    gates_ref[...] = _dot(hb, wg_ref[...])

    tm = hb.shape[0]
    n_fwd = SSD_GROUPS * HEADS_PADDED
    ii = lax.broadcasted_iota(jnp.int32, (tm, tm), 0)
    jj = lax.broadcasted_iota(jnp.int32, (tm, tm), 1)
    same = (ii // SSD_CHUNK) == (jj // SSD_CHUNK)
    tri_f = (same & (jj <= ii)).astype(BF16)
    tri_b = (same & (jj >= ii)).astype(BF16)

    dt_c = _softplus(_dot(hb, wdtc_ref[...]) + bias_row_ref[...])
    pieces = _split3(dt_c * a_row_ref[...])
    acs_f = sum(_dot(tri_f, p) for p in pieces)
    acs_b = sum(_dot(tri_b, p) for p in pieces)
    lane = lax.broadcasted_iota(jnp.int32, acs_f.shape, 1)
    acsc_ref[...] = jnp.where(lane < n_fwd, acs_f, acs_b)

    dt_t = _softplus(_dot_nt(wdtT_ref[...], hb) + bias_col_ref[...])
    pieces = _split3(dt_t * a_col_ref[...])
    acs_f = sum(_dot_nt(p, tri_f) for p in pieces)
    acs_b = sum(_dot_nt(p, tri_b) for p in pieces)
    sub = lax.broadcasted_iota(jnp.int32, acs_f.shape, 0)
    dtT_ref[...] = dt_t
    acsT_ref[...] = jnp.where(sub < n_fwd, acs_f, acs_b)


def _inproj(x2, mod, g, wz, wxbc, wdtc, wdtT, wuf, wg, bias_row, a_row, bias_col, a_col, seq):
    t, d = x2.shape
    tm = TOKEN_TILE
    per_b = seq // tm
    full = lambda a: pl.BlockSpec(a.shape, lambda i: (0,) * a.ndim)
    row = lambda n: pl.BlockSpec((tm, n), lambda i: (i, 0))
    nd = wdtT.shape[0]
    colspec = pl.BlockSpec((nd, tm), lambda i: (0, i))
    return pl.pallas_call(
        _inproj_kernel,
        grid=(t // tm,),
        in_specs=[row(d), pl.BlockSpec((1, 6, d), lambda i: (i // per_b, 0, 0)), full(g),
                  full(wz), full(wxbc), full(wdtc), full(wdtT), full(wuf), full(wg),
                  full(bias_row), full(a_row), full(bias_col), full(a_col)],
        out_specs=[row(wz.shape[1]), row(wxbc.shape[1]), row(LANES), colspec, colspec,
                   row(wuf.shape[1]), row(wg.shape[1])],
        out_shape=[jax.ShapeDtypeStruct((t, wz.shape[1]), F32),
                   jax.ShapeDtypeStruct((t, wxbc.shape[1]), F32),
                   jax.ShapeDtypeStruct((t, LANES), F32),
                   jax.ShapeDtypeStruct((nd, t), F32),
                   jax.ShapeDtypeStruct((nd, t), F32),
                   jax.ShapeDtypeStruct((t, wuf.shape[1]), F32),
                   jax.ShapeDtypeStruct((t, wg.shape[1]), F32)],
        compiler_params=_cparams(("arbitrary",)),
        name="inproj",
    )(x2, mod, g, wz, wxbc, wdtc, wdtT, wuf, wg, bias_row, a_row, bias_col, a_col)


def _conv_kernel(u_ref, w_ref, b_ref, o_ref, pad_ref):
    s = u_ref.shape[1]
    halo = 8
    pad_ref[0:halo, :] = jnp.zeros((halo, LANES), F32)
    pad_ref[halo + s:2 * halo + s, :] = jnp.zeros((halo, LANES), F32)
    pad_ref[halo:halo + s, :] = u_ref[0]
    w = w_ref[...]
    b = b_ref[...]
    half = (SSD_CONV - 1) // 2
    for r in range(s // CONV_ROWS):
        base = r * CONV_ROWS
        acc = b
        for k in range(SSD_CONV):
            lo = base + halo + k - half
            acc = acc + w[k:k + 1, :] * pad_ref[lo:lo + CONV_ROWS, :]
        o_ref[0, base:base + CONV_ROWS, :] = _silu(acc)


def _conv(xbc3, conv_w, conv_b):
    bsz, s, c = xbc3.shape
    return pl.pallas_call(
        _conv_kernel,
        grid=(bsz, c // LANES),
        in_specs=[pl.BlockSpec((1, s, LANES), lambda b, j: (b, 0, j)),
                  pl.BlockSpec((SSD_CONV, LANES), lambda b, j: (0, j)),
                  pl.BlockSpec((1, LANES), lambda b, j: (0, j))],
        out_specs=pl.BlockSpec((1, s, LANES), lambda b, j: (b, 0, j)),
        out_shape=jax.ShapeDtypeStruct((bsz, s, c), F32),
        scratch_shapes=[pltpu.VMEM((s + 16, LANES), F32)],
        compiler_params=_cparams(("arbitrary", "arbitrary")),
        name="conv",
    )(xbc3, conv_w, conv_b.reshape(1, c))


HEAD_PAIRS = HEADS_PER_GROUP // 2


def _ssd_direction(x, bm, cm, dt_r, acs_r, acsc_all, lane_off, s_ref, reverse):
    L, N = bm.shape
    assert L == N == LANES
    ii = lax.broadcasted_iota(jnp.int32, (L, L), 0)
    jj = lax.broadcasted_iota(jnp.int32, (L, L), 1)
    mask = (jj >= ii) if reverse else (jj <= ii)
    lo_half = jj < SSD_HEAD_DIM
    shift = jnp.where(lane_off == 0, 0, LANES - lane_off)
    acs_c = pltpu.roll(acsc_all, shift, 1)
    last = 0 if reverse else L - 1
    tot_r = acs_r[:, last:last + 1]
    w_r = jnp.exp(tot_r - acs_r) * dt_r
    etot = jnp.broadcast_to(jnp.exp(tot_r), (HEADS_PADDED, LANES))

    cbt = _dot_nt(cm.astype(BF16), bm.astype(BF16))
    bt = bm.T
    ys = []
    for q in range(HEAD_PAIRS):
        xq = x[:, q * LANES:(q + 1) * LANES]
        sq = s_ref[q]
        x_a = jnp.where(lo_half, xq, 0.0).astype(BF16)
        x_b = jnp.where(lo_half, 0.0, xq).astype(BF16)
        s_a = jnp.where(lo_half, sq, 0.0).astype(BF16)
        s_b = jnp.where(lo_half, 0.0, sq).astype(BF16)
        m_parts, c_parts, b_parts = [], [], []
        for h in (2 * q, 2 * q + 1):
            col = jnp.broadcast_to(acs_c[:, h:h + 1], (L, L))
            decay = jnp.exp(jnp.where(mask, col - acs_r[h:h + 1, :], -jnp.inf))
            m_parts.append((cbt * decay * dt_r[h:h + 1, :]).astype(BF16))
            c_parts.append((cm * jnp.exp(col)).astype(BF16))
            b_parts.append((bt * w_r[h:h + 1, :]).astype(BF16))
        x_diag = jnp.concatenate([x_a, x_b], axis=0)
        lhs = jnp.concatenate(m_parts + c_parts, axis=1)
        rhs = jnp.concatenate([x_diag, s_a, s_b], axis=0)
        ys.append(_dot(lhs, rhs))
        dec = jnp.where(lo_half[0:1], etot[2 * q:2 * q + 1], etot[2 * q + 1:2 * q + 2])
        s_ref[q] = sq * dec + _dot(jnp.concatenate(b_parts, axis=1), x_diag)
    return jnp.concatenate(ys, axis=1)


def _ssd_kernel(xf_ref, bf_ref, cf_ref, dtTf_ref, acsTf_ref, acscf_ref,
                xb_ref, bb_ref, cb_ref, dtTb_ref, acsTb_ref, acscb_ref,
                yf_ref, yb_ref, sf_ref, sb_ref):
    g = pl.program_id(1)

    @pl.when(pl.program_id(2) == 0)
    def _():
        sf_ref[...] = jnp.zeros_like(sf_ref)
        sb_ref[...] = jnp.zeros_like(sb_ref)

    yf_ref[0] = _ssd_direction(xf_ref[0], bf_ref[0], cf_ref[0], dtTf_ref[...], acsTf_ref[...],
                               acscf_ref[...], g * HEADS_PADDED, sf_ref, False)
    yb_ref[0] = _ssd_direction(xb_ref[0], bb_ref[0], cb_ref[0], dtTb_ref[...], acsTb_ref[...],
                               acscb_ref[...], (SSD_GROUPS + g) * HEADS_PADDED, sb_ref, True)


def _ssd(xbc3, dtT, acsT, acsc):
    bsz, s, _ = xbc3.shape
    L = SSD_CHUNK
    nc = s // L
    G = SSD_GROUPS
    nb = SSD_INNER // SSD_STATE
    ncb = nb + G
    fwd = lambda c: c
    bwd = lambda c: nc - 1 - c

    def specs(cidx, dirn):
        rowspec = pl.BlockSpec((HEADS_PADDED, L), lambda b, g, c: (dirn * G + g, b * nc + cidx(c)))
        return [
            pl.BlockSpec((1, L, GROUP_X), lambda b, g, c: (b, cidx(c), g)),
            pl.BlockSpec((1, L, SSD_STATE), lambda b, g, c: (b, cidx(c), nb + g)),
            pl.BlockSpec((1, L, SSD_STATE), lambda b, g, c: (b, cidx(c), ncb + g)),
            rowspec, rowspec,
            pl.BlockSpec((L, LANES), lambda b, g, c: (b * nc + cidx(c), 0)),
        ]

    out_specs = [pl.BlockSpec((1, L, GROUP_X), lambda b, g, c: (b, c, g)),
                 pl.BlockSpec((1, L, GROUP_X), lambda b, g, c: (b, nc - 1 - c, g))]
    return pl.pallas_call(
        _ssd_kernel,
        grid=(bsz, G, nc),
        in_specs=specs(fwd, 0) + specs(bwd, 1),
        out_specs=out_specs,
        out_shape=[jax.ShapeDtypeStruct((bsz, s, SSD_INNER), F32)] * 2,
        scratch_shapes=[pltpu.VMEM((HEAD_PAIRS, SSD_STATE, LANES), F32)] * 2,
        compiler_params=_cparams(("arbitrary", "arbitrary", "arbitrary")),
        name="ssd",
    )(xbc3, xbc3, xbc3, dtT, acsT, acsc, xbc3, xbc3, xbc3, dtT, acsT, acsc)


def _dft_tables(seq):
    n2n = LANES
    n1n = seq // n2n
    n1 = np.arange(n1n)
    k1 = np.arange(n1n)
    n2 = np.arange(n2n)
    ang = -2.0 * np.pi * (n2[:, None, None] * k1[None, :, None] / seq
                          + n1[None, None, :] * k1[None, :, None] / n1n)
    f1 = np.concatenate([np.cos(ang), np.sin(ang)], axis=1)
    k2 = np.arange(n2n)
    a2 = 2.0 * np.pi * np.outer(k2, n2) / n2n
    c2, s2 = np.cos(a2), np.sin(a2)
    g = np.block([[c2, s2], [-s2, c2]])
    ch = np.arange(FOURIER_GROUP_DIM)
    ac = 2.0 * np.pi * np.outer(ch, ch) / FOURIER_GROUP_DIM
    scale = 1.0 / math.sqrt(seq * FOURIER_GROUP_DIM)
    fc = np.concatenate([np.cos(ac), np.sin(ac)], axis=0) * scale
    return (jnp.asarray(f1, BF16), jnp.asarray(g, BF16), jnp.asarray(fc, BF16))


DFT_UNROLL = 4


def _dft_pitch(n1n):
    return 2 * n1n + 8


def _fourier_kernel(u_ref, f1_ref, g_ref, fc_ref, o_ref, a_ref):
    n2n = LANES
    n1n = u_ref.shape[2]
    pitch = _dft_pitch(n1n)

    def stage1(i, carry):
        for u in range(DFT_UNROLL):
            n2 = i * DFT_UNROLL + u
            xs = u_ref[0, 0, :, pl.ds(pl.multiple_of(n2 * LANES, LANES), LANES)].astype(BF16)
            a_ref[pl.ds(pl.multiple_of(n2 * pitch, 8), 2 * n1n), :] = _dot(f1_ref[n2], xs)
        return carry

    lax.fori_loop(0, n2n // DFT_UNROLL, stage1, 0)
    gm = g_ref[...]
    fc = fc_ref[...]

    def stage2(i, carry):
        for u in range(DFT_UNROLL):
            k1 = i * DFT_UNROLL + u
            re = a_ref[pl.ds(k1, n2n, stride=pitch), :]
            im = a_ref[pl.ds(n1n + k1, n2n, stride=pitch), :]
            a = jnp.concatenate([re, im], axis=0).astype(BF16)
            z = _dot(gm, a)
            zz = jnp.concatenate([z[:n2n], z[n2n:]], axis=1).astype(BF16)
            o_ref[0, pl.ds(k1, n2n, stride=n1n), :] = _dot(zz, fc)
        return carry

    lax.fori_loop(0, n1n // DFT_UNROLL, stage2, 0)


def _fourier(uf3):
    bsz, s, w = uf3.shape
    f1, g, fc = _dft_tables(s)
    n1n = s // LANES
    ng = w // FOURIER_GROUP_DIM
    u = uf3.reshape(bsz, n1n, LANES, ng, FOURIER_GROUP_DIM).transpose(0, 3, 1, 2, 4)
    u = u.reshape(bsz, ng, n1n, LANES * FOURIER_GROUP_DIM)
    return pl.pallas_call(
        _fourier_kernel,
        grid=(bsz, ng),
        in_specs=[pl.BlockSpec((1, 1, n1n, LANES * FOURIER_GROUP_DIM), lambda b, j: (b, j, 0, 0)),
                  pl.BlockSpec(f1.shape, lambda b, j: (0, 0, 0)),
                  pl.BlockSpec(g.shape, lambda b, j: (0, 0)),
                  pl.BlockSpec(fc.shape, lambda b, j: (0, 0))],
        out_specs=pl.BlockSpec((1, s, LANES), lambda b, j: (b, 0, j)),
        out_shape=jax.ShapeDtypeStruct((bsz, s, w), F32),
        scratch_shapes=[pltpu.VMEM((LANES * _dft_pitch(n1n), LANES), F32)],
        compiler_params=_cparams(("arbitrary", "arbitrary")),
        name="fourier",
    )(u, f1, g, fc)


def _mix_kernel(yf_ref, yb_ref, xs_ref, z_ref, fm_ref, gates_ref, x_ref, mod_ref, dskip_ref,
                ssdn_ref, wbs_ref, wbf_ref, wout_ref, gpost_ref, o_ref):
    m = mod_ref[0]
    y = yf_ref[...] + yb_ref[...] + dskip_ref[...] * xs_ref[...]
    v = y * _silu(z_ref[...])
    parts = []
    for g in range(SSD_GROUPS):
        vg = v[:, g * GROUP_X:(g + 1) * GROUP_X]
        parts.append(vg * lax.rsqrt(jnp.mean(vg * vg, axis=-1, keepdims=True) + RMS_EPS))
    vn = jnp.concatenate(parts, axis=1) * ssdn_ref[...]
    y_ssd = _dot(vn.astype(BF16), wbs_ref[...])
    y_fou = _dot(fm_ref[...].astype(BF16), wbf_ref[...])
    gt = _sigmoid(gates_ref[...])
    d = y_ssd.shape[1]
    mixed = gt[:, :d] * y_ssd + gt[:, d:] * y_fou
    mo = _dot(mixed.astype(BF16), wout_ref[...])
    o_ref[...] = x_ref[...] + m[2:3] * _rms(mo, gpost_ref[...])


def _mix(yf, yb, xbc, z, fm, gates, x2, mod, dskip, ssdn, wbs, wbf, wout, gpost, seq):
    t, d = x2.shape
    tm = TOKEN_TILE
    per_b = seq // tm
    full = lambda a: pl.BlockSpec(a.shape, lambda i: (0,) * a.ndim)
    row = lambda n: pl.BlockSpec((tm, n), lambda i: (i, 0))
    return pl.pallas_call(
        _mix_kernel,
        grid=(t // tm,),
        in_specs=[row(SSD_INNER), row(SSD_INNER), row(SSD_INNER), row(SSD_INNER),
                  row(FOURIER_WIDTH), row(2 * d), row(d),
                  pl.BlockSpec((1, 6, d), lambda i: (i // per_b, 0, 0)),
                  full(dskip), full(ssdn), full(wbs), full(wbf), full(wout), full(gpost)],
        out_specs=row(d),
        out_shape=jax.ShapeDtypeStruct((t, d), F32),
        compiler_params=_cparams(("arbitrary",)),
        name="mix",
    )(yf, yb, xbc, z, fm, gates, x2, mod, dskip, ssdn, wbs, wbf, wout, gpost)


def _route_kernel(x_ref, mod_ref, gpre_ref, wr_ref, rb_ref, hp_ref, idx_ref, w_ref, rank_ref,
                  counts_ref, run_ref):
    @pl.when(pl.program_id(0) == 0)
    def _():
        run_ref[...] = jnp.zeros_like(run_ref)

    m = mod_ref[0]
    h = _rms(x_ref[...], gpre_ref[...]) * (1.0 + m[4:5]) + m[3:4]
    tm, d = h.shape
    half = d // 2
    bits = pltpu.bitcast(h.astype(BF16).astype(F32), jnp.uint32)
    hp_ref[...] = (bits[:, :half] >> 16) | (bits[:, half:] & jnp.uint32(0xFFFF0000))

    h_hi = h.astype(BF16)
    h_lo = (h - h_hi.astype(F32)).astype(BF16)
    w_hi = wr_ref[0]
    logits = _dot(h_hi, w_hi) + (_dot(h_hi, wr_ref[1]) + _dot(h_lo, w_hi))
    scores = _sigmoid(logits)
    biased = scores + rb_ref[...]
    ne = scores.shape[1]
    per_group = ne // N_EXPERT_GROUPS
    lane = lax.broadcasted_iota(jnp.int32, (tm, ne), 1)
    lane_f = lane.astype(F32)
    grp = lane // per_group
    out_lane = lax.broadcasted_iota(jnp.int32, (tm, LANES), 1)
    neg = -jnp.inf

    def argmax_first(v):
        mx = jnp.max(v, axis=-1, keepdims=True)
        ix = jnp.min(jnp.where(v == mx, lane_f, float(ne)), axis=-1, keepdims=True)
        return mx, ix

    gs = []
    gmat = jnp.full((tm, LANES), neg, F32)
    for g in range(N_EXPERT_GROUPS):
        vg = jnp.where(grp == g, biased, neg)
        m1, i1 = argmax_first(vg)
        m2 = jnp.max(jnp.where(lane_f == i1, neg, vg), axis=-1, keepdims=True)
        gs.append(m1 + m2)
        gmat = jnp.where(out_lane == g, gs[g], gmat)
    ahead = jnp.zeros((tm, LANES), F32)
    for o in range(N_EXPERT_GROUPS):
        beats = (gs[o] > gmat) | ((gs[o] == gmat) & (out_lane > o))
        ahead = ahead + beats.astype(F32)
    sel = ((ahead < TOPK_GROUPS) & (out_lane < N_EXPERT_GROUPS)).astype(BF16)
    eg = lax.broadcasted_iota(jnp.int32, (LANES, ne), 0)
    ee = lax.broadcasted_iota(jnp.int32, (LANES, ne), 1)
    allowed = _dot(sel, (eg == ee // per_group).astype(BF16)) > 0.5
    masked = jnp.where(allowed, biased, neg)

    idx_out = jnp.zeros((tm, LANES), F32)
    w_out = jnp.zeros((tm, LANES), F32)
    w_sum = jnp.zeros((tm, 1), F32)
    hits = []
    for k in range(TOP_K):
        _, ik = argmax_first(masked)
        hit = lane_f == ik
        hits.append(hit)
        wk = jnp.sum(jnp.where(hit, scores, 0.0), axis=-1, keepdims=True)
        masked = jnp.where(hit, neg, masked)
        idx_out = jnp.where(out_lane == k, ik, idx_out)
        w_out = jnp.where(out_lane == k, wk, w_out)
        w_sum = w_sum + wk
    idx_ref[...] = idx_out.astype(jnp.int32)
    w_ref[...] = w_out / w_sum * ROUTED_SCALE

    chosen = functools.reduce(jnp.logical_or, hits).astype(F32)
    ti = lax.broadcasted_iota(jnp.int32, (tm, tm), 0)
    tj = lax.broadcasted_iota(jnp.int32, (tm, tm), 1)
    before = _dot((tj < ti).astype(BF16), chosen.astype(BF16)) + run_ref[...]
    rank_out = jnp.zeros((tm, LANES), jnp.int32)
    for k in range(TOP_K):
        rk = jnp.sum(jnp.where(hits[k], before, 0.0), axis=-1, keepdims=True)
        rank_out = jnp.where(out_lane == k, rk.astype(jnp.int32), rank_out)
    rank_ref[...] = rank_out
    run_ref[...] = run_ref[...] + jnp.sum(chosen, axis=0, keepdims=True)
    counts_ref[...] = run_ref[...]


def _route(x1, mod, gpre, w_router, router_bias, seq):
    t, d = x1.shape
    tm = TOKEN_TILE
    per_b = seq // tm
    full = lambda a: pl.BlockSpec(a.shape, lambda i: (0,) * a.ndim)
    row = lambda n: pl.BlockSpec((tm, n), lambda i: (i, 0))
    rb = router_bias.reshape(1, -1)
    w_hi = w_router.astype(BF16)
    w_router = jnp.stack([w_hi, (w_router - w_hi.astype(F32)).astype(BF16)])
    return pl.pallas_call(
        _route_kernel,
        grid=(t // tm,),
        in_specs=[row(d), pl.BlockSpec((1, 6, d), lambda i: (i // per_b, 0, 0)), full(gpre),
                  full(w_router), full(rb)],
        out_specs=[row(d // 2), row(LANES), row(LANES), row(LANES),
                   pl.BlockSpec((1, N_EXPERTS), lambda i: (0, 0))],
        out_shape=[jax.ShapeDtypeStruct((t, d // 2), jnp.uint32),
                   jax.ShapeDtypeStruct((t, LANES), jnp.int32),
                   jax.ShapeDtypeStruct((t, LANES), F32),
                   jax.ShapeDtypeStruct((t, LANES), jnp.int32),
                   jax.ShapeDtypeStruct((1, N_EXPERTS), F32)],
        scratch_shapes=[pltpu.VMEM((1, N_EXPERTS), F32)],
        compiler_params=_cparams(("arbitrary",)),
        name="route",
    )(x1, mod, gpre, w_router, rb)


GATHER_UNROLL = 8


ROW_TILE = 8


def _tile_rows(r):
    return pl.ds(pl.multiple_of(r * ROW_TILE, ROW_TILE), ROW_TILE)


def _gather_rows(idx_ref, n_rows, src_hbm, dst_of, sem):
    def body(q, carry):
        for u in range(GATHER_UNROLL):
            src = idx_ref[0, 0, q * GATHER_UNROLL + u]
            pltpu.make_async_copy(src_hbm.at[_tile_rows(src), :], dst_of(q, u), sem).start(
                priority=u % 2)
        return carry

    lax.fori_loop(0, n_rows // GATHER_UNROLL, body, 0)


def _to_tiled_rows(o_ref, y):
    rows, width = y.shape
    for r in range(rows // ROW_TILE):
        for j in range(width // LANES):
            o_ref[pl.ds(r * ROW_TILE * ROW_TILE + j, ROW_TILE, stride=ROW_TILE), :] = (
                y[r * ROW_TILE:(r + 1) * ROW_TILE, j * LANES:(j + 1) * LANES])


def _from_tiled_rows(load, rows, width):
    cols = []
    for j in range(width // LANES):
        pieces = [load(r * ROW_TILE * ROW_TILE + j, ROW_TILE, ROW_TILE) for r in range(rows // ROW_TILE)]
        cols.append(jnp.concatenate(pieces, axis=0))
    return jnp.concatenate(cols, axis=1)


def _dest_kernel(idx_ref, rank_ref, start_ref, o_ref):
    idx = idx_ref[...]
    rank = rank_ref[...]
    start = start_ref[...]
    tm = idx.shape[0]
    lane = lax.broadcasted_iota(jnp.int32, (tm, start.shape[1]), 1)
    out_lane = lax.broadcasted_iota(jnp.int32, (tm, LANES), 1)
    out = jnp.zeros((tm, LANES), jnp.int32)
    for k in range(TOP_K):
        base = jnp.sum(jnp.where(lane == idx[:, k:k + 1], start, 0.0), axis=-1, keepdims=True)
        out = jnp.where(out_lane == k, base.astype(jnp.int32) + rank[:, k:k + 1], out)
    o_ref[...] = out


def _dest(idx_pad, rank_pad, pad_start):
    t = idx_pad.shape[0]
    tm = TOKEN_TILE
    row = pl.BlockSpec((tm, LANES), lambda i: (i, 0))
    return pl.pallas_call(
        _dest_kernel,
        grid=(t // tm,),
        in_specs=[row, row, pl.BlockSpec((1, N_EXPERTS), lambda i: (0, 0))],
        out_specs=row,
        out_shape=jax.ShapeDtypeStruct((t, LANES), jnp.int32),
        compiler_params=_cparams(("arbitrary",)),
        name="dest",
    )(idx_pad, rank_pad, pad_start.astype(F32).reshape(1, N_EXPERTS))


def _scatter_kernel(dest_ref, hp_ref, init_hbm, xs_hbm, sem):
    del init_hbm
    ts = hp_ref.shape[0]

    def body(t, carry):
        for k in range(TOP_K):
            dst = dest_ref[0, 0, t * TOP_K + k]
            pltpu.make_async_copy(hp_ref.at[pl.ds(t, 1), :], xs_hbm.at[pl.ds(dst, 1), :], sem).start(
                priority=k % 2)
        return carry

    lax.fori_loop(0, ts, body, 0)
    for k in range(TOP_K):
        pltpu.make_async_copy(hp_ref, xs_hbm.at[pl.ds(0, ts), :], sem).wait()


def _scatter(dest, hp, n_rows):
    t, half = hp.shape
    ts = SCATTER_TILE
    n_steps = t // ts
    dest3 = dest.reshape(n_steps, 1, ts * TOP_K)
    return pl.pallas_call(
        _scatter_kernel,
        grid=(n_steps,),
        in_specs=[pl.BlockSpec((1, 1, ts * TOP_K), lambda i: (i, 0, 0), memory_space=pltpu.SMEM),
                  pl.BlockSpec((ts, half), lambda i: (i, 0)),
                  pl.BlockSpec(memory_space=pl.ANY)],
        out_specs=pl.BlockSpec(memory_space=pl.ANY),
        out_shape=jax.ShapeDtypeStruct((n_rows, half), jnp.uint32),
        scratch_shapes=[pltpu.SemaphoreType.DMA(())],
        input_output_aliases={2: 0},
        compiler_params=_cparams(("arbitrary",)),
        name="scatter",
    )(dest3, hp, jnp.zeros((n_rows, half), jnp.uint32))


def _expert_kernel(be_ref, nu_ref, x_ref, w13_ref, w2_ref, o_ref, w13b, w2b):
    i = pl.program_id(0)
    n_used = nu_ref[0]

    @pl.when((i == 0) | (be_ref[i] != be_ref[jnp.maximum(i - 1, 0)]))
    def _():
        w13b[...] = w13_ref[0].astype(BF16)
        w2b[...] = w2_ref[0].astype(BF16)

    @pl.when(i < n_used)
    def _():
        w = x_ref[...]
        lo = pltpu.bitcast(w << 16, F32).astype(BF16)
        hi = pltpu.bitcast(w & jnp.uint32(0xFFFF0000), F32).astype(BF16)
        half = lo.shape[1]
        ag = _dot(lo, w13b[:half, :]) + _dot(hi, w13b[half:, :])
        hh = ag.shape[1] // 2
        act = (_silu(ag[:, :hh]) * ag[:, hh:]).astype(BF16)
        _to_tiled_rows(o_ref, _dot(act, w2b[...]))

    @pl.when(i >= n_used)
    def _():
        o_ref[...] = jnp.zeros_like(o_ref)


def _experts(xs, block_e, n_used, w13, w2):
    n_rows, half = xs.shape
    ne, d, h2 = w13.shape
    bm = EXPERT_ROWS
    n_blocks = n_rows // bm
    grid_spec = pltpu.PrefetchScalarGridSpec(
        num_scalar_prefetch=2,
        grid=(n_blocks,),
        in_specs=[
            pl.BlockSpec((bm, half), lambda i, be, nu: (jnp.maximum(jnp.minimum(i, nu[0] - 1), 0), 0)),
            pl.BlockSpec((1, d, h2), lambda i, be, nu: (be[i], 0, 0)),
            pl.BlockSpec((1, h2 // 2, d), lambda i, be, nu: (be[i], 0, 0)),
        ],
        out_specs=pl.BlockSpec((bm * d // LANES, LANES), lambda i, be, nu: (i, 0)),
        scratch_shapes=[pltpu.VMEM((d, h2), BF16),
                        pltpu.VMEM((h2 // 2, d), BF16)],
    )
    return pl.pallas_call(
        _expert_kernel,
        grid_spec=grid_spec,
        out_shape=jax.ShapeDtypeStruct((n_rows * d // LANES, LANES), F32),
        compiler_params=_cparams(("arbitrary",)),
        name="experts",
    )(block_e, n_used, xs, w13, w2)


def _final_kernel(dc_ref, dn_ref, y_hbm, w_ref, x_ref, mod_ref, gpre_ref, gpost_ref, w13s_ref,
                  w2s_ref, o_ref, buf, sem):
    i = pl.program_id(0)
    slot = i % 2
    tm = x_ref.shape[0]

    def issue(d_ref, s):
        _gather_rows(d_ref, tm * TOP_K, y_hbm,
                     lambda q, u: buf.at[s, u, _tile_rows(q), :], sem.at[s])

    @pl.when(i == 0)
    def _():
        issue(dc_ref, 0)

    @pl.when(i + 1 < pl.num_programs(0))
    def _():
        issue(dn_ref, 1 - slot)

    for k in range(TOP_K):
        pltpu.make_async_copy(y_hbm.at[pl.ds(0, tm * ROW_TILE), :], buf.at[slot, k],
                              sem.at[slot]).wait()

    w = w_ref[...]
    x1 = x_ref[...]
    routed = None
    for k in range(TOP_K):
        yk = _from_tiled_rows(lambda a, n, st: buf[slot, k, pl.ds(a, n, stride=st), :], tm, x1.shape[1])
        routed = yk * w[:, k:k + 1] if routed is None else routed + yk * w[:, k:k + 1]

    m = mod_ref[0]
    h = (_rms(x1, gpre_ref[...]) * (1.0 + m[4:5]) + m[3:4]).astype(BF16)
    ag = _dot(h, w13s_ref[...])
    hh = ag.shape[1] // 2
    act = (_silu(ag[:, :hh]) * ag[:, hh:]).astype(BF16)
    ffn = routed + _dot(act, w2s_ref[...])
    o_ref[...] = x1 + m[5:6] * _rms(ffn, gpost_ref[...])


def _final(dest, y_sorted, top_w, x1, mod, gpre, gpost, w13s, w2s, seq):
    t, d = x1.shape
    tm = COMBINE_TILE
    per_b = seq // tm
    n_steps = t // tm
    dest3 = dest.reshape(n_steps, 1, tm * TOP_K)
    full = lambda a: pl.BlockSpec(a.shape, lambda i: (0,) * a.ndim)
    row = lambda n: pl.BlockSpec((tm, n), lambda i: (i, 0))
    return pl.pallas_call(
        _final_kernel,
        grid=(n_steps,),
        in_specs=[
            pl.BlockSpec((1, 1, tm * TOP_K), lambda i: (i, 0, 0), memory_space=pltpu.SMEM),
            pl.BlockSpec((1, 1, tm * TOP_K), lambda i: (jnp.minimum(i + 1, n_steps - 1), 0, 0),
                         memory_space=pltpu.SMEM),
            pl.BlockSpec(memory_space=pl.ANY),
            row(LANES), row(d), pl.BlockSpec((1, 6, d), lambda i: (i // per_b, 0, 0)),
            full(gpre), full(gpost), full(w13s), full(w2s)],
        out_specs=row(d),
        out_shape=jax.ShapeDtypeStruct((t, d), F32),
        scratch_shapes=[pltpu.VMEM((2, TOP_K, tm * ROW_TILE, LANES), F32),
                        pltpu.SemaphoreType.DMA((2,))],
        compiler_params=_cparams(("arbitrary",)),
        name="combine_final",
    )(dest3, dest3, y_sorted, top_w, x1, mod, gpre, gpost, w13s, w2s)


def _pad_heads(v):
    lead = v.shape[:-1]
    v = v.reshape(lead + (2, SSD_GROUPS, HEADS_PER_GROUP))
    v = jnp.pad(v, [(0, 0)] * (len(lead) + 2) + [(0, HEADS_PADDED - HEADS_PER_GROUP)])
    return v.reshape(lead + (2 * SSD_GROUPS * HEADS_PADDED,))


def _dispatch_plan(counts, n_tokens):
    bm = EXPERT_ROWS
    counts = counts.reshape(N_EXPERTS).astype(jnp.int32)
    padded = (counts + bm - 1) // bm * bm
    pad_end = jnp.cumsum(padded)
    pad_start = pad_end - padded
    n_rows = -(-n_tokens * TOP_K // bm) * bm + N_EXPERTS * bm
    n_blocks = n_rows // bm
    block_start = jnp.arange(n_blocks, dtype=jnp.int32) * bm
    block_e = jnp.minimum(jnp.sum(block_start[:, None] >= pad_end[None, :], axis=1),
                          N_EXPERTS - 1).astype(jnp.int32)
    n_used = (pad_end[-1] // bm).astype(jnp.int32).reshape(1)
    return pad_start, block_e, n_used, n_rows


def _layer(x, c, w_ada, b_ada, pre_norm_mix, post_norm_mix, pre_norm_ffn, post_norm_ffn, w_in,
           conv_w, conv_b, dt_bias_fwd, dt_bias_bwd, a_log_fwd, a_log_bwd, d_skip, ssd_norm,
           w_branch_ssd, w_branch_fourier, w_out, w_router, router_bias, w13_experts, w2_experts,
           w13_shared, w2_shared):
    bsz, seq, d = x.shape
    t = bsz * seq
    x2 = x.reshape(t, d)
    row = lambda v: v.reshape(1, -1).astype(F32)

    mod = _ada(c, w_ada, b_ada)

    i1 = SSD_INNER
    i2 = i1 + XBC_WIDTH
    i3 = i2 + 2 * SSD_HEADS
    i4 = i3 + FOURIER_WIDTH
    n_dt = 2 * SSD_GROUPS * HEADS_PADDED
    w_dtp = _pad_heads(w_in[:, i2:i3])
    wdtc = jnp.pad(w_dtp, ((0, 0), (0, LANES - n_dt))).astype(BF16)
    wdtT = w_dtp.T.astype(BF16)
    bias_p = _pad_heads(jnp.concatenate([dt_bias_fwd, dt_bias_bwd]).astype(F32))
    a_p = _pad_heads(-jnp.exp(jnp.concatenate([a_log_fwd, a_log_bwd]).astype(F32)))
    pad_row = lambda v: jnp.pad(v, (0, LANES - n_dt)).reshape(1, LANES)
    z, xbc, acsc, dtT, acsT, uf, gates = _inproj(
        x2, mod, row(pre_norm_mix), w_in[:, :i1].astype(BF16), w_in[:, i1:i2].astype(BF16),
        wdtc, wdtT, w_in[:, i3:i4].astype(BF16), w_in[:, i4:].astype(BF16),
        pad_row(bias_p), pad_row(a_p), bias_p.reshape(n_dt, 1), a_p.reshape(n_dt, 1), seq)

    xbc3 = _conv(xbc.reshape(bsz, seq, XBC_WIDTH), conv_w, conv_b)

    yf, yb = _ssd(xbc3, dtT, acsT, acsc)

    fm = _fourier(uf.reshape(bsz, seq, FOURIER_WIDTH))

    x1 = _mix(yf.reshape(t, SSD_INNER), yb.reshape(t, SSD_INNER), xbc3.reshape(t, XBC_WIDTH), z,
              fm.reshape(t, FOURIER_WIDTH), gates, x2, mod,
              row(jnp.repeat(d_skip, SSD_HEAD_DIM)), row(ssd_norm), w_branch_ssd.astype(BF16),
              w_branch_fourier.astype(BF16), w_out.astype(BF16), row(post_norm_mix), seq)

    hp, idx_pad, w_pad, rank_pad, counts = _route(x1, mod, row(pre_norm_ffn),
                                                  w_router.astype(F32), router_bias, seq)
    pad_start, block_e, n_used, n_rows = _dispatch_plan(counts, t)
    dest = _dest(idx_pad, rank_pad, pad_start)[:, :TOP_K]
    xs = _scatter(dest, hp, n_rows)
    y_sorted = _experts(xs, block_e, n_used, w13_experts, w2_experts)
    out = _final(dest, y_sorted, w_pad, x1, mod, row(pre_norm_ffn), row(post_norm_ffn),
                 w13_shared.astype(BF16), w2_shared.astype(BF16), seq)
    return out.reshape(bsz, seq, d)


def kernel(x, c, w_ada, b_ada, pre_norm_mix, post_norm_mix, pre_norm_ffn, post_norm_ffn, w_in,
           conv_w, conv_b, dt_bias_fwd, dt_bias_bwd, a_log_fwd, a_log_bwd, d_skip, ssd_norm,
           w_branch_ssd, w_branch_fourier, w_out, w_router, router_bias, w13_experts, w2_experts,
           w13_shared, w2_shared):
    for layer in range(w_ada.shape[0]):
        x = _layer(x, c, w_ada[layer], b_ada[layer], pre_norm_mix[layer], post_norm_mix[layer],
                   pre_norm_ffn[layer], post_norm_ffn[layer], w_in[layer], conv_w[layer],
                   conv_b[layer], dt_bias_fwd[layer], dt_bias_bwd[layer], a_log_fwd[layer],
                   a_log_bwd[layer], d_skip[layer], ssd_norm[layer], w_branch_ssd[layer],
                   w_branch_fourier[layer], w_out[layer], w_router[layer], router_bias[layer],
                   w13_experts[layer], w2_experts[layer], w13_shared[layer], w2_shared[layer])
    return x
```

```python
import functools
import math

import numpy as np
import jax
import jax.numpy as jnp
from jax import lax
from jax.experimental import pallas as pl
from jax.experimental.pallas import tpu as pltpu

F32 = jnp.float32
BF16 = jnp.bfloat16
HIGHEST = lax.Precision.HIGHEST

D_MODEL = 1024
SSD_HEADS = 24
SSD_HEAD_DIM = 64
SSD_INNER = SSD_HEADS * SSD_HEAD_DIM
SSD_GROUPS = 4
HEADS_PER_GROUP = SSD_HEADS // SSD_GROUPS
HEADS_PADDED = 8
SSD_STATE = 128
SSD_CONV = 5
SSD_CHUNK = 128
XBC_WIDTH = SSD_INNER + 2 * SSD_GROUPS * SSD_STATE
GROUP_X = HEADS_PER_GROUP * SSD_HEAD_DIM
FOURIER_WIDTH = 512
FOURIER_GROUP_DIM = 128
N_EXPERTS = 256
TOP_K = 8
N_EXPERT_GROUPS = 8
TOPK_GROUPS = 4
EXPERT_HIDDEN = 256
SHARED_HIDDEN = 256
ROUTED_SCALE = 2.5
RMS_EPS = 1e-6

LANES = 128
VMEM_LIMIT = 56 * 1024 * 1024
TOKEN_TILE = 256
EXPERT_ROWS = 256
COMBINE_TILE = 128
SCATTER_TILE = 512
DEST_TILE = 1024
CONV_ROWS = 256


def _cparams(sem):
    return pltpu.CompilerParams(dimension_semantics=sem, vmem_limit_bytes=VMEM_LIMIT)


def _dot(a, b, precision=None):
    return jnp.dot(a, b, preferred_element_type=F32, precision=precision)


def _dot_nt(a, b, precision=None):
    return lax.dot_general(a, b, (((1,), (1,)), ((), ())), preferred_element_type=F32,
                           precision=precision)


def _dot_tn(a, b):
    return lax.dot_general(a, b, (((0,), (0,)), ((), ())), preferred_element_type=F32)


def _sigmoid(x):
    return 1.0 / (1.0 + jnp.exp(-x))


def _silu(x):
    return x * _sigmoid(x)


def _softplus(x):
    return jnp.maximum(x, 0.0) + jnp.log1p(jnp.exp(-jnp.abs(x)))


def _rms(x, g):
    return x * lax.rsqrt(jnp.mean(x * x, axis=-1, keepdims=True) + RMS_EPS) * g


def _ada_kernel(c_ref, w_ref, b_ref, o_ref):
    o_ref[...] = _dot(_silu(c_ref[...]), w_ref[...], HIGHEST) + b_ref[...]


def _ada(c, w_ada, b_ada):
    bsz, d = c.shape
    rows = 8
    cp = jnp.zeros((rows, d), F32).at[:bsz].set(c)
    n = w_ada.shape[1]
    tn = 1536
    out = pl.pallas_call(
        _ada_kernel,
        grid=(n // tn,),
        in_specs=[pl.BlockSpec((rows, d), lambda j: (0, 0)),
                  pl.BlockSpec((d, tn), lambda j: (0, j)),
                  pl.BlockSpec((1, tn), lambda j: (0, j))],
        out_specs=pl.BlockSpec((rows, tn), lambda j: (0, j)),
        out_shape=jax.ShapeDtypeStruct((rows, n), F32),
        compiler_params=_cparams(("arbitrary",)),
        name="adaln",
    )(cp, w_ada, b_ada.reshape(1, n))
    return out[:bsz].reshape(bsz, 6, d)


def _split3(a):
    a1 = a.astype(BF16)
    r1 = a - a1.astype(F32)
    a2 = r1.astype(BF16)
    a3 = (r1 - a2.astype(F32)).astype(BF16)
    return a1, a2, a3


def _inproj_kernel(x_ref, mod_ref, g_ref, wz_ref, wxbc_ref, wdtc_ref, wdtT_ref, wuf_ref, wg_ref,
                   bias_row_ref, a_row_ref, bias_col_ref, a_col_ref,
                   z_ref, xbc_ref, acsc_ref, dtT_ref, acsT_ref, uf_ref, gates_ref):
    m = mod_ref[0]
    h = _rms(x_ref[...], g_ref[...]) * (1.0 + m[1:2]) + m[0:1]
    hb = h.astype(BF16)
    z_ref[...] = _dot(hb, wz_ref[...])
    xbc_ref[...] = _dot(hb, wxbc_ref[...])
    uf_ref[...] = _dot(hb, wuf_ref[...])
    gates_ref[...] = _dot(hb, wg_ref[...])

    tm = hb.shape[0]
    n_fwd = SSD_GROUPS * HEADS_PADDED
    ii = lax.broadcasted_iota(jnp.int32, (tm, tm), 0)
    jj = lax.broadcasted_iota(jnp.int32, (tm, tm), 1)
    same = (ii // SSD_CHUNK) == (jj // SSD_CHUNK)
    tri_f = (same & (jj <= ii)).astype(BF16)
    tri_b = (same & (jj >= ii)).astype(BF16)

    dt_c = _softplus(_dot(hb, wdtc_ref[...]) + bias_row_ref[...])
    pieces = _split3(dt_c * a_row_ref[...])
    acs_f = sum(_dot(tri_f, p) for p in pieces)
    acs_b = sum(_dot(tri_b, p) for p in pieces)
    lane = lax.broadcasted_iota(jnp.int32, acs_f.shape, 1)
    acsc_ref[...] = jnp.where(lane < n_fwd, acs_f, acs_b)

    dt_t = _softplus(_dot_nt(wdtT_ref[...], hb) + bias_col_ref[...])
    pieces = _split3(dt_t * a_col_ref[...])
    acs_f = sum(_dot_nt(p, tri_f) for p in pieces)
    acs_b = sum(_dot_nt(p, tri_b) for p in pieces)
    sub = lax.broadcasted_iota(jnp.int32, acs_f.shape, 0)
    dtT_ref[...] = dt_t
    acsT_ref[...] = jnp.where(sub < n_fwd, acs_f, acs_b)


def _inproj(x2, mod, g, wz, wxbc, wdtc, wdtT, wuf, wg, bias_row, a_row, bias_col, a_col, seq):
    t, d = x2.shape
    tm = TOKEN_TILE
    per_b = seq // tm
    full = lambda a: pl.BlockSpec(a.shape, lambda i: (0,) * a.ndim)
    row = lambda n: pl.BlockSpec((tm, n), lambda i: (i, 0))
    nd = wdtT.shape[0]
    colspec = pl.BlockSpec((nd, tm), lambda i: (0, i))
    return pl.pallas_call(
        _inproj_kernel,
        grid=(t // tm,),
        in_specs=[row(d), pl.BlockSpec((1, 6, d), lambda i: (i // per_b, 0, 0)), full(g),
                  full(wz), full(wxbc), full(wdtc), full(wdtT), full(wuf), full(wg),
                  full(bias_row), full(a_row), full(bias_col), full(a_col)],
        out_specs=[row(wz.shape[1]), row(wxbc.shape[1]), row(LANES), colspec, colspec,
                   row(wuf.shape[1]), row(wg.shape[1])],
        out_shape=[jax.ShapeDtypeStruct((t, wz.shape[1]), F32),
                   jax.ShapeDtypeStruct((t, wxbc.shape[1]), F32),
                   jax.ShapeDtypeStruct((t, LANES), F32),
                   jax.ShapeDtypeStruct((nd, t), F32),
                   jax.ShapeDtypeStruct((nd, t), F32),
                   jax.ShapeDtypeStruct((t, wuf.shape[1]), F32),
                   jax.ShapeDtypeStruct((t, wg.shape[1]), F32)],
        compiler_params=_cparams(("arbitrary",)),
        name="inproj",
    )(x2, mod, g, wz, wxbc, wdtc, wdtT, wuf, wg, bias_row, a_row, bias_col, a_col)


def _conv_kernel(u_ref, w_ref, b_ref, o_ref, pad_ref):
    s = u_ref.shape[1]
    halo = 8
    pad_ref[0:halo, :] = jnp.zeros((halo, LANES), F32)
    pad_ref[halo + s:2 * halo + s, :] = jnp.zeros((halo, LANES), F32)
    pad_ref[halo:halo + s, :] = u_ref[0]
    w = w_ref[...]
    b = b_ref[...]
    half = (SSD_CONV - 1) // 2
    for r in range(s // CONV_ROWS):
        base = r * CONV_ROWS
        acc = b
        for k in range(SSD_CONV):
            lo = base + halo + k - half
            acc = acc + w[k:k + 1, :] * pad_ref[lo:lo + CONV_ROWS, :]
        o_ref[0, base:base + CONV_ROWS, :] = _silu(acc)


def _conv(xbc3, conv_w, conv_b):
    bsz, s, c = xbc3.shape
    return pl.pallas_call(
        _conv_kernel,
        grid=(bsz, c // LANES),
        in_specs=[pl.BlockSpec((1, s, LANES), lambda b, j: (b, 0, j)),
                  pl.BlockSpec((SSD_CONV, LANES), lambda b, j: (0, j)),
                  pl.BlockSpec((1, LANES), lambda b, j: (0, j))],
        out_specs=pl.BlockSpec((1, s, LANES), lambda b, j: (b, 0, j)),
        out_shape=jax.ShapeDtypeStruct((bsz, s, c), F32),
        scratch_shapes=[pltpu.VMEM((s + 16, LANES), F32)],
        compiler_params=_cparams(("arbitrary", "arbitrary")),
        name="conv",
    )(xbc3, conv_w, conv_b.reshape(1, c))


HEAD_PAIRS = HEADS_PER_GROUP // 2


def _ssd_direction(x, bm, cm, dt_r, acs_r, acsc_all, lane_off, s_ref, reverse):
    L, N = bm.shape
    assert L == N == LANES
    ii = lax.broadcasted_iota(jnp.int32, (L, L), 0)
    jj = lax.broadcasted_iota(jnp.int32, (L, L), 1)
    mask = (jj >= ii) if reverse else (jj <= ii)
    lo_half = jj < SSD_HEAD_DIM
    shift = jnp.where(lane_off == 0, 0, LANES - lane_off)
    acs_c = pltpu.roll(acsc_all, shift, 1)
    last = 0 if reverse else L - 1
    tot_r = acs_r[:, last:last + 1]
    w_r = jnp.exp(tot_r - acs_r) * dt_r
    etot = jnp.broadcast_to(jnp.exp(tot_r), (HEADS_PADDED, LANES))

    cbt = _dot_nt(cm.astype(BF16), bm.astype(BF16))
    bt = bm.T
    ys = []
    for q in range(HEAD_PAIRS):
        xq = x[:, q * LANES:(q + 1) * LANES]
        sq = s_ref[q]
        x_a = jnp.where(lo_half, xq, 0.0).astype(BF16)
        x_b = jnp.where(lo_half, 0.0, xq).astype(BF16)
        s_a = jnp.where(lo_half, sq, 0.0).astype(BF16)
        s_b = jnp.where(lo_half, 0.0, sq).astype(BF16)
        m_parts, c_parts, b_parts = [], [], []
        for h in (2 * q, 2 * q + 1):
            col = jnp.broadcast_to(acs_c[:, h:h + 1], (L, L))
            decay = jnp.exp(jnp.where(mask, col - acs_r[h:h + 1, :], -jnp.inf))
            m_parts.append((cbt * decay * dt_r[h:h + 1, :]).astype(BF16))
            c_parts.append((cm * jnp.exp(col)).astype(BF16))
            b_parts.append((bt * w_r[h:h + 1, :]).astype(BF16))
        x_diag = jnp.concatenate([x_a, x_b], axis=0)
        lhs = jnp.concatenate(m_parts + c_parts, axis=1)
        rhs = jnp.concatenate([x_diag, s_a, s_b], axis=0)
        ys.append(_dot(lhs, rhs))
        dec = jnp.where(lo_half[0:1], etot[2 * q:2 * q + 1], etot[2 * q + 1:2 * q + 2])
        s_ref[q] = sq * dec + _dot(jnp.concatenate(b_parts, axis=1), x_diag)
    return jnp.concatenate(ys, axis=1)


def _ssd_kernel(xf_ref, bf_ref, cf_ref, dtTf_ref, acsTf_ref, acscf_ref,
                xb_ref, bb_ref, cb_ref, dtTb_ref, acsTb_ref, acscb_ref,
                yf_ref, yb_ref, sf_ref, sb_ref):
    g = pl.program_id(1)

    @pl.when(pl.program_id(2) == 0)
    def _():
        sf_ref[...] = jnp.zeros_like(sf_ref)
        sb_ref[...] = jnp.zeros_like(sb_ref)

    yf_ref[0] = _ssd_direction(xf_ref[0], bf_ref[0], cf_ref[0], dtTf_ref[...], acsTf_ref[...],
                               acscf_ref[...], g * HEADS_PADDED, sf_ref, False)
    yb_ref[0] = _ssd_direction(xb_ref[0], bb_ref[0], cb_ref[0], dtTb_ref[...], acsTb_ref[...],
                               acscb_ref[...], (SSD_GROUPS + g) * HEADS_PADDED, sb_ref, True)


def _ssd(xbc3, dtT, acsT, acsc):
    bsz, s, _ = xbc3.shape
    L = SSD_CHUNK
    nc = s // L
    G = SSD_GROUPS
    nb = SSD_INNER // SSD_STATE
    ncb = nb + G
    fwd = lambda c: c
    bwd = lambda c: nc - 1 - c

    def specs(cidx, dirn):
        rowspec = pl.BlockSpec((HEADS_PADDED, L), lambda b, g, c: (dirn * G + g, b * nc + cidx(c)))
        return [
            pl.BlockSpec((1, L, GROUP_X), lambda b, g, c: (b, cidx(c), g)),
            pl.BlockSpec((1, L, SSD_STATE), lambda b, g, c: (b, cidx(c), nb + g)),
            pl.BlockSpec((1, L, SSD_STATE), lambda b, g, c: (b, cidx(c), ncb + g)),
            rowspec, rowspec,
            pl.BlockSpec((L, LANES), lambda b, g, c: (b * nc + cidx(c), 0)),
        ]

    out_specs = [pl.BlockSpec((1, L, GROUP_X), lambda b, g, c: (b, c, g)),
                 pl.BlockSpec((1, L, GROUP_X), lambda b, g, c: (b, nc - 1 - c, g))]
    return pl.pallas_call(
        _ssd_kernel,
        grid=(bsz, G, nc),
        in_specs=specs(fwd, 0) + specs(bwd, 1),
        out_specs=out_specs,
        out_shape=[jax.ShapeDtypeStruct((bsz, s, SSD_INNER), F32)] * 2,
        scratch_shapes=[pltpu.VMEM((HEAD_PAIRS, SSD_STATE, LANES), F32)] * 2,
        compiler_params=_cparams(("arbitrary", "arbitrary", "arbitrary")),
        name="ssd",
    )(xbc3, xbc3, xbc3, dtT, acsT, acsc, xbc3, xbc3, xbc3, dtT, acsT, acsc)


def _dft_tables(seq):
    n2n = LANES
    n1n = seq // n2n
    n1 = np.arange(n1n)
    k1 = np.arange(n1n)
    n2 = np.arange(n2n)
    ang = -2.0 * np.pi * (n2[:, None, None] * k1[None, :, None] / seq
                          + n1[None, None, :] * k1[None, :, None] / n1n)
    f1 = np.concatenate([np.cos(ang), np.sin(ang)], axis=1)
    k2 = np.arange(n2n)
    a2 = 2.0 * np.pi * np.outer(k2, n2) / n2n
    c2, s2 = np.cos(a2), np.sin(a2)
    g = np.block([[c2, s2], [-s2, c2]])
    ch = np.arange(FOURIER_GROUP_DIM)
    ac = 2.0 * np.pi * np.outer(ch, ch) / FOURIER_GROUP_DIM
    scale = 1.0 / math.sqrt(seq * FOURIER_GROUP_DIM)
    fc = np.concatenate([np.cos(ac), np.sin(ac)], axis=0) * scale
    return (jnp.asarray(f1, BF16), jnp.asarray(g, BF16), jnp.asarray(fc, BF16))


DFT_UNROLL = 4


def _dft_pitch(n1n):
    return 2 * n1n + 8


def _fourier_kernel(u_ref, f1_ref, g_ref, fc_ref, o_ref, a_ref):
    n2n = LANES
    n1n = u_ref.shape[2]
    pitch = _dft_pitch(n1n)

    def stage1(i, carry):
        for u in range(DFT_UNROLL):
            n2 = i * DFT_UNROLL + u
            xs = u_ref[0, 0, :, pl.ds(pl.multiple_of(n2 * LANES, LANES), LANES)].astype(BF16)
            a_ref[pl.ds(pl.multiple_of(n2 * pitch, 8), 2 * n1n), :] = _dot(f1_ref[n2], xs)
        return carry

    lax.fori_loop(0, n2n // DFT_UNROLL, stage1, 0)
    gm = g_ref[...]
    fc = fc_ref[...]

    def stage2(i, carry):
        for u in range(DFT_UNROLL):
            k1 = i * DFT_UNROLL + u
            re = a_ref[pl.ds(k1, n2n, stride=pitch), :]
            im = a_ref[pl.ds(n1n + k1, n2n, stride=pitch), :]
            a = jnp.concatenate([re, im], axis=0).astype(BF16)
            z = _dot(gm, a)
            zz = jnp.concatenate([z[:n2n], z[n2n:]], axis=1).astype(BF16)
            o_ref[0, pl.ds(k1, n2n, stride=n1n), :] = _dot(zz, fc)
        return carry

    lax.fori_loop(0, n1n // DFT_UNROLL, stage2, 0)


def _fourier(uf3):
    bsz, s, w = uf3.shape
    f1, g, fc = _dft_tables(s)
    n1n = s // LANES
    ng = w // FOURIER_GROUP_DIM
    u = uf3.reshape(bsz, n1n, LANES, ng, FOURIER_GROUP_DIM).transpose(0, 3, 1, 2, 4)
    u = u.reshape(bsz, ng, n1n, LANES * FOURIER_GROUP_DIM)
    return pl.pallas_call(
        _fourier_kernel,
        grid=(bsz, ng),
        in_specs=[pl.BlockSpec((1, 1, n1n, LANES * FOURIER_GROUP_DIM), lambda b, j: (b, j, 0, 0)),
                  pl.BlockSpec(f1.shape, lambda b, j: (0, 0, 0)),
                  pl.BlockSpec(g.shape, lambda b, j: (0, 0)),
                  pl.BlockSpec(fc.shape, lambda b, j: (0, 0))],
        out_specs=pl.BlockSpec((1, s, LANES), lambda b, j: (b, 0, j)),
        out_shape=jax.ShapeDtypeStruct((bsz, s, w), F32),
        scratch_shapes=[pltpu.VMEM((LANES * _dft_pitch(n1n), LANES), F32)],
        compiler_params=_cparams(("arbitrary", "arbitrary")),
        name="fourier",
    )(u, f1, g, fc)


def _mix_kernel(yf_ref, yb_ref, xs_ref, z_ref, fm_ref, gates_ref, x_ref, mod_ref, dskip_ref,
                ssdn_ref, wbs_ref, wbf_ref, wout_ref, gpost_ref, o_ref):
    m = mod_ref[0]
    y = yf_ref[...] + yb_ref[...] + dskip_ref[...] * xs_ref[...]
    v = y * _silu(z_ref[...])
    parts = []
    for g in range(SSD_GROUPS):
        vg = v[:, g * GROUP_X:(g + 1) * GROUP_X]
        parts.append(vg * lax.rsqrt(jnp.mean(vg * vg, axis=-1, keepdims=True) + RMS_EPS))
    vn = jnp.concatenate(parts, axis=1) * ssdn_ref[...]
    y_ssd = _dot(vn.astype(BF16), wbs_ref[...])
    y_fou = _dot(fm_ref[...].astype(BF16), wbf_ref[...])
    gt = _sigmoid(gates_ref[...])
    d = y_ssd.shape[1]
    mixed = gt[:, :d] * y_ssd + gt[:, d:] * y_fou
    mo = _dot(mixed.astype(BF16), wout_ref[...])
    o_ref[...] = x_ref[...] + m[2:3] * _rms(mo, gpost_ref[...])


def _mix(yf, yb, xbc, z, fm, gates, x2, mod, dskip, ssdn, wbs, wbf, wout, gpost, seq):
    t, d = x2.shape
    tm = TOKEN_TILE
    per_b = seq // tm
    full = lambda a: pl.BlockSpec(a.shape, lambda i: (0,) * a.ndim)
    row = lambda n: pl.BlockSpec((tm, n), lambda i: (i, 0))
    return pl.pallas_call(
        _mix_kernel,
        grid=(t // tm,),
        in_specs=[row(SSD_INNER), row(SSD_INNER), row(SSD_INNER), row(SSD_INNER),
                  row(FOURIER_WIDTH), row(2 * d), row(d),
                  pl.BlockSpec((1, 6, d), lambda i: (i // per_b, 0, 0)),
                  full(dskip), full(ssdn), full(wbs), full(wbf), full(wout), full(gpost)],
        out_specs=row(d),
        out_shape=jax.ShapeDtypeStruct((t, d), F32),
        compiler_params=_cparams(("arbitrary",)),
        name="mix",
    )(yf, yb, xbc, z, fm, gates, x2, mod, dskip, ssdn, wbs, wbf, wout, gpost)


def _route_kernel(x_ref, mod_ref, gpre_ref, wr_ref, rb_ref, hp_ref, idx_ref, w_ref, rank_ref,
                  counts_ref, run_ref):
    @pl.when(pl.program_id(0) == 0)
    def _():
        run_ref[...] = jnp.zeros_like(run_ref)

    m = mod_ref[0]
    h = _rms(x_ref[...], gpre_ref[...]) * (1.0 + m[4:5]) + m[3:4]
    tm, d = h.shape
    half = d // 2
    bits = pltpu.bitcast(h.astype(BF16).astype(F32), jnp.uint32)
    hp_ref[...] = (bits[:, :half] >> 16) | (bits[:, half:] & jnp.uint32(0xFFFF0000))

    h_hi = h.astype(BF16)
    h_lo = (h - h_hi.astype(F32)).astype(BF16)
    w_hi = wr_ref[0]
    logits = _dot(h_hi, w_hi) + (_dot(h_hi, wr_ref[1]) + _dot(h_lo, w_hi))
    scores = _sigmoid(logits)
    biased = scores + rb_ref[...]
    ne = scores.shape[1]
    per_group = ne // N_EXPERT_GROUPS
    lane = lax.broadcasted_iota(jnp.int32, (tm, ne), 1)
    lane_f = lane.astype(F32)
    grp = lane // per_group
    out_lane = lax.broadcasted_iota(jnp.int32, (tm, LANES), 1)
    neg = -jnp.inf

    def argmax_first(v):
        mx = jnp.max(v, axis=-1, keepdims=True)
        ix = jnp.min(jnp.where(v == mx, lane_f, float(ne)), axis=-1, keepdims=True)
        return mx, ix

    gs = []
    gmat = jnp.full((tm, LANES), neg, F32)
    for g in range(N_EXPERT_GROUPS):
        vg = jnp.where(grp == g, biased, neg)
        m1, i1 = argmax_first(vg)
        m2 = jnp.max(jnp.where(lane_f == i1, neg, vg), axis=-1, keepdims=True)
        gs.append(m1 + m2)
        gmat = jnp.where(out_lane == g, gs[g], gmat)
    ahead = jnp.zeros((tm, LANES), F32)
    for o in range(N_EXPERT_GROUPS):
        beats = (gs[o] > gmat) | ((gs[o] == gmat) & (out_lane > o))
        ahead = ahead + beats.astype(F32)
    sel = ((ahead < TOPK_GROUPS) & (out_lane < N_EXPERT_GROUPS)).astype(BF16)
    eg = lax.broadcasted_iota(jnp.int32, (LANES, ne), 0)
    ee = lax.broadcasted_iota(jnp.int32, (LANES, ne), 1)
    allowed = _dot(sel, (eg == ee // per_group).astype(BF16)) > 0.5
    masked = jnp.where(allowed, biased, neg)

    idx_out = jnp.zeros((tm, LANES), F32)
    w_out = jnp.zeros((tm, LANES), F32)
    w_sum = jnp.zeros((tm, 1), F32)
    hits = []
    for k in range(TOP_K):
        _, ik = argmax_first(masked)
        hit = lane_f == ik
        hits.append(hit)
        wk = jnp.sum(jnp.where(hit, scores, 0.0), axis=-1, keepdims=True)
        masked = jnp.where(hit, neg, masked)
        idx_out = jnp.where(out_lane == k, ik, idx_out)
        w_out = jnp.where(out_lane == k, wk, w_out)
        w_sum = w_sum + wk
    idx_ref[...] = idx_out.astype(jnp.int32)
    w_ref[...] = w_out / w_sum * ROUTED_SCALE

    chosen = functools.reduce(jnp.logical_or, hits).astype(F32)
    ti = lax.broadcasted_iota(jnp.int32, (tm, tm), 0)
    tj = lax.broadcasted_iota(jnp.int32, (tm, tm), 1)
    before = _dot((tj < ti).astype(BF16), chosen.astype(BF16)) + run_ref[...]
    rank_out = jnp.zeros((tm, LANES), jnp.int32)
    for k in range(TOP_K):
        rk = jnp.sum(jnp.where(hits[k], before, 0.0), axis=-1, keepdims=True)
        rank_out = jnp.where(out_lane == k, rk.astype(jnp.int32), rank_out)
    rank_ref[...] = rank_out
    run_ref[...] = run_ref[...] + jnp.sum(chosen, axis=0, keepdims=True)
    counts_ref[...] = run_ref[...]


def _route(x1, mod, gpre, w_router, router_bias, seq):
    t, d = x1.shape
    tm = TOKEN_TILE
    per_b = seq // tm
    full = lambda a: pl.BlockSpec(a.shape, lambda i: (0,) * a.ndim)
    row = lambda n: pl.BlockSpec((tm, n), lambda i: (i, 0))
    rb = router_bias.reshape(1, -1)
    w_hi = w_router.astype(BF16)
    w_router = jnp.stack([w_hi, (w_router - w_hi.astype(F32)).astype(BF16)])
    return pl.pallas_call(
        _route_kernel,
        grid=(t // tm,),
        in_specs=[row(d), pl.BlockSpec((1, 6, d), lambda i: (i // per_b, 0, 0)), full(gpre),
                  full(w_router), full(rb)],
        out_specs=[row(d // 2), row(LANES), row(LANES), row(LANES),
                   pl.BlockSpec((1, N_EXPERTS), lambda i: (0, 0))],
        out_shape=[jax.ShapeDtypeStruct((t, d // 2), jnp.uint32),
                   jax.ShapeDtypeStruct((t, LANES), jnp.int32),
                   jax.ShapeDtypeStruct((t, LANES), F32),
                   jax.ShapeDtypeStruct((t, LANES), jnp.int32),
                   jax.ShapeDtypeStruct((1, N_EXPERTS), F32)],
        scratch_shapes=[pltpu.VMEM((1, N_EXPERTS), F32)],
        compiler_params=_cparams(("arbitrary",)),
        name="route",
    )(x1, mod, gpre, w_router, rb)


GATHER_UNROLL = 8


ROW_TILE = 8


def _tile_rows(r):
    return pl.ds(pl.multiple_of(r * ROW_TILE, ROW_TILE), ROW_TILE)


def _gather_rows(idx_ref, n_rows, src_hbm, dst_of, sem):
    def body(q, carry):
        for u in range(GATHER_UNROLL):
            src = idx_ref[0, 0, q * GATHER_UNROLL + u]
            pltpu.make_async_copy(src_hbm.at[_tile_rows(src), :], dst_of(q, u), sem).start(
                priority=u % 2)
        return carry

    lax.fori_loop(0, n_rows // GATHER_UNROLL, body, 0)


def _to_tiled_rows(o_ref, y):
    rows, width = y.shape
    for r in range(rows // ROW_TILE):
        for j in range(width // LANES):
            o_ref[pl.ds(r * ROW_TILE * ROW_TILE + j, ROW_TILE, stride=ROW_TILE), :] = (
                y[r * ROW_TILE:(r + 1) * ROW_TILE, j * LANES:(j + 1) * LANES])


def _from_tiled_rows(load, rows, width):
    cols = []
    for j in range(width // LANES):
        pieces = [load(r * ROW_TILE * ROW_TILE + j, ROW_TILE, ROW_TILE) for r in range(rows // ROW_TILE)]
        cols.append(jnp.concatenate(pieces, axis=0))
    return jnp.concatenate(cols, axis=1)


def _dest_kernel(idx_ref, rank_ref, start_ref, o_ref):
    idx = idx_ref[...]
    rank = rank_ref[...]
    start = start_ref[...]
    tm = idx.shape[0]
    lane = lax.broadcasted_iota(jnp.int32, (tm, start.shape[1]), 1)
    out_lane = lax.broadcasted_iota(jnp.int32, (tm, LANES), 1)
    out = jnp.zeros((tm, LANES), jnp.int32)
    for k in range(TOP_K):
        base = jnp.sum(jnp.where(lane == idx[:, k:k + 1], start, 0.0), axis=-1, keepdims=True)
        out = jnp.where(out_lane == k, base.astype(jnp.int32) + rank[:, k:k + 1], out)
    o_ref[...] = out


def _dest(idx_pad, rank_pad, pad_start):
    t = idx_pad.shape[0]
    tm = DEST_TILE
    row = pl.BlockSpec((tm, LANES), lambda i: (i, 0))
    return pl.pallas_call(
        _dest_kernel,
        grid=(t // tm,),
        in_specs=[row, row, pl.BlockSpec((1, N_EXPERTS), lambda i: (0, 0))],
        out_specs=row,
        out_shape=jax.ShapeDtypeStruct((t, LANES), jnp.int32),
        compiler_params=_cparams(("arbitrary",)),
        name="dest",
    )(idx_pad, rank_pad, pad_start.astype(F32).reshape(1, N_EXPERTS))


def _scatter_kernel(b0_ref, nb_ref, nu_ref, dest_ref, hp_ref, xs_hbm, zbuf, sem, zsem):
    ts = hp_ref.shape[0]
    bm = zbuf.shape[0]
    n_blocks = xs_hbm.shape[0] // bm

    @pl.when(pl.program_id(0) == 0)
    def _():
        zbuf[...] = jnp.zeros_like(zbuf)
        n_used = nu_ref[0]

        def zero_block(g):
            return pltpu.make_async_copy(zbuf, xs_hbm.at[pl.ds(pl.multiple_of(g * bm, bm), bm), :], zsem)

        def tails(action):
            def step(e, carry):
                @pl.when(nb_ref[e] > 0)
                def _():
                    action(zero_block(b0_ref[e] + nb_ref[e] - 1))
                return carry
            lax.fori_loop(0, N_EXPERTS, step, 0)

        def unused(action):
            def step(g, carry):
                action(zero_block(g))
                return carry
            lax.fori_loop(n_used, n_blocks, step, 0)

        tails(lambda c: c.start())
        unused(lambda c: c.start())
        tails(lambda c: c.wait())
        unused(lambda c: c.wait())

    def body(t, carry):
        for k in range(TOP_K):
            dst = dest_ref[0, 0, t * TOP_K + k]
            pltpu.make_async_copy(hp_ref.at[pl.ds(t, 1), :], xs_hbm.at[pl.ds(dst, 1), :], sem).start(
                priority=k % 2)
        return carry

    lax.fori_loop(0, ts, body, 0)
    for k in range(TOP_K):
        pltpu.make_async_copy(hp_ref, xs_hbm.at[pl.ds(0, ts), :], sem).wait()


def _scatter(dest, hp, first_block, n_block, n_used, n_rows):
    t, half = hp.shape
    ts = SCATTER_TILE
    n_steps = t // ts
    dest3 = dest.reshape(n_steps, 1, ts * TOP_K)
    grid_spec = pltpu.PrefetchScalarGridSpec(
        num_scalar_prefetch=3,
        grid=(n_steps,),
        in_specs=[pl.BlockSpec((1, 1, ts * TOP_K), lambda i, b0, nb, nu: (i, 0, 0),
                               memory_space=pltpu.SMEM),
                  pl.BlockSpec((ts, half), lambda i, b0, nb, nu: (i, 0))],
        out_specs=pl.BlockSpec(memory_space=pl.ANY),
        scratch_shapes=[pltpu.VMEM((EXPERT_ROWS, half), jnp.uint32),
                        pltpu.SemaphoreType.DMA(()), pltpu.SemaphoreType.DMA(())],
    )
    return pl.pallas_call(
        _scatter_kernel,
        grid_spec=grid_spec,
        out_shape=jax.ShapeDtypeStruct((n_rows, half), jnp.uint32),
        compiler_params=_cparams(("arbitrary",)),
        name="scatter",
    )(first_block, n_block, n_used, dest3, hp)


def _expert_kernel(b0_ref, nb_ref, nu_ref, xs_hbm, w13_ref, w2_ref, y_hbm,
                   xbuf, ybuf, w13b, w2b, sem_in, sem_out):
    e = pl.program_id(0)
    n_used = nu_ref[0]
    b0 = b0_ref[e]
    nb = nb_ref[e]
    bm = xbuf.shape[1]
    y_rows = ybuf.shape[1]

    def in_copy(g, slot):
        return pltpu.make_async_copy(xs_hbm.at[pl.ds(pl.multiple_of(g * bm, bm), bm), :],
                                     xbuf.at[slot], sem_in.at[slot])

    def out_copy(g, slot):
        return pltpu.make_async_copy(ybuf.at[slot],
                                     y_hbm.at[pl.ds(pl.multiple_of(g * y_rows, y_rows), y_rows), :],
                                     sem_out.at[slot])

    @pl.when(nb > 0)
    def _():
        w13b[...] = w13_ref[0].astype(BF16)
        w2b[...] = w2_ref[0].astype(BF16)

    @pl.when((nb > 0) & (b0 == 0))
    def _():
        in_copy(0, 0).start()

    def block(g, carry):
        slot = g % 2
        in_copy(g, slot).wait()

        @pl.when(g + 1 < n_used)
        def _():
            in_copy(g + 1, 1 - slot).start()

        @pl.when(g >= 2)
        def _():
            out_copy(g - 2, slot).wait()

        w = xbuf[slot]
        lo = pltpu.bitcast(w << 16, F32).astype(BF16)
        hi = pltpu.bitcast(w & jnp.uint32(0xFFFF0000), F32).astype(BF16)
        half = lo.shape[1]
        ag = _dot(lo, w13b[:half, :]) + _dot(hi, w13b[half:, :])
        hh = ag.shape[1] // 2
        act = (_silu(ag[:, :hh]) * ag[:, hh:]).astype(BF16)
        _to_tiled_rows(ybuf.at[slot], _dot(act, w2b[...]))
        out_copy(g, slot).start()
        return carry

    lax.fori_loop(b0, b0 + nb, block, 0)

    @pl.when(e == pl.num_programs(0) - 1)
    def _():
        @pl.when(n_used >= 2)
        def _():
            out_copy(n_used - 2, n_used % 2).wait()

        out_copy(n_used - 1, (n_used - 1) % 2).wait()

        n_blocks = y_hbm.shape[0] // y_rows
        ybuf[0] = jnp.zeros((y_rows, LANES), F32)

        def start_zero(g, carry):
            out_copy(g, 0).start()
            return carry

        def wait_zero(g, carry):
            out_copy(g, 0).wait()
            return carry

        lax.fori_loop(n_used, n_blocks, start_zero, 0)
        lax.fori_loop(n_used, n_blocks, wait_zero, 0)


def _experts(xs, first_block, n_block, n_used, w13, w2):
    n_rows, half = xs.shape
    ne, d, h2 = w13.shape
    bm = EXPERT_ROWS
    y_rows = bm * d // LANES
    grid_spec = pltpu.PrefetchScalarGridSpec(
        num_scalar_prefetch=3,
        grid=(ne,),
        in_specs=[
            pl.BlockSpec(memory_space=pl.ANY),
            pl.BlockSpec((1, d, h2), lambda e, b0, nb, nu: (e, 0, 0)),
            pl.BlockSpec((1, h2 // 2, d), lambda e, b0, nb, nu: (e, 0, 0)),
        ],
        out_specs=pl.BlockSpec(memory_space=pl.ANY),
        scratch_shapes=[pltpu.VMEM((2, bm, half), jnp.uint32),
                        pltpu.VMEM((2, y_rows, LANES), F32),
                        pltpu.VMEM((d, h2), BF16),
                        pltpu.VMEM((h2 // 2, d), BF16),
                        pltpu.SemaphoreType.DMA((2,)),
                        pltpu.SemaphoreType.DMA((2,))],
    )
    return pl.pallas_call(
        _expert_kernel,
        grid_spec=grid_spec,
        out_shape=jax.ShapeDtypeStruct((n_rows * d // LANES, LANES), F32),
        compiler_params=_cparams(("arbitrary",)),
        name="experts",
    )(first_block, n_block, n_used, xs, w13, w2)


def _final_kernel(dc_ref, dn_ref, y_hbm, w_ref, x_ref, mod_ref, gpre_ref, gpost_ref, w13s_ref,
                  w2s_ref, o_ref, buf, sem):
    i = pl.program_id(0)
    slot = i % 2
    tm = x_ref.shape[0]

    def issue(d_ref, s):
        _gather_rows(d_ref, tm * TOP_K, y_hbm,
                     lambda q, u: buf.at[s, u, _tile_rows(q), :], sem.at[s])

    @pl.when(i == 0)
    def _():
        issue(dc_ref, 0)

    @pl.when(i + 1 < pl.num_programs(0))
    def _():
        issue(dn_ref, 1 - slot)

    for k in range(TOP_K):
        pltpu.make_async_copy(y_hbm.at[pl.ds(0, tm * ROW_TILE), :], buf.at[slot, k],
                              sem.at[slot]).wait()

    w = w_ref[...]
    x1 = x_ref[...]
    routed = None
    for k in range(TOP_K):
        yk = _from_tiled_rows(lambda a, n, st: buf[slot, k, pl.ds(a, n, stride=st), :], tm, x1.shape[1])
        routed = yk * w[:, k:k + 1] if routed is None else routed + yk * w[:, k:k + 1]

    m = mod_ref[0]
    h = (_rms(x1, gpre_ref[...]) * (1.0 + m[4:5]) + m[3:4]).astype(BF16)
    ag = _dot(h, w13s_ref[...])
    hh = ag.shape[1] // 2
    act = (_silu(ag[:, :hh]) * ag[:, hh:]).astype(BF16)
    ffn = routed + _dot(act, w2s_ref[...])
    o_ref[...] = x1 + m[5:6] * _rms(ffn, gpost_ref[...])


def _final(dest, y_sorted, top_w, x1, mod, gpre, gpost, w13s, w2s, seq):
    t, d = x1.shape
    tm = COMBINE_TILE
    per_b = seq // tm
    n_steps = t // tm
    dest3 = dest.reshape(n_steps, 1, tm * TOP_K)
    full = lambda a: pl.BlockSpec(a.shape, lambda i: (0,) * a.ndim)
    row = lambda n: pl.BlockSpec((tm, n), lambda i: (i, 0))
    return pl.pallas_call(
        _final_kernel,
        grid=(n_steps,),
        in_specs=[
            pl.BlockSpec((1, 1, tm * TOP_K), lambda i: (i, 0, 0), memory_space=pltpu.SMEM),
            pl.BlockSpec((1, 1, tm * TOP_K), lambda i: (jnp.minimum(i + 1, n_steps - 1), 0, 0),
                         memory_space=pltpu.SMEM),
            pl.BlockSpec(memory_space=pl.ANY),
            row(LANES), row(d), pl.BlockSpec((1, 6, d), lambda i: (i // per_b, 0, 0)),
            full(gpre), full(gpost), full(w13s), full(w2s)],
        out_specs=row(d),
        out_shape=jax.ShapeDtypeStruct((t, d), F32),
        scratch_shapes=[pltpu.VMEM((2, TOP_K, tm * ROW_TILE, LANES), F32),
                        pltpu.SemaphoreType.DMA((2,))],
        compiler_params=_cparams(("arbitrary",)),
        name="combine_final",
    )(dest3, dest3, y_sorted, top_w, x1, mod, gpre, gpost, w13s, w2s)


def _pad_heads(v):
    lead = v.shape[:-1]
    v = v.reshape(lead + (2, SSD_GROUPS, HEADS_PER_GROUP))
    v = jnp.pad(v, [(0, 0)] * (len(lead) + 2) + [(0, HEADS_PADDED - HEADS_PER_GROUP)])
    return v.reshape(lead + (2 * SSD_GROUPS * HEADS_PADDED,))


def _dispatch_plan(counts, n_tokens):
    bm = EXPERT_ROWS
    counts = counts.reshape(N_EXPERTS).astype(jnp.int32)
    padded = (counts + bm - 1) // bm * bm
    pad_end = jnp.cumsum(padded)
    pad_start = pad_end - padded
    n_rows = -(-n_tokens * TOP_K // bm) * bm + N_EXPERTS * bm
    first_block = (pad_start // bm).astype(jnp.int32)
    n_block = (padded // bm).astype(jnp.int32)
    n_used = (pad_end[-1] // bm).astype(jnp.int32).reshape(1)
    return pad_start, first_block, n_block, n_used, n_rows


def _layer(x, c, w_ada, b_ada, pre_norm_mix, post_norm_mix, pre_norm_ffn, post_norm_ffn, w_in,
           conv_w, conv_b, dt_bias_fwd, dt_bias_bwd, a_log_fwd, a_log_bwd, d_skip, ssd_norm,
           w_branch_ssd, w_branch_fourier, w_out, w_router, router_bias, w13_experts, w2_experts,
           w13_shared, w2_shared):
    bsz, seq, d = x.shape
    t = bsz * seq
    x2 = x.reshape(t, d)
    row = lambda v: v.reshape(1, -1).astype(F32)

    mod = _ada(c, w_ada, b_ada)

    i1 = SSD_INNER
    i2 = i1 + XBC_WIDTH
    i3 = i2 + 2 * SSD_HEADS
    i4 = i3 + FOURIER_WIDTH
    n_dt = 2 * SSD_GROUPS * HEADS_PADDED
    w_dtp = _pad_heads(w_in[:, i2:i3])
    wdtc = jnp.pad(w_dtp, ((0, 0), (0, LANES - n_dt))).astype(BF16)
    wdtT = w_dtp.T.astype(BF16)
    bias_p = _pad_heads(jnp.concatenate([dt_bias_fwd, dt_bias_bwd]).astype(F32))
    a_p = _pad_heads(-jnp.exp(jnp.concatenate([a_log_fwd, a_log_bwd]).astype(F32)))
    pad_row = lambda v: jnp.pad(v, (0, LANES - n_dt)).reshape(1, LANES)
    z, xbc, acsc, dtT, acsT, uf, gates = _inproj(
        x2, mod, row(pre_norm_mix), w_in[:, :i1].astype(BF16), w_in[:, i1:i2].astype(BF16),
        wdtc, wdtT, w_in[:, i3:i4].astype(BF16), w_in[:, i4:].astype(BF16),
        pad_row(bias_p), pad_row(a_p), bias_p.reshape(n_dt, 1), a_p.reshape(n_dt, 1), seq)

    xbc3 = _conv(xbc.reshape(bsz, seq, XBC_WIDTH), conv_w, conv_b)

    yf, yb = _ssd(xbc3, dtT, acsT, acsc)

    fm = _fourier(uf.reshape(bsz, seq, FOURIER_WIDTH))

    x1 = _mix(yf.reshape(t, SSD_INNER), yb.reshape(t, SSD_INNER), xbc3.reshape(t, XBC_WIDTH), z,
              fm.reshape(t, FOURIER_WIDTH), gates, x2, mod,
              row(jnp.repeat(d_skip, SSD_HEAD_DIM)), row(ssd_norm), w_branch_ssd.astype(BF16),
              w_branch_fourier.astype(BF16), w_out.astype(BF16), row(post_norm_mix), seq)

    hp, idx_pad, w_pad, rank_pad, counts = _route(x1, mod, row(pre_norm_ffn),
                                                  w_router.astype(F32), router_bias, seq)
    pad_start, first_block, n_block, n_used, n_rows = _dispatch_plan(counts, t)
    dest = _dest(idx_pad, rank_pad, pad_start)[:, :TOP_K]
    xs = _scatter(dest, hp, first_block, n_block, n_used, n_rows)
    y_sorted = _experts(xs, first_block, n_block, n_used, w13_experts, w2_experts)
    out = _final(dest, y_sorted, w_pad, x1, mod, row(pre_norm_ffn), row(post_norm_ffn),
                 w13_shared.astype(BF16), w2_shared.astype(BF16), seq)
    return out.reshape(bsz, seq, d)


def kernel(x, c, w_ada, b_ada, pre_norm_mix, post_norm_mix, pre_norm_ffn, post_norm_ffn, w_in,
           conv_w, conv_b, dt_bias_fwd, dt_bias_bwd, a_log_fwd, a_log_bwd, d_skip, ssd_norm,
           w_branch_ssd, w_branch_fourier, w_out, w_router, router_bias, w13_experts, w2_experts,
           w13_shared, w2_shared):
    for layer in range(w_ada.shape[0]):
        x = _layer(x, c, w_ada[layer], b_ada[layer], pre_norm_mix[layer], post_norm_mix[layer],
                   pre_norm_ffn[layer], post_norm_ffn[layer], w_in[layer], conv_w[layer],
                   conv_b[layer], dt_bias_fwd[layer], dt_bias_bwd[layer], a_log_fwd[layer],
                   a_log_bwd[layer], d_skip[layer], ssd_norm[layer], w_branch_ssd[layer],
                   w_branch_fourier[layer], w_out[layer], w_router[layer], router_bias[layer],
                   w13_experts[layer], w2_experts[layer], w13_shared[layer], w2_shared[layer])
    return x
```

```python
import functools
import math

import numpy as np
import jax
import jax.numpy as jnp
from jax import lax
from jax.experimental import pallas as pl
from jax.experimental.pallas import tpu as pltpu

F32 = jnp.float32
BF16 = jnp.bfloat16
HIGHEST = lax.Precision.HIGHEST

D_MODEL = 1024
SSD_HEADS = 24
SSD_HEAD_DIM = 64
SSD_INNER = SSD_HEADS * SSD_HEAD_DIM
SSD_GROUPS = 4
HEADS_PER_GROUP = SSD_HEADS // SSD_GROUPS
HEADS_PADDED = 8
SSD_STATE = 128
SSD_CONV = 5
SSD_CHUNK = 128
XBC_WIDTH = SSD_INNER + 2 * SSD_GROUPS * SSD_STATE
GROUP_X = HEADS_PER_GROUP * SSD_HEAD_DIM
FOURIER_WIDTH = 512
FOURIER_GROUP_DIM = 128
N_EXPERTS = 256
TOP_K = 8
N_EXPERT_GROUPS = 8
TOPK_GROUPS = 4
EXPERT_HIDDEN = 256
SHARED_HIDDEN = 256
ROUTED_SCALE = 2.5
RMS_EPS = 1e-6

LANES = 128
VMEM_LIMIT = 56 * 1024 * 1024
TOKEN_TILE = 256
EXPERT_ROWS = 256
COMBINE_TILE = 128
SCATTER_TILE = 512
DEST_TILE = 1024
EXPERT_IN_SLOTS = 4
EXPERT_OUT_SLOTS = 3
CONV_ROWS = 256


def _cparams(sem):
    return pltpu.CompilerParams(dimension_semantics=sem, vmem_limit_bytes=VMEM_LIMIT)


def _dot(a, b, precision=None):
    return jnp.dot(a, b, preferred_element_type=F32, precision=precision)


def _dot_nt(a, b, precision=None):
    return lax.dot_general(a, b, (((1,), (1,)), ((), ())), preferred_element_type=F32,
                           precision=precision)


def _dot_tn(a, b):
    return lax.dot_general(a, b, (((0,), (0,)), ((), ())), preferred_element_type=F32)


def _sigmoid(x):
    return 1.0 / (1.0 + jnp.exp(-x))


def _silu(x):
    return x * _sigmoid(x)


def _softplus(x):
    return jnp.maximum(x, 0.0) + jnp.log1p(jnp.exp(-jnp.abs(x)))


def _rms(x, g):
    return x * lax.rsqrt(jnp.mean(x * x, axis=-1, keepdims=True) + RMS_EPS) * g


def _ada_kernel(c_ref, w_ref, b_ref, o_ref):
    o_ref[...] = _dot(_silu(c_ref[...]), w_ref[...], HIGHEST) + b_ref[...]


def _ada(c, w_ada, b_ada):
    bsz, d = c.shape
    rows = 8
    cp = jnp.zeros((rows, d), F32).at[:bsz].set(c)
    n = w_ada.shape[1]
    tn = 1536
    out = pl.pallas_call(
        _ada_kernel,
        grid=(n // tn,),
        in_specs=[pl.BlockSpec((rows, d), lambda j: (0, 0)),
                  pl.BlockSpec((d, tn), lambda j: (0, j)),
                  pl.BlockSpec((1, tn), lambda j: (0, j))],
        out_specs=pl.BlockSpec((rows, tn), lambda j: (0, j)),
        out_shape=jax.ShapeDtypeStruct((rows, n), F32),
        compiler_params=_cparams(("arbitrary",)),
        name="adaln",
    )(cp, w_ada, b_ada.reshape(1, n))
    return out[:bsz].reshape(bsz, 6, d)


def _split3(a):
    a1 = a.astype(BF16)
    r1 = a - a1.astype(F32)
    a2 = r1.astype(BF16)
    a3 = (r1 - a2.astype(F32)).astype(BF16)
    return a1, a2, a3


def _inproj_kernel(x_ref, mod_ref, g_ref, wz_ref, wxbc_ref, wdtc_ref, wdtT_ref, wuf_ref, wg_ref,
                   bias_row_ref, a_row_ref, bias_col_ref, a_col_ref,
                   z_ref, xbc_ref, acsc_ref, dtT_ref, acsT_ref, uf_ref, gates_ref):
    m = mod_ref[0]
    h = _rms(x_ref[...], g_ref[...]) * (1.0 + m[1:2]) + m[0:1]
    hb = h.astype(BF16)
    z_ref[...] = _dot(hb, wz_ref[...])
    xbc_ref[...] = _dot(hb, wxbc_ref[...])
    uf_ref[...] = _dot(hb, wuf_ref[...])
    gates_ref[...] = _dot(hb, wg_ref[...])

    tm = hb.shape[0]
    n_fwd = SSD_GROUPS * HEADS_PADDED
    ii = lax.broadcasted_iota(jnp.int32, (tm, tm), 0)
    jj = lax.broadcasted_iota(jnp.int32, (tm, tm), 1)
    same = (ii // SSD_CHUNK) == (jj // SSD_CHUNK)
    tri_f = (same & (jj <= ii)).astype(BF16)
    tri_b = (same & (jj >= ii)).astype(BF16)

    dt_c = _softplus(_dot(hb, wdtc_ref[...]) + bias_row_ref[...])
    pieces = _split3(dt_c * a_row_ref[...])
    acs_f = sum(_dot(tri_f, p) for p in pieces)
    acs_b = sum(_dot(tri_b, p) for p in pieces)
    lane = lax.broadcasted_iota(jnp.int32, acs_f.shape, 1)
    acsc_ref[...] = jnp.where(lane < n_fwd, acs_f, acs_b)

    dt_t = _softplus(_dot_nt(wdtT_ref[...], hb) + bias_col_ref[...])
    pieces = _split3(dt_t * a_col_ref[...])
    acs_f = sum(_dot_nt(p, tri_f) for p in pieces)
    acs_b = sum(_dot_nt(p, tri_b) for p in pieces)
    sub = lax.broadcasted_iota(jnp.int32, acs_f.shape, 0)
    dtT_ref[...] = dt_t
    acsT_ref[...] = jnp.where(sub < n_fwd, acs_f, acs_b)


def _inproj(x2, mod, g, wz, wxbc, wdtc, wdtT, wuf, wg, bias_row, a_row, bias_col, a_col, seq):
    t, d = x2.shape
    tm = TOKEN_TILE
    per_b = seq // tm
    full = lambda a: pl.BlockSpec(a.shape, lambda i: (0,) * a.ndim)
    row = lambda n: pl.BlockSpec((tm, n), lambda i: (i, 0))
    nd = wdtT.shape[0]
    colspec = pl.BlockSpec((nd, tm), lambda i: (0, i))
    return pl.pallas_call(
        _inproj_kernel,
        grid=(t // tm,),
        in_specs=[row(d), pl.BlockSpec((1, 6, d), lambda i: (i // per_b, 0, 0)), full(g),
                  full(wz), full(wxbc), full(wdtc), full(wdtT), full(wuf), full(wg),
                  full(bias_row), full(a_row), full(bias_col), full(a_col)],
        out_specs=[row(wz.shape[1]), row(wxbc.shape[1]), row(LANES), colspec, colspec,
                   row(wuf.shape[1]), row(wg.shape[1])],
        out_shape=[jax.ShapeDtypeStruct((t, wz.shape[1]), F32),
                   jax.ShapeDtypeStruct((t, wxbc.shape[1]), F32),
                   jax.ShapeDtypeStruct((t, LANES), F32),
                   jax.ShapeDtypeStruct((nd, t), F32),
                   jax.ShapeDtypeStruct((nd, t), F32),
                   jax.ShapeDtypeStruct((t, wuf.shape[1]), F32),
                   jax.ShapeDtypeStruct((t, wg.shape[1]), F32)],
        compiler_params=_cparams(("arbitrary",)),
        name="inproj",
    )(x2, mod, g, wz, wxbc, wdtc, wdtT, wuf, wg, bias_row, a_row, bias_col, a_col)


def _conv_kernel(u_ref, w_ref, b_ref, o_ref, pad_ref):
    s = u_ref.shape[1]
    halo = 8
    pad_ref[0:halo, :] = jnp.zeros((halo, LANES), F32)
    pad_ref[halo + s:2 * halo + s, :] = jnp.zeros((halo, LANES), F32)
    pad_ref[halo:halo + s, :] = u_ref[0]
    w = w_ref[...]
    b = b_ref[...]
    half = (SSD_CONV - 1) // 2
    for r in range(s // CONV_ROWS):
        base = r * CONV_ROWS
        acc = b
        for k in range(SSD_CONV):
            lo = base + halo + k - half
            acc = acc + w[k:k + 1, :] * pad_ref[lo:lo + CONV_ROWS, :]
        o_ref[0, base:base + CONV_ROWS, :] = _silu(acc)


def _conv(xbc3, conv_w, conv_b):
    bsz, s, c = xbc3.shape
    return pl.pallas_call(
        _conv_kernel,
        grid=(bsz, c // LANES),
        in_specs=[pl.BlockSpec((1, s, LANES), lambda b, j: (b, 0, j)),
                  pl.BlockSpec((SSD_CONV, LANES), lambda b, j: (0, j)),
                  pl.BlockSpec((1, LANES), lambda b, j: (0, j))],
        out_specs=pl.BlockSpec((1, s, LANES), lambda b, j: (b, 0, j)),
        out_shape=jax.ShapeDtypeStruct((bsz, s, c), F32),
        scratch_shapes=[pltpu.VMEM((s + 16, LANES), F32)],
        compiler_params=_cparams(("arbitrary", "arbitrary")),
        name="conv",
    )(xbc3, conv_w, conv_b.reshape(1, c))


HEAD_PAIRS = HEADS_PER_GROUP // 2


def _ssd_direction(x, bm, cm, dt_r, acs_r, acsc_all, lane_off, s_ref, reverse):
    L, N = bm.shape
    assert L == N == LANES
    ii = lax.broadcasted_iota(jnp.int32, (L, L), 0)
    jj = lax.broadcasted_iota(jnp.int32, (L, L), 1)
    mask = (jj >= ii) if reverse else (jj <= ii)
    lo_half = jj < SSD_HEAD_DIM
    shift = jnp.where(lane_off == 0, 0, LANES - lane_off)
    acs_c = pltpu.roll(acsc_all, shift, 1)
    last = 0 if reverse else L - 1
    tot_r = acs_r[:, last:last + 1]
    w_r = jnp.exp(tot_r - acs_r) * dt_r
    etot = jnp.broadcast_to(jnp.exp(tot_r), (HEADS_PADDED, LANES))

    cbt = _dot_nt(cm.astype(BF16), bm.astype(BF16))
    bt = bm.T
    ys = []
    for q in range(HEAD_PAIRS):
        xq = x[:, q * LANES:(q + 1) * LANES]
        sq = s_ref[q]
        x_a = jnp.where(lo_half, xq, 0.0).astype(BF16)
        x_b = jnp.where(lo_half, 0.0, xq).astype(BF16)
        s_a = jnp.where(lo_half, sq, 0.0).astype(BF16)
        s_b = jnp.where(lo_half, 0.0, sq).astype(BF16)
        m_parts, c_parts, b_parts = [], [], []
        for h in (2 * q, 2 * q + 1):
            col = jnp.broadcast_to(acs_c[:, h:h + 1], (L, L))
            decay = jnp.exp(jnp.where(mask, col - acs_r[h:h + 1, :], -jnp.inf))
            m_parts.append((cbt * decay * dt_r[h:h + 1, :]).astype(BF16))
            c_parts.append((cm * jnp.exp(col)).astype(BF16))
            b_parts.append((bt * w_r[h:h + 1, :]).astype(BF16))
        x_diag = jnp.concatenate([x_a, x_b], axis=0)
        lhs = jnp.concatenate(m_parts + c_parts, axis=1)
        rhs = jnp.concatenate([x_diag, s_a, s_b], axis=0)
        ys.append(_dot(lhs, rhs))
        dec = jnp.where(lo_half[0:1], etot[2 * q:2 * q + 1], etot[2 * q + 1:2 * q + 2])
        s_ref[q] = sq * dec + _dot(jnp.concatenate(b_parts, axis=1), x_diag)
    return jnp.concatenate(ys, axis=1)


def _ssd_kernel(xf_ref, bf_ref, cf_ref, dtTf_ref, acsTf_ref, acscf_ref,
                xb_ref, bb_ref, cb_ref, dtTb_ref, acsTb_ref, acscb_ref,
                yf_ref, yb_ref, sf_ref, sb_ref):
    g = pl.program_id(1)

    @pl.when(pl.program_id(2) == 0)
    def _():
        sf_ref[...] = jnp.zeros_like(sf_ref)
        sb_ref[...] = jnp.zeros_like(sb_ref)

    yf_ref[0] = _ssd_direction(xf_ref[0], bf_ref[0], cf_ref[0], dtTf_ref[...], acsTf_ref[...],
                               acscf_ref[...], g * HEADS_PADDED, sf_ref, False)
    yb_ref[0] = _ssd_direction(xb_ref[0], bb_ref[0], cb_ref[0], dtTb_ref[...], acsTb_ref[...],
                               acscb_ref[...], (SSD_GROUPS + g) * HEADS_PADDED, sb_ref, True)


def _ssd(xbc3, dtT, acsT, acsc):
    bsz, s, _ = xbc3.shape
    L = SSD_CHUNK
    nc = s // L
    G = SSD_GROUPS
    nb = SSD_INNER // SSD_STATE
    ncb = nb + G
    fwd = lambda c: c
    bwd = lambda c: nc - 1 - c

    def specs(cidx, dirn):
        rowspec = pl.BlockSpec((HEADS_PADDED, L), lambda b, g, c: (dirn * G + g, b * nc + cidx(c)))
        return [
            pl.BlockSpec((1, L, GROUP_X), lambda b, g, c: (b, cidx(c), g)),
            pl.BlockSpec((1, L, SSD_STATE), lambda b, g, c: (b, cidx(c), nb + g)),
            pl.BlockSpec((1, L, SSD_STATE), lambda b, g, c: (b, cidx(c), ncb + g)),
            rowspec, rowspec,
            pl.BlockSpec((L, LANES), lambda b, g, c: (b * nc + cidx(c), 0)),
        ]

    out_specs = [pl.BlockSpec((1, L, GROUP_X), lambda b, g, c: (b, c, g)),
                 pl.BlockSpec((1, L, GROUP_X), lambda b, g, c: (b, nc - 1 - c, g))]
    return pl.pallas_call(
        _ssd_kernel,
        grid=(bsz, G, nc),
        in_specs=specs(fwd, 0) + specs(bwd, 1),
        out_specs=out_specs,
        out_shape=[jax.ShapeDtypeStruct((bsz, s, SSD_INNER), F32)] * 2,
        scratch_shapes=[pltpu.VMEM((HEAD_PAIRS, SSD_STATE, LANES), F32)] * 2,
        compiler_params=_cparams(("arbitrary", "arbitrary", "arbitrary")),
        name="ssd",
    )(xbc3, xbc3, xbc3, dtT, acsT, acsc, xbc3, xbc3, xbc3, dtT, acsT, acsc)


def _dft_tables(seq):
    n2n = LANES
    n1n = seq // n2n
    n1 = np.arange(n1n)
    k1 = np.arange(n1n)
    n2 = np.arange(n2n)
    ang = -2.0 * np.pi * (n2[:, None, None] * k1[None, :, None] / seq
                          + n1[None, None, :] * k1[None, :, None] / n1n)
    f1 = np.concatenate([np.cos(ang), np.sin(ang)], axis=1)
    k2 = np.arange(n2n)
    a2 = 2.0 * np.pi * np.outer(k2, n2) / n2n
    c2, s2 = np.cos(a2), np.sin(a2)
    g = np.block([[c2, s2], [-s2, c2]])
    ch = np.arange(FOURIER_GROUP_DIM)
    ac = 2.0 * np.pi * np.outer(ch, ch) / FOURIER_GROUP_DIM
    scale = 1.0 / math.sqrt(seq * FOURIER_GROUP_DIM)
    fc = np.concatenate([np.cos(ac), np.sin(ac)], axis=0) * scale
    return (jnp.asarray(f1, BF16), jnp.asarray(g, BF16), jnp.asarray(fc, BF16))


DFT_UNROLL = 4


def _dft_pitch(n1n):
    return 2 * n1n + 8


def _fourier_kernel(u_ref, f1_ref, g_ref, fc_ref, o_ref, a_ref):
    n2n = LANES
    n1n = u_ref.shape[2]
    pitch = _dft_pitch(n1n)

    def stage1(i, carry):
        for u in range(DFT_UNROLL):
            n2 = i * DFT_UNROLL + u
            xs = u_ref[0, 0, :, pl.ds(pl.multiple_of(n2 * LANES, LANES), LANES)].astype(BF16)
            a_ref[pl.ds(pl.multiple_of(n2 * pitch, 8), 2 * n1n), :] = _dot(f1_ref[n2], xs)
        return carry

    lax.fori_loop(0, n2n // DFT_UNROLL, stage1, 0)
    gm = g_ref[...]
    fc = fc_ref[...]

    def stage2(i, carry):
        for u in range(DFT_UNROLL):
            k1 = i * DFT_UNROLL + u
            re = a_ref[pl.ds(k1, n2n, stride=pitch), :]
            im = a_ref[pl.ds(n1n + k1, n2n, stride=pitch), :]
            a = jnp.concatenate([re, im], axis=0).astype(BF16)
            z = _dot(gm, a)
            zz = jnp.concatenate([z[:n2n], z[n2n:]], axis=1).astype(BF16)
            o_ref[0, pl.ds(k1, n2n, stride=n1n), :] = _dot(zz, fc)
        return carry

    lax.fori_loop(0, n1n // DFT_UNROLL, stage2, 0)


def _fourier(uf3):
    bsz, s, w = uf3.shape
    f1, g, fc = _dft_tables(s)
    n1n = s // LANES
    ng = w // FOURIER_GROUP_DIM
    u = uf3.reshape(bsz, n1n, LANES, ng, FOURIER_GROUP_DIM).transpose(0, 3, 1, 2, 4)
    u = u.reshape(bsz, ng, n1n, LANES * FOURIER_GROUP_DIM)
    return pl.pallas_call(
        _fourier_kernel,
        grid=(bsz, ng),
        in_specs=[pl.BlockSpec((1, 1, n1n, LANES * FOURIER_GROUP_DIM), lambda b, j: (b, j, 0, 0)),
                  pl.BlockSpec(f1.shape, lambda b, j: (0, 0, 0)),
                  pl.BlockSpec(g.shape, lambda b, j: (0, 0)),
                  pl.BlockSpec(fc.shape, lambda b, j: (0, 0))],
        out_specs=pl.BlockSpec((1, s, LANES), lambda b, j: (b, 0, j)),
        out_shape=jax.ShapeDtypeStruct((bsz, s, w), F32),
        scratch_shapes=[pltpu.VMEM((LANES * _dft_pitch(n1n), LANES), F32)],
        compiler_params=_cparams(("arbitrary", "arbitrary")),
        name="fourier",
    )(u, f1, g, fc)


def _mix_kernel(yf_ref, yb_ref, xs_ref, z_ref, fm_ref, gates_ref, x_ref, mod_ref, dskip_ref,
                ssdn_ref, wbs_ref, wbf_ref, wout_ref, gpost_ref, o_ref):
    m = mod_ref[0]
    y = yf_ref[...] + yb_ref[...] + dskip_ref[...] * xs_ref[...]
    v = y * _silu(z_ref[...])
    parts = []
    for g in range(SSD_GROUPS):
        vg = v[:, g * GROUP_X:(g + 1) * GROUP_X]
        parts.append(vg * lax.rsqrt(jnp.mean(vg * vg, axis=-1, keepdims=True) + RMS_EPS))
    vn = jnp.concatenate(parts, axis=1) * ssdn_ref[...]
    y_ssd = _dot(vn.astype(BF16), wbs_ref[...])
    y_fou = _dot(fm_ref[...].astype(BF16), wbf_ref[...])
    gt = _sigmoid(gates_ref[...])
    d = y_ssd.shape[1]
    mixed = gt[:, :d] * y_ssd + gt[:, d:] * y_fou
    mo = _dot(mixed.astype(BF16), wout_ref[...])
    o_ref[...] = x_ref[...] + m[2:3] * _rms(mo, gpost_ref[...])


def _mix(yf, yb, xbc, z, fm, gates, x2, mod, dskip, ssdn, wbs, wbf, wout, gpost, seq):
    t, d = x2.shape
    tm = TOKEN_TILE
    per_b = seq // tm
    full = lambda a: pl.BlockSpec(a.shape, lambda i: (0,) * a.ndim)
    row = lambda n: pl.BlockSpec((tm, n), lambda i: (i, 0))
    return pl.pallas_call(
        _mix_kernel,
        grid=(t // tm,),
        in_specs=[row(SSD_INNER), row(SSD_INNER), row(SSD_INNER), row(SSD_INNER),
                  row(FOURIER_WIDTH), row(2 * d), row(d),
                  pl.BlockSpec((1, 6, d), lambda i: (i // per_b, 0, 0)),
                  full(dskip), full(ssdn), full(wbs), full(wbf), full(wout), full(gpost)],
        out_specs=row(d),
        out_shape=jax.ShapeDtypeStruct((t, d), F32),
        compiler_params=_cparams(("arbitrary",)),
        name="mix",
    )(yf, yb, xbc, z, fm, gates, x2, mod, dskip, ssdn, wbs, wbf, wout, gpost)


def _route_kernel(x_ref, mod_ref, gpre_ref, wr_ref, rb_ref, hp_ref, idx_ref, w_ref, rank_ref,
                  counts_ref, run_ref):
    @pl.when(pl.program_id(0) == 0)
    def _():
        run_ref[...] = jnp.zeros_like(run_ref)

    m = mod_ref[0]
    h = _rms(x_ref[...], gpre_ref[...]) * (1.0 + m[4:5]) + m[3:4]
    tm, d = h.shape
    half = d // 2
    bits = pltpu.bitcast(h.astype(BF16).astype(F32), jnp.uint32)
    hp_ref[...] = (bits[:, :half] >> 16) | (bits[:, half:] & jnp.uint32(0xFFFF0000))

    h_hi = h.astype(BF16)
    h_lo = (h - h_hi.astype(F32)).astype(BF16)
    w_hi = wr_ref[0]
    logits = _dot(h_hi, w_hi) + (_dot(h_hi, wr_ref[1]) + _dot(h_lo, w_hi))
    scores = _sigmoid(logits)
    biased = scores + rb_ref[...]
    ne = scores.shape[1]
    per_group = ne // N_EXPERT_GROUPS
    lane = lax.broadcasted_iota(jnp.int32, (tm, ne), 1)
    lane_f = lane.astype(F32)
    grp = lane // per_group
    out_lane = lax.broadcasted_iota(jnp.int32, (tm, LANES), 1)
    neg = -jnp.inf

    def argmax_first(v):
        mx = jnp.max(v, axis=-1, keepdims=True)
        ix = jnp.min(jnp.where(v == mx, lane_f, float(ne)), axis=-1, keepdims=True)
        return mx, ix

    gs = []
    gmat = jnp.full((tm, LANES), neg, F32)
    for g in range(N_EXPERT_GROUPS):
        vg = jnp.where(grp == g, biased, neg)
        m1, i1 = argmax_first(vg)
        m2 = jnp.max(jnp.where(lane_f == i1, neg, vg), axis=-1, keepdims=True)
        gs.append(m1 + m2)
        gmat = jnp.where(out_lane == g, gs[g], gmat)
    ahead = jnp.zeros((tm, LANES), F32)
    for o in range(N_EXPERT_GROUPS):
        beats = (gs[o] > gmat) | ((gs[o] == gmat) & (out_lane > o))
        ahead = ahead + beats.astype(F32)
    sel = ((ahead < TOPK_GROUPS) & (out_lane < N_EXPERT_GROUPS)).astype(BF16)
    eg = lax.broadcasted_iota(jnp.int32, (LANES, ne), 0)
    ee = lax.broadcasted_iota(jnp.int32, (LANES, ne), 1)
    allowed = _dot(sel, (eg == ee // per_group).astype(BF16)) > 0.5
    masked = jnp.where(allowed, biased, neg)

    idx_out = jnp.zeros((tm, LANES), F32)
    w_out = jnp.zeros((tm, LANES), F32)
    w_sum = jnp.zeros((tm, 1), F32)
    hits = []
    for k in range(TOP_K):
        _, ik = argmax_first(masked)
        hit = lane_f == ik
        hits.append(hit)
        wk = jnp.sum(jnp.where(hit, scores, 0.0), axis=-1, keepdims=True)
        masked = jnp.where(hit, neg, masked)
        idx_out = jnp.where(out_lane == k, ik, idx_out)
        w_out = jnp.where(out_lane == k, wk, w_out)
        w_sum = w_sum + wk
    idx_ref[...] = idx_out.astype(jnp.int32)
    w_ref[...] = w_out / w_sum * ROUTED_SCALE

    chosen = functools.reduce(jnp.logical_or, hits).astype(F32)
    ti = lax.broadcasted_iota(jnp.int32, (tm, tm), 0)
    tj = lax.broadcasted_iota(jnp.int32, (tm, tm), 1)
    before = _dot((tj < ti).astype(BF16), chosen.astype(BF16)) + run_ref[...]
    rank_out = jnp.zeros((tm, LANES), jnp.int32)
    for k in range(TOP_K):
        rk = jnp.sum(jnp.where(hits[k], before, 0.0), axis=-1, keepdims=True)
        rank_out = jnp.where(out_lane == k, rk.astype(jnp.int32), rank_out)
    rank_ref[...] = rank_out
    run_ref[...] = run_ref[...] + jnp.sum(chosen, axis=0, keepdims=True)
    counts_ref[...] = run_ref[...]


def _route(x1, mod, gpre, w_router, router_bias, seq):
    t, d = x1.shape
    tm = TOKEN_TILE
    per_b = seq // tm
    full = lambda a: pl.BlockSpec(a.shape, lambda i: (0,) * a.ndim)
    row = lambda n: pl.BlockSpec((tm, n), lambda i: (i, 0))
    rb = router_bias.reshape(1, -1)
    w_hi = w_router.astype(BF16)
    w_router = jnp.stack([w_hi, (w_router - w_hi.astype(F32)).astype(BF16)])
    return pl.pallas_call(
        _route_kernel,
        grid=(t // tm,),
        in_specs=[row(d), pl.BlockSpec((1, 6, d), lambda i: (i // per_b, 0, 0)), full(gpre),
                  full(w_router), full(rb)],
        out_specs=[row(d // 2), row(LANES), row(LANES), row(LANES),
                   pl.BlockSpec((1, N_EXPERTS), lambda i: (0, 0))],
        out_shape=[jax.ShapeDtypeStruct((t, d // 2), jnp.uint32),
                   jax.ShapeDtypeStruct((t, LANES), jnp.int32),
                   jax.ShapeDtypeStruct((t, LANES), F32),
                   jax.ShapeDtypeStruct((t, LANES), jnp.int32),
                   jax.ShapeDtypeStruct((1, N_EXPERTS), F32)],
        scratch_shapes=[pltpu.VMEM((1, N_EXPERTS), F32)],
        compiler_params=_cparams(("arbitrary",)),
        name="route",
    )(x1, mod, gpre, w_router, rb)


GATHER_UNROLL = 8


ROW_TILE = 8


def _tile_rows(r):
    return pl.ds(pl.multiple_of(r * ROW_TILE, ROW_TILE), ROW_TILE)


def _gather_rows(idx_ref, n_rows, src_hbm, dst_of, sem):
    def body(q, carry):
        for u in range(GATHER_UNROLL):
            src = idx_ref[0, 0, q * GATHER_UNROLL + u]
            pltpu.make_async_copy(src_hbm.at[_tile_rows(src), :], dst_of(q, u), sem).start(
                priority=u % 2)
        return carry

    lax.fori_loop(0, n_rows // GATHER_UNROLL, body, 0)


def _to_tiled_rows(o_ref, y):
    rows, width = y.shape
    for r in range(rows // ROW_TILE):
        for j in range(width // LANES):
            o_ref[pl.ds(r * ROW_TILE * ROW_TILE + j, ROW_TILE, stride=ROW_TILE), :] = (
                y[r * ROW_TILE:(r + 1) * ROW_TILE, j * LANES:(j + 1) * LANES])


def _from_tiled_rows(load, rows, width):
    cols = []
    for j in range(width // LANES):
        pieces = [load(r * ROW_TILE * ROW_TILE + j, ROW_TILE, ROW_TILE) for r in range(rows // ROW_TILE)]
        cols.append(jnp.concatenate(pieces, axis=0))
    return jnp.concatenate(cols, axis=1)


def _dest_kernel(idx_ref, rank_ref, start_ref, o_ref):
    idx = idx_ref[...]
    rank = rank_ref[...]
    start = start_ref[...]
    tm = idx.shape[0]
    lane = lax.broadcasted_iota(jnp.int32, (tm, start.shape[1]), 1)
    out_lane = lax.broadcasted_iota(jnp.int32, (tm, LANES), 1)
    out = jnp.zeros((tm, LANES), jnp.int32)
    for k in range(TOP_K):
        base = jnp.sum(jnp.where(lane == idx[:, k:k + 1], start, 0.0), axis=-1, keepdims=True)
        out = jnp.where(out_lane == k, base.astype(jnp.int32) + rank[:, k:k + 1], out)
    o_ref[...] = out


def _dest(idx_pad, rank_pad, pad_start):
    t = idx_pad.shape[0]
    tm = DEST_TILE
    row = pl.BlockSpec((tm, LANES), lambda i: (i, 0))
    return pl.pallas_call(
        _dest_kernel,
        grid=(t // tm,),
        in_specs=[row, row, pl.BlockSpec((1, N_EXPERTS), lambda i: (0, 0))],
        out_specs=row,
        out_shape=jax.ShapeDtypeStruct((t, LANES), jnp.int32),
        compiler_params=_cparams(("arbitrary",)),
        name="dest",
    )(idx_pad, rank_pad, pad_start.astype(F32).reshape(1, N_EXPERTS))


def _scatter_kernel(b0_ref, nb_ref, nu_ref, dest_ref, hp_ref, xs_hbm, zbuf, sem, zsem):
    ts = hp_ref.shape[0]
    bm = zbuf.shape[0]
    n_blocks = xs_hbm.shape[0] // bm

    @pl.when(pl.program_id(0) == 0)
    def _():
        zbuf[...] = jnp.zeros_like(zbuf)
        n_used = nu_ref[0]

        def zero_block(g):
            return pltpu.make_async_copy(zbuf, xs_hbm.at[pl.ds(pl.multiple_of(g * bm, bm), bm), :], zsem)

        def tails(action):
            def step(e, carry):
                @pl.when(nb_ref[e] > 0)
                def _():
                    action(zero_block(b0_ref[e] + nb_ref[e] - 1))
                return carry
            lax.fori_loop(0, N_EXPERTS, step, 0)

        def unused(action):
            def step(g, carry):
                action(zero_block(g))
                return carry
            lax.fori_loop(n_used, n_blocks, step, 0)

        tails(lambda c: c.start())
        unused(lambda c: c.start())
        tails(lambda c: c.wait())
        unused(lambda c: c.wait())

    def body(t, carry):
        for k in range(TOP_K):
            dst = dest_ref[0, 0, t * TOP_K + k]
            pltpu.make_async_copy(hp_ref.at[pl.ds(t, 1), :], xs_hbm.at[pl.ds(dst, 1), :], sem).start(
                priority=k % 2)
        return carry

    lax.fori_loop(0, ts, body, 0)
    for k in range(TOP_K):
        pltpu.make_async_copy(hp_ref, xs_hbm.at[pl.ds(0, ts), :], sem).wait()


def _scatter(dest, hp, first_block, n_block, n_used, n_rows):
    t, half = hp.shape
    ts = SCATTER_TILE
    n_steps = t // ts
    dest3 = dest.reshape(n_steps, 1, ts * TOP_K)
    grid_spec = pltpu.PrefetchScalarGridSpec(
        num_scalar_prefetch=3,
        grid=(n_steps,),
        in_specs=[pl.BlockSpec((1, 1, ts * TOP_K), lambda i, b0, nb, nu: (i, 0, 0),
                               memory_space=pltpu.SMEM),
                  pl.BlockSpec((ts, half), lambda i, b0, nb, nu: (i, 0))],
        out_specs=pl.BlockSpec(memory_space=pl.ANY),
        scratch_shapes=[pltpu.VMEM((EXPERT_ROWS, half), jnp.uint32),
                        pltpu.SemaphoreType.DMA(()), pltpu.SemaphoreType.DMA(())],
    )
    return pl.pallas_call(
        _scatter_kernel,
        grid_spec=grid_spec,
        out_shape=jax.ShapeDtypeStruct((n_rows, half), jnp.uint32),
        compiler_params=_cparams(("arbitrary",)),
        name="scatter",
    )(first_block, n_block, n_used, dest3, hp)


def _expert_kernel(b0_ref, nb_ref, nu_ref, xs_hbm, w13_ref, w2_ref, y_hbm,
                   xbuf, ybuf, w13b, w2b, sem_in, sem_out):
    e = pl.program_id(0)
    n_used = nu_ref[0]
    b0 = b0_ref[e]
    nb = nb_ref[e]
    n_in, bm = xbuf.shape[:2]
    n_out, y_rows = ybuf.shape[:2]
    ahead = n_in - 1

    def in_copy(g, slot):
        return pltpu.make_async_copy(xs_hbm.at[pl.ds(pl.multiple_of(g * bm, bm), bm), :],
                                     xbuf.at[slot], sem_in.at[slot])

    def out_copy(g, slot):
        return pltpu.make_async_copy(ybuf.at[slot],
                                     y_hbm.at[pl.ds(pl.multiple_of(g * y_rows, y_rows), y_rows), :],
                                     sem_out.at[slot])

    @pl.when(nb > 0)
    def _():
        w13b[...] = w13_ref[0].astype(BF16)
        w2b[...] = w2_ref[0].astype(BF16)

    @pl.when((nb > 0) & (b0 == 0))
    def _():
        for a in range(ahead):
            @pl.when(a < n_used)
            def _():
                in_copy(a, a).start()

    def block(g, carry):
        in_slot = g % n_in
        out_slot = g % n_out
        in_copy(g, in_slot).wait()

        @pl.when(g + ahead < n_used)
        def _():
            in_copy(g + ahead, (g + ahead) % n_in).start()

        @pl.when(g >= n_out)
        def _():
            out_copy(g - n_out, out_slot).wait()

        w = xbuf[in_slot]
        lo = pltpu.bitcast(w << 16, F32).astype(BF16)
        hi = pltpu.bitcast(w & jnp.uint32(0xFFFF0000), F32).astype(BF16)
        half = lo.shape[1]
        ag = _dot(lo, w13b[:half, :]) + _dot(hi, w13b[half:, :])
        hh = ag.shape[1] // 2
        act = (_silu(ag[:, :hh]) * ag[:, hh:]).astype(BF16)
        _to_tiled_rows(ybuf.at[out_slot], _dot(act, w2b[...]))
        out_copy(g, out_slot).start()
        return carry

    lax.fori_loop(b0, b0 + nb, block, 0)

    @pl.when(e == pl.num_programs(0) - 1)
    def _():
        for back in range(1, n_out + 1):
            @pl.when(n_used >= back)
            def _():
                out_copy(n_used - back, (n_used - back) % n_out).wait()

        n_blocks = y_hbm.shape[0] // y_rows
        ybuf[0] = jnp.zeros((y_rows, LANES), F32)

        def start_zero(g, carry):
            out_copy(g, 0).start()
            return carry

        def wait_zero(g, carry):
            out_copy(g, 0).wait()
            return carry

        lax.fori_loop(n_used, n_blocks, start_zero, 0)
        lax.fori_loop(n_used, n_blocks, wait_zero, 0)


def _experts(xs, first_block, n_block, n_used, w13, w2):
    n_rows, half = xs.shape
    ne, d, h2 = w13.shape
    bm = EXPERT_ROWS
    y_rows = bm * d // LANES
    grid_spec = pltpu.PrefetchScalarGridSpec(
        num_scalar_prefetch=3,
        grid=(ne,),
        in_specs=[
            pl.BlockSpec(memory_space=pl.ANY),
            pl.BlockSpec((1, d, h2), lambda e, b0, nb, nu: (e, 0, 0)),
            pl.BlockSpec((1, h2 // 2, d), lambda e, b0, nb, nu: (e, 0, 0)),
        ],
        out_specs=pl.BlockSpec(memory_space=pl.ANY),
        scratch_shapes=[pltpu.VMEM((EXPERT_IN_SLOTS, bm, half), jnp.uint32),
                        pltpu.VMEM((EXPERT_OUT_SLOTS, y_rows, LANES), F32),
                        pltpu.VMEM((d, h2), BF16),
                        pltpu.VMEM((h2 // 2, d), BF16),
                        pltpu.SemaphoreType.DMA((EXPERT_IN_SLOTS,)),
                        pltpu.SemaphoreType.DMA((EXPERT_OUT_SLOTS,))],
    )
    return pl.pallas_call(
        _expert_kernel,
        grid_spec=grid_spec,
        out_shape=jax.ShapeDtypeStruct((n_rows * d // LANES, LANES), F32),
        compiler_params=_cparams(("arbitrary",)),
        name="experts",
    )(first_block, n_block, n_used, xs, w13, w2)


def _final_kernel(dc_ref, dn_ref, y_hbm, w_ref, x_ref, mod_ref, gpre_ref, gpost_ref, w13s_ref,
                  w2s_ref, o_ref, buf, sem):
    i = pl.program_id(0)
    slot = i % 2
    tm = x_ref.shape[0]

    def issue(d_ref, s):
        _gather_rows(d_ref, tm * TOP_K, y_hbm,
                     lambda q, u: buf.at[s, u, _tile_rows(q), :], sem.at[s])

    @pl.when(i == 0)
    def _():
        issue(dc_ref, 0)

    @pl.when(i + 1 < pl.num_programs(0))
    def _():
        issue(dn_ref, 1 - slot)

    for k in range(TOP_K):
        pltpu.make_async_copy(y_hbm.at[pl.ds(0, tm * ROW_TILE), :], buf.at[slot, k],
                              sem.at[slot]).wait()

    w = w_ref[...]
    x1 = x_ref[...]
    routed = None
    for k in range(TOP_K):
        yk = _from_tiled_rows(lambda a, n, st: buf[slot, k, pl.ds(a, n, stride=st), :], tm, x1.shape[1])
        routed = yk * w[:, k:k + 1] if routed is None else routed + yk * w[:, k:k + 1]

    m = mod_ref[0]
    h = (_rms(x1, gpre_ref[...]) * (1.0 + m[4:5]) + m[3:4]).astype(BF16)
    ag = _dot(h, w13s_ref[...])
    hh = ag.shape[1] // 2
    act = (_silu(ag[:, :hh]) * ag[:, hh:]).astype(BF16)
    ffn = routed + _dot(act, w2s_ref[...])
    o_ref[...] = x1 + m[5:6] * _rms(ffn, gpost_ref[...])


def _final(dest, y_sorted, top_w, x1, mod, gpre, gpost, w13s, w2s, seq):
    t, d = x1.shape
    tm = COMBINE_TILE
    per_b = seq // tm
    n_steps = t // tm
    dest3 = dest.reshape(n_steps, 1, tm * TOP_K)
    full = lambda a: pl.BlockSpec(a.shape, lambda i: (0,) * a.ndim)
    row = lambda n: pl.BlockSpec((tm, n), lambda i: (i, 0))
    return pl.pallas_call(
        _final_kernel,
        grid=(n_steps,),
        in_specs=[
            pl.BlockSpec((1, 1, tm * TOP_K), lambda i: (i, 0, 0), memory_space=pltpu.SMEM),
            pl.BlockSpec((1, 1, tm * TOP_K), lambda i: (jnp.minimum(i + 1, n_steps - 1), 0, 0),
                         memory_space=pltpu.SMEM),
            pl.BlockSpec(memory_space=pl.ANY),
            row(LANES), row(d), pl.BlockSpec((1, 6, d), lambda i: (i // per_b, 0, 0)),
            full(gpre), full(gpost), full(w13s), full(w2s)],
        out_specs=row(d),
        out_shape=jax.ShapeDtypeStruct((t, d), F32),
        scratch_shapes=[pltpu.VMEM((2, TOP_K, tm * ROW_TILE, LANES), F32),
                        pltpu.SemaphoreType.DMA((2,))],
        compiler_params=_cparams(("arbitrary",)),
        name="combine_final",
    )(dest3, dest3, y_sorted, top_w, x1, mod, gpre, gpost, w13s, w2s)


def _pad_heads(v):
    lead = v.shape[:-1]
    v = v.reshape(lead + (2, SSD_GROUPS, HEADS_PER_GROUP))
    v = jnp.pad(v, [(0, 0)] * (len(lead) + 2) + [(0, HEADS_PADDED - HEADS_PER_GROUP)])
    return v.reshape(lead + (2 * SSD_GROUPS * HEADS_PADDED,))


def _dispatch_plan(counts, n_tokens):
    bm = EXPERT_ROWS
    counts = counts.reshape(N_EXPERTS).astype(jnp.int32)
    padded = (counts + bm - 1) // bm * bm
    pad_end = jnp.cumsum(padded)
    pad_start = pad_end - padded
    n_rows = -(-n_tokens * TOP_K // bm) * bm + N_EXPERTS * bm
    first_block = (pad_start // bm).astype(jnp.int32)
    n_block = (padded // bm).astype(jnp.int32)
    n_used = (pad_end[-1] // bm).astype(jnp.int32).reshape(1)
    return pad_start, first_block, n_block, n_used, n_rows


def _layer(x, c, w_ada, b_ada, pre_norm_mix, post_norm_mix, pre_norm_ffn, post_norm_ffn, w_in,
           conv_w, conv_b, dt_bias_fwd, dt_bias_bwd, a_log_fwd, a_log_bwd, d_skip, ssd_norm,
           w_branch_ssd, w_branch_fourier, w_out, w_router, router_bias, w13_experts, w2_experts,
           w13_shared, w2_shared):
    bsz, seq, d = x.shape
    t = bsz * seq
    x2 = x.reshape(t, d)
    row = lambda v: v.reshape(1, -1).astype(F32)

    mod = _ada(c, w_ada, b_ada)

    i1 = SSD_INNER
    i2 = i1 + XBC_WIDTH
    i3 = i2 + 2 * SSD_HEADS
    i4 = i3 + FOURIER_WIDTH
    n_dt = 2 * SSD_GROUPS * HEADS_PADDED
    w_dtp = _pad_heads(w_in[:, i2:i3])
    wdtc = jnp.pad(w_dtp, ((0, 0), (0, LANES - n_dt))).astype(BF16)
    wdtT = w_dtp.T.astype(BF16)
    bias_p = _pad_heads(jnp.concatenate([dt_bias_fwd, dt_bias_bwd]).astype(F32))
    a_p = _pad_heads(-jnp.exp(jnp.concatenate([a_log_fwd, a_log_bwd]).astype(F32)))
    pad_row = lambda v: jnp.pad(v, (0, LANES - n_dt)).reshape(1, LANES)
    z, xbc, acsc, dtT, acsT, uf, gates = _inproj(
        x2, mod, row(pre_norm_mix), w_in[:, :i1].astype(BF16), w_in[:, i1:i2].astype(BF16),
        wdtc, wdtT, w_in[:, i3:i4].astype(BF16), w_in[:, i4:].astype(BF16),
        pad_row(bias_p), pad_row(a_p), bias_p.reshape(n_dt, 1), a_p.reshape(n_dt, 1), seq)

    xbc3 = _conv(xbc.reshape(bsz, seq, XBC_WIDTH), conv_w, conv_b)

    yf, yb = _ssd(xbc3, dtT, acsT, acsc)

    fm = _fourier(uf.reshape(bsz, seq, FOURIER_WIDTH))

    x1 = _mix(yf.reshape(t, SSD_INNER), yb.reshape(t, SSD_INNER), xbc3.reshape(t, XBC_WIDTH), z,
              fm.reshape(t, FOURIER_WIDTH), gates, x2, mod,
              row(jnp.repeat(d_skip, SSD_HEAD_DIM)), row(ssd_norm), w_branch_ssd.astype(BF16),
              w_branch_fourier.astype(BF16), w_out.astype(BF16), row(post_norm_mix), seq)

    hp, idx_pad, w_pad, rank_pad, counts = _route(x1, mod, row(pre_norm_ffn),
                                                  w_router.astype(F32), router_bias, seq)
    pad_start, first_block, n_block, n_used, n_rows = _dispatch_plan(counts, t)
    dest = _dest(idx_pad, rank_pad, pad_start)[:, :TOP_K]
    xs = _scatter(dest, hp, first_block, n_block, n_used, n_rows)
    y_sorted = _experts(xs, first_block, n_block, n_used, w13_experts, w2_experts)
    out = _final(dest, y_sorted, w_pad, x1, mod, row(pre_norm_ffn), row(post_norm_ffn),
                 w13_shared.astype(BF16), w2_shared.astype(BF16), seq)
    return out.reshape(bsz, seq, d)


def kernel(x, c, w_ada, b_ada, pre_norm_mix, post_norm_mix, pre_norm_ffn, post_norm_ffn, w_in,
           conv_w, conv_b, dt_bias_fwd, dt_bias_bwd, a_log_fwd, a_log_bwd, d_skip, ssd_norm,
           w_branch_ssd, w_branch_fourier, w_out, w_router, router_bias, w13_experts, w2_experts,
           w13_shared, w2_shared):
    for layer in range(w_ada.shape[0]):
        x = _layer(x, c, w_ada[layer], b_ada[layer], pre_norm_mix[layer], post_norm_mix[layer],
                   pre_norm_ffn[layer], post_norm_ffn[layer], w_in[layer], conv_w[layer],
                   conv_b[layer], dt_bias_fwd[layer], dt_bias_bwd[layer], a_log_fwd[layer],
                   a_log_bwd[layer], d_skip[layer], ssd_norm[layer], w_branch_ssd[layer],
                   w_branch_fourier[layer], w_out[layer], w_router[layer], router_bias[layer],
                   w13_experts[layer], w2_experts[layer], w13_shared[layer], w2_shared[layer])
    return x
```

```python
import functools
import math

import numpy as np
import jax
import jax.numpy as jnp
from jax import lax
from jax.experimental import pallas as pl
from jax.experimental.pallas import tpu as pltpu

F32 = jnp.float32
BF16 = jnp.bfloat16
HIGHEST = lax.Precision.HIGHEST

D_MODEL = 1024
SSD_HEADS = 24
SSD_HEAD_DIM = 64
SSD_INNER = SSD_HEADS * SSD_HEAD_DIM
SSD_GROUPS = 4
HEADS_PER_GROUP = SSD_HEADS // SSD_GROUPS
HEADS_PADDED = 8
SSD_STATE = 128
SSD_CONV = 5
SSD_CHUNK = 128
XBC_WIDTH = SSD_INNER + 2 * SSD_GROUPS * SSD_STATE
GROUP_X = HEADS_PER_GROUP * SSD_HEAD_DIM
FOURIER_WIDTH = 512
FOURIER_GROUP_DIM = 128
N_EXPERTS = 256
TOP_K = 8
N_EXPERT_GROUPS = 8
TOPK_GROUPS = 4
EXPERT_HIDDEN = 256
SHARED_HIDDEN = 256
ROUTED_SCALE = 2.5
RMS_EPS = 1e-6

LANES = 128
VMEM_LIMIT = 56 * 1024 * 1024
TOKEN_TILE = 256
EXPERT_ROWS = 256
COMBINE_TILE = 128
SCATTER_TILE = 512
DEST_TILE = 1024
EXPERT_IN_SLOTS = 4
EXPERT_OUT_SLOTS = 3
CONV_ROWS = 256


def _cparams(sem):
    return pltpu.CompilerParams(dimension_semantics=sem, vmem_limit_bytes=VMEM_LIMIT)


def _dot(a, b, precision=None):
    return jnp.dot(a, b, preferred_element_type=F32, precision=precision)


def _dot_nt(a, b, precision=None):
    return lax.dot_general(a, b, (((1,), (1,)), ((), ())), preferred_element_type=F32,
                           precision=precision)


def _dot_tn(a, b):
    return lax.dot_general(a, b, (((0,), (0,)), ((), ())), preferred_element_type=F32)


def _sigmoid(x):
    return 1.0 / (1.0 + jnp.exp(-x))


def _silu(x):
    return x * _sigmoid(x)


def _softplus(x):
    return jnp.maximum(x, 0.0) + jnp.log1p(jnp.exp(-jnp.abs(x)))


def _rms(x, g):
    return x * lax.rsqrt(jnp.mean(x * x, axis=-1, keepdims=True) + RMS_EPS) * g


def _ada_kernel(c_ref, w_ref, b_ref, o_ref):
    o_ref[...] = _dot(_silu(c_ref[...]), w_ref[...], HIGHEST) + b_ref[...]


def _ada(c, w_ada, b_ada):
    bsz, d = c.shape
    rows = 8
    cp = jnp.zeros((rows, d), F32).at[:bsz].set(c)
    n = w_ada.shape[1]
    tn = 1536
    out = pl.pallas_call(
        _ada_kernel,
        grid=(n // tn,),
        in_specs=[pl.BlockSpec((rows, d), lambda j: (0, 0)),
                  pl.BlockSpec((d, tn), lambda j: (0, j)),
                  pl.BlockSpec((1, tn), lambda j: (0, j))],
        out_specs=pl.BlockSpec((rows, tn), lambda j: (0, j)),
        out_shape=jax.ShapeDtypeStruct((rows, n), F32),
        compiler_params=_cparams(("arbitrary",)),
        name="adaln",
    )(cp, w_ada, b_ada.reshape(1, n))
    return out[:bsz].reshape(bsz, 6, d)


def _split3(a):
    a1 = a.astype(BF16)
    r1 = a - a1.astype(F32)
    a2 = r1.astype(BF16)
    a3 = (r1 - a2.astype(F32)).astype(BF16)
    return a1, a2, a3


def _inproj_kernel(x_ref, mod_ref, g_ref, wz_ref, wxbc_ref, wdtc_ref, wdtT_ref, wuf_ref, wg_ref,
                   bias_row_ref, a_row_ref, bias_col_ref, a_col_ref,
                   z_ref, xbc_ref, acsc_ref, dtT_ref, acsT_ref, uf_ref, gates_ref):
    m = mod_ref[0]
    h = _rms(x_ref[...], g_ref[...]) * (1.0 + m[1:2]) + m[0:1]
    hb = h.astype(BF16)
    z_ref[...] = _dot(hb, wz_ref[...]).astype(z_ref.dtype)
    xbc_ref[...] = _dot(hb, wxbc_ref[...]).astype(xbc_ref.dtype)
    uf_ref[...] = _dot(hb, wuf_ref[...]).astype(uf_ref.dtype)
    gates_ref[...] = _dot(hb, wg_ref[...]).astype(gates_ref.dtype)

    tm = hb.shape[0]
    n_fwd = SSD_GROUPS * HEADS_PADDED
    ii = lax.broadcasted_iota(jnp.int32, (tm, tm), 0)
    jj = lax.broadcasted_iota(jnp.int32, (tm, tm), 1)
    same = (ii // SSD_CHUNK) == (jj // SSD_CHUNK)
    tri_f = (same & (jj <= ii)).astype(BF16)
    tri_b = (same & (jj >= ii)).astype(BF16)

    dt_c = _softplus(_dot(hb, wdtc_ref[...]) + bias_row_ref[...])
    pieces = _split3(dt_c * a_row_ref[...])
    acs_f = sum(_dot(tri_f, p) for p in pieces)
    acs_b = sum(_dot(tri_b, p) for p in pieces)
    lane = lax.broadcasted_iota(jnp.int32, acs_f.shape, 1)
    acsc_ref[...] = jnp.where(lane < n_fwd, acs_f, acs_b)

    dt_t = _softplus(_dot_nt(wdtT_ref[...], hb) + bias_col_ref[...])
    pieces = _split3(dt_t * a_col_ref[...])
    acs_f = sum(_dot_nt(p, tri_f) for p in pieces)
    acs_b = sum(_dot_nt(p, tri_b) for p in pieces)
    sub = lax.broadcasted_iota(jnp.int32, acs_f.shape, 0)
    dtT_ref[...] = dt_t
    acsT_ref[...] = jnp.where(sub < n_fwd, acs_f, acs_b)


def _inproj(x2, mod, g, wz, wxbc, wdtc, wdtT, wuf, wg, bias_row, a_row, bias_col, a_col, seq):
    t, d = x2.shape
    tm = TOKEN_TILE
    per_b = seq // tm
    full = lambda a: pl.BlockSpec(a.shape, lambda i: (0,) * a.ndim)
    row = lambda n: pl.BlockSpec((tm, n), lambda i: (i, 0))
    nd = wdtT.shape[0]
    colspec = pl.BlockSpec((nd, tm), lambda i: (0, i))
    return pl.pallas_call(
        _inproj_kernel,
        grid=(t // tm,),
        in_specs=[row(d), pl.BlockSpec((1, 6, d), lambda i: (i // per_b, 0, 0)), full(g),
                  full(wz), full(wxbc), full(wdtc), full(wdtT), full(wuf), full(wg),
                  full(bias_row), full(a_row), full(bias_col), full(a_col)],
        out_specs=[row(wz.shape[1]), row(wxbc.shape[1]), row(LANES), colspec, colspec,
                   row(wuf.shape[1]), row(wg.shape[1])],
        out_shape=[jax.ShapeDtypeStruct((t, wz.shape[1]), BF16),
                   jax.ShapeDtypeStruct((t, wxbc.shape[1]), BF16),
                   jax.ShapeDtypeStruct((t, LANES), F32),
                   jax.ShapeDtypeStruct((nd, t), F32),
                   jax.ShapeDtypeStruct((nd, t), F32),
                   jax.ShapeDtypeStruct((t, wuf.shape[1]), BF16),
                   jax.ShapeDtypeStruct((t, wg.shape[1]), BF16)],
        compiler_params=_cparams(("arbitrary",)),
        name="inproj",
    )(x2, mod, g, wz, wxbc, wdtc, wdtT, wuf, wg, bias_row, a_row, bias_col, a_col)


def _conv_kernel(u_ref, w_ref, b_ref, o_ref, pad_ref):
    s = u_ref.shape[1]
    halo = 8
    pad_ref[0:halo, :] = jnp.zeros((halo, LANES), F32)
    pad_ref[halo + s:2 * halo + s, :] = jnp.zeros((halo, LANES), F32)
    pad_ref[halo:halo + s, :] = u_ref[0].astype(F32)
    w = w_ref[...]
    b = b_ref[...]
    half = (SSD_CONV - 1) // 2
    for r in range(s // CONV_ROWS):
        base = r * CONV_ROWS
        acc = b
        for k in range(SSD_CONV):
            lo = base + halo + k - half
            acc = acc + w[k:k + 1, :] * pad_ref[lo:lo + CONV_ROWS, :]
        o_ref[0, base:base + CONV_ROWS, :] = _silu(acc)


def _conv(xbc3, conv_w, conv_b):
    bsz, s, c = xbc3.shape
    return pl.pallas_call(
        _conv_kernel,
        grid=(bsz, c // LANES),
        in_specs=[pl.BlockSpec((1, s, LANES), lambda b, j: (b, 0, j)),
                  pl.BlockSpec((SSD_CONV, LANES), lambda b, j: (0, j)),
                  pl.BlockSpec((1, LANES), lambda b, j: (0, j))],
        out_specs=pl.BlockSpec((1, s, LANES), lambda b, j: (b, 0, j)),
        out_shape=jax.ShapeDtypeStruct((bsz, s, c), F32),
        scratch_shapes=[pltpu.VMEM((s + 16, LANES), F32)],
        compiler_params=_cparams(("arbitrary", "arbitrary")),
        name="conv",
    )(xbc3, conv_w, conv_b.reshape(1, c))


HEAD_PAIRS = HEADS_PER_GROUP // 2
SSD_GROUPS_PER_STEP = 2


def _ssd_direction(x, bm, cm, dt_r, acs_r, acsc_all, lane_off, s_ref, reverse):
    L, N = bm.shape
    assert L == N == LANES
    ii = lax.broadcasted_iota(jnp.int32, (L, L), 0)
    jj = lax.broadcasted_iota(jnp.int32, (L, L), 1)
    mask = (jj >= ii) if reverse else (jj <= ii)
    lo_half = jj < SSD_HEAD_DIM
    shift = jnp.where(lane_off == 0, 0, LANES - lane_off)
    acs_c = pltpu.roll(acsc_all, shift, 1)
    last = 0 if reverse else L - 1
    tot_r = acs_r[:, last:last + 1]
    w_r = jnp.exp(tot_r - acs_r) * dt_r
    etot = jnp.broadcast_to(jnp.exp(tot_r), (HEADS_PADDED, LANES))

    cbt = _dot_nt(cm.astype(BF16), bm.astype(BF16))
    bt = bm.T
    ys = []
    for q in range(HEAD_PAIRS):
        xq = x[:, q * LANES:(q + 1) * LANES]
        sq = s_ref[q]
        x_a = jnp.where(lo_half, xq, 0.0).astype(BF16)
        x_b = jnp.where(lo_half, 0.0, xq).astype(BF16)
        s_a = jnp.where(lo_half, sq, 0.0).astype(BF16)
        s_b = jnp.where(lo_half, 0.0, sq).astype(BF16)
        m_parts, c_parts, b_parts = [], [], []
        for h in (2 * q, 2 * q + 1):
            col = jnp.broadcast_to(acs_c[:, h:h + 1], (L, L))
            decay = jnp.exp(jnp.where(mask, col - acs_r[h:h + 1, :], -jnp.inf))
            m_parts.append((cbt * decay * dt_r[h:h + 1, :]).astype(BF16))
            c_parts.append((cm * jnp.exp(col)).astype(BF16))
            b_parts.append((bt * w_r[h:h + 1, :]).astype(BF16))
        x_diag = jnp.concatenate([x_a, x_b], axis=0)
        lhs = jnp.concatenate(m_parts + c_parts, axis=1)
        rhs = jnp.concatenate([x_diag, s_a, s_b], axis=0)
        ys.append(_dot(lhs, rhs))
        dec = jnp.where(lo_half[0:1], etot[2 * q:2 * q + 1], etot[2 * q + 1:2 * q + 2])
        s_ref[q] = sq * dec + _dot(jnp.concatenate(b_parts, axis=1), x_diag)
    return jnp.concatenate(ys, axis=1)


def _ssd_kernel(xf_ref, bf_ref, cf_ref, dtTf_ref, acsTf_ref, acscf_ref,
                xb_ref, bb_ref, cb_ref, dtTb_ref, acsTb_ref, acscb_ref,
                yf_ref, yb_ref, sf_ref, sb_ref):
    gp = pl.program_id(1)

    @pl.when(pl.program_id(2) == 0)
    def _():
        sf_ref[...] = jnp.zeros_like(sf_ref)
        sb_ref[...] = jnp.zeros_like(sb_ref)

    for i in range(SSD_GROUPS_PER_STEP):
        g = gp * SSD_GROUPS_PER_STEP + i
        xs = slice(i * GROUP_X, (i + 1) * GROUP_X)
        ns = slice(i * SSD_STATE, (i + 1) * SSD_STATE)
        hs = slice(i * HEADS_PADDED, (i + 1) * HEADS_PADDED)
        yf_ref[0, :, xs] = _ssd_direction(
            xf_ref[0, :, xs], bf_ref[0, :, ns], cf_ref[0, :, ns], dtTf_ref[hs, :], acsTf_ref[hs, :],
            acscf_ref[...], g * HEADS_PADDED, sf_ref.at[i], False).astype(yf_ref.dtype)
        yb_ref[0, :, xs] = _ssd_direction(
            xb_ref[0, :, xs], bb_ref[0, :, ns], cb_ref[0, :, ns], dtTb_ref[hs, :], acsTb_ref[hs, :],
            acscb_ref[...], (SSD_GROUPS + g) * HEADS_PADDED, sb_ref.at[i], True).astype(yb_ref.dtype)


def _ssd(xbc3, dtT, acsT, acsc):
    bsz, s, _ = xbc3.shape
    L = SSD_CHUNK
    nc = s // L
    P = SSD_GROUPS_PER_STEP
    GP = SSD_GROUPS // P
    nb = SSD_INNER // (P * SSD_STATE)
    ncb = nb + GP
    fwd = lambda c: c
    bwd = lambda c: nc - 1 - c

    def specs(cidx, dirn):
        rowspec = pl.BlockSpec((P * HEADS_PADDED, L),
                               lambda b, g, c: (dirn * GP + g, b * nc + cidx(c)))
        return [
            pl.BlockSpec((1, L, P * GROUP_X), lambda b, g, c: (b, cidx(c), g)),
            pl.BlockSpec((1, L, P * SSD_STATE), lambda b, g, c: (b, cidx(c), nb + g)),
            pl.BlockSpec((1, L, P * SSD_STATE), lambda b, g, c: (b, cidx(c), ncb + g)),
            rowspec, rowspec,
            pl.BlockSpec((L, LANES), lambda b, g, c: (b * nc + cidx(c), 0)),
        ]

    out_specs = [pl.BlockSpec((1, L, P * GROUP_X), lambda b, g, c: (b, c, g)),
                 pl.BlockSpec((1, L, P * GROUP_X), lambda b, g, c: (b, nc - 1 - c, g))]
    return pl.pallas_call(
        _ssd_kernel,
        grid=(bsz, GP, nc),
        in_specs=specs(fwd, 0) + specs(bwd, 1),
        out_specs=out_specs,
        out_shape=[jax.ShapeDtypeStruct((bsz, s, SSD_INNER), BF16)] * 2,
        scratch_shapes=[pltpu.VMEM((P, HEAD_PAIRS, SSD_STATE, LANES), F32)] * 2,
        compiler_params=_cparams(("arbitrary", "arbitrary", "arbitrary")),
        name="ssd",
    )(xbc3, xbc3, xbc3, dtT, acsT, acsc, xbc3, xbc3, xbc3, dtT, acsT, acsc)


def _dft_tables(seq):
    n2n = LANES
    n1n = seq // n2n
    n1 = np.arange(n1n)
    k1 = np.arange(n1n)
    n2 = np.arange(n2n)
    ang = -2.0 * np.pi * (n2[:, None, None] * k1[None, :, None] / seq
                          + n1[None, None, :] * k1[None, :, None] / n1n)
    f1 = np.concatenate([np.cos(ang), np.sin(ang)], axis=1)
    k2 = np.arange(n2n)
    a2 = 2.0 * np.pi * np.outer(k2, n2) / n2n
    c2, s2 = np.cos(a2), np.sin(a2)
    g = np.block([[c2, s2], [-s2, c2]])
    ch = np.arange(FOURIER_GROUP_DIM)
    ac = 2.0 * np.pi * np.outer(ch, ch) / FOURIER_GROUP_DIM
    scale = 1.0 / math.sqrt(seq * FOURIER_GROUP_DIM)
    fc = np.concatenate([np.cos(ac), np.sin(ac)], axis=0) * scale
    return (jnp.asarray(f1, BF16), jnp.asarray(g, BF16), jnp.asarray(fc, BF16))


DFT_UNROLL = 4


def _dft_pitch(n1n):
    return 2 * n1n + 8


def _fourier_kernel(u_ref, f1_ref, g_ref, fc_ref, o_ref, a_ref):
    n2n = LANES
    n1n = u_ref.shape[2]
    pitch = _dft_pitch(n1n)

    def stage1(i, carry):
        for u in range(DFT_UNROLL):
            n2 = i * DFT_UNROLL + u
            xs = u_ref[0, 0, :, pl.ds(pl.multiple_of(n2 * LANES, LANES), LANES)].astype(BF16)
            a_ref[pl.ds(pl.multiple_of(n2 * pitch, 8), 2 * n1n), :] = _dot(f1_ref[n2], xs)
        return carry

    lax.fori_loop(0, n2n // DFT_UNROLL, stage1, 0)
    gm = g_ref[...]
    fc = fc_ref[...]

    def stage2(i, carry):
        for u in range(DFT_UNROLL):
            k1 = i * DFT_UNROLL + u
            re = a_ref[pl.ds(k1, n2n, stride=pitch), :]
            im = a_ref[pl.ds(n1n + k1, n2n, stride=pitch), :]
            a = jnp.concatenate([re, im], axis=0).astype(BF16)
            z = _dot(gm, a)
            zz = jnp.concatenate([z[:n2n], z[n2n:]], axis=1).astype(BF16)
            o_ref[0, pl.ds(k1, n2n, stride=n1n), :] = _dot(zz, fc)
        return carry

    lax.fori_loop(0, n1n // DFT_UNROLL, stage2, 0)


def _fourier(uf3):
    bsz, s, w = uf3.shape
    f1, g, fc = _dft_tables(s)
    n1n = s // LANES
    ng = w // FOURIER_GROUP_DIM
    u = uf3.reshape(bsz, n1n, LANES, ng, FOURIER_GROUP_DIM).transpose(0, 3, 1, 2, 4)
    u = u.reshape(bsz, ng, n1n, LANES * FOURIER_GROUP_DIM)
    return pl.pallas_call(
        _fourier_kernel,
        grid=(bsz, ng),
        in_specs=[pl.BlockSpec((1, 1, n1n, LANES * FOURIER_GROUP_DIM), lambda b, j: (b, j, 0, 0)),
                  pl.BlockSpec(f1.shape, lambda b, j: (0, 0, 0)),
                  pl.BlockSpec(g.shape, lambda b, j: (0, 0)),
                  pl.BlockSpec(fc.shape, lambda b, j: (0, 0))],
        out_specs=pl.BlockSpec((1, s, LANES), lambda b, j: (b, 0, j)),
        out_shape=jax.ShapeDtypeStruct((bsz, s, w), F32),
        scratch_shapes=[pltpu.VMEM((LANES * _dft_pitch(n1n), LANES), F32)],
        compiler_params=_cparams(("arbitrary", "arbitrary")),
        name="fourier",
    )(u, f1, g, fc)


def _mix_kernel(yf_ref, yb_ref, xs_ref, z_ref, fm_ref, gates_ref, x_ref, mod_ref, dskip_ref,
                ssdn_ref, wbs_ref, wbf_ref, wout_ref, gpost_ref, o_ref):
    m = mod_ref[0]
    y = yf_ref[...].astype(F32) + yb_ref[...].astype(F32) + dskip_ref[...] * xs_ref[...]
    v = y * _silu(z_ref[...].astype(F32))
    parts = []
    for g in range(SSD_GROUPS):
        vg = v[:, g * GROUP_X:(g + 1) * GROUP_X]
        parts.append(vg * lax.rsqrt(jnp.mean(vg * vg, axis=-1, keepdims=True) + RMS_EPS))
    vn = jnp.concatenate(parts, axis=1) * ssdn_ref[...]
    y_ssd = _dot(vn.astype(BF16), wbs_ref[...])
    y_fou = _dot(fm_ref[...].astype(BF16), wbf_ref[...])
    gt = _sigmoid(gates_ref[...].astype(F32))
    d = y_ssd.shape[1]
    mixed = gt[:, :d] * y_ssd + gt[:, d:] * y_fou
    mo = _dot(mixed.astype(BF16), wout_ref[...])
    o_ref[...] = x_ref[...] + m[2:3] * _rms(mo, gpost_ref[...])


def _mix(yf, yb, xbc, z, fm, gates, x2, mod, dskip, ssdn, wbs, wbf, wout, gpost, seq):
    t, d = x2.shape
    tm = TOKEN_TILE
    per_b = seq // tm
    full = lambda a: pl.BlockSpec(a.shape, lambda i: (0,) * a.ndim)
    row = lambda n: pl.BlockSpec((tm, n), lambda i: (i, 0))
    return pl.pallas_call(
        _mix_kernel,
        grid=(t // tm,),
        in_specs=[row(SSD_INNER), row(SSD_INNER), row(SSD_INNER), row(SSD_INNER),
                  row(FOURIER_WIDTH), row(2 * d), row(d),
                  pl.BlockSpec((1, 6, d), lambda i: (i // per_b, 0, 0)),
                  full(dskip), full(ssdn), full(wbs), full(wbf), full(wout), full(gpost)],
        out_specs=row(d),
        out_shape=jax.ShapeDtypeStruct((t, d), F32),
        compiler_params=_cparams(("arbitrary",)),
        name="mix",
    )(yf, yb, xbc, z, fm, gates, x2, mod, dskip, ssdn, wbs, wbf, wout, gpost)


def _route_kernel(x_ref, mod_ref, gpre_ref, wr_ref, rb_ref, hp_ref, idx_ref, w_ref, rank_ref,
                  counts_ref, run_ref):
    @pl.when(pl.program_id(0) == 0)
    def _():
        run_ref[...] = jnp.zeros_like(run_ref)

    m = mod_ref[0]
    h = _rms(x_ref[...], gpre_ref[...]) * (1.0 + m[4:5]) + m[3:4]
    tm, d = h.shape
    half = d // 2
    bits = pltpu.bitcast(h.astype(BF16).astype(F32), jnp.uint32)
    hp_ref[...] = (bits[:, :half] >> 16) | (bits[:, half:] & jnp.uint32(0xFFFF0000))

    h_hi = h.astype(BF16)
    h_lo = (h - h_hi.astype(F32)).astype(BF16)
    w_hi = wr_ref[0]
    logits = _dot(h_hi, w_hi) + (_dot(h_hi, wr_ref[1]) + _dot(h_lo, w_hi))
    scores = _sigmoid(logits)
    biased = scores + rb_ref[...]
    ne = scores.shape[1]
    per_group = ne // N_EXPERT_GROUPS
    lane = lax.broadcasted_iota(jnp.int32, (tm, ne), 1)
    lane_f = lane.astype(F32)
    grp = lane // per_group
    out_lane = lax.broadcasted_iota(jnp.int32, (tm, LANES), 1)
    neg = -jnp.inf

    def argmax_first(v):
        mx = jnp.max(v, axis=-1, keepdims=True)
        ix = jnp.min(jnp.where(v == mx, lane_f, float(ne)), axis=-1, keepdims=True)
        return mx, ix

    gs = []
    gmat = jnp.full((tm, LANES), neg, F32)
    for g in range(N_EXPERT_GROUPS):
        vg = jnp.where(grp == g, biased, neg)
        m1, i1 = argmax_first(vg)
        m2 = jnp.max(jnp.where(lane_f == i1, neg, vg), axis=-1, keepdims=True)
        gs.append(m1 + m2)
        gmat = jnp.where(out_lane == g, gs[g], gmat)
    ahead = jnp.zeros((tm, LANES), F32)
    for o in range(N_EXPERT_GROUPS):
        beats = (gs[o] > gmat) | ((gs[o] == gmat) & (out_lane > o))
        ahead = ahead + beats.astype(F32)
    sel = ((ahead < TOPK_GROUPS) & (out_lane < N_EXPERT_GROUPS)).astype(BF16)
    eg = lax.broadcasted_iota(jnp.int32, (LANES, ne), 0)
    ee = lax.broadcasted_iota(jnp.int32, (LANES, ne), 1)
    allowed = _dot(sel, (eg == ee // per_group).astype(BF16)) > 0.5
    masked = jnp.where(allowed, biased, neg)

    idx_out = jnp.zeros((tm, LANES), F32)
    w_out = jnp.zeros((tm, LANES), F32)
    w_sum = jnp.zeros((tm, 1), F32)
    hits = []
    for k in range(TOP_K):
        _, ik = argmax_first(masked)
        hit = lane_f == ik
        hits.append(hit)
        wk = jnp.sum(jnp.where(hit, scores, 0.0), axis=-1, keepdims=True)
        masked = jnp.where(hit, neg, masked)
        idx_out = jnp.where(out_lane == k, ik, idx_out)
        w_out = jnp.where(out_lane == k, wk, w_out)
        w_sum = w_sum + wk
    idx_ref[...] = idx_out.astype(jnp.int32)
    w_ref[...] = w_out / w_sum * ROUTED_SCALE

    chosen = functools.reduce(jnp.logical_or, hits).astype(F32)
    ti = lax.broadcasted_iota(jnp.int32, (tm, tm), 0)
    tj = lax.broadcasted_iota(jnp.int32, (tm, tm), 1)
    before = _dot((tj < ti).astype(BF16), chosen.astype(BF16)) + run_ref[...]
    rank_out = jnp.zeros((tm, LANES), jnp.int32)
    for k in range(TOP_K):
        rk = jnp.sum(jnp.where(hits[k], before, 0.0), axis=-1, keepdims=True)
        rank_out = jnp.where(out_lane == k, rk.astype(jnp.int32), rank_out)
    rank_ref[...] = rank_out
    run_ref[...] = run_ref[...] + jnp.sum(chosen, axis=0, keepdims=True)
    counts_ref[...] = run_ref[...]


def _route(x1, mod, gpre, w_router, router_bias, seq):
    t, d = x1.shape
    tm = TOKEN_TILE
    per_b = seq // tm
    full = lambda a: pl.BlockSpec(a.shape, lambda i: (0,) * a.ndim)
    row = lambda n: pl.BlockSpec((tm, n), lambda i: (i, 0))
    rb = router_bias.reshape(1, -1)
    w_hi = w_router.astype(BF16)
    w_router = jnp.stack([w_hi, (w_router - w_hi.astype(F32)).astype(BF16)])
    return pl.pallas_call(
        _route_kernel,
        grid=(t // tm,),
        in_specs=[row(d), pl.BlockSpec((1, 6, d), lambda i: (i // per_b, 0, 0)), full(gpre),
                  full(w_router), full(rb)],
        out_specs=[row(d // 2), row(LANES), row(LANES), row(LANES),
                   pl.BlockSpec((1, N_EXPERTS), lambda i: (0, 0))],
        out_shape=[jax.ShapeDtypeStruct((t, d // 2), jnp.uint32),
                   jax.ShapeDtypeStruct((t, LANES), jnp.int32),
                   jax.ShapeDtypeStruct((t, LANES), F32),
                   jax.ShapeDtypeStruct((t, LANES), jnp.int32),
                   jax.ShapeDtypeStruct((1, N_EXPERTS), F32)],
        scratch_shapes=[pltpu.VMEM((1, N_EXPERTS), F32)],
        compiler_params=_cparams(("arbitrary",)),
        name="route",
    )(x1, mod, gpre, w_router, rb)


GATHER_UNROLL = 8


ROW_TILE = 8


def _tile_rows(r):
    return pl.ds(pl.multiple_of(r * ROW_TILE, ROW_TILE), ROW_TILE)


def _gather_rows(idx_ref, n_rows, src_hbm, dst_of, sem):
    def body(q, carry):
        for u in range(GATHER_UNROLL):
            src = idx_ref[0, 0, q * GATHER_UNROLL + u]
            pltpu.make_async_copy(src_hbm.at[_tile_rows(src), :], dst_of(q, u), sem).start(
                priority=u % 2)
        return carry

    lax.fori_loop(0, n_rows // GATHER_UNROLL, body, 0)


def _to_tiled_rows(o_ref, y):
    rows, width = y.shape
    for r in range(rows // ROW_TILE):
        for j in range(width // LANES):
            o_ref[pl.ds(r * ROW_TILE * ROW_TILE + j, ROW_TILE, stride=ROW_TILE), :] = (
                y[r * ROW_TILE:(r + 1) * ROW_TILE, j * LANES:(j + 1) * LANES])


def _from_tiled_rows(load, rows, width):
    cols = []
    for j in range(width // LANES):
        pieces = [load(r * ROW_TILE * ROW_TILE + j, ROW_TILE, ROW_TILE) for r in range(rows // ROW_TILE)]
        cols.append(jnp.concatenate(pieces, axis=0))
    return jnp.concatenate(cols, axis=1)


def _dest_kernel(idx_ref, rank_ref, start_ref, o_ref):
    idx = idx_ref[...]
    rank = rank_ref[...]
    start = start_ref[...]
    tm = idx.shape[0]
    lane = lax.broadcasted_iota(jnp.int32, (tm, start.shape[1]), 1)
    out_lane = lax.broadcasted_iota(jnp.int32, (tm, LANES), 1)
    out = jnp.zeros((tm, LANES), jnp.int32)
    for k in range(TOP_K):
        base = jnp.sum(jnp.where(lane == idx[:, k:k + 1], start, 0.0), axis=-1, keepdims=True)
        out = jnp.where(out_lane == k, base.astype(jnp.int32) + rank[:, k:k + 1], out)
    o_ref[...] = out


def _dest(idx_pad, rank_pad, pad_start):
    t = idx_pad.shape[0]
    tm = DEST_TILE
    row = pl.BlockSpec((tm, LANES), lambda i: (i, 0))
    return pl.pallas_call(
        _dest_kernel,
        grid=(t // tm,),
        in_specs=[row, row, pl.BlockSpec((1, N_EXPERTS), lambda i: (0, 0))],
        out_specs=row,
        out_shape=jax.ShapeDtypeStruct((t, LANES), jnp.int32),
        compiler_params=_cparams(("arbitrary",)),
        name="dest",
    )(idx_pad, rank_pad, pad_start.astype(F32).reshape(1, N_EXPERTS))


def _scatter_kernel(b0_ref, nb_ref, nu_ref, dest_ref, hp_ref, xs_hbm, zbuf, sem, zsem):
    ts = hp_ref.shape[0]
    bm = zbuf.shape[0]
    n_blocks = xs_hbm.shape[0] // bm

    @pl.when(pl.program_id(0) == 0)
    def _():
        zbuf[...] = jnp.zeros_like(zbuf)
        n_used = nu_ref[0]

        def zero_block(g):
            return pltpu.make_async_copy(zbuf, xs_hbm.at[pl.ds(pl.multiple_of(g * bm, bm), bm), :], zsem)

        def tails(action):
            def step(e, carry):
                @pl.when(nb_ref[e] > 0)
                def _():
                    action(zero_block(b0_ref[e] + nb_ref[e] - 1))
                return carry
            lax.fori_loop(0, N_EXPERTS, step, 0)

        def unused(action):
            def step(g, carry):
                action(zero_block(g))
                return carry
            lax.fori_loop(n_used, n_blocks, step, 0)

        tails(lambda c: c.start())
        unused(lambda c: c.start())
        tails(lambda c: c.wait())
        unused(lambda c: c.wait())

    def body(t, carry):
        for k in range(TOP_K):
            dst = dest_ref[0, 0, t * TOP_K + k]
            pltpu.make_async_copy(hp_ref.at[pl.ds(t, 1), :], xs_hbm.at[pl.ds(dst, 1), :], sem).start(
                priority=k % 2)
        return carry

    lax.fori_loop(0, ts, body, 0)
    for k in range(TOP_K):
        pltpu.make_async_copy(hp_ref, xs_hbm.at[pl.ds(0, ts), :], sem).wait()


def _scatter(dest, hp, first_block, n_block, n_used, n_rows):
    t, half = hp.shape
    ts = SCATTER_TILE
    n_steps = t // ts
    dest3 = dest.reshape(n_steps, 1, ts * TOP_K)
    grid_spec = pltpu.PrefetchScalarGridSpec(
        num_scalar_prefetch=3,
        grid=(n_steps,),
        in_specs=[pl.BlockSpec((1, 1, ts * TOP_K), lambda i, b0, nb, nu: (i, 0, 0),
                               memory_space=pltpu.SMEM),
                  pl.BlockSpec((ts, half), lambda i, b0, nb, nu: (i, 0))],
        out_specs=pl.BlockSpec(memory_space=pl.ANY),
        scratch_shapes=[pltpu.VMEM((EXPERT_ROWS, half), jnp.uint32),
                        pltpu.SemaphoreType.DMA(()), pltpu.SemaphoreType.DMA(())],
    )
    return pl.pallas_call(
        _scatter_kernel,
        grid_spec=grid_spec,
        out_shape=jax.ShapeDtypeStruct((n_rows, half), jnp.uint32),
        compiler_params=_cparams(("arbitrary",)),
        name="scatter",
    )(first_block, n_block, n_used, dest3, hp)


def _expert_kernel(b0_ref, nb_ref, nu_ref, xs_hbm, w13_ref, w2_ref, y_hbm,
                   xbuf, ybuf, w13b, w2b, sem_in, sem_out):
    e = pl.program_id(0)
    n_used = nu_ref[0]
    b0 = b0_ref[e]
    nb = nb_ref[e]
    n_in, bm = xbuf.shape[:2]
    n_out, y_rows = ybuf.shape[:2]
    ahead = n_in - 1

    def in_copy(g, slot):
        return pltpu.make_async_copy(xs_hbm.at[pl.ds(pl.multiple_of(g * bm, bm), bm), :],
                                     xbuf.at[slot], sem_in.at[slot])

    def out_copy(g, slot):
        return pltpu.make_async_copy(ybuf.at[slot],
                                     y_hbm.at[pl.ds(pl.multiple_of(g * y_rows, y_rows), y_rows), :],
                                     sem_out.at[slot])

    @pl.when(nb > 0)
    def _():
        w13b[...] = w13_ref[0].astype(BF16)
        w2b[...] = w2_ref[0].astype(BF16)

    @pl.when((nb > 0) & (b0 == 0))
    def _():
        for a in range(ahead):
            @pl.when(a < n_used)
            def _():
                in_copy(a, a).start()

    def block(g, carry):
        in_slot = g % n_in
        out_slot = g % n_out
        in_copy(g, in_slot).wait()

        @pl.when(g + ahead < n_used)
        def _():
            in_copy(g + ahead, (g + ahead) % n_in).start()

        @pl.when(g >= n_out)
        def _():
            out_copy(g - n_out, out_slot).wait()

        w = xbuf[in_slot]
        lo = pltpu.bitcast(w << 16, F32).astype(BF16)
        hi = pltpu.bitcast(w & jnp.uint32(0xFFFF0000), F32).astype(BF16)
        half = lo.shape[1]
        ag = _dot(lo, w13b[:half, :]) + _dot(hi, w13b[half:, :])
        hh = ag.shape[1] // 2
        act = (_silu(ag[:, :hh]) * ag[:, hh:]).astype(BF16)
        _to_tiled_rows(ybuf.at[out_slot], _dot(act, w2b[...]))
        out_copy(g, out_slot).start()
        return carry

    lax.fori_loop(b0, b0 + nb, block, 0)

    @pl.when(e == pl.num_programs(0) - 1)
    def _():
        for back in range(1, n_out + 1):
            @pl.when(n_used >= back)
            def _():
                out_copy(n_used - back, (n_used - back) % n_out).wait()

        n_blocks = y_hbm.shape[0] // y_rows
        ybuf[0] = jnp.zeros((y_rows, LANES), F32)

        def start_zero(g, carry):
            out_copy(g, 0).start()
            return carry

        def wait_zero(g, carry):
            out_copy(g, 0).wait()
            return carry

        lax.fori_loop(n_used, n_blocks, start_zero, 0)
        lax.fori_loop(n_used, n_blocks, wait_zero, 0)


def _experts(xs, first_block, n_block, n_used, w13, w2):
    n_rows, half = xs.shape
    ne, d, h2 = w13.shape
    bm = EXPERT_ROWS
    y_rows = bm * d // LANES
    grid_spec = pltpu.PrefetchScalarGridSpec(
        num_scalar_prefetch=3,
        grid=(ne,),
        in_specs=[
            pl.BlockSpec(memory_space=pl.ANY),
            pl.BlockSpec((1, d, h2), lambda e, b0, nb, nu: (e, 0, 0)),
            pl.BlockSpec((1, h2 // 2, d), lambda e, b0, nb, nu: (e, 0, 0)),
        ],
        out_specs=pl.BlockSpec(memory_space=pl.ANY),
        scratch_shapes=[pltpu.VMEM((EXPERT_IN_SLOTS, bm, half), jnp.uint32),
                        pltpu.VMEM((EXPERT_OUT_SLOTS, y_rows, LANES), F32),
                        pltpu.VMEM((d, h2), BF16),
                        pltpu.VMEM((h2 // 2, d), BF16),
                        pltpu.SemaphoreType.DMA((EXPERT_IN_SLOTS,)),
                        pltpu.SemaphoreType.DMA((EXPERT_OUT_SLOTS,))],
    )
    return pl.pallas_call(
        _expert_kernel,
        grid_spec=grid_spec,
        out_shape=jax.ShapeDtypeStruct((n_rows * d // LANES, LANES), F32),
        compiler_params=_cparams(("arbitrary",)),
        name="experts",
    )(first_block, n_block, n_used, xs, w13, w2)


def _final_kernel(dc_ref, dn_ref, y_hbm, w_ref, x_ref, mod_ref, gpre_ref, gpost_ref, w13s_ref,
                  w2s_ref, o_ref, buf, sem):
    i = pl.program_id(0)
    slot = i % 2
    tm = x_ref.shape[0]

    def issue(d_ref, s):
        _gather_rows(d_ref, tm * TOP_K, y_hbm,
                     lambda q, u: buf.at[s, u, _tile_rows(q), :], sem.at[s])

    @pl.when(i == 0)
    def _():
        issue(dc_ref, 0)

    @pl.when(i + 1 < pl.num_programs(0))
    def _():
        issue(dn_ref, 1 - slot)

    for k in range(TOP_K):
        pltpu.make_async_copy(y_hbm.at[pl.ds(0, tm * ROW_TILE), :], buf.at[slot, k],
                              sem.at[slot]).wait()

    w = w_ref[...]
    x1 = x_ref[...]
    routed = None
    for k in range(TOP_K):
        yk = _from_tiled_rows(lambda a, n, st: buf[slot, k, pl.ds(a, n, stride=st), :], tm, x1.shape[1])
        routed = yk * w[:, k:k + 1] if routed is None else routed + yk * w[:, k:k + 1]

    m = mod_ref[0]
    h = (_rms(x1, gpre_ref[...]) * (1.0 + m[4:5]) + m[3:4]).astype(BF16)
    ag = _dot(h, w13s_ref[...])
    hh = ag.shape[1] // 2
    act = (_silu(ag[:, :hh]) * ag[:, hh:]).astype(BF16)
    ffn = routed + _dot(act, w2s_ref[...])
    o_ref[...] = x1 + m[5:6] * _rms(ffn, gpost_ref[...])


def _final(dest, y_sorted, top_w, x1, mod, gpre, gpost, w13s, w2s, seq):
    t, d = x1.shape
    tm = COMBINE_TILE
    per_b = seq // tm
    n_steps = t // tm
    dest3 = dest.reshape(n_steps, 1, tm * TOP_K)
    full = lambda a: pl.BlockSpec(a.shape, lambda i: (0,) * a.ndim)
    row = lambda n: pl.BlockSpec((tm, n), lambda i: (i, 0))
    return pl.pallas_call(
        _final_kernel,
        grid=(n_steps,),
        in_specs=[
            pl.BlockSpec((1, 1, tm * TOP_K), lambda i: (i, 0, 0), memory_space=pltpu.SMEM),
            pl.BlockSpec((1, 1, tm * TOP_K), lambda i: (jnp.minimum(i + 1, n_steps - 1), 0, 0),
                         memory_space=pltpu.SMEM),
            pl.BlockSpec(memory_space=pl.ANY),
            row(LANES), row(d), pl.BlockSpec((1, 6, d), lambda i: (i // per_b, 0, 0)),
            full(gpre), full(gpost), full(w13s), full(w2s)],
        out_specs=row(d),
        out_shape=jax.ShapeDtypeStruct((t, d), F32),
        scratch_shapes=[pltpu.VMEM((2, TOP_K, tm * ROW_TILE, LANES), F32),
                        pltpu.SemaphoreType.DMA((2,))],
        compiler_params=_cparams(("arbitrary",)),
        name="combine_final",
    )(dest3, dest3, y_sorted, top_w, x1, mod, gpre, gpost, w13s, w2s)


def _pad_heads(v):
    lead = v.shape[:-1]
    v = v.reshape(lead + (2, SSD_GROUPS, HEADS_PER_GROUP))
    v = jnp.pad(v, [(0, 0)] * (len(lead) + 2) + [(0, HEADS_PADDED - HEADS_PER_GROUP)])
    return v.reshape(lead + (2 * SSD_GROUPS * HEADS_PADDED,))


def _dispatch_plan(counts, n_tokens):
    bm = EXPERT_ROWS
    counts = counts.reshape(N_EXPERTS).astype(jnp.int32)
    padded = (counts + bm - 1) // bm * bm
    pad_end = jnp.cumsum(padded)
    pad_start = pad_end - padded
    n_rows = -(-n_tokens * TOP_K // bm) * bm + N_EXPERTS * bm
    first_block = (pad_start // bm).astype(jnp.int32)
    n_block = (padded // bm).astype(jnp.int32)
    n_used = (pad_end[-1] // bm).astype(jnp.int32).reshape(1)
    return pad_start, first_block, n_block, n_used, n_rows


def _layer(x, c, w_ada, b_ada, pre_norm_mix, post_norm_mix, pre_norm_ffn, post_norm_ffn, w_in,
           conv_w, conv_b, dt_bias_fwd, dt_bias_bwd, a_log_fwd, a_log_bwd, d_skip, ssd_norm,
           w_branch_ssd, w_branch_fourier, w_out, w_router, router_bias, w13_experts, w2_experts,
           w13_shared, w2_shared):
    bsz, seq, d = x.shape
    t = bsz * seq
    x2 = x.reshape(t, d)
    row = lambda v: v.reshape(1, -1).astype(F32)

    mod = _ada(c, w_ada, b_ada)

    i1 = SSD_INNER
    i2 = i1 + XBC_WIDTH
    i3 = i2 + 2 * SSD_HEADS
    i4 = i3 + FOURIER_WIDTH
    n_dt = 2 * SSD_GROUPS * HEADS_PADDED
    w_dtp = _pad_heads(w_in[:, i2:i3])
    wdtc = jnp.pad(w_dtp, ((0, 0), (0, LANES - n_dt))).astype(BF16)
    wdtT = w_dtp.T.astype(BF16)
    bias_p = _pad_heads(jnp.concatenate([dt_bias_fwd, dt_bias_bwd]).astype(F32))
    a_p = _pad_heads(-jnp.exp(jnp.concatenate([a_log_fwd, a_log_bwd]).astype(F32)))
    pad_row = lambda v: jnp.pad(v, (0, LANES - n_dt)).reshape(1, LANES)
    z, xbc, acsc, dtT, acsT, uf, gates = _inproj(
        x2, mod, row(pre_norm_mix), w_in[:, :i1].astype(BF16), w_in[:, i1:i2].astype(BF16),
        wdtc, wdtT, w_in[:, i3:i4].astype(BF16), w_in[:, i4:].astype(BF16),
        pad_row(bias_p), pad_row(a_p), bias_p.reshape(n_dt, 1), a_p.reshape(n_dt, 1), seq)

    xbc3 = _conv(xbc.reshape(bsz, seq, XBC_WIDTH), conv_w, conv_b)

    yf, yb = _ssd(xbc3, dtT, acsT, acsc)

    fm = _fourier(uf.reshape(bsz, seq, FOURIER_WIDTH))

    x1 = _mix(yf.reshape(t, SSD_INNER), yb.reshape(t, SSD_INNER), xbc3.reshape(t, XBC_WIDTH), z,
              fm.reshape(t, FOURIER_WIDTH), gates, x2, mod,
              row(jnp.repeat(d_skip, SSD_HEAD_DIM)), row(ssd_norm), w_branch_ssd.astype(BF16),
              w_branch_fourier.astype(BF16), w_out.astype(BF16), row(post_norm_mix), seq)

    hp, idx_pad, w_pad, rank_pad, counts = _route(x1, mod, row(pre_norm_ffn),
                                                  w_router.astype(F32), router_bias, seq)
    pad_start, first_block, n_block, n_used, n_rows = _dispatch_plan(counts, t)
    dest = _dest(idx_pad, rank_pad, pad_start)[:, :TOP_K]
    xs = _scatter(dest, hp, first_block, n_block, n_used, n_rows)
    y_sorted = _experts(xs, first_block, n_block, n_used, w13_experts, w2_experts)
    out = _final(dest, y_sorted, w_pad, x1, mod, row(pre_norm_ffn), row(post_norm_ffn),
                 w13_shared.astype(BF16), w2_shared.astype(BF16), seq)
    return out.reshape(bsz, seq, d)


def kernel(x, c, w_ada, b_ada, pre_norm_mix, post_norm_mix, pre_norm_ffn, post_norm_ffn, w_in,
           conv_w, conv_b, dt_bias_fwd, dt_bias_bwd, a_log_fwd, a_log_bwd, d_skip, ssd_norm,
           w_branch_ssd, w_branch_fourier, w_out, w_router, router_bias, w13_experts, w2_experts,
           w13_shared, w2_shared):
    for layer in range(w_ada.shape[0]):
        x = _layer(x, c, w_ada[layer], b_ada[layer], pre_norm_mix[layer], post_norm_mix[layer],
                   pre_norm_ffn[layer], post_norm_ffn[layer], w_in[layer], conv_w[layer],
                   conv_b[layer], dt_bias_fwd[layer], dt_bias_bwd[layer], a_log_fwd[layer],
                   a_log_bwd[layer], d_skip[layer], ssd_norm[layer], w_branch_ssd[layer],
                   w_branch_fourier[layer], w_out[layer], w_router[layer], router_bias[layer],
                   w13_experts[layer], w2_experts[layer], w13_shared[layer], w2_shared[layer])
    return x
```

```python
import functools
import math

import numpy as np
import jax
import jax.numpy as jnp
from jax import lax
from jax.experimental import pallas as pl
from jax.experimental.pallas import tpu as pltpu

F32 = jnp.float32
BF16 = jnp.bfloat16
HIGHEST = lax.Precision.HIGHEST

D_MODEL = 1024
SSD_HEADS = 24
SSD_HEAD_DIM = 64
SSD_INNER = SSD_HEADS * SSD_HEAD_DIM
SSD_GROUPS = 4
HEADS_PER_GROUP = SSD_HEADS // SSD_GROUPS
HEADS_PADDED = 8
SSD_STATE = 128
SSD_CONV = 5
SSD_CHUNK = 128
XBC_WIDTH = SSD_INNER + 2 * SSD_GROUPS * SSD_STATE
GROUP_X = HEADS_PER_GROUP * SSD_HEAD_DIM
FOURIER_WIDTH = 512
FOURIER_GROUP_DIM = 128
N_EXPERTS = 256
TOP_K = 8
N_EXPERT_GROUPS = 8
TOPK_GROUPS = 4
EXPERT_HIDDEN = 256
SHARED_HIDDEN = 256
ROUTED_SCALE = 2.5
RMS_EPS = 1e-6

LANES = 128
VMEM_LIMIT = 56 * 1024 * 1024
TOKEN_TILE = 256
INPROJ_TILE = 256
EXPERT_ROWS = 256
COMBINE_TILE = 128
SCATTER_TILE = 512
DEST_TILE = 1024
EXPERT_IN_SLOTS = 4
EXPERT_OUT_SLOTS = 3
CONV_ROWS = 256


def _cparams(sem):
    return pltpu.CompilerParams(dimension_semantics=sem, vmem_limit_bytes=VMEM_LIMIT)


def _dot(a, b, precision=None):
    return jnp.dot(a, b, preferred_element_type=F32, precision=precision)


def _dot_nt(a, b, precision=None):
    return lax.dot_general(a, b, (((1,), (1,)), ((), ())), preferred_element_type=F32,
                           precision=precision)


def _dot_tn(a, b):
    return lax.dot_general(a, b, (((0,), (0,)), ((), ())), preferred_element_type=F32)


def _sigmoid(x):
    return 1.0 / (1.0 + jnp.exp(-x))


def _silu(x):
    return x * _sigmoid(x)


def _softplus(x):
    return jnp.maximum(x, 0.0) + jnp.log1p(jnp.exp(-jnp.abs(x)))


def _rms(x, g):
    return x * lax.rsqrt(jnp.mean(x * x, axis=-1, keepdims=True) + RMS_EPS) * g


def _ada_kernel(c_ref, w_ref, b_ref, o_ref):
    o_ref[...] = _dot(_silu(c_ref[...]), w_ref[...], HIGHEST) + b_ref[...]


def _ada(c, w_ada, b_ada):
    bsz, d = c.shape
    rows = 8
    cp = jnp.zeros((rows, d), F32).at[:bsz].set(c)
    n = w_ada.shape[1]
    tn = 1536
    out = pl.pallas_call(
        _ada_kernel,
        grid=(n // tn,),
        in_specs=[pl.BlockSpec((rows, d), lambda j: (0, 0)),
                  pl.BlockSpec((d, tn), lambda j: (0, j)),
                  pl.BlockSpec((1, tn), lambda j: (0, j))],
        out_specs=pl.BlockSpec((rows, tn), lambda j: (0, j)),
        out_shape=jax.ShapeDtypeStruct((rows, n), F32),
        compiler_params=_cparams(("arbitrary",)),
        name="adaln",
    )(cp, w_ada, b_ada.reshape(1, n))
    return out[:bsz].reshape(bsz, 6, d)


def _split3(a):
    a1 = a.astype(BF16)
    r1 = a - a1.astype(F32)
    a2 = r1.astype(BF16)
    a3 = (r1 - a2.astype(F32)).astype(BF16)
    return a1, a2, a3


def _inproj_kernel(x_ref, mod_ref, g_ref, wz_ref, wxbc_ref, wdtc_ref, wdtT_ref, wuf_ref, wg_ref,
                   bias_row_ref, a_row_ref, bias_col_ref, a_col_ref,
                   z_ref, xbc_ref, acsc_ref, dtT_ref, acsT_ref, uf_ref, gates_ref):
    m = mod_ref[0]
    h = _rms(x_ref[...], g_ref[...]) * (1.0 + m[1:2]) + m[0:1]
    hb = h.astype(BF16)
    z_ref[...] = _dot(hb, wz_ref[...]).astype(z_ref.dtype)
    xbc_ref[...] = _dot(hb, wxbc_ref[...]).astype(xbc_ref.dtype)
    uf_ref[...] = _dot(hb, wuf_ref[...]).astype(uf_ref.dtype)
    gates_ref[...] = _dot(hb, wg_ref[...]).astype(gates_ref.dtype)

    tm = hb.shape[0]
    n_fwd = SSD_GROUPS * HEADS_PADDED
    ii = lax.broadcasted_iota(jnp.int32, (tm, tm), 0)
    jj = lax.broadcasted_iota(jnp.int32, (tm, tm), 1)
    same = (ii // SSD_CHUNK) == (jj // SSD_CHUNK)
    tri_f = (same & (jj <= ii)).astype(BF16)
    tri_b = (same & (jj >= ii)).astype(BF16)

    dt_c = _softplus(_dot(hb, wdtc_ref[...]) + bias_row_ref[...])
    pieces = _split3(dt_c * a_row_ref[...])
    acs_f = sum(_dot(tri_f, p) for p in pieces)
    acs_b = sum(_dot(tri_b, p) for p in pieces)
    lane = lax.broadcasted_iota(jnp.int32, acs_f.shape, 1)
    acsc_ref[...] = jnp.where(lane < n_fwd, acs_f, acs_b)

    dt_t = _softplus(_dot_nt(wdtT_ref[...], hb) + bias_col_ref[...])
    pieces = _split3(dt_t * a_col_ref[...])
    acs_f = sum(_dot_nt(p, tri_f) for p in pieces)
    acs_b = sum(_dot_nt(p, tri_b) for p in pieces)
    sub = lax.broadcasted_iota(jnp.int32, acs_f.shape, 0)
    dtT_ref[...] = dt_t
    acsT_ref[...] = jnp.where(sub < n_fwd, acs_f, acs_b)


def _inproj(x2, mod, g, wz, wxbc, wdtc, wdtT, wuf, wg, bias_row, a_row, bias_col, a_col, seq):
    t, d = x2.shape
    tm = INPROJ_TILE
    per_b = seq // tm
    full = lambda a: pl.BlockSpec(a.shape, lambda i: (0,) * a.ndim)
    row = lambda n: pl.BlockSpec((tm, n), lambda i: (i, 0))
    nd = wdtT.shape[0]
    colspec = pl.BlockSpec((nd, tm), lambda i: (0, i))
    return pl.pallas_call(
        _inproj_kernel,
        grid=(t // tm,),
        in_specs=[row(d), pl.BlockSpec((1, 6, d), lambda i: (i // per_b, 0, 0)), full(g),
                  full(wz), full(wxbc), full(wdtc), full(wdtT), full(wuf), full(wg),
                  full(bias_row), full(a_row), full(bias_col), full(a_col)],
        out_specs=[row(wz.shape[1]), row(wxbc.shape[1]), row(LANES), colspec, colspec,
                   row(wuf.shape[1]), row(wg.shape[1])],
        out_shape=[jax.ShapeDtypeStruct((t, wz.shape[1]), BF16),
                   jax.ShapeDtypeStruct((t, wxbc.shape[1]), BF16),
                   jax.ShapeDtypeStruct((t, LANES), F32),
                   jax.ShapeDtypeStruct((nd, t), F32),
                   jax.ShapeDtypeStruct((nd, t), F32),
                   jax.ShapeDtypeStruct((t, wuf.shape[1]), BF16),
                   jax.ShapeDtypeStruct((t, wg.shape[1]), BF16)],
        compiler_params=_cparams(("arbitrary",)),
        name="inproj",
    )(x2, mod, g, wz, wxbc, wdtc, wdtT, wuf, wg, bias_row, a_row, bias_col, a_col)


def _conv_kernel(u_ref, w_ref, b_ref, o_ref, pad_ref):
    s = u_ref.shape[1]
    halo = 8
    pad_ref[0:halo, :] = jnp.zeros((halo, LANES), F32)
    pad_ref[halo + s:2 * halo + s, :] = jnp.zeros((halo, LANES), F32)
    pad_ref[halo:halo + s, :] = u_ref[0].astype(F32)
    w = w_ref[...]
    b = b_ref[...]
    half = (SSD_CONV - 1) // 2
    for r in range(s // CONV_ROWS):
        base = r * CONV_ROWS
        acc = b
        for k in range(SSD_CONV):
            lo = base + halo + k - half
            acc = acc + w[k:k + 1, :] * pad_ref[lo:lo + CONV_ROWS, :]
        o_ref[0, base:base + CONV_ROWS, :] = _silu(acc)


def _conv(xbc3, conv_w, conv_b):
    bsz, s, c = xbc3.shape
    return pl.pallas_call(
        _conv_kernel,
        grid=(bsz, c // LANES),
        in_specs=[pl.BlockSpec((1, s, LANES), lambda b, j: (b, 0, j)),
                  pl.BlockSpec((SSD_CONV, LANES), lambda b, j: (0, j)),
                  pl.BlockSpec((1, LANES), lambda b, j: (0, j))],
        out_specs=pl.BlockSpec((1, s, LANES), lambda b, j: (b, 0, j)),
        out_shape=jax.ShapeDtypeStruct((bsz, s, c), F32),
        scratch_shapes=[pltpu.VMEM((s + 16, LANES), F32)],
        compiler_params=_cparams(("arbitrary", "arbitrary")),
        name="conv",
    )(xbc3, conv_w, conv_b.reshape(1, c))


HEAD_PAIRS = HEADS_PER_GROUP // 2
SSD_GROUPS_PER_STEP = 4


def _ssd_direction(x, bm, cm, dt_r, acs_r, acsc_all, lane_off, s_ref, reverse):
    L, N = bm.shape
    assert L == N == LANES
    ii = lax.broadcasted_iota(jnp.int32, (L, L), 0)
    jj = lax.broadcasted_iota(jnp.int32, (L, L), 1)
    mask = (jj >= ii) if reverse else (jj <= ii)
    lo_half = jj < SSD_HEAD_DIM
    shift = jnp.where(lane_off == 0, 0, LANES - lane_off)
    acs_c = pltpu.roll(acsc_all, shift, 1)
    last = 0 if reverse else L - 1
    tot_r = acs_r[:, last:last + 1]
    w_r = jnp.exp(tot_r - acs_r) * dt_r
    etot = jnp.broadcast_to(jnp.exp(tot_r), (HEADS_PADDED, LANES))

    cbt = _dot_nt(cm.astype(BF16), bm.astype(BF16))
    bt = bm.T
    ys = []
    for q in range(HEAD_PAIRS):
        xq = x[:, q * LANES:(q + 1) * LANES]
        sq = s_ref[q]
        x_a = jnp.where(lo_half, xq, 0.0).astype(BF16)
        x_b = jnp.where(lo_half, 0.0, xq).astype(BF16)
        s_a = jnp.where(lo_half, sq, 0.0).astype(BF16)
        s_b = jnp.where(lo_half, 0.0, sq).astype(BF16)
        m_parts, c_parts, b_parts = [], [], []
        for h in (2 * q, 2 * q + 1):
            col = jnp.broadcast_to(acs_c[:, h:h + 1], (L, L))
            decay = jnp.exp(jnp.where(mask, col - acs_r[h:h + 1, :], -jnp.inf))
            m_parts.append((cbt * decay * dt_r[h:h + 1, :]).astype(BF16))
            c_parts.append((cm * jnp.exp(col)).astype(BF16))
            b_parts.append((bt * w_r[h:h + 1, :]).astype(BF16))
        x_diag = jnp.concatenate([x_a, x_b], axis=0)
        lhs = jnp.concatenate(m_parts + c_parts, axis=1)
        rhs = jnp.concatenate([x_diag, s_a, s_b], axis=0)
        ys.append(_dot(lhs, rhs))
        dec = jnp.where(lo_half[0:1], etot[2 * q:2 * q + 1], etot[2 * q + 1:2 * q + 2])
        s_ref[q] = sq * dec + _dot(jnp.concatenate(b_parts, axis=1), x_diag)
    return jnp.concatenate(ys, axis=1)


def _ssd_kernel(xf_ref, bf_ref, cf_ref, dtTf_ref, acsTf_ref, acscf_ref,
                xb_ref, bb_ref, cb_ref, dtTb_ref, acsTb_ref, acscb_ref,
                yf_ref, yb_ref, sf_ref, sb_ref):
    gp = pl.program_id(1)

    @pl.when(pl.program_id(2) == 0)
    def _():
        sf_ref[...] = jnp.zeros_like(sf_ref)
        sb_ref[...] = jnp.zeros_like(sb_ref)

    for i in range(SSD_GROUPS_PER_STEP):
        g = gp * SSD_GROUPS_PER_STEP + i
        xs = slice(i * GROUP_X, (i + 1) * GROUP_X)
        ns = slice(i * SSD_STATE, (i + 1) * SSD_STATE)
        hs = slice(i * HEADS_PADDED, (i + 1) * HEADS_PADDED)
        yf_ref[0, :, xs] = _ssd_direction(
            xf_ref[0, :, xs], bf_ref[0, :, ns], cf_ref[0, :, ns], dtTf_ref[hs, :], acsTf_ref[hs, :],
            acscf_ref[...], g * HEADS_PADDED, sf_ref.at[i], False).astype(yf_ref.dtype)
        yb_ref[0, :, xs] = _ssd_direction(
            xb_ref[0, :, xs], bb_ref[0, :, ns], cb_ref[0, :, ns], dtTb_ref[hs, :], acsTb_ref[hs, :],
            acscb_ref[...], (SSD_GROUPS + g) * HEADS_PADDED, sb_ref.at[i], True).astype(yb_ref.dtype)


def _ssd(xbc3, dtT, acsT, acsc):
    bsz, s, _ = xbc3.shape
    L = SSD_CHUNK
    nc = s // L
    P = SSD_GROUPS_PER_STEP
    GP = SSD_GROUPS // P
    nb = SSD_INNER // (P * SSD_STATE)
    ncb = nb + GP
    fwd = lambda c: c
    bwd = lambda c: nc - 1 - c

    def specs(cidx, dirn):
        rowspec = pl.BlockSpec((P * HEADS_PADDED, L),
                               lambda b, g, c: (dirn * GP + g, b * nc + cidx(c)))
        return [
            pl.BlockSpec((1, L, P * GROUP_X), lambda b, g, c: (b, cidx(c), g)),
            pl.BlockSpec((1, L, P * SSD_STATE), lambda b, g, c: (b, cidx(c), nb + g)),
            pl.BlockSpec((1, L, P * SSD_STATE), lambda b, g, c: (b, cidx(c), ncb + g)),
            rowspec, rowspec,
            pl.BlockSpec((L, LANES), lambda b, g, c: (b * nc + cidx(c), 0)),
        ]

    out_specs = [pl.BlockSpec((1, L, P * GROUP_X), lambda b, g, c: (b, c, g)),
                 pl.BlockSpec((1, L, P * GROUP_X), lambda b, g, c: (b, nc - 1 - c, g))]
    return pl.pallas_call(
        _ssd_kernel,
        grid=(bsz, GP, nc),
        in_specs=specs(fwd, 0) + specs(bwd, 1),
        out_specs=out_specs,
        out_shape=[jax.ShapeDtypeStruct((bsz, s, SSD_INNER), BF16)] * 2,
        scratch_shapes=[pltpu.VMEM((P, HEAD_PAIRS, SSD_STATE, LANES), F32)] * 2,
        compiler_params=_cparams(("arbitrary", "arbitrary", "arbitrary")),
        name="ssd",
    )(xbc3, xbc3, xbc3, dtT, acsT, acsc, xbc3, xbc3, xbc3, dtT, acsT, acsc)


def _dft_tables(seq):
    n2n = LANES
    n1n = seq // n2n
    n1 = np.arange(n1n)
    k1 = np.arange(n1n)
    n2 = np.arange(n2n)
    ang = -2.0 * np.pi * (n2[:, None, None] * k1[None, :, None] / seq
                          + n1[None, None, :] * k1[None, :, None] / n1n)
    f1 = np.concatenate([np.cos(ang), np.sin(ang)], axis=1)
    k2 = np.arange(n2n)
    a2 = 2.0 * np.pi * np.outer(k2, n2) / n2n
    c2, s2 = np.cos(a2), np.sin(a2)
    g = np.block([[c2, s2], [-s2, c2]])
    ch = np.arange(FOURIER_GROUP_DIM)
    ac = 2.0 * np.pi * np.outer(ch, ch) / FOURIER_GROUP_DIM
    scale = 1.0 / math.sqrt(seq * FOURIER_GROUP_DIM)
    fc = np.concatenate([np.cos(ac), np.sin(ac)], axis=0) * scale
    return (jnp.asarray(f1, BF16), jnp.asarray(g, BF16), jnp.asarray(fc, BF16))


DFT_UNROLL = 4


def _dft_pitch(n1n):
    return 2 * n1n + 8


def _fourier_kernel(u_ref, f1_ref, g_ref, fc_ref, o_ref, a_ref):
    n2n = LANES
    n1n = u_ref.shape[2]
    pitch = _dft_pitch(n1n)

    def stage1(i, carry):
        for u in range(DFT_UNROLL):
            n2 = i * DFT_UNROLL + u
            xs = u_ref[0, 0, :, pl.ds(pl.multiple_of(n2 * LANES, LANES), LANES)].astype(BF16)
            a_ref[pl.ds(pl.multiple_of(n2 * pitch, 8), 2 * n1n), :] = _dot(f1_ref[n2], xs)
        return carry

    lax.fori_loop(0, n2n // DFT_UNROLL, stage1, 0)
    gm = g_ref[...]
    fc = fc_ref[...]

    def stage2(i, carry):
        for u in range(DFT_UNROLL):
            k1 = i * DFT_UNROLL + u
            re = a_ref[pl.ds(k1, n2n, stride=pitch), :]
            im = a_ref[pl.ds(n1n + k1, n2n, stride=pitch), :]
            a = jnp.concatenate([re, im], axis=0).astype(BF16)
            z = _dot(gm, a)
            zz = jnp.concatenate([z[:n2n], z[n2n:]], axis=1).astype(BF16)
            o_ref[0, pl.ds(k1, n2n, stride=n1n), :] = _dot(zz, fc)
        return carry

    lax.fori_loop(0, n1n // DFT_UNROLL, stage2, 0)


def _fourier(uf3):
    bsz, s, w = uf3.shape
    f1, g, fc = _dft_tables(s)
    n1n = s // LANES
    ng = w // FOURIER_GROUP_DIM
    u = uf3.reshape(bsz, n1n, LANES, ng, FOURIER_GROUP_DIM).transpose(0, 3, 1, 2, 4)
    u = u.reshape(bsz, ng, n1n, LANES * FOURIER_GROUP_DIM)
    return pl.pallas_call(
        _fourier_kernel,
        grid=(bsz, ng),
        in_specs=[pl.BlockSpec((1, 1, n1n, LANES * FOURIER_GROUP_DIM), lambda b, j: (b, j, 0, 0)),
                  pl.BlockSpec(f1.shape, lambda b, j: (0, 0, 0)),
                  pl.BlockSpec(g.shape, lambda b, j: (0, 0)),
                  pl.BlockSpec(fc.shape, lambda b, j: (0, 0))],
        out_specs=pl.BlockSpec((1, s, LANES), lambda b, j: (b, 0, j)),
        out_shape=jax.ShapeDtypeStruct((bsz, s, w), F32),
        scratch_shapes=[pltpu.VMEM((LANES * _dft_pitch(n1n), LANES), F32)],
        compiler_params=_cparams(("arbitrary", "arbitrary")),
        name="fourier",
    )(u, f1, g, fc)


def _mix_kernel(yf_ref, yb_ref, xs_ref, z_ref, fm_ref, gates_ref, x_ref, mod_ref, dskip_ref,
                ssdn_ref, wbs_ref, wbf_ref, wout_ref, gpost_ref, o_ref):
    m = mod_ref[0]
    y = yf_ref[...].astype(F32) + yb_ref[...].astype(F32) + dskip_ref[...] * xs_ref[...]
    v = y * _silu(z_ref[...].astype(F32))
    parts = []
    for g in range(SSD_GROUPS):
        vg = v[:, g * GROUP_X:(g + 1) * GROUP_X]
        parts.append(vg * lax.rsqrt(jnp.mean(vg * vg, axis=-1, keepdims=True) + RMS_EPS))
    vn = jnp.concatenate(parts, axis=1) * ssdn_ref[...]
    y_ssd = _dot(vn.astype(BF16), wbs_ref[...])
    y_fou = _dot(fm_ref[...].astype(BF16), wbf_ref[...])
    gt = _sigmoid(gates_ref[...].astype(F32))
    d = y_ssd.shape[1]
    mixed = gt[:, :d] * y_ssd + gt[:, d:] * y_fou
    mo = _dot(mixed.astype(BF16), wout_ref[...])
    o_ref[...] = x_ref[...] + m[2:3] * _rms(mo, gpost_ref[...])


def _mix(yf, yb, xbc, z, fm, gates, x2, mod, dskip, ssdn, wbs, wbf, wout, gpost, seq):
    t, d = x2.shape
    tm = TOKEN_TILE
    per_b = seq // tm
    full = lambda a: pl.BlockSpec(a.shape, lambda i: (0,) * a.ndim)
    row = lambda n: pl.BlockSpec((tm, n), lambda i: (i, 0))
    return pl.pallas_call(
        _mix_kernel,
        grid=(t // tm,),
        in_specs=[row(SSD_INNER), row(SSD_INNER), row(SSD_INNER), row(SSD_INNER),
                  row(FOURIER_WIDTH), row(2 * d), row(d),
                  pl.BlockSpec((1, 6, d), lambda i: (i // per_b, 0, 0)),
                  full(dskip), full(ssdn), full(wbs), full(wbf), full(wout), full(gpost)],
        out_specs=row(d),
        out_shape=jax.ShapeDtypeStruct((t, d), F32),
        compiler_params=_cparams(("arbitrary",)),
        name="mix",
    )(yf, yb, xbc, z, fm, gates, x2, mod, dskip, ssdn, wbs, wbf, wout, gpost)


def _route_kernel(x_ref, mod_ref, gpre_ref, wr_ref, rb_ref, hp_ref, idx_ref, w_ref, rank_ref,
                  counts_ref, run_ref):
    @pl.when(pl.program_id(0) == 0)
    def _():
        run_ref[...] = jnp.zeros_like(run_ref)

    m = mod_ref[0]
    h = _rms(x_ref[...], gpre_ref[...]) * (1.0 + m[4:5]) + m[3:4]
    tm, d = h.shape
    half = d // 2
    bits = pltpu.bitcast(h.astype(BF16).astype(F32), jnp.uint32)
    hp_ref[...] = (bits[:, :half] >> 16) | (bits[:, half:] & jnp.uint32(0xFFFF0000))

    h_hi = h.astype(BF16)
    h_lo = (h - h_hi.astype(F32)).astype(BF16)
    w_hi = wr_ref[0]
    logits = _dot(h_hi, w_hi) + (_dot(h_hi, wr_ref[1]) + _dot(h_lo, w_hi))
    scores = _sigmoid(logits)
    biased = scores + rb_ref[...]
    ne = scores.shape[1]
    per_group = ne // N_EXPERT_GROUPS
    lane = lax.broadcasted_iota(jnp.int32, (tm, ne), 1)
    lane_f = lane.astype(F32)
    grp = lane // per_group
    out_lane = lax.broadcasted_iota(jnp.int32, (tm, LANES), 1)
    neg = -jnp.inf

    def argmax_first(v):
        mx = jnp.max(v, axis=-1, keepdims=True)
        ix = jnp.min(jnp.where(v == mx, lane_f, float(ne)), axis=-1, keepdims=True)
        return mx, ix

    gs = []
    gmat = jnp.full((tm, LANES), neg, F32)
    for g in range(N_EXPERT_GROUPS):
        vg = jnp.where(grp == g, biased, neg)
        m1, i1 = argmax_first(vg)
        m2 = jnp.max(jnp.where(lane_f == i1, neg, vg), axis=-1, keepdims=True)
        gs.append(m1 + m2)
        gmat = jnp.where(out_lane == g, gs[g], gmat)
    ahead = jnp.zeros((tm, LANES), F32)
    for o in range(N_EXPERT_GROUPS):
        beats = (gs[o] > gmat) | ((gs[o] == gmat) & (out_lane > o))
        ahead = ahead + beats.astype(F32)
    sel = ((ahead < TOPK_GROUPS) & (out_lane < N_EXPERT_GROUPS)).astype(BF16)
    eg = lax.broadcasted_iota(jnp.int32, (LANES, ne), 0)
    ee = lax.broadcasted_iota(jnp.int32, (LANES, ne), 1)
    allowed = _dot(sel, (eg == ee // per_group).astype(BF16)) > 0.5
    masked = jnp.where(allowed, biased, neg)

    idx_out = jnp.zeros((tm, LANES), F32)
    w_out = jnp.zeros((tm, LANES), F32)
    w_sum = jnp.zeros((tm, 1), F32)
    hits = []
    for k in range(TOP_K):
        _, ik = argmax_first(masked)
        hit = lane_f == ik
        hits.append(hit)
        wk = jnp.sum(jnp.where(hit, scores, 0.0), axis=-1, keepdims=True)
        masked = jnp.where(hit, neg, masked)
        idx_out = jnp.where(out_lane == k, ik, idx_out)
        w_out = jnp.where(out_lane == k, wk, w_out)
        w_sum = w_sum + wk
    idx_ref[...] = idx_out.astype(jnp.int32)
    w_ref[...] = w_out / w_sum * ROUTED_SCALE

    chosen = functools.reduce(jnp.logical_or, hits).astype(F32)
    ti = lax.broadcasted_iota(jnp.int32, (tm, tm), 0)
    tj = lax.broadcasted_iota(jnp.int32, (tm, tm), 1)
    before = _dot((tj < ti).astype(BF16), chosen.astype(BF16)) + run_ref[...]
    rank_out = jnp.zeros((tm, LANES), jnp.int32)
    for k in range(TOP_K):
        rk = jnp.sum(jnp.where(hits[k], before, 0.0), axis=-1, keepdims=True)
        rank_out = jnp.where(out_lane == k, rk.astype(jnp.int32), rank_out)
    rank_ref[...] = rank_out
    run_ref[...] = run_ref[...] + jnp.sum(chosen, axis=0, keepdims=True)
    counts_ref[...] = run_ref[...]


def _route(x1, mod, gpre, w_router, router_bias, seq):
    t, d = x1.shape
    tm = TOKEN_TILE
    per_b = seq // tm
    full = lambda a: pl.BlockSpec(a.shape, lambda i: (0,) * a.ndim)
    row = lambda n: pl.BlockSpec((tm, n), lambda i: (i, 0))
    rb = router_bias.reshape(1, -1)
    w_hi = w_router.astype(BF16)
    w_router = jnp.stack([w_hi, (w_router - w_hi.astype(F32)).astype(BF16)])
    return pl.pallas_call(
        _route_kernel,
        grid=(t // tm,),
        in_specs=[row(d), pl.BlockSpec((1, 6, d), lambda i: (i // per_b, 0, 0)), full(gpre),
                  full(w_router), full(rb)],
        out_specs=[row(d // 2), row(LANES), row(LANES), row(LANES),
                   pl.BlockSpec((1, N_EXPERTS), lambda i: (0, 0))],
        out_shape=[jax.ShapeDtypeStruct((t, d // 2), jnp.uint32),
                   jax.ShapeDtypeStruct((t, LANES), jnp.int32),
                   jax.ShapeDtypeStruct((t, LANES), F32),
                   jax.ShapeDtypeStruct((t, LANES), jnp.int32),
                   jax.ShapeDtypeStruct((1, N_EXPERTS), F32)],
        scratch_shapes=[pltpu.VMEM((1, N_EXPERTS), F32)],
        compiler_params=_cparams(("arbitrary",)),
        name="route",
    )(x1, mod, gpre, w_router, rb)


GATHER_UNROLL = 8


ROW_TILE = 8


def _tile_rows(r):
    return pl.ds(pl.multiple_of(r * ROW_TILE, ROW_TILE), ROW_TILE)


def _gather_rows(idx_ref, n_rows, src_hbm, dst_of, sem):
    def body(q, carry):
        for u in range(GATHER_UNROLL):
            src = idx_ref[0, 0, q * GATHER_UNROLL + u]
            pltpu.make_async_copy(src_hbm.at[_tile_rows(src), :], dst_of(q, u), sem).start(
                priority=u % 2)
        return carry

    lax.fori_loop(0, n_rows // GATHER_UNROLL, body, 0)


def _to_tiled_rows(o_ref, y):
    rows, width = y.shape
    for r in range(rows // ROW_TILE):
        for j in range(width // LANES):
            o_ref[pl.ds(r * ROW_TILE * ROW_TILE + j, ROW_TILE, stride=ROW_TILE), :] = (
                y[r * ROW_TILE:(r + 1) * ROW_TILE, j * LANES:(j + 1) * LANES])


def _from_tiled_rows(load, rows, width):
    cols = []
    for j in range(width // LANES):
        pieces = [load(r * ROW_TILE * ROW_TILE + j, ROW_TILE, ROW_TILE) for r in range(rows // ROW_TILE)]
        cols.append(jnp.concatenate(pieces, axis=0))
    return jnp.concatenate(cols, axis=1)


def _dest_kernel(idx_ref, rank_ref, start_ref, o_ref):
    idx = idx_ref[...]
    rank = rank_ref[...]
    start = start_ref[...]
    tm = idx.shape[0]
    lane = lax.broadcasted_iota(jnp.int32, (tm, start.shape[1]), 1)
    out_lane = lax.broadcasted_iota(jnp.int32, (tm, LANES), 1)
    out = jnp.zeros((tm, LANES), jnp.int32)
    for k in range(TOP_K):
        base = jnp.sum(jnp.where(lane == idx[:, k:k + 1], start, 0.0), axis=-1, keepdims=True)
        out = jnp.where(out_lane == k, base.astype(jnp.int32) + rank[:, k:k + 1], out)
    o_ref[...] = out


def _dest(idx_pad, rank_pad, pad_start):
    t = idx_pad.shape[0]
    tm = DEST_TILE
    row = pl.BlockSpec((tm, LANES), lambda i: (i, 0))
    return pl.pallas_call(
        _dest_kernel,
        grid=(t // tm,),
        in_specs=[row, row, pl.BlockSpec((1, N_EXPERTS), lambda i: (0, 0))],
        out_specs=row,
        out_shape=jax.ShapeDtypeStruct((t, LANES), jnp.int32),
        compiler_params=_cparams(("arbitrary",)),
        name="dest",
    )(idx_pad, rank_pad, pad_start.astype(F32).reshape(1, N_EXPERTS))


def _scatter_kernel(b0_ref, nb_ref, nu_ref, dest_ref, hp_ref, xs_hbm, zbuf, sem, zsem):
    ts = hp_ref.shape[0]
    bm = zbuf.shape[0]
    n_blocks = xs_hbm.shape[0] // bm

    @pl.when(pl.program_id(0) == 0)
    def _():
        zbuf[...] = jnp.zeros_like(zbuf)
        n_used = nu_ref[0]

        def zero_block(g):
            return pltpu.make_async_copy(zbuf, xs_hbm.at[pl.ds(pl.multiple_of(g * bm, bm), bm), :], zsem)

        def tails(action):
            def step(e, carry):
                @pl.when(nb_ref[e] > 0)
                def _():
                    action(zero_block(b0_ref[e] + nb_ref[e] - 1))
                return carry
            lax.fori_loop(0, N_EXPERTS, step, 0)

        def unused(action):
            def step(g, carry):
                action(zero_block(g))
                return carry
            lax.fori_loop(n_used, n_blocks, step, 0)

        tails(lambda c: c.start())
        unused(lambda c: c.start())
        tails(lambda c: c.wait())
        unused(lambda c: c.wait())

    def body(t, carry):
        for k in range(TOP_K):
            dst = dest_ref[0, 0, t * TOP_K + k]
            pltpu.make_async_copy(hp_ref.at[pl.ds(t, 1), :], xs_hbm.at[pl.ds(dst, 1), :], sem).start(
                priority=k % 2)
        return carry

    lax.fori_loop(0, ts, body, 0)
    for k in range(TOP_K):
        pltpu.make_async_copy(hp_ref, xs_hbm.at[pl.ds(0, ts), :], sem).wait()


def _scatter(dest, hp, first_block, n_block, n_used, n_rows):
    t, half = hp.shape
    ts = SCATTER_TILE
    n_steps = t // ts
    dest3 = dest.reshape(n_steps, 1, ts * TOP_K)
    grid_spec = pltpu.PrefetchScalarGridSpec(
        num_scalar_prefetch=3,
        grid=(n_steps,),
        in_specs=[pl.BlockSpec((1, 1, ts * TOP_K), lambda i, b0, nb, nu: (i, 0, 0),
                               memory_space=pltpu.SMEM),
                  pl.BlockSpec((ts, half), lambda i, b0, nb, nu: (i, 0))],
        out_specs=pl.BlockSpec(memory_space=pl.ANY),
        scratch_shapes=[pltpu.VMEM((EXPERT_ROWS, half), jnp.uint32),
                        pltpu.SemaphoreType.DMA(()), pltpu.SemaphoreType.DMA(())],
    )
    return pl.pallas_call(
        _scatter_kernel,
        grid_spec=grid_spec,
        out_shape=jax.ShapeDtypeStruct((n_rows, half), jnp.uint32),
        compiler_params=_cparams(("arbitrary",)),
        name="scatter",
    )(first_block, n_block, n_used, dest3, hp)


def _expert_kernel(b0_ref, nb_ref, nu_ref, xs_hbm, w13_ref, w2_ref, y_hbm,
                   xbuf, ybuf, w13b, w2b, sem_in, sem_out):
    e = pl.program_id(0)
    n_used = nu_ref[0]
    b0 = b0_ref[e]
    nb = nb_ref[e]
    n_in, bm = xbuf.shape[:2]
    n_out, y_rows = ybuf.shape[:2]
    ahead = n_in - 1

    def in_copy(g, slot):
        return pltpu.make_async_copy(xs_hbm.at[pl.ds(pl.multiple_of(g * bm, bm), bm), :],
                                     xbuf.at[slot], sem_in.at[slot])

    def out_copy(g, slot):
        return pltpu.make_async_copy(ybuf.at[slot],
                                     y_hbm.at[pl.ds(pl.multiple_of(g * y_rows, y_rows), y_rows), :],
                                     sem_out.at[slot])

    @pl.when(nb > 0)
    def _():
        w13b[...] = w13_ref[0].astype(BF16)
        w2b[...] = w2_ref[0].astype(BF16)

    @pl.when((nb > 0) & (b0 == 0))
    def _():
        for a in range(ahead):
            @pl.when(a < n_used)
            def _():
                in_copy(a, a).start()

    def block(g, carry):
        in_slot = g % n_in
        out_slot = g % n_out
        in_copy(g, in_slot).wait()

        @pl.when(g + ahead < n_used)
        def _():
            in_copy(g + ahead, (g + ahead) % n_in).start()

        @pl.when(g >= n_out)
        def _():
            out_copy(g - n_out, out_slot).wait()

        w = xbuf[in_slot]
        lo = pltpu.bitcast(w << 16, F32).astype(BF16)
        hi = pltpu.bitcast(w & jnp.uint32(0xFFFF0000), F32).astype(BF16)
        half = lo.shape[1]
        ag = _dot(lo, w13b[:half, :]) + _dot(hi, w13b[half:, :])
        hh = ag.shape[1] // 2
        act = (_silu(ag[:, :hh]) * ag[:, hh:]).astype(BF16)
        _to_tiled_rows(ybuf.at[out_slot], _dot(act, w2b[...]))
        out_copy(g, out_slot).start()
        return carry

    lax.fori_loop(b0, b0 + nb, block, 0)

    @pl.when(e == pl.num_programs(0) - 1)
    def _():
        for back in range(1, n_out + 1):
            @pl.when(n_used >= back)
            def _():
                out_copy(n_used - back, (n_used - back) % n_out).wait()

        n_blocks = y_hbm.shape[0] // y_rows
        ybuf[0] = jnp.zeros((y_rows, LANES), F32)

        def start_zero(g, carry):
            out_copy(g, 0).start()
            return carry

        def wait_zero(g, carry):
            out_copy(g, 0).wait()
            return carry

        lax.fori_loop(n_used, n_blocks, start_zero, 0)
        lax.fori_loop(n_used, n_blocks, wait_zero, 0)


def _experts(xs, first_block, n_block, n_used, w13, w2):
    n_rows, half = xs.shape
    ne, d, h2 = w13.shape
    bm = EXPERT_ROWS
    y_rows = bm * d // LANES
    grid_spec = pltpu.PrefetchScalarGridSpec(
        num_scalar_prefetch=3,
        grid=(ne,),
        in_specs=[
            pl.BlockSpec(memory_space=pl.ANY),
            pl.BlockSpec((1, d, h2), lambda e, b0, nb, nu: (e, 0, 0)),
            pl.BlockSpec((1, h2 // 2, d), lambda e, b0, nb, nu: (e, 0, 0)),
        ],
        out_specs=pl.BlockSpec(memory_space=pl.ANY),
        scratch_shapes=[pltpu.VMEM((EXPERT_IN_SLOTS, bm, half), jnp.uint32),
                        pltpu.VMEM((EXPERT_OUT_SLOTS, y_rows, LANES), F32),
                        pltpu.VMEM((d, h2), BF16),
                        pltpu.VMEM((h2 // 2, d), BF16),
                        pltpu.SemaphoreType.DMA((EXPERT_IN_SLOTS,)),
                        pltpu.SemaphoreType.DMA((EXPERT_OUT_SLOTS,))],
    )
    return pl.pallas_call(
        _expert_kernel,
        grid_spec=grid_spec,
        out_shape=jax.ShapeDtypeStruct((n_rows * d // LANES, LANES), F32),
        compiler_params=_cparams(("arbitrary",)),
        name="experts",
    )(first_block, n_block, n_used, xs, w13, w2)


def _final_kernel(dc_ref, dn_ref, y_hbm, w_ref, x_ref, mod_ref, gpre_ref, gpost_ref, w13s_ref,
                  w2s_ref, o_ref, buf, sem):
    i = pl.program_id(0)
    slot = i % 2
    tm = x_ref.shape[0]

    def issue(d_ref, s):
        _gather_rows(d_ref, tm * TOP_K, y_hbm,
                     lambda q, u: buf.at[s, u, _tile_rows(q), :], sem.at[s])

    @pl.when(i == 0)
    def _():
        issue(dc_ref, 0)

    @pl.when(i + 1 < pl.num_programs(0))
    def _():
        issue(dn_ref, 1 - slot)

    for k in range(TOP_K):
        pltpu.make_async_copy(y_hbm.at[pl.ds(0, tm * ROW_TILE), :], buf.at[slot, k],
                              sem.at[slot]).wait()

    w = w_ref[...]
    x1 = x_ref[...]
    routed = None
    for k in range(TOP_K):
        yk = _from_tiled_rows(lambda a, n, st: buf[slot, k, pl.ds(a, n, stride=st), :], tm, x1.shape[1])
        routed = yk * w[:, k:k + 1] if routed is None else routed + yk * w[:, k:k + 1]

    m = mod_ref[0]
    h = (_rms(x1, gpre_ref[...]) * (1.0 + m[4:5]) + m[3:4]).astype(BF16)
    ag = _dot(h, w13s_ref[...])
    hh = ag.shape[1] // 2
    act = (_silu(ag[:, :hh]) * ag[:, hh:]).astype(BF16)
    ffn = routed + _dot(act, w2s_ref[...])
    o_ref[...] = x1 + m[5:6] * _rms(ffn, gpost_ref[...])


def _final(dest, y_sorted, top_w, x1, mod, gpre, gpost, w13s, w2s, seq):
    t, d = x1.shape
    tm = COMBINE_TILE
    per_b = seq // tm
    n_steps = t // tm
    dest3 = dest.reshape(n_steps, 1, tm * TOP_K)
    full = lambda a: pl.BlockSpec(a.shape, lambda i: (0,) * a.ndim)
    row = lambda n: pl.BlockSpec((tm, n), lambda i: (i, 0))
    return pl.pallas_call(
        _final_kernel,
        grid=(n_steps,),
        in_specs=[
            pl.BlockSpec((1, 1, tm * TOP_K), lambda i: (i, 0, 0), memory_space=pltpu.SMEM),
            pl.BlockSpec((1, 1, tm * TOP_K), lambda i: (jnp.minimum(i + 1, n_steps - 1), 0, 0),
                         memory_space=pltpu.SMEM),
            pl.BlockSpec(memory_space=pl.ANY),
            row(LANES), row(d), pl.BlockSpec((1, 6, d), lambda i: (i // per_b, 0, 0)),
            full(gpre), full(gpost), full(w13s), full(w2s)],
        out_specs=row(d),
        out_shape=jax.ShapeDtypeStruct((t, d), F32),
        scratch_shapes=[pltpu.VMEM((2, TOP_K, tm * ROW_TILE, LANES), F32),
                        pltpu.SemaphoreType.DMA((2,))],
        compiler_params=_cparams(("arbitrary",)),
        name="combine_final",
    )(dest3, dest3, y_sorted, top_w, x1, mod, gpre, gpost, w13s, w2s)


def _pad_heads(v):
    lead = v.shape[:-1]
    v = v.reshape(lead + (2, SSD_GROUPS, HEADS_PER_GROUP))
    v = jnp.pad(v, [(0, 0)] * (len(lead) + 2) + [(0, HEADS_PADDED - HEADS_PER_GROUP)])
    return v.reshape(lead + (2 * SSD_GROUPS * HEADS_PADDED,))


def _dispatch_plan(counts, n_tokens):
    bm = EXPERT_ROWS
    counts = counts.reshape(N_EXPERTS).astype(jnp.int32)
    padded = (counts + bm - 1) // bm * bm
    pad_end = jnp.cumsum(padded)
    pad_start = pad_end - padded
    n_rows = -(-n_tokens * TOP_K // bm) * bm + N_EXPERTS * bm
    first_block = (pad_start // bm).astype(jnp.int32)
    n_block = (padded // bm).astype(jnp.int32)
    n_used = (pad_end[-1] // bm).astype(jnp.int32).reshape(1)
    return pad_start, first_block, n_block, n_used, n_rows


def _layer(x, c, w_ada, b_ada, pre_norm_mix, post_norm_mix, pre_norm_ffn, post_norm_ffn, w_in,
           conv_w, conv_b, dt_bias_fwd, dt_bias_bwd, a_log_fwd, a_log_bwd, d_skip, ssd_norm,
           w_branch_ssd, w_branch_fourier, w_out, w_router, router_bias, w13_experts, w2_experts,
           w13_shared, w2_shared):
    bsz, seq, d = x.shape
    t = bsz * seq
    x2 = x.reshape(t, d)
    row = lambda v: v.reshape(1, -1).astype(F32)

    mod = _ada(c, w_ada, b_ada)

    i1 = SSD_INNER
    i2 = i1 + XBC_WIDTH
    i3 = i2 + 2 * SSD_HEADS
    i4 = i3 + FOURIER_WIDTH
    n_dt = 2 * SSD_GROUPS * HEADS_PADDED
    w_dtp = _pad_heads(w_in[:, i2:i3])
    wdtc = jnp.pad(w_dtp, ((0, 0), (0, LANES - n_dt))).astype(BF16)
    wdtT = w_dtp.T.astype(BF16)
    bias_p = _pad_heads(jnp.concatenate([dt_bias_fwd, dt_bias_bwd]).astype(F32))
    a_p = _pad_heads(-jnp.exp(jnp.concatenate([a_log_fwd, a_log_bwd]).astype(F32)))
    pad_row = lambda v: jnp.pad(v, (0, LANES - n_dt)).reshape(1, LANES)
    z, xbc, acsc, dtT, acsT, uf, gates = _inproj(
        x2, mod, row(pre_norm_mix), w_in[:, :i1].astype(BF16), w_in[:, i1:i2].astype(BF16),
        wdtc, wdtT, w_in[:, i3:i4].astype(BF16), w_in[:, i4:].astype(BF16),
        pad_row(bias_p), pad_row(a_p), bias_p.reshape(n_dt, 1), a_p.reshape(n_dt, 1), seq)

    xbc3 = _conv(xbc.reshape(bsz, seq, XBC_WIDTH), conv_w, conv_b)

    yf, yb = _ssd(xbc3, dtT, acsT, acsc)

    fm = _fourier(uf.reshape(bsz, seq, FOURIER_WIDTH))

    x1 = _mix(yf.reshape(t, SSD_INNER), yb.reshape(t, SSD_INNER), xbc3.reshape(t, XBC_WIDTH), z,
              fm.reshape(t, FOURIER_WIDTH), gates, x2, mod,
              row(jnp.repeat(d_skip, SSD_HEAD_DIM)), row(ssd_norm), w_branch_ssd.astype(BF16),
              w_branch_fourier.astype(BF16), w_out.astype(BF16), row(post_norm_mix), seq)

    hp, idx_pad, w_pad, rank_pad, counts = _route(x1, mod, row(pre_norm_ffn),
                                                  w_router.astype(F32), router_bias, seq)
    pad_start, first_block, n_block, n_used, n_rows = _dispatch_plan(counts, t)
    dest = _dest(idx_pad, rank_pad, pad_start)[:, :TOP_K]
    xs = _scatter(dest, hp, first_block, n_block, n_used, n_rows)
    y_sorted = _experts(xs, first_block, n_block, n_used, w13_experts, w2_experts)
    out = _final(dest, y_sorted, w_pad, x1, mod, row(pre_norm_ffn), row(post_norm_ffn),
                 w13_shared.astype(BF16), w2_shared.astype(BF16), seq)
    return out.reshape(bsz, seq, d)


def kernel(x, c, w_ada, b_ada, pre_norm_mix, post_norm_mix, pre_norm_ffn, post_norm_ffn, w_in,
           conv_w, conv_b, dt_bias_fwd, dt_bias_bwd, a_log_fwd, a_log_bwd, d_skip, ssd_norm,
           w_branch_ssd, w_branch_fourier, w_out, w_router, router_bias, w13_experts, w2_experts,
           w13_shared, w2_shared):
    for layer in range(w_ada.shape[0]):
        x = _layer(x, c, w_ada[layer], b_ada[layer], pre_norm_mix[layer], post_norm_mix[layer],
                   pre_norm_ffn[layer], post_norm_ffn[layer], w_in[layer], conv_w[layer],
                   conv_b[layer], dt_bias_fwd[layer], dt_bias_bwd[layer], a_log_fwd[layer],
                   a_log_bwd[layer], d_skip[layer], ssd_norm[layer], w_branch_ssd[layer],
                   w_branch_fourier[layer], w_out[layer], w_router[layer], router_bias[layer],
                   w13_experts[layer], w2_experts[layer], w13_shared[layer], w2_shared[layer])
    return x
```

```python
import functools
import math

import numpy as np
import jax
import jax.numpy as jnp
from jax import lax
from jax.experimental import pallas as pl
from jax.experimental.pallas import tpu as pltpu

F32 = jnp.float32
BF16 = jnp.bfloat16
HIGHEST = lax.Precision.HIGHEST

D_MODEL = 1024
SSD_HEADS = 24
SSD_HEAD_DIM = 64
SSD_INNER = SSD_HEADS * SSD_HEAD_DIM
SSD_GROUPS = 4
HEADS_PER_GROUP = SSD_HEADS // SSD_GROUPS
HEADS_PADDED = 8
SSD_STATE = 128
SSD_CONV = 5
SSD_CHUNK = 128
XBC_WIDTH = SSD_INNER + 2 * SSD_GROUPS * SSD_STATE
GROUP_X = HEADS_PER_GROUP * SSD_HEAD_DIM
FOURIER_WIDTH = 512
FOURIER_GROUP_DIM = 128
N_EXPERTS = 256
TOP_K = 8
N_EXPERT_GROUPS = 8
TOPK_GROUPS = 4
EXPERT_HIDDEN = 256
SHARED_HIDDEN = 256
ROUTED_SCALE = 2.5
RMS_EPS = 1e-6

LANES = 128
VMEM_LIMIT = 56 * 1024 * 1024
TOKEN_TILE = 256
INPROJ_TILE = 256
EXPERT_ROWS = 256
COMBINE_TILE = 128
SCATTER_TILE = 512
DEST_TILE = 1024
EXPERT_IN_SLOTS = 4
EXPERT_OUT_SLOTS = 3
CONV_ROWS = 256


def _cparams(sem):
    return pltpu.CompilerParams(dimension_semantics=sem, vmem_limit_bytes=VMEM_LIMIT)


def _dot(a, b, precision=None):
    return jnp.dot(a, b, preferred_element_type=F32, precision=precision)


def _dot_nt(a, b, precision=None):
    return lax.dot_general(a, b, (((1,), (1,)), ((), ())), preferred_element_type=F32,
                           precision=precision)


def _dot_tn(a, b):
    return lax.dot_general(a, b, (((0,), (0,)), ((), ())), preferred_element_type=F32)


def _sigmoid(x):
    return 1.0 / (1.0 + jnp.exp(-x))


def _silu(x):
    return x * _sigmoid(x)


def _softplus(x):
    return jnp.maximum(x, 0.0) + jnp.log1p(jnp.exp(-jnp.abs(x)))


def _rms(x, g):
    return x * lax.rsqrt(jnp.mean(x * x, axis=-1, keepdims=True) + RMS_EPS) * g


def _ada_kernel(c_ref, w_ref, b_ref, o_ref):
    o_ref[...] = _dot(_silu(c_ref[...]), w_ref[...], HIGHEST) + b_ref[...]


def _ada(c, w_ada, b_ada):
    bsz, d = c.shape
    rows = 8
    cp = jnp.zeros((rows, d), F32).at[:bsz].set(c)
    n = w_ada.shape[1]
    tn = 1536
    out = pl.pallas_call(
        _ada_kernel,
        grid=(n // tn,),
        in_specs=[pl.BlockSpec((rows, d), lambda j: (0, 0)),
                  pl.BlockSpec((d, tn), lambda j: (0, j)),
                  pl.BlockSpec((1, tn), lambda j: (0, j))],
        out_specs=pl.BlockSpec((rows, tn), lambda j: (0, j)),
        out_shape=jax.ShapeDtypeStruct((rows, n), F32),
        compiler_params=_cparams(("arbitrary",)),
        name="adaln",
    )(cp, w_ada, b_ada.reshape(1, n))
    return out[:bsz].reshape(bsz, 6, d)


def _split3(a):
    a1 = a.astype(BF16)
    r1 = a - a1.astype(F32)
    a2 = r1.astype(BF16)
    a3 = (r1 - a2.astype(F32)).astype(BF16)
    return a1, a2, a3


def _inproj_kernel(x_ref, mod_ref, g_ref, wz_ref, wxbc_ref, wdtc_ref, wdtT_ref, wuf_ref, wg_ref,
                   bias_row_ref, a_row_ref, bias_col_ref, a_col_ref,
                   z_ref, xbc_ref, acsc_ref, dtT_ref, acsT_ref, uf_ref, gates_ref):
    m = mod_ref[0]
    h = _rms(x_ref[...], g_ref[...]) * (1.0 + m[1:2]) + m[0:1]
    hb = h.astype(BF16)
    z_ref[...] = _dot(hb, wz_ref[...]).astype(z_ref.dtype)
    xbc_ref[...] = _dot(hb, wxbc_ref[...]).astype(xbc_ref.dtype)
    uf_ref[...] = _dot(hb, wuf_ref[...]).astype(uf_ref.dtype)
    gates_ref[...] = _dot(hb, wg_ref[...]).astype(gates_ref.dtype)

    tm = hb.shape[0]
    n_fwd = SSD_GROUPS * HEADS_PADDED
    ii = lax.broadcasted_iota(jnp.int32, (tm, tm), 0)
    jj = lax.broadcasted_iota(jnp.int32, (tm, tm), 1)
    same = (ii // SSD_CHUNK) == (jj // SSD_CHUNK)
    tri_f = (same & (jj <= ii)).astype(BF16)
    tri_b = (same & (jj >= ii)).astype(BF16)

    dt_c = _softplus(_dot(hb, wdtc_ref[...]) + bias_row_ref[...])
    pieces = _split3(dt_c * a_row_ref[...])
    acs_f = sum(_dot(tri_f, p) for p in pieces)
    acs_b = sum(_dot(tri_b, p) for p in pieces)
    lane = lax.broadcasted_iota(jnp.int32, acs_f.shape, 1)
    acsc_ref[...] = jnp.where(lane < n_fwd, acs_f, acs_b)

    dt_t = _softplus(_dot_nt(wdtT_ref[...], hb) + bias_col_ref[...])
    pieces = _split3(dt_t * a_col_ref[...])
    acs_f = sum(_dot_nt(p, tri_f) for p in pieces)
    acs_b = sum(_dot_nt(p, tri_b) for p in pieces)
    sub = lax.broadcasted_iota(jnp.int32, acs_f.shape, 0)
    dtT_ref[...] = dt_t
    acsT_ref[...] = jnp.where(sub < n_fwd, acs_f, acs_b)


def _inproj(x2, mod, g, wz, wxbc, wdtc, wdtT, wuf, wg, bias_row, a_row, bias_col, a_col, seq):
    t, d = x2.shape
    tm = INPROJ_TILE
    per_b = seq // tm
    full = lambda a: pl.BlockSpec(a.shape, lambda i: (0,) * a.ndim)
    row = lambda n: pl.BlockSpec((tm, n), lambda i: (i, 0))
    nd = wdtT.shape[0]
    colspec = pl.BlockSpec((nd, tm), lambda i: (0, i))
    return pl.pallas_call(
        _inproj_kernel,
        grid=(t // tm,),
        in_specs=[row(d), pl.BlockSpec((1, 6, d), lambda i: (i // per_b, 0, 0)), full(g),
                  full(wz), full(wxbc), full(wdtc), full(wdtT), full(wuf), full(wg),
                  full(bias_row), full(a_row), full(bias_col), full(a_col)],
        out_specs=[row(wz.shape[1]), row(wxbc.shape[1]), row(LANES), colspec, colspec,
                   row(wuf.shape[1]), row(wg.shape[1])],
        out_shape=[jax.ShapeDtypeStruct((t, wz.shape[1]), BF16),
                   jax.ShapeDtypeStruct((t, wxbc.shape[1]), BF16),
                   jax.ShapeDtypeStruct((t, LANES), F32),
                   jax.ShapeDtypeStruct((nd, t), F32),
                   jax.ShapeDtypeStruct((nd, t), F32),
                   jax.ShapeDtypeStruct((t, wuf.shape[1]), BF16),
                   jax.ShapeDtypeStruct((t, wg.shape[1]), BF16)],
        compiler_params=_cparams(("arbitrary",)),
        name="inproj",
    )(x2, mod, g, wz, wxbc, wdtc, wdtT, wuf, wg, bias_row, a_row, bias_col, a_col)


def _conv_kernel(u_ref, w_ref, b_ref, o_ref, pad_ref):
    s = u_ref.shape[1]
    halo = 8
    pad_ref[0:halo, :] = jnp.zeros((halo, LANES), F32)
    pad_ref[halo + s:2 * halo + s, :] = jnp.zeros((halo, LANES), F32)
    pad_ref[halo:halo + s, :] = u_ref[0].astype(F32)
    w = w_ref[...]
    b = b_ref[...]
    half = (SSD_CONV - 1) // 2
    for r in range(s // CONV_ROWS):
        base = r * CONV_ROWS
        acc = b
        for k in range(SSD_CONV):
            lo = base + halo + k - half
            acc = acc + w[k:k + 1, :] * pad_ref[lo:lo + CONV_ROWS, :]
        o_ref[0, base:base + CONV_ROWS, :] = _silu(acc)


def _conv(xbc3, conv_w, conv_b):
    bsz, s, c = xbc3.shape
    return pl.pallas_call(
        _conv_kernel,
        grid=(bsz, c // LANES),
        in_specs=[pl.BlockSpec((1, s, LANES), lambda b, j: (b, 0, j)),
                  pl.BlockSpec((SSD_CONV, LANES), lambda b, j: (0, j)),
                  pl.BlockSpec((1, LANES), lambda b, j: (0, j))],
        out_specs=pl.BlockSpec((1, s, LANES), lambda b, j: (b, 0, j)),
        out_shape=jax.ShapeDtypeStruct((bsz, s, c), F32),
        scratch_shapes=[pltpu.VMEM((s + 16, LANES), F32)],
        compiler_params=_cparams(("arbitrary", "arbitrary")),
        name="conv",
    )(xbc3, conv_w, conv_b.reshape(1, c))


HEAD_PAIRS = HEADS_PER_GROUP // 2
SSD_GROUPS_PER_STEP = 4


def _ssd_direction(x, bm, cm, dt_r, acs_r, acsc_all, lane_off, s_ref, reverse):
    L, N = bm.shape
    assert L == N == LANES
    ii = lax.broadcasted_iota(jnp.int32, (L, L), 0)
    jj = lax.broadcasted_iota(jnp.int32, (L, L), 1)
    mask = (jj >= ii) if reverse else (jj <= ii)
    lo_half = jj < SSD_HEAD_DIM
    shift = jnp.where(lane_off == 0, 0, LANES - lane_off)
    acs_c = pltpu.roll(acsc_all, shift, 1)
    last = 0 if reverse else L - 1
    tot_r = acs_r[:, last:last + 1]
    w_r = jnp.exp(tot_r - acs_r) * dt_r
    etot = jnp.broadcast_to(jnp.exp(tot_r), (HEADS_PADDED, LANES))

    cbt = _dot_nt(cm.astype(BF16), bm.astype(BF16))
    bt = bm.T
    ys = []
    for q in range(HEAD_PAIRS):
        xq = x[:, q * LANES:(q + 1) * LANES]
        sq = s_ref[q]
        x_a = jnp.where(lo_half, xq, 0.0).astype(BF16)
        x_b = jnp.where(lo_half, 0.0, xq).astype(BF16)
        s_a = jnp.where(lo_half, sq, 0.0).astype(BF16)
        s_b = jnp.where(lo_half, 0.0, sq).astype(BF16)
        m_parts, c_parts, b_parts = [], [], []
        for h in (2 * q, 2 * q + 1):
            col = jnp.broadcast_to(acs_c[:, h:h + 1], (L, L))
            decay = jnp.exp(jnp.where(mask, col - acs_r[h:h + 1, :], -jnp.inf))
            m_parts.append((cbt * decay * dt_r[h:h + 1, :]).astype(BF16))
            c_parts.append((cm * jnp.exp(col)).astype(BF16))
            b_parts.append((bt * w_r[h:h + 1, :]).astype(BF16))
        x_diag = jnp.concatenate([x_a, x_b], axis=0)
        lhs = jnp.concatenate(m_parts + c_parts, axis=1)
        rhs = jnp.concatenate([x_diag, s_a, s_b], axis=0)
        ys.append(_dot(lhs, rhs))
        dec = jnp.where(lo_half[0:1], etot[2 * q:2 * q + 1], etot[2 * q + 1:2 * q + 2])
        s_ref[q] = sq * dec + _dot(jnp.concatenate(b_parts, axis=1), x_diag)
    return jnp.concatenate(ys, axis=1)


def _ssd_kernel(xf_ref, bf_ref, cf_ref, dtTf_ref, acsTf_ref, acscf_ref,
                xb_ref, bb_ref, cb_ref, dtTb_ref, acsTb_ref, acscb_ref,
                yf_ref, yb_ref, sf_ref, sb_ref):
    gp = pl.program_id(1)

    @pl.when(pl.program_id(2) == 0)
    def _():
        sf_ref[...] = jnp.zeros_like(sf_ref)
        sb_ref[...] = jnp.zeros_like(sb_ref)

    for i in range(SSD_GROUPS_PER_STEP):
        g = gp * SSD_GROUPS_PER_STEP + i
        xs = slice(i * GROUP_X, (i + 1) * GROUP_X)
        ns = slice(i * SSD_STATE, (i + 1) * SSD_STATE)
        hs = slice(i * HEADS_PADDED, (i + 1) * HEADS_PADDED)
        yf_ref[0, :, xs] = _ssd_direction(
            xf_ref[0, :, xs], bf_ref[0, :, ns], cf_ref[0, :, ns], dtTf_ref[hs, :], acsTf_ref[hs, :],
            acscf_ref[...], g * HEADS_PADDED, sf_ref.at[i], False).astype(yf_ref.dtype)
        yb_ref[0, :, xs] = _ssd_direction(
            xb_ref[0, :, xs], bb_ref[0, :, ns], cb_ref[0, :, ns], dtTb_ref[hs, :], acsTb_ref[hs, :],
            acscb_ref[...], (SSD_GROUPS + g) * HEADS_PADDED, sb_ref.at[i], True).astype(yb_ref.dtype)


def _ssd(xbc3, dtT, acsT, acsc):
    bsz, s, _ = xbc3.shape
    L = SSD_CHUNK
    nc = s // L
    P = SSD_GROUPS_PER_STEP
    GP = SSD_GROUPS // P
    nb = SSD_INNER // (P * SSD_STATE)
    ncb = nb + GP
    fwd = lambda c: c
    bwd = lambda c: nc - 1 - c

    def specs(cidx, dirn):
        rowspec = pl.BlockSpec((P * HEADS_PADDED, L),
                               lambda b, g, c: (dirn * GP + g, b * nc + cidx(c)))
        return [
            pl.BlockSpec((1, L, P * GROUP_X), lambda b, g, c: (b, cidx(c), g)),
            pl.BlockSpec((1, L, P * SSD_STATE), lambda b, g, c: (b, cidx(c), nb + g)),
            pl.BlockSpec((1, L, P * SSD_STATE), lambda b, g, c: (b, cidx(c), ncb + g)),
            rowspec, rowspec,
            pl.BlockSpec((L, LANES), lambda b, g, c: (b * nc + cidx(c), 0)),
        ]

    out_specs = [pl.BlockSpec((1, L, P * GROUP_X), lambda b, g, c: (b, c, g)),
                 pl.BlockSpec((1, L, P * GROUP_X), lambda b, g, c: (b, nc - 1 - c, g))]
    return pl.pallas_call(
        _ssd_kernel,
        grid=(bsz, GP, nc),
        in_specs=specs(fwd, 0) + specs(bwd, 1),
        out_specs=out_specs,
        out_shape=[jax.ShapeDtypeStruct((bsz, s, SSD_INNER), BF16)] * 2,
        scratch_shapes=[pltpu.VMEM((P, HEAD_PAIRS, SSD_STATE, LANES), F32)] * 2,
        compiler_params=_cparams(("arbitrary", "arbitrary", "arbitrary")),
        name="ssd",
    )(xbc3, xbc3, xbc3, dtT, acsT, acsc, xbc3, xbc3, xbc3, dtT, acsT, acsc)


def _dft_tables(seq):
    n2n = LANES
    n1n = seq // n2n
    n1 = np.arange(n1n)
    k1 = np.arange(n1n)
    n2 = np.arange(n2n)
    ang = -2.0 * np.pi * (n2[:, None, None] * k1[None, :, None] / seq
                          + n1[None, None, :] * k1[None, :, None] / n1n)
    f1 = np.concatenate([np.cos(ang), np.sin(ang)], axis=1)
    k2 = np.arange(n2n)
    a2 = 2.0 * np.pi * np.outer(k2, n2) / n2n
    c2, s2 = np.cos(a2), np.sin(a2)
    g = np.block([[c2, s2], [-s2, c2]])
    ch = np.arange(FOURIER_GROUP_DIM)
    ac = 2.0 * np.pi * np.outer(ch, ch) / FOURIER_GROUP_DIM
    scale = 1.0 / math.sqrt(seq * FOURIER_GROUP_DIM)
    fc = np.concatenate([np.cos(ac), np.sin(ac)], axis=0) * scale
    return (jnp.asarray(f1, BF16), jnp.asarray(g, BF16), jnp.asarray(fc, BF16))


DFT_UNROLL = 4


def _dft_pitch(n1n):
    return 2 * n1n + 8


def _fourier_kernel(u_ref, f1_ref, g_ref, fc_ref, o_ref, a_ref):
    n2n = LANES
    n1n = u_ref.shape[2]
    pitch = _dft_pitch(n1n)

    def stage1(i, carry):
        for u in range(DFT_UNROLL):
            n2 = i * DFT_UNROLL + u
            xs = u_ref[0, 0, :, pl.ds(pl.multiple_of(n2 * LANES, LANES), LANES)].astype(BF16)
            a_ref[pl.ds(pl.multiple_of(n2 * pitch, 8), 2 * n1n), :] = _dot(f1_ref[n2], xs)
        return carry

    lax.fori_loop(0, n2n // DFT_UNROLL, stage1, 0)
    gm = g_ref[...]
    fc = fc_ref[...]

    def stage2(i, carry):
        for u in range(DFT_UNROLL):
            k1 = i * DFT_UNROLL + u
            re = a_ref[pl.ds(k1, n2n, stride=pitch), :]
            im = a_ref[pl.ds(n1n + k1, n2n, stride=pitch), :]
            a = jnp.concatenate([re, im], axis=0).astype(BF16)
            z = _dot(gm, a)
            zz = jnp.concatenate([z[:n2n], z[n2n:]], axis=1).astype(BF16)
            o_ref[0, pl.ds(k1, n2n, stride=n1n), :] = _dot(zz, fc)
        return carry

    lax.fori_loop(0, n1n // DFT_UNROLL, stage2, 0)


def _fourier(uf3):
    bsz, s, w = uf3.shape
    f1, g, fc = _dft_tables(s)
    n1n = s // LANES
    ng = w // FOURIER_GROUP_DIM
    u = uf3.reshape(bsz, n1n, LANES, ng, FOURIER_GROUP_DIM).transpose(0, 3, 1, 2, 4)
    u = u.reshape(bsz, ng, n1n, LANES * FOURIER_GROUP_DIM)
    return pl.pallas_call(
        _fourier_kernel,
        grid=(bsz, ng),
        in_specs=[pl.BlockSpec((1, 1, n1n, LANES * FOURIER_GROUP_DIM), lambda b, j: (b, j, 0, 0)),
                  pl.BlockSpec(f1.shape, lambda b, j: (0, 0, 0)),
                  pl.BlockSpec(g.shape, lambda b, j: (0, 0)),
                  pl.BlockSpec(fc.shape, lambda b, j: (0, 0))],
        out_specs=pl.BlockSpec((1, s, LANES), lambda b, j: (b, 0, j)),
        out_shape=jax.ShapeDtypeStruct((bsz, s, w), F32),
        scratch_shapes=[pltpu.VMEM((LANES * _dft_pitch(n1n), LANES), F32)],
        compiler_params=_cparams(("arbitrary", "arbitrary")),
        name="fourier",
    )(u, f1, g, fc)


def _mix_kernel(yf_ref, yb_ref, xs_ref, z_ref, fm_ref, gates_ref, x_ref, mod_ref, dskip_ref,
                ssdn_ref, wbs_ref, wbf_ref, wout_ref, gpost_ref, o_ref):
    m = mod_ref[0]
    y = yf_ref[...].astype(F32) + yb_ref[...].astype(F32) + dskip_ref[...] * xs_ref[...]
    v = y * _silu(z_ref[...].astype(F32))
    parts = []
    for g in range(SSD_GROUPS):
        vg = v[:, g * GROUP_X:(g + 1) * GROUP_X]
        parts.append(vg * lax.rsqrt(jnp.mean(vg * vg, axis=-1, keepdims=True) + RMS_EPS))
    vn = jnp.concatenate(parts, axis=1) * ssdn_ref[...]
    y_ssd = _dot(vn.astype(BF16), wbs_ref[...])
    y_fou = _dot(fm_ref[...].astype(BF16), wbf_ref[...])
    gt = _sigmoid(gates_ref[...].astype(F32))
    d = y_ssd.shape[1]
    mixed = gt[:, :d] * y_ssd + gt[:, d:] * y_fou
    mo = _dot(mixed.astype(BF16), wout_ref[...])
    o_ref[...] = x_ref[...] + m[2:3] * _rms(mo, gpost_ref[...])


def _mix(yf, yb, xbc, z, fm, gates, x2, mod, dskip, ssdn, wbs, wbf, wout, gpost, seq):
    t, d = x2.shape
    tm = TOKEN_TILE
    per_b = seq // tm
    full = lambda a: pl.BlockSpec(a.shape, lambda i: (0,) * a.ndim)
    row = lambda n: pl.BlockSpec((tm, n), lambda i: (i, 0))
    return pl.pallas_call(
        _mix_kernel,
        grid=(t // tm,),
        in_specs=[row(SSD_INNER), row(SSD_INNER), row(SSD_INNER), row(SSD_INNER),
                  row(FOURIER_WIDTH), row(2 * d), row(d),
                  pl.BlockSpec((1, 6, d), lambda i: (i // per_b, 0, 0)),
                  full(dskip), full(ssdn), full(wbs), full(wbf), full(wout), full(gpost)],
        out_specs=row(d),
        out_shape=jax.ShapeDtypeStruct((t, d), F32),
        compiler_params=_cparams(("arbitrary",)),
        name="mix",
    )(yf, yb, xbc, z, fm, gates, x2, mod, dskip, ssdn, wbs, wbf, wout, gpost)


def _route_kernel(x_ref, mod_ref, gpre_ref, wr_ref, rb_ref, hp_ref, idx_ref, w_ref, rank_ref,
                  counts_ref, run_ref):
    @pl.when(pl.program_id(0) == 0)
    def _():
        run_ref[...] = jnp.zeros_like(run_ref)

    m = mod_ref[0]
    h = _rms(x_ref[...], gpre_ref[...]) * (1.0 + m[4:5]) + m[3:4]
    tm, d = h.shape
    half = d // 2
    bits = pltpu.bitcast(h.astype(BF16).astype(F32), jnp.uint32)
    hp_ref[...] = (bits[:, :half] >> 16) | (bits[:, half:] & jnp.uint32(0xFFFF0000))

    h_hi = h.astype(BF16)
    h_lo = (h - h_hi.astype(F32)).astype(BF16)
    w_hi = wr_ref[0]
    logits = _dot(h_hi, w_hi) + (_dot(h_hi, wr_ref[1]) + _dot(h_lo, w_hi))
    scores = _sigmoid(logits)
    biased = scores + rb_ref[...]
    ne = scores.shape[1]
    per_group = ne // N_EXPERT_GROUPS
    lane = lax.broadcasted_iota(jnp.int32, (tm, ne), 1)
    lane_f = lane.astype(F32)
    out_lane = lax.broadcasted_iota(jnp.int32, (tm, LANES), 1)
    neg = -jnp.inf

    def argmax_first(v):
        mx = jnp.max(v, axis=-1, keepdims=True)
        ix = jnp.min(jnp.where(v == mx, lane_f, float(ne)), axis=-1, keepdims=True)
        return mx, ix

    gs = []
    gmat = jnp.full((tm, LANES), neg, F32)
    out_lane_f = out_lane.astype(F32)
    groups_per_tile = LANES // per_group
    for g in range(N_EXPERT_GROUPS):
        blk = g // groups_per_tile
        vb = biased[:, blk * LANES:(blk + 1) * LANES]
        vg = jnp.where(out_lane // per_group == g % groups_per_tile, vb, neg)
        m1 = jnp.max(vg, axis=-1, keepdims=True)
        i1 = jnp.min(jnp.where(vg == m1, out_lane_f, float(LANES)), axis=-1, keepdims=True)
        m2 = jnp.max(jnp.where(out_lane_f == i1, neg, vg), axis=-1, keepdims=True)
        gs.append(m1 + m2)
        gmat = jnp.where(out_lane == g, gs[g], gmat)
    ahead = jnp.zeros((tm, LANES), F32)
    for o in range(N_EXPERT_GROUPS):
        beats = (gs[o] > gmat) | ((gs[o] == gmat) & (out_lane > o))
        ahead = ahead + beats.astype(F32)
    sel = ((ahead < TOPK_GROUPS) & (out_lane < N_EXPERT_GROUPS)).astype(BF16)
    eg = lax.broadcasted_iota(jnp.int32, (LANES, ne), 0)
    ee = lax.broadcasted_iota(jnp.int32, (LANES, ne), 1)
    allowed = _dot(sel, (eg == ee // per_group).astype(BF16)) > 0.5
    masked = jnp.where(allowed, biased, neg)

    idx_out = jnp.zeros((tm, LANES), F32)
    w_out = jnp.zeros((tm, LANES), F32)
    w_sum = jnp.zeros((tm, 1), F32)
    hits = []
    for k in range(TOP_K):
        _, ik = argmax_first(masked)
        hit = lane_f == ik
        hits.append(hit)
        wk = jnp.sum(jnp.where(hit, scores, 0.0), axis=-1, keepdims=True)
        masked = jnp.where(hit, neg, masked)
        idx_out = jnp.where(out_lane == k, ik, idx_out)
        w_out = jnp.where(out_lane == k, wk, w_out)
        w_sum = w_sum + wk
    idx_ref[...] = idx_out.astype(jnp.int32)
    w_ref[...] = w_out / w_sum * ROUTED_SCALE

    chosen = functools.reduce(jnp.logical_or, hits).astype(F32)
    ti = lax.broadcasted_iota(jnp.int32, (tm, tm), 0)
    tj = lax.broadcasted_iota(jnp.int32, (tm, tm), 1)
    before = _dot((tj < ti).astype(BF16), chosen.astype(BF16)) + run_ref[...]
    rank_out = jnp.zeros((tm, LANES), jnp.int32)
    for k in range(TOP_K):
        rk = jnp.sum(jnp.where(hits[k], before, 0.0), axis=-1, keepdims=True)
        rank_out = jnp.where(out_lane == k, rk.astype(jnp.int32), rank_out)
    rank_ref[...] = rank_out
    run_ref[...] = run_ref[...] + jnp.sum(chosen, axis=0, keepdims=True)
    counts_ref[...] = run_ref[...]


def _route(x1, mod, gpre, w_router, router_bias, seq):
    t, d = x1.shape
    tm = TOKEN_TILE
    per_b = seq // tm
    full = lambda a: pl.BlockSpec(a.shape, lambda i: (0,) * a.ndim)
    row = lambda n: pl.BlockSpec((tm, n), lambda i: (i, 0))
    rb = router_bias.reshape(1, -1)
    w_hi = w_router.astype(BF16)
    w_router = jnp.stack([w_hi, (w_router - w_hi.astype(F32)).astype(BF16)])
    return pl.pallas_call(
        _route_kernel,
        grid=(t // tm,),
        in_specs=[row(d), pl.BlockSpec((1, 6, d), lambda i: (i // per_b, 0, 0)), full(gpre),
                  full(w_router), full(rb)],
        out_specs=[row(d // 2), row(LANES), row(LANES), row(LANES),
                   pl.BlockSpec((1, N_EXPERTS), lambda i: (0, 0))],
        out_shape=[jax.ShapeDtypeStruct((t, d // 2), jnp.uint32),
                   jax.ShapeDtypeStruct((t, LANES), jnp.int32),
                   jax.ShapeDtypeStruct((t, LANES), F32),
                   jax.ShapeDtypeStruct((t, LANES), jnp.int32),
                   jax.ShapeDtypeStruct((1, N_EXPERTS), F32)],
        scratch_shapes=[pltpu.VMEM((1, N_EXPERTS), F32)],
        compiler_params=_cparams(("arbitrary",)),
        name="route",
    )(x1, mod, gpre, w_router, rb)


GATHER_UNROLL = 8


ROW_TILE = 8


def _tile_rows(r):
    return pl.ds(pl.multiple_of(r * ROW_TILE, ROW_TILE), ROW_TILE)


def _gather_rows(idx_ref, n_rows, src_hbm, dst_of, sem):
    def body(q, carry):
        for u in range(GATHER_UNROLL):
            src = idx_ref[u, q]
            pltpu.make_async_copy(src_hbm.at[_tile_rows(src), :], dst_of(q, u), sem).start(
                priority=u % 2)
        return carry

    lax.fori_loop(0, n_rows // GATHER_UNROLL, body, 0)


def _to_tiled_rows(o_ref, y):
    rows, width = y.shape
    for r in range(rows // ROW_TILE):
        for j in range(width // LANES):
            o_ref[pl.ds(r * ROW_TILE * ROW_TILE + j, ROW_TILE, stride=ROW_TILE), :] = (
                y[r * ROW_TILE:(r + 1) * ROW_TILE, j * LANES:(j + 1) * LANES])


def _from_tiled_rows(load, rows, width):
    cols = []
    for j in range(width // LANES):
        pieces = [load(r * ROW_TILE * ROW_TILE + j, ROW_TILE, ROW_TILE) for r in range(rows // ROW_TILE)]
        cols.append(jnp.concatenate(pieces, axis=0))
    return jnp.concatenate(cols, axis=1)


def _dest_kernel(idx_ref, rank_ref, start_ref, o_ref):
    idx = idx_ref[...]
    rank = rank_ref[...]
    start = start_ref[...]
    tm = idx.shape[0]
    lane = lax.broadcasted_iota(jnp.int32, (tm, start.shape[1]), 1)
    out_lane = lax.broadcasted_iota(jnp.int32, (tm, LANES), 1)
    out = jnp.zeros((tm, LANES), jnp.int32)
    for k in range(TOP_K):
        base = jnp.sum(jnp.where(lane == idx[:, k:k + 1], start, 0.0), axis=-1, keepdims=True)
        out = jnp.where(out_lane == k, base.astype(jnp.int32) + rank[:, k:k + 1], out)
    o_ref[...] = out.T[:TOP_K]


def _dest(idx_pad, rank_pad, pad_start):
    t = idx_pad.shape[0]
    tm = DEST_TILE
    row = pl.BlockSpec((tm, LANES), lambda i: (i, 0))
    return pl.pallas_call(
        _dest_kernel,
        grid=(t // tm,),
        in_specs=[row, row, pl.BlockSpec((1, N_EXPERTS), lambda i: (0, 0))],
        out_specs=pl.BlockSpec((TOP_K, tm), lambda i: (0, i)),
        out_shape=jax.ShapeDtypeStruct((TOP_K, t), jnp.int32),
        compiler_params=_cparams(("arbitrary",)),
        name="dest",
    )(idx_pad, rank_pad, pad_start.astype(F32).reshape(1, N_EXPERTS))


def _scatter_kernel(b0_ref, nb_ref, nu_ref, dest_ref, hp_ref, xs_hbm, zbuf, sem, zsem):
    ts = hp_ref.shape[0]
    bm = zbuf.shape[0]
    n_blocks = xs_hbm.shape[0] // bm

    @pl.when(pl.program_id(0) == 0)
    def _():
        zbuf[...] = jnp.zeros_like(zbuf)
        n_used = nu_ref[0]

        def zero_block(g):
            return pltpu.make_async_copy(zbuf, xs_hbm.at[pl.ds(pl.multiple_of(g * bm, bm), bm), :], zsem)

        def tails(action):
            def step(e, carry):
                @pl.when(nb_ref[e] > 0)
                def _():
                    action(zero_block(b0_ref[e] + nb_ref[e] - 1))
                return carry
            lax.fori_loop(0, N_EXPERTS, step, 0)

        def unused(action):
            def step(g, carry):
                action(zero_block(g))
                return carry
            lax.fori_loop(n_used, n_blocks, step, 0)

        tails(lambda c: c.start())
        unused(lambda c: c.start())
        tails(lambda c: c.wait())
        unused(lambda c: c.wait())

    def body(t, carry):
        for k in range(TOP_K):
            dst = dest_ref[k, t]
            pltpu.make_async_copy(hp_ref.at[pl.ds(t, 1), :], xs_hbm.at[pl.ds(dst, 1), :], sem).start(
                priority=k % 2)
        return carry

    lax.fori_loop(0, ts, body, 0)
    for k in range(TOP_K):
        pltpu.make_async_copy(hp_ref, xs_hbm.at[pl.ds(0, ts), :], sem).wait()


def _scatter(dest, hp, first_block, n_block, n_used, n_rows):
    t, half = hp.shape
    ts = SCATTER_TILE
    n_steps = t // ts
    grid_spec = pltpu.PrefetchScalarGridSpec(
        num_scalar_prefetch=3,
        grid=(n_steps,),
        in_specs=[pl.BlockSpec((TOP_K, ts), lambda i, b0, nb, nu: (0, i),
                               memory_space=pltpu.SMEM),
                  pl.BlockSpec((ts, half), lambda i, b0, nb, nu: (i, 0))],
        out_specs=pl.BlockSpec(memory_space=pl.ANY),
        scratch_shapes=[pltpu.VMEM((EXPERT_ROWS, half), jnp.uint32),
                        pltpu.SemaphoreType.DMA(()), pltpu.SemaphoreType.DMA(())],
    )
    return pl.pallas_call(
        _scatter_kernel,
        grid_spec=grid_spec,
        out_shape=jax.ShapeDtypeStruct((n_rows, half), jnp.uint32),
        compiler_params=_cparams(("arbitrary",)),
        name="scatter",
    )(first_block, n_block, n_used, dest, hp)


def _expert_kernel(b0_ref, nb_ref, nu_ref, xs_hbm, w13_ref, w2_ref, y_hbm,
                   xbuf, ybuf, w13b, w2b, sem_in, sem_out):
    e = pl.program_id(0)
    n_used = nu_ref[0]
    b0 = b0_ref[e]
    nb = nb_ref[e]
    n_in, bm = xbuf.shape[:2]
    n_out, y_rows = ybuf.shape[:2]
    ahead = n_in - 1

    def in_copy(g, slot):
        return pltpu.make_async_copy(xs_hbm.at[pl.ds(pl.multiple_of(g * bm, bm), bm), :],
                                     xbuf.at[slot], sem_in.at[slot])

    def out_copy(g, slot):
        return pltpu.make_async_copy(ybuf.at[slot],
                                     y_hbm.at[pl.ds(pl.multiple_of(g * y_rows, y_rows), y_rows), :],
                                     sem_out.at[slot])

    @pl.when(nb > 0)
    def _():
        w13b[...] = w13_ref[0].astype(BF16)
        w2b[...] = w2_ref[0].astype(BF16)

    @pl.when((nb > 0) & (b0 == 0))
    def _():
        for a in range(ahead):
            @pl.when(a < n_used)
            def _():
                in_copy(a, a).start()

    def block(g, carry):
        in_slot = g % n_in
        out_slot = g % n_out
        in_copy(g, in_slot).wait()

        @pl.when(g + ahead < n_used)
        def _():
            in_copy(g + ahead, (g + ahead) % n_in).start()

        @pl.when(g >= n_out)
        def _():
            out_copy(g - n_out, out_slot).wait()

        w = xbuf[in_slot]
        lo = pltpu.bitcast(w << 16, F32).astype(BF16)
        hi = pltpu.bitcast(w & jnp.uint32(0xFFFF0000), F32).astype(BF16)
        half = lo.shape[1]
        ag = _dot(lo, w13b[:half, :]) + _dot(hi, w13b[half:, :])
        hh = ag.shape[1] // 2
        act = (_silu(ag[:, :hh]) * ag[:, hh:]).astype(BF16)
        _to_tiled_rows(ybuf.at[out_slot], _dot(act, w2b[...]))
        out_copy(g, out_slot).start()
        return carry

    lax.fori_loop(b0, b0 + nb, block, 0)

    @pl.when(e == pl.num_programs(0) - 1)
    def _():
        for back in range(1, n_out + 1):
            @pl.when(n_used >= back)
            def _():
                out_copy(n_used - back, (n_used - back) % n_out).wait()

        n_blocks = y_hbm.shape[0] // y_rows
        ybuf[0] = jnp.zeros((y_rows, LANES), F32)

        def start_zero(g, carry):
            out_copy(g, 0).start()
            return carry

        def wait_zero(g, carry):
            out_copy(g, 0).wait()
            return carry

        lax.fori_loop(n_used, n_blocks, start_zero, 0)
        lax.fori_loop(n_used, n_blocks, wait_zero, 0)


def _experts(xs, first_block, n_block, n_used, w13, w2):
    n_rows, half = xs.shape
    ne, d, h2 = w13.shape
    bm = EXPERT_ROWS
    y_rows = bm * d // LANES
    grid_spec = pltpu.PrefetchScalarGridSpec(
        num_scalar_prefetch=3,
        grid=(ne,),
        in_specs=[
            pl.BlockSpec(memory_space=pl.ANY),
            pl.BlockSpec((1, d, h2), lambda e, b0, nb, nu: (e, 0, 0)),
            pl.BlockSpec((1, h2 // 2, d), lambda e, b0, nb, nu: (e, 0, 0)),
        ],
        out_specs=pl.BlockSpec(memory_space=pl.ANY),
        scratch_shapes=[pltpu.VMEM((EXPERT_IN_SLOTS, bm, half), jnp.uint32),
                        pltpu.VMEM((EXPERT_OUT_SLOTS, y_rows, LANES), F32),
                        pltpu.VMEM((d, h2), BF16),
                        pltpu.VMEM((h2 // 2, d), BF16),
                        pltpu.SemaphoreType.DMA((EXPERT_IN_SLOTS,)),
                        pltpu.SemaphoreType.DMA((EXPERT_OUT_SLOTS,))],
    )
    return pl.pallas_call(
        _expert_kernel,
        grid_spec=grid_spec,
        out_shape=jax.ShapeDtypeStruct((n_rows * d // LANES, LANES), F32),
        compiler_params=_cparams(("arbitrary",)),
        name="experts",
    )(first_block, n_block, n_used, xs, w13, w2)


def _final_kernel(dc_ref, dn_ref, y_hbm, w_ref, x_ref, mod_ref, gpre_ref, gpost_ref, w13s_ref,
                  w2s_ref, o_ref, buf, sem):
    i = pl.program_id(0)
    slot = i % 2
    tm = x_ref.shape[0]

    def issue(d_ref, s):
        _gather_rows(d_ref, tm * TOP_K, y_hbm,
                     lambda q, u: buf.at[s, u, _tile_rows(q), :], sem.at[s])

    @pl.when(i == 0)
    def _():
        issue(dc_ref, 0)

    @pl.when(i + 1 < pl.num_programs(0))
    def _():
        issue(dn_ref, 1 - slot)

    for k in range(TOP_K):
        pltpu.make_async_copy(y_hbm.at[pl.ds(0, tm * ROW_TILE), :], buf.at[slot, k],
                              sem.at[slot]).wait()

    w = w_ref[...]
    x1 = x_ref[...]
    routed = None
    for k in range(TOP_K):
        yk = _from_tiled_rows(lambda a, n, st: buf[slot, k, pl.ds(a, n, stride=st), :], tm, x1.shape[1])
        routed = yk * w[:, k:k + 1] if routed is None else routed + yk * w[:, k:k + 1]

    m = mod_ref[0]
    h = (_rms(x1, gpre_ref[...]) * (1.0 + m[4:5]) + m[3:4]).astype(BF16)
    ag = _dot(h, w13s_ref[...])
    hh = ag.shape[1] // 2
    act = (_silu(ag[:, :hh]) * ag[:, hh:]).astype(BF16)
    ffn = routed + _dot(act, w2s_ref[...])
    o_ref[...] = x1 + m[5:6] * _rms(ffn, gpost_ref[...])


def _final(dest, y_sorted, top_w, x1, mod, gpre, gpost, w13s, w2s, seq):
    t, d = x1.shape
    tm = COMBINE_TILE
    per_b = seq // tm
    n_steps = t // tm
    full = lambda a: pl.BlockSpec(a.shape, lambda i: (0,) * a.ndim)
    row = lambda n: pl.BlockSpec((tm, n), lambda i: (i, 0))
    return pl.pallas_call(
        _final_kernel,
        grid=(n_steps,),
        in_specs=[
            pl.BlockSpec((TOP_K, tm), lambda i: (0, i), memory_space=pltpu.SMEM),
            pl.BlockSpec((TOP_K, tm), lambda i: (0, jnp.minimum(i + 1, n_steps - 1)),
                         memory_space=pltpu.SMEM),
            pl.BlockSpec(memory_space=pl.ANY),
            row(LANES), row(d), pl.BlockSpec((1, 6, d), lambda i: (i // per_b, 0, 0)),
            full(gpre), full(gpost), full(w13s), full(w2s)],
        out_specs=row(d),
        out_shape=jax.ShapeDtypeStruct((t, d), F32),
        scratch_shapes=[pltpu.VMEM((2, TOP_K, tm * ROW_TILE, LANES), F32),
                        pltpu.SemaphoreType.DMA((2,))],
        compiler_params=_cparams(("arbitrary",)),
        name="combine_final",
    )(dest, dest, y_sorted, top_w, x1, mod, gpre, gpost, w13s, w2s)


def _pad_heads(v):
    lead = v.shape[:-1]
    v = v.reshape(lead + (2, SSD_GROUPS, HEADS_PER_GROUP))
    v = jnp.pad(v, [(0, 0)] * (len(lead) + 2) + [(0, HEADS_PADDED - HEADS_PER_GROUP)])
    return v.reshape(lead + (2 * SSD_GROUPS * HEADS_PADDED,))


def _dispatch_plan(counts, n_tokens):
    bm = EXPERT_ROWS
    counts = counts.reshape(N_EXPERTS).astype(jnp.int32)
    padded = (counts + bm - 1) // bm * bm
    pad_end = jnp.cumsum(padded)
    pad_start = pad_end - padded
    n_rows = -(-n_tokens * TOP_K // bm) * bm + N_EXPERTS * bm
    first_block = (pad_start // bm).astype(jnp.int32)
    n_block = (padded // bm).astype(jnp.int32)
    n_used = (pad_end[-1] // bm).astype(jnp.int32).reshape(1)
    return pad_start, first_block, n_block, n_used, n_rows


def _layer(x, c, w_ada, b_ada, pre_norm_mix, post_norm_mix, pre_norm_ffn, post_norm_ffn, w_in,
           conv_w, conv_b, dt_bias_fwd, dt_bias_bwd, a_log_fwd, a_log_bwd, d_skip, ssd_norm,
           w_branch_ssd, w_branch_fourier, w_out, w_router, router_bias, w13_experts, w2_experts,
           w13_shared, w2_shared):
    bsz, seq, d = x.shape
    t = bsz * seq
    x2 = x.reshape(t, d)
    row = lambda v: v.reshape(1, -1).astype(F32)

    mod = _ada(c, w_ada, b_ada)

    i1 = SSD_INNER
    i2 = i1 + XBC_WIDTH
    i3 = i2 + 2 * SSD_HEADS
    i4 = i3 + FOURIER_WIDTH
    n_dt = 2 * SSD_GROUPS * HEADS_PADDED
    w_dtp = _pad_heads(w_in[:, i2:i3])
    wdtc = jnp.pad(w_dtp, ((0, 0), (0, LANES - n_dt))).astype(BF16)
    wdtT = w_dtp.T.astype(BF16)
    bias_p = _pad_heads(jnp.concatenate([dt_bias_fwd, dt_bias_bwd]).astype(F32))
    a_p = _pad_heads(-jnp.exp(jnp.concatenate([a_log_fwd, a_log_bwd]).astype(F32)))
    pad_row = lambda v: jnp.pad(v, (0, LANES - n_dt)).reshape(1, LANES)
    z, xbc, acsc, dtT, acsT, uf, gates = _inproj(
        x2, mod, row(pre_norm_mix), w_in[:, :i1].astype(BF16), w_in[:, i1:i2].astype(BF16),
        wdtc, wdtT, w_in[:, i3:i4].astype(BF16), w_in[:, i4:].astype(BF16),
        pad_row(bias_p), pad_row(a_p), bias_p.reshape(n_dt, 1), a_p.reshape(n_dt, 1), seq)

    xbc3 = _conv(xbc.reshape(bsz, seq, XBC_WIDTH), conv_w, conv_b)

    yf, yb = _ssd(xbc3, dtT, acsT, acsc)

    fm = _fourier(uf.reshape(bsz, seq, FOURIER_WIDTH))

    x1 = _mix(yf.reshape(t, SSD_INNER), yb.reshape(t, SSD_INNER), xbc3.reshape(t, XBC_WIDTH), z,
              fm.reshape(t, FOURIER_WIDTH), gates, x2, mod,
              row(jnp.repeat(d_skip, SSD_HEAD_DIM)), row(ssd_norm), w_branch_ssd.astype(BF16),
              w_branch_fourier.astype(BF16), w_out.astype(BF16), row(post_norm_mix), seq)

    hp, idx_pad, w_pad, rank_pad, counts = _route(x1, mod, row(pre_norm_ffn),
                                                  w_router.astype(F32), router_bias, seq)
    pad_start, first_block, n_block, n_used, n_rows = _dispatch_plan(counts, t)
    dest = _dest(idx_pad, rank_pad, pad_start)
    xs = _scatter(dest, hp, first_block, n_block, n_used, n_rows)
    y_sorted = _experts(xs, first_block, n_block, n_used, w13_experts, w2_experts)
    out = _final(dest, y_sorted, w_pad, x1, mod, row(pre_norm_ffn), row(post_norm_ffn),
                 w13_shared.astype(BF16), w2_shared.astype(BF16), seq)
    return out.reshape(bsz, seq, d)


def kernel(x, c, w_ada, b_ada, pre_norm_mix, post_norm_mix, pre_norm_ffn, post_norm_ffn, w_in,
           conv_w, conv_b, dt_bias_fwd, dt_bias_bwd, a_log_fwd, a_log_bwd, d_skip, ssd_norm,
           w_branch_ssd, w_branch_fourier, w_out, w_router, router_bias, w13_experts, w2_experts,
           w13_shared, w2_shared):
    for layer in range(w_ada.shape[0]):
        x = _layer(x, c, w_ada[layer], b_ada[layer], pre_norm_mix[layer], post_norm_mix[layer],
                   pre_norm_ffn[layer], post_norm_ffn[layer], w_in[layer], conv_w[layer],
                   conv_b[layer], dt_bias_fwd[layer], dt_bias_bwd[layer], a_log_fwd[layer],
                   a_log_bwd[layer], d_skip[layer], ssd_norm[layer], w_branch_ssd[layer],
                   w_branch_fourier[layer], w_out[layer], w_router[layer], router_bias[layer],
                   w13_experts[layer], w2_experts[layer], w13_shared[layer], w2_shared[layer])
    return x
```

```python
import functools
import math

import numpy as np
import jax
import jax.numpy as jnp
from jax import lax
from jax.experimental import pallas as pl
from jax.experimental.pallas import tpu as pltpu

F32 = jnp.float32
BF16 = jnp.bfloat16
HIGHEST = lax.Precision.HIGHEST

D_MODEL = 1024
SSD_HEADS = 24
SSD_HEAD_DIM = 64
SSD_INNER = SSD_HEADS * SSD_HEAD_DIM
SSD_GROUPS = 4
HEADS_PER_GROUP = SSD_HEADS // SSD_GROUPS
HEADS_PADDED = 8
SSD_STATE = 128
SSD_CONV = 5
SSD_CHUNK = 128
XBC_WIDTH = SSD_INNER + 2 * SSD_GROUPS * SSD_STATE
GROUP_X = HEADS_PER_GROUP * SSD_HEAD_DIM
FOURIER_WIDTH = 512
FOURIER_GROUP_DIM = 128
N_EXPERTS = 256
TOP_K = 8
N_EXPERT_GROUPS = 8
TOPK_GROUPS = 4
EXPERT_HIDDEN = 256
SHARED_HIDDEN = 256
ROUTED_SCALE = 2.5
RMS_EPS = 1e-6

LANES = 128
VMEM_LIMIT = 56 * 1024 * 1024
TOKEN_TILE = 256
INPROJ_TILE = 512
INPROJ_SUBTILE = 256
EXPERT_ROWS = 256
COMBINE_TILE = 128
SCATTER_TILE = 512
DEST_TILE = 1024
EXPERT_IN_SLOTS = 4
EXPERT_OUT_SLOTS = 3
CONV_ROWS = 256


def _cparams(sem):
    return pltpu.CompilerParams(dimension_semantics=sem, vmem_limit_bytes=VMEM_LIMIT)


def _dot(a, b, precision=None):
    return jnp.dot(a, b, preferred_element_type=F32, precision=precision)


def _dot_nt(a, b, precision=None):
    return lax.dot_general(a, b, (((1,), (1,)), ((), ())), preferred_element_type=F32,
                           precision=precision)


def _dot_tn(a, b):
    return lax.dot_general(a, b, (((0,), (0,)), ((), ())), preferred_element_type=F32)


def _sigmoid(x):
    return 1.0 / (1.0 + jnp.exp(-x))


def _silu(x):
    return x * _sigmoid(x)


def _softplus(x):
    return jnp.maximum(x, 0.0) + jnp.log1p(jnp.exp(-jnp.abs(x)))


def _rms(x, g):
    return x * lax.rsqrt(jnp.mean(x * x, axis=-1, keepdims=True) + RMS_EPS) * g


def _ada_kernel(c_ref, w_ref, b_ref, o_ref):
    o_ref[...] = _dot(_silu(c_ref[...]), w_ref[...], HIGHEST) + b_ref[...]


def _ada(c, w_ada, b_ada):
    bsz, d = c.shape
    rows = 8
    cp = jnp.zeros((rows, d), F32).at[:bsz].set(c)
    n = w_ada.shape[1]
    tn = 1536
    out = pl.pallas_call(
        _ada_kernel,
        grid=(n // tn,),
        in_specs=[pl.BlockSpec((rows, d), lambda j: (0, 0)),
                  pl.BlockSpec((d, tn), lambda j: (0, j)),
                  pl.BlockSpec((1, tn), lambda j: (0, j))],
        out_specs=pl.BlockSpec((rows, tn), lambda j: (0, j)),
        out_shape=jax.ShapeDtypeStruct((rows, n), F32),
        compiler_params=_cparams(("arbitrary",)),
        name="adaln",
    )(cp, w_ada, b_ada.reshape(1, n))
    return out[:bsz].reshape(bsz, 6, d)


def _split3(a):
    a1 = a.astype(BF16)
    r1 = a - a1.astype(F32)
    a2 = r1.astype(BF16)
    a3 = (r1 - a2.astype(F32)).astype(BF16)
    return a1, a2, a3


def _inproj_kernel(x_ref, mod_ref, g_ref, wz_ref, wxbc_ref, wdtc_ref, wdtT_ref, wuf_ref, wg_ref,
                   bias_row_ref, a_row_ref, bias_col_ref, a_col_ref,
                   z_ref, xbc_ref, acsc_ref, dtT_ref, acsT_ref, uf_ref, gates_ref):
    m = mod_ref[0]
    tm = INPROJ_SUBTILE
    n_fwd = SSD_GROUPS * HEADS_PADDED
    ii = lax.broadcasted_iota(jnp.int32, (tm, tm), 0)
    jj = lax.broadcasted_iota(jnp.int32, (tm, tm), 1)
    same = (ii // SSD_CHUNK) == (jj // SSD_CHUNK)
    tri_f = (same & (jj <= ii)).astype(BF16)
    tri_b = (same & (jj >= ii)).astype(BF16)

    for s in range(x_ref.shape[0] // tm):
        rows = slice(s * tm, (s + 1) * tm)
        h = _rms(x_ref[rows, :], g_ref[...]) * (1.0 + m[1:2]) + m[0:1]
        hb = h.astype(BF16)
        z_ref[rows, :] = _dot(hb, wz_ref[...]).astype(z_ref.dtype)
        xbc_ref[rows, :] = _dot(hb, wxbc_ref[...]).astype(xbc_ref.dtype)
        uf_ref[rows, :] = _dot(hb, wuf_ref[...]).astype(uf_ref.dtype)
        gates_ref[rows, :] = _dot(hb, wg_ref[...]).astype(gates_ref.dtype)

        dt_c = _softplus(_dot(hb, wdtc_ref[...]) + bias_row_ref[...])
        pieces = _split3(dt_c * a_row_ref[...])
        acs_f = sum(_dot(tri_f, p) for p in pieces)
        acs_b = sum(_dot(tri_b, p) for p in pieces)
        lane = lax.broadcasted_iota(jnp.int32, acs_f.shape, 1)
        acsc_ref[rows, :] = jnp.where(lane < n_fwd, acs_f, acs_b)

        dt_t = _softplus(_dot_nt(wdtT_ref[...], hb) + bias_col_ref[...])
        pieces = _split3(dt_t * a_col_ref[...])
        acs_f = sum(_dot_nt(p, tri_f) for p in pieces)
        acs_b = sum(_dot_nt(p, tri_b) for p in pieces)
        sub = lax.broadcasted_iota(jnp.int32, acs_f.shape, 0)
        dtT_ref[:, rows] = dt_t
        acsT_ref[:, rows] = jnp.where(sub < n_fwd, acs_f, acs_b)


def _inproj(x2, mod, g, wz, wxbc, wdtc, wdtT, wuf, wg, bias_row, a_row, bias_col, a_col, seq):
    t, d = x2.shape
    tm = INPROJ_TILE
    per_b = seq // tm
    full = lambda a: pl.BlockSpec(a.shape, lambda i: (0,) * a.ndim)
    row = lambda n: pl.BlockSpec((tm, n), lambda i: (i, 0))
    nd = wdtT.shape[0]
    colspec = pl.BlockSpec((nd, tm), lambda i: (0, i))
    return pl.pallas_call(
        _inproj_kernel,
        grid=(t // tm,),
        in_specs=[row(d), pl.BlockSpec((1, 6, d), lambda i: (i // per_b, 0, 0)), full(g),
                  full(wz), full(wxbc), full(wdtc), full(wdtT), full(wuf), full(wg),
                  full(bias_row), full(a_row), full(bias_col), full(a_col)],
        out_specs=[row(wz.shape[1]), row(wxbc.shape[1]), row(LANES), colspec, colspec,
                   row(wuf.shape[1]), row(wg.shape[1])],
        out_shape=[jax.ShapeDtypeStruct((t, wz.shape[1]), BF16),
                   jax.ShapeDtypeStruct((t, wxbc.shape[1]), BF16),
                   jax.ShapeDtypeStruct((t, LANES), F32),
                   jax.ShapeDtypeStruct((nd, t), F32),
                   jax.ShapeDtypeStruct((nd, t), F32),
                   jax.ShapeDtypeStruct((t, wuf.shape[1]), BF16),
                   jax.ShapeDtypeStruct((t, wg.shape[1]), BF16)],
        compiler_params=_cparams(("arbitrary",)),
        name="inproj",
    )(x2, mod, g, wz, wxbc, wdtc, wdtT, wuf, wg, bias_row, a_row, bias_col, a_col)


def _conv_kernel(u_ref, w_ref, b_ref, o_ref, pad_ref):
    s = u_ref.shape[1]
    halo = 8
    pad_ref[0:halo, :] = jnp.zeros((halo, LANES), F32)
    pad_ref[halo + s:2 * halo + s, :] = jnp.zeros((halo, LANES), F32)
    pad_ref[halo:halo + s, :] = u_ref[0].astype(F32)
    w = w_ref[...]
    b = b_ref[...]
    half = (SSD_CONV - 1) // 2
    for r in range(s // CONV_ROWS):
        base = r * CONV_ROWS
        acc = b
        for k in range(SSD_CONV):
            lo = base + halo + k - half
            acc = acc + w[k:k + 1, :] * pad_ref[lo:lo + CONV_ROWS, :]
        o_ref[0, base:base + CONV_ROWS, :] = _silu(acc).astype(o_ref.dtype)


def _conv(xbc3, conv_w, conv_b):
    bsz, s, c = xbc3.shape
    return pl.pallas_call(
        _conv_kernel,
        grid=(bsz, c // LANES),
        in_specs=[pl.BlockSpec((1, s, LANES), lambda b, j: (b, 0, j)),
                  pl.BlockSpec((SSD_CONV, LANES), lambda b, j: (0, j)),
                  pl.BlockSpec((1, LANES), lambda b, j: (0, j))],
        out_specs=pl.BlockSpec((1, s, LANES), lambda b, j: (b, 0, j)),
        out_shape=jax.ShapeDtypeStruct((bsz, s, c), BF16),
        scratch_shapes=[pltpu.VMEM((s + 16, LANES), F32)],
        compiler_params=_cparams(("arbitrary", "arbitrary")),
        name="conv",
    )(xbc3, conv_w, conv_b.reshape(1, c))


HEAD_PAIRS = HEADS_PER_GROUP // 2
SSD_GROUPS_PER_STEP = 4


def _ssd_direction(x, bm, cm, dt_r, acs_r, acsc_all, lane_off, s_ref, reverse):
    L, N = bm.shape
    assert L == N == LANES
    ii = lax.broadcasted_iota(jnp.int32, (L, L), 0)
    jj = lax.broadcasted_iota(jnp.int32, (L, L), 1)
    mask = (jj >= ii) if reverse else (jj <= ii)
    lo_half = jj < SSD_HEAD_DIM
    shift = jnp.where(lane_off == 0, 0, LANES - lane_off)
    acs_c = pltpu.roll(acsc_all, shift, 1)
    last = 0 if reverse else L - 1
    tot_r = acs_r[:, last:last + 1]
    w_r = jnp.exp(tot_r - acs_r) * dt_r
    etot = jnp.broadcast_to(jnp.exp(tot_r), (HEADS_PADDED, LANES))

    cbt = _dot_nt(cm.astype(BF16), bm.astype(BF16))
    bt = bm.astype(F32).T
    cm = cm.astype(F32)
    ys = []
    for q in range(HEAD_PAIRS):
        xq = x[:, q * LANES:(q + 1) * LANES]
        sq = s_ref[q]
        x_a = jnp.where(lo_half, xq, jnp.zeros_like(xq)).astype(BF16)
        x_b = jnp.where(lo_half, jnp.zeros_like(xq), xq).astype(BF16)
        s_a = jnp.where(lo_half, sq, 0.0).astype(BF16)
        s_b = jnp.where(lo_half, 0.0, sq).astype(BF16)
        m_parts, c_parts, b_parts = [], [], []
        for h in (2 * q, 2 * q + 1):
            col = jnp.broadcast_to(acs_c[:, h:h + 1], (L, L))
            decay = jnp.exp(jnp.where(mask, col - acs_r[h:h + 1, :], -jnp.inf))
            m_parts.append((cbt * decay * dt_r[h:h + 1, :]).astype(BF16))
            c_parts.append((cm * jnp.exp(col)).astype(BF16))
            b_parts.append((bt * w_r[h:h + 1, :]).astype(BF16))
        x_diag = jnp.concatenate([x_a, x_b], axis=0)
        lhs = jnp.concatenate(m_parts + c_parts, axis=1)
        rhs = jnp.concatenate([x_diag, s_a, s_b], axis=0)
        ys.append(_dot(lhs, rhs))
        dec = jnp.where(lo_half[0:1], etot[2 * q:2 * q + 1], etot[2 * q + 1:2 * q + 2])
        s_ref[q] = sq * dec + _dot(jnp.concatenate(b_parts, axis=1), x_diag)
    return jnp.concatenate(ys, axis=1)


def _ssd_kernel(xf_ref, bf_ref, cf_ref, dtTf_ref, acsTf_ref, acscf_ref,
                xb_ref, bb_ref, cb_ref, dtTb_ref, acsTb_ref, acscb_ref, dskip_ref,
                yf_ref, yb_ref, sf_ref, sb_ref):
    gp = pl.program_id(1)

    @pl.when(pl.program_id(2) == 0)
    def _():
        sf_ref[...] = jnp.zeros_like(sf_ref)
        sb_ref[...] = jnp.zeros_like(sb_ref)

    for i in range(SSD_GROUPS_PER_STEP):
        g = gp * SSD_GROUPS_PER_STEP + i
        xs = slice(i * GROUP_X, (i + 1) * GROUP_X)
        ns = slice(i * SSD_STATE, (i + 1) * SSD_STATE)
        hs = slice(i * HEADS_PADDED, (i + 1) * HEADS_PADDED)
        y_fwd = _ssd_direction(
            xf_ref[0, :, xs], bf_ref[0, :, ns], cf_ref[0, :, ns], dtTf_ref[hs, :], acsTf_ref[hs, :],
            acscf_ref[...], g * HEADS_PADDED, sf_ref.at[i], False)
        y_fwd = y_fwd + dskip_ref[:, xs] * xf_ref[0, :, xs].astype(F32)
        yf_ref[0, :, xs] = y_fwd.astype(yf_ref.dtype)
        yb_ref[0, :, xs] = _ssd_direction(
            xb_ref[0, :, xs], bb_ref[0, :, ns], cb_ref[0, :, ns], dtTb_ref[hs, :], acsTb_ref[hs, :],
            acscb_ref[...], (SSD_GROUPS + g) * HEADS_PADDED, sb_ref.at[i], True).astype(yb_ref.dtype)


def _ssd(xbc3, dtT, acsT, acsc, dskip):
    bsz, s, _ = xbc3.shape
    L = SSD_CHUNK
    nc = s // L
    P = SSD_GROUPS_PER_STEP
    GP = SSD_GROUPS // P
    nb = SSD_INNER // (P * SSD_STATE)
    ncb = nb + GP
    fwd = lambda c: c
    bwd = lambda c: nc - 1 - c

    def specs(cidx, dirn):
        rowspec = pl.BlockSpec((P * HEADS_PADDED, L),
                               lambda b, g, c: (dirn * GP + g, b * nc + cidx(c)))
        return [
            pl.BlockSpec((1, L, P * GROUP_X), lambda b, g, c: (b, cidx(c), g)),
            pl.BlockSpec((1, L, P * SSD_STATE), lambda b, g, c: (b, cidx(c), nb + g)),
            pl.BlockSpec((1, L, P * SSD_STATE), lambda b, g, c: (b, cidx(c), ncb + g)),
            rowspec, rowspec,
            pl.BlockSpec((L, LANES), lambda b, g, c: (b * nc + cidx(c), 0)),
        ]

    out_specs = [pl.BlockSpec((1, L, P * GROUP_X), lambda b, g, c: (b, c, g)),
                 pl.BlockSpec((1, L, P * GROUP_X), lambda b, g, c: (b, nc - 1 - c, g))]
    return pl.pallas_call(
        _ssd_kernel,
        grid=(bsz, GP, nc),
        in_specs=specs(fwd, 0) + specs(bwd, 1)
        + [pl.BlockSpec((1, P * GROUP_X), lambda b, g, c: (0, g))],
        out_specs=out_specs,
        out_shape=[jax.ShapeDtypeStruct((bsz, s, SSD_INNER), BF16)] * 2,
        scratch_shapes=[pltpu.VMEM((P, HEAD_PAIRS, SSD_STATE, LANES), F32)] * 2,
        compiler_params=_cparams(("arbitrary", "arbitrary", "arbitrary")),
        name="ssd",
    )(xbc3, xbc3, xbc3, dtT, acsT, acsc, xbc3, xbc3, xbc3, dtT, acsT, acsc, dskip)


def _dft_tables(seq):
    n2n = LANES
    n1n = seq // n2n
    n1 = np.arange(n1n)
    k1 = np.arange(n1n)
    n2 = np.arange(n2n)
    ang = -2.0 * np.pi * (n2[:, None, None] * k1[None, :, None] / seq
                          + n1[None, None, :] * k1[None, :, None] / n1n)
    f1 = np.concatenate([np.cos(ang), np.sin(ang)], axis=1)
    k2 = np.arange(n2n)
    a2 = 2.0 * np.pi * np.outer(k2, n2) / n2n
    c2, s2 = np.cos(a2), np.sin(a2)
    g = np.block([[c2, s2], [-s2, c2]])
    ch = np.arange(FOURIER_GROUP_DIM)
    ac = 2.0 * np.pi * np.outer(ch, ch) / FOURIER_GROUP_DIM
    scale = 1.0 / math.sqrt(seq * FOURIER_GROUP_DIM)
    fc = np.concatenate([np.cos(ac), np.sin(ac)], axis=0) * scale
    return (jnp.asarray(f1, BF16), jnp.asarray(g, BF16), jnp.asarray(fc, BF16))


DFT_UNROLL = 4


def _dft_pitch(n1n):
    return 2 * n1n + 8


def _fourier_kernel(u_ref, f1_ref, g_ref, fc_ref, o_ref, a_ref):
    n2n = LANES
    n1n = u_ref.shape[2]
    pitch = _dft_pitch(n1n)

    def stage1(i, carry):
        for u in range(DFT_UNROLL):
            n2 = i * DFT_UNROLL + u
            xs = u_ref[0, 0, :, pl.ds(pl.multiple_of(n2 * LANES, LANES), LANES)].astype(BF16)
            a_ref[pl.ds(pl.multiple_of(n2 * pitch, 8), 2 * n1n), :] = _dot(f1_ref[n2], xs)
        return carry

    lax.fori_loop(0, n2n // DFT_UNROLL, stage1, 0)
    gm = g_ref[...]
    fc = fc_ref[...]

    def stage2(i, carry):
        for u in range(DFT_UNROLL):
            k1 = i * DFT_UNROLL + u
            re = a_ref[pl.ds(k1, n2n, stride=pitch), :]
            im = a_ref[pl.ds(n1n + k1, n2n, stride=pitch), :]
            a = jnp.concatenate([re, im], axis=0).astype(BF16)
            z = _dot(gm, a)
            zz = jnp.concatenate([z[:n2n], z[n2n:]], axis=1).astype(BF16)
            o_ref[0, pl.ds(k1, n2n, stride=n1n), :] = _dot(zz, fc)
        return carry

    lax.fori_loop(0, n1n // DFT_UNROLL, stage2, 0)


def _fourier(uf3):
    bsz, s, w = uf3.shape
    f1, g, fc = _dft_tables(s)
    n1n = s // LANES
    ng = w // FOURIER_GROUP_DIM
    u = uf3.reshape(bsz, n1n, LANES, ng, FOURIER_GROUP_DIM).transpose(0, 3, 1, 2, 4)
    u = u.reshape(bsz, ng, n1n, LANES * FOURIER_GROUP_DIM)
    return pl.pallas_call(
        _fourier_kernel,
        grid=(bsz, ng),
        in_specs=[pl.BlockSpec((1, 1, n1n, LANES * FOURIER_GROUP_DIM), lambda b, j: (b, j, 0, 0)),
                  pl.BlockSpec(f1.shape, lambda b, j: (0, 0, 0)),
                  pl.BlockSpec(g.shape, lambda b, j: (0, 0)),
                  pl.BlockSpec(fc.shape, lambda b, j: (0, 0))],
        out_specs=pl.BlockSpec((1, s, LANES), lambda b, j: (b, 0, j)),
        out_shape=jax.ShapeDtypeStruct((bsz, s, w), F32),
        scratch_shapes=[pltpu.VMEM((LANES * _dft_pitch(n1n), LANES), F32)],
        compiler_params=_cparams(("arbitrary", "arbitrary")),
        name="fourier",
    )(u, f1, g, fc)


def _mix_kernel(yf_ref, yb_ref, z_ref, fm_ref, gates_ref, x_ref, mod_ref,
                ssdn_ref, wbs_ref, wbf_ref, wout_ref, gpost_ref, o_ref):
    m = mod_ref[0]
    y = yf_ref[...].astype(F32) + yb_ref[...].astype(F32)
    v = y * _silu(z_ref[...].astype(F32))
    parts = []
    for g in range(SSD_GROUPS):
        vg = v[:, g * GROUP_X:(g + 1) * GROUP_X]
        parts.append(vg * lax.rsqrt(jnp.mean(vg * vg, axis=-1, keepdims=True) + RMS_EPS))
    vn = jnp.concatenate(parts, axis=1) * ssdn_ref[...]
    y_ssd = _dot(vn.astype(BF16), wbs_ref[...])
    y_fou = _dot(fm_ref[...].astype(BF16), wbf_ref[...])
    gt = _sigmoid(gates_ref[...].astype(F32))
    d = y_ssd.shape[1]
    mixed = gt[:, :d] * y_ssd + gt[:, d:] * y_fou
    mo = _dot(mixed.astype(BF16), wout_ref[...])
    o_ref[...] = x_ref[...] + m[2:3] * _rms(mo, gpost_ref[...])


def _mix(yf, yb, z, fm, gates, x2, mod, ssdn, wbs, wbf, wout, gpost, seq):
    t, d = x2.shape
    tm = TOKEN_TILE
    per_b = seq // tm
    full = lambda a: pl.BlockSpec(a.shape, lambda i: (0,) * a.ndim)
    row = lambda n: pl.BlockSpec((tm, n), lambda i: (i, 0))
    return pl.pallas_call(
        _mix_kernel,
        grid=(t // tm,),
        in_specs=[row(SSD_INNER), row(SSD_INNER), row(SSD_INNER),
                  row(FOURIER_WIDTH), row(2 * d), row(d),
                  pl.BlockSpec((1, 6, d), lambda i: (i // per_b, 0, 0)),
                  full(ssdn), full(wbs), full(wbf), full(wout), full(gpost)],
        out_specs=row(d),
        out_shape=jax.ShapeDtypeStruct((t, d), F32),
        compiler_params=_cparams(("arbitrary",)),
        name="mix",
    )(yf, yb, z, fm, gates, x2, mod, ssdn, wbs, wbf, wout, gpost)


def _route_kernel(x_ref, mod_ref, gpre_ref, wr_ref, rb_ref, hp_ref, idx_ref, w_ref, rank_ref,
                  counts_ref, run_ref):
    @pl.when(pl.program_id(0) == 0)
    def _():
        run_ref[...] = jnp.zeros_like(run_ref)

    m = mod_ref[0]
    h = _rms(x_ref[...], gpre_ref[...]) * (1.0 + m[4:5]) + m[3:4]
    tm, d = h.shape
    half = d // 2
    bits = pltpu.bitcast(h.astype(BF16).astype(F32), jnp.uint32)
    hp_ref[...] = (bits[:, :half] >> 16) | (bits[:, half:] & jnp.uint32(0xFFFF0000))

    h_hi = h.astype(BF16)
    h_lo = (h - h_hi.astype(F32)).astype(BF16)
    w_hi = wr_ref[0]
    logits = _dot(h_hi, w_hi) + (_dot(h_hi, wr_ref[1]) + _dot(h_lo, w_hi))
    scores = _sigmoid(logits)
    biased = scores + rb_ref[...]
    ne = scores.shape[1]
    per_group = ne // N_EXPERT_GROUPS
    lane = lax.broadcasted_iota(jnp.int32, (tm, ne), 1)
    lane_f = lane.astype(F32)
    grp = lane // per_group
    out_lane = lax.broadcasted_iota(jnp.int32, (tm, LANES), 1)
    neg = -jnp.inf

    def argmax_first(v):
        mx = jnp.max(v, axis=-1, keepdims=True)
        ix = jnp.min(jnp.where(v == mx, lane_f, float(ne)), axis=-1, keepdims=True)
        return mx, ix

    gs = []
    gmat = jnp.full((tm, LANES), neg, F32)
    for g in range(N_EXPERT_GROUPS):
        vg = jnp.where(grp == g, biased, neg)
        m1, i1 = argmax_first(vg)
        m2 = jnp.max(jnp.where(lane_f == i1, neg, vg), axis=-1, keepdims=True)
        gs.append(m1 + m2)
        gmat = jnp.where(out_lane == g, gs[g], gmat)
    ahead = jnp.zeros((tm, LANES), F32)
    for o in range(N_EXPERT_GROUPS):
        beats = (gs[o] > gmat) | ((gs[o] == gmat) & (out_lane > o))
        ahead = ahead + beats.astype(F32)
    sel = ((ahead < TOPK_GROUPS) & (out_lane < N_EXPERT_GROUPS)).astype(BF16)
    eg = lax.broadcasted_iota(jnp.int32, (LANES, ne), 0)
    ee = lax.broadcasted_iota(jnp.int32, (LANES, ne), 1)
    allowed = _dot(sel, (eg == ee // per_group).astype(BF16)) > 0.5
    masked = jnp.where(allowed, biased, neg)

    idx_out = jnp.zeros((tm, LANES), F32)
    w_out = jnp.zeros((tm, LANES), F32)
    w_sum = jnp.zeros((tm, 1), F32)
    hits = []
    for k in range(TOP_K):
        _, ik = argmax_first(masked)
        hit = lane_f == ik
        hits.append(hit)
        wk = jnp.sum(jnp.where(hit, scores, 0.0), axis=-1, keepdims=True)
        masked = jnp.where(hit, neg, masked)
        idx_out = jnp.where(out_lane == k, ik, idx_out)
        w_out = jnp.where(out_lane == k, wk, w_out)
        w_sum = w_sum + wk
    idx_ref[...] = idx_out.astype(jnp.int32)
    w_ref[...] = w_out / w_sum * ROUTED_SCALE

    chosen = functools.reduce(jnp.logical_or, hits).astype(F32)
    ti = lax.broadcasted_iota(jnp.int32, (tm, tm), 0)
    tj = lax.broadcasted_iota(jnp.int32, (tm, tm), 1)
    before = _dot((tj < ti).astype(BF16), chosen.astype(BF16)) + run_ref[...]
    rank_out = jnp.zeros((tm, LANES), jnp.int32)
    for k in range(TOP_K):
        rk = jnp.sum(jnp.where(hits[k], before, 0.0), axis=-1, keepdims=True)
        rank_out = jnp.where(out_lane == k, rk.astype(jnp.int32), rank_out)
    rank_ref[...] = rank_out
    run_ref[...] = run_ref[...] + jnp.sum(chosen, axis=0, keepdims=True)
    counts_ref[...] = run_ref[...]


def _route(x1, mod, gpre, w_router, router_bias, seq):
    t, d = x1.shape
    tm = TOKEN_TILE
    per_b = seq // tm
    full = lambda a: pl.BlockSpec(a.shape, lambda i: (0,) * a.ndim)
    row = lambda n: pl.BlockSpec((tm, n), lambda i: (i, 0))
    rb = router_bias.reshape(1, -1)
    w_hi = w_router.astype(BF16)
    w_router = jnp.stack([w_hi, (w_router - w_hi.astype(F32)).astype(BF16)])
    return pl.pallas_call(
        _route_kernel,
        grid=(t // tm,),
        in_specs=[row(d), pl.BlockSpec((1, 6, d), lambda i: (i // per_b, 0, 0)), full(gpre),
                  full(w_router), full(rb)],
        out_specs=[row(d // 2), row(LANES), row(LANES), row(LANES),
                   pl.BlockSpec((1, N_EXPERTS), lambda i: (0, 0))],
        out_shape=[jax.ShapeDtypeStruct((t, d // 2), jnp.uint32),
                   jax.ShapeDtypeStruct((t, LANES), jnp.int32),
                   jax.ShapeDtypeStruct((t, LANES), F32),
                   jax.ShapeDtypeStruct((t, LANES), jnp.int32),
                   jax.ShapeDtypeStruct((1, N_EXPERTS), F32)],
        scratch_shapes=[pltpu.VMEM((1, N_EXPERTS), F32)],
        compiler_params=_cparams(("arbitrary",)),
        name="route",
    )(x1, mod, gpre, w_router, rb)


GATHER_UNROLL = 8


ROW_TILE = 8


def _tile_rows(r):
    return pl.ds(pl.multiple_of(r * ROW_TILE, ROW_TILE), ROW_TILE)


def _gather_rows(idx_ref, n_rows, src_hbm, dst_of, sem):
    def body(q, carry):
        for u in range(GATHER_UNROLL):
            src = idx_ref[0, 0, q * GATHER_UNROLL + u]
            pltpu.make_async_copy(src_hbm.at[_tile_rows(src), :], dst_of(q, u), sem).start(
                priority=u % 2)
        return carry

    lax.fori_loop(0, n_rows // GATHER_UNROLL, body, 0)


def _to_tiled_rows(o_ref, y):
    rows, width = y.shape
    for r in range(rows // ROW_TILE):
        for j in range(width // LANES):
            o_ref[pl.ds(r * ROW_TILE * ROW_TILE + j, ROW_TILE, stride=ROW_TILE), :] = (
                y[r * ROW_TILE:(r + 1) * ROW_TILE, j * LANES:(j + 1) * LANES])


def _from_tiled_rows(load, rows, width):
    cols = []
    for j in range(width // LANES):
        pieces = [load(r * ROW_TILE * ROW_TILE + j, ROW_TILE, ROW_TILE) for r in range(rows // ROW_TILE)]
        cols.append(jnp.concatenate(pieces, axis=0))
    return jnp.concatenate(cols, axis=1)


def _dest_kernel(idx_ref, rank_ref, start_ref, o_ref):
    idx = idx_ref[...]
    rank = rank_ref[...]
    start = start_ref[...]
    tm = idx.shape[0]
    lane = lax.broadcasted_iota(jnp.int32, (tm, start.shape[1]), 1)
    out_lane = lax.broadcasted_iota(jnp.int32, (tm, LANES), 1)
    out = jnp.zeros((tm, LANES), jnp.int32)
    for k in range(TOP_K):
        base = jnp.sum(jnp.where(lane == idx[:, k:k + 1], start, 0.0), axis=-1, keepdims=True)
        out = jnp.where(out_lane == k, base.astype(jnp.int32) + rank[:, k:k + 1], out)
    o_ref[...] = out


def _dest(idx_pad, rank_pad, pad_start):
    t = idx_pad.shape[0]
    tm = DEST_TILE
    row = pl.BlockSpec((tm, LANES), lambda i: (i, 0))
    return pl.pallas_call(
        _dest_kernel,
        grid=(t // tm,),
        in_specs=[row, row, pl.BlockSpec((1, N_EXPERTS), lambda i: (0, 0))],
        out_specs=row,
        out_shape=jax.ShapeDtypeStruct((t, LANES), jnp.int32),
        compiler_params=_cparams(("arbitrary",)),
        name="dest",
    )(idx_pad, rank_pad, pad_start.astype(F32).reshape(1, N_EXPERTS))


def _scatter_kernel(b0_ref, nb_ref, nu_ref, dest_ref, hp_ref, xs_hbm, zbuf, sem, zsem):
    ts = hp_ref.shape[0]
    bm = zbuf.shape[0]
    n_blocks = xs_hbm.shape[0] // bm

    @pl.when(pl.program_id(0) == 0)
    def _():
        zbuf[...] = jnp.zeros_like(zbuf)
        n_used = nu_ref[0]

        def zero_block(g):
            return pltpu.make_async_copy(zbuf, xs_hbm.at[pl.ds(pl.multiple_of(g * bm, bm), bm), :], zsem)

        def tails(action):
            def step(e, carry):
                @pl.when(nb_ref[e] > 0)
                def _():
                    action(zero_block(b0_ref[e] + nb_ref[e] - 1))
                return carry
            lax.fori_loop(0, N_EXPERTS, step, 0)

        def unused(action):
            def step(g, carry):
                action(zero_block(g))
                return carry
            lax.fori_loop(n_used, n_blocks, step, 0)

        tails(lambda c: c.start())
        unused(lambda c: c.start())
        tails(lambda c: c.wait())
        unused(lambda c: c.wait())

    def body(t, carry):
        for k in range(TOP_K):
            dst = dest_ref[0, 0, t * TOP_K + k]
            pltpu.make_async_copy(hp_ref.at[pl.ds(t, 1), :], xs_hbm.at[pl.ds(dst, 1), :], sem).start(
                priority=k % 2)
        return carry

    lax.fori_loop(0, ts, body, 0)
    for k in range(TOP_K):
        pltpu.make_async_copy(hp_ref, xs_hbm.at[pl.ds(0, ts), :], sem).wait()


def _scatter(dest, hp, first_block, n_block, n_used, n_rows):
    t, half = hp.shape
    ts = SCATTER_TILE
    n_steps = t // ts
    dest3 = dest.reshape(n_steps, 1, ts * TOP_K)
    grid_spec = pltpu.PrefetchScalarGridSpec(
        num_scalar_prefetch=3,
        grid=(n_steps,),
        in_specs=[pl.BlockSpec((1, 1, ts * TOP_K), lambda i, b0, nb, nu: (i, 0, 0),
                               memory_space=pltpu.SMEM),
                  pl.BlockSpec((ts, half), lambda i, b0, nb, nu: (i, 0))],
        out_specs=pl.BlockSpec(memory_space=pl.ANY),
        scratch_shapes=[pltpu.VMEM((EXPERT_ROWS, half), jnp.uint32),
                        pltpu.SemaphoreType.DMA(()), pltpu.SemaphoreType.DMA(())],
    )
    return pl.pallas_call(
        _scatter_kernel,
        grid_spec=grid_spec,
        out_shape=jax.ShapeDtypeStruct((n_rows, half), jnp.uint32),
        compiler_params=_cparams(("arbitrary",)),
        name="scatter",
    )(first_block, n_block, n_used, dest3, hp)


def _expert_kernel(b0_ref, nb_ref, nu_ref, xs_hbm, w13_ref, w2_ref, y_hbm,
                   xbuf, ybuf, w13b, w2b, sem_in, sem_out):
    e = pl.program_id(0)
    n_used = nu_ref[0]
    b0 = b0_ref[e]
    nb = nb_ref[e]
    n_in, bm = xbuf.shape[:2]
    n_out, y_rows = ybuf.shape[:2]
    ahead = n_in - 1

    def in_copy(g, slot):
        return pltpu.make_async_copy(xs_hbm.at[pl.ds(pl.multiple_of(g * bm, bm), bm), :],
                                     xbuf.at[slot], sem_in.at[slot])

    def out_copy(g, slot):
        return pltpu.make_async_copy(ybuf.at[slot],
                                     y_hbm.at[pl.ds(pl.multiple_of(g * y_rows, y_rows), y_rows), :],
                                     sem_out.at[slot])

    @pl.when(nb > 0)
    def _():
        w13b[...] = w13_ref[0].astype(BF16)
        w2b[...] = w2_ref[0].astype(BF16)

    @pl.when((nb > 0) & (b0 == 0))
    def _():
        for a in range(ahead):
            @pl.when(a < n_used)
            def _():
                in_copy(a, a).start()

    def block(g, carry):
        in_slot = g % n_in
        out_slot = g % n_out
        in_copy(g, in_slot).wait()

        @pl.when(g + ahead < n_used)
        def _():
            in_copy(g + ahead, (g + ahead) % n_in).start()

        @pl.when(g >= n_out)
        def _():
            out_copy(g - n_out, out_slot).wait()

        w = xbuf[in_slot]
        lo = pltpu.bitcast(w << 16, F32).astype(BF16)
        hi = pltpu.bitcast(w & jnp.uint32(0xFFFF0000), F32).astype(BF16)
        half = lo.shape[1]
        ag = _dot(lo, w13b[:half, :]) + _dot(hi, w13b[half:, :])
        hh = ag.shape[1] // 2
        act = (_silu(ag[:, :hh]) * ag[:, hh:]).astype(BF16)
        _to_tiled_rows(ybuf.at[out_slot], _dot(act, w2b[...]))
        out_copy(g, out_slot).start()
        return carry

    lax.fori_loop(b0, b0 + nb, block, 0)

    @pl.when(e == pl.num_programs(0) - 1)
    def _():
        for back in range(1, n_out + 1):
            @pl.when(n_used >= back)
            def _():
                out_copy(n_used - back, (n_used - back) % n_out).wait()

        n_blocks = y_hbm.shape[0] // y_rows
        ybuf[0] = jnp.zeros((y_rows, LANES), F32)

        def start_zero(g, carry):
            out_copy(g, 0).start()
            return carry

        def wait_zero(g, carry):
            out_copy(g, 0).wait()
            return carry

        lax.fori_loop(n_used, n_blocks, start_zero, 0)
        lax.fori_loop(n_used, n_blocks, wait_zero, 0)


def _experts(xs, first_block, n_block, n_used, w13, w2):
    n_rows, half = xs.shape
    ne, d, h2 = w13.shape
    bm = EXPERT_ROWS
    y_rows = bm * d // LANES
    grid_spec = pltpu.PrefetchScalarGridSpec(
        num_scalar_prefetch=3,
        grid=(ne,),
        in_specs=[
            pl.BlockSpec(memory_space=pl.ANY),
            pl.BlockSpec((1, d, h2), lambda e, b0, nb, nu: (e, 0, 0)),
            pl.BlockSpec((1, h2 // 2, d), lambda e, b0, nb, nu: (e, 0, 0)),
        ],
        out_specs=pl.BlockSpec(memory_space=pl.ANY),
        scratch_shapes=[pltpu.VMEM((EXPERT_IN_SLOTS, bm, half), jnp.uint32),
                        pltpu.VMEM((EXPERT_OUT_SLOTS, y_rows, LANES), F32),
                        pltpu.VMEM((d, h2), BF16),
                        pltpu.VMEM((h2 // 2, d), BF16),
                        pltpu.SemaphoreType.DMA((EXPERT_IN_SLOTS,)),
                        pltpu.SemaphoreType.DMA((EXPERT_OUT_SLOTS,))],
    )
    return pl.pallas_call(
        _expert_kernel,
        grid_spec=grid_spec,
        out_shape=jax.ShapeDtypeStruct((n_rows * d // LANES, LANES), F32),
        compiler_params=_cparams(("arbitrary",)),
        name="experts",
    )(first_block, n_block, n_used, xs, w13, w2)


def _final_kernel(dc_ref, dn_ref, y_hbm, w_ref, x_ref, mod_ref, gpre_ref, gpost_ref, w13s_ref,
                  w2s_ref, o_ref, buf, sem):
    i = pl.program_id(0)
    slot = i % 2
    tm = x_ref.shape[0]

    def issue(d_ref, s):
        _gather_rows(d_ref, tm * TOP_K, y_hbm,
                     lambda q, u: buf.at[s, u, _tile_rows(q), :], sem.at[s])

    @pl.when(i == 0)
    def _():
        issue(dc_ref, 0)

    @pl.when(i + 1 < pl.num_programs(0))
    def _():
        issue(dn_ref, 1 - slot)

    for k in range(TOP_K):
        pltpu.make_async_copy(y_hbm.at[pl.ds(0, tm * ROW_TILE), :], buf.at[slot, k],
                              sem.at[slot]).wait()

    w = w_ref[...]
    x1 = x_ref[...]
    routed = None
    for k in range(TOP_K):
        yk = _from_tiled_rows(lambda a, n, st: buf[slot, k, pl.ds(a, n, stride=st), :], tm, x1.shape[1])
        routed = yk * w[:, k:k + 1] if routed is None else routed + yk * w[:, k:k + 1]

    m = mod_ref[0]
    h = (_rms(x1, gpre_ref[...]) * (1.0 + m[4:5]) + m[3:4]).astype(BF16)
    ag = _dot(h, w13s_ref[...])
    hh = ag.shape[1] // 2
    act = (_silu(ag[:, :hh]) * ag[:, hh:]).astype(BF16)
    ffn = routed + _dot(act, w2s_ref[...])
    o_ref[...] = x1 + m[5:6] * _rms(ffn, gpost_ref[...])


def _final(dest, y_sorted, top_w, x1, mod, gpre, gpost, w13s, w2s, seq):
    t, d = x1.shape
    tm = COMBINE_TILE
    per_b = seq // tm
    n_steps = t // tm
    dest3 = dest.reshape(n_steps, 1, tm * TOP_K)
    full = lambda a: pl.BlockSpec(a.shape, lambda i: (0,) * a.ndim)
    row = lambda n: pl.BlockSpec((tm, n), lambda i: (i, 0))
    return pl.pallas_call(
        _final_kernel,
        grid=(n_steps,),
        in_specs=[
            pl.BlockSpec((1, 1, tm * TOP_K), lambda i: (i, 0, 0), memory_space=pltpu.SMEM),
            pl.BlockSpec((1, 1, tm * TOP_K), lambda i: (jnp.minimum(i + 1, n_steps - 1), 0, 0),
                         memory_space=pltpu.SMEM),
            pl.BlockSpec(memory_space=pl.ANY),
            row(LANES), row(d), pl.BlockSpec((1, 6, d), lambda i: (i // per_b, 0, 0)),
            full(gpre), full(gpost), full(w13s), full(w2s)],
        out_specs=row(d),
        out_shape=jax.ShapeDtypeStruct((t, d), F32),
        scratch_shapes=[pltpu.VMEM((2, TOP_K, tm * ROW_TILE, LANES), F32),
                        pltpu.SemaphoreType.DMA((2,))],
        compiler_params=_cparams(("arbitrary",)),
        name="combine_final",
    )(dest3, dest3, y_sorted, top_w, x1, mod, gpre, gpost, w13s, w2s)


def _pad_heads(v):
    lead = v.shape[:-1]
    v = v.reshape(lead + (2, SSD_GROUPS, HEADS_PER_GROUP))
    v = jnp.pad(v, [(0, 0)] * (len(lead) + 2) + [(0, HEADS_PADDED - HEADS_PER_GROUP)])
    return v.reshape(lead + (2 * SSD_GROUPS * HEADS_PADDED,))


def _dispatch_plan(counts, n_tokens):
    bm = EXPERT_ROWS
    counts = counts.reshape(N_EXPERTS).astype(jnp.int32)
    padded = (counts + bm - 1) // bm * bm
    pad_end = jnp.cumsum(padded)
    pad_start = pad_end - padded
    n_rows = -(-n_tokens * TOP_K // bm) * bm + N_EXPERTS * bm
    first_block = (pad_start // bm).astype(jnp.int32)
    n_block = (padded // bm).astype(jnp.int32)
    n_used = (pad_end[-1] // bm).astype(jnp.int32).reshape(1)
    return pad_start, first_block, n_block, n_used, n_rows


def _layer(x, c, w_ada, b_ada, pre_norm_mix, post_norm_mix, pre_norm_ffn, post_norm_ffn, w_in,
           conv_w, conv_b, dt_bias_fwd, dt_bias_bwd, a_log_fwd, a_log_bwd, d_skip, ssd_norm,
           w_branch_ssd, w_branch_fourier, w_out, w_router, router_bias, w13_experts, w2_experts,
           w13_shared, w2_shared):
    bsz, seq, d = x.shape
    t = bsz * seq
    x2 = x.reshape(t, d)
    row = lambda v: v.reshape(1, -1).astype(F32)

    mod = _ada(c, w_ada, b_ada)

    i1 = SSD_INNER
    i2 = i1 + XBC_WIDTH
    i3 = i2 + 2 * SSD_HEADS
    i4 = i3 + FOURIER_WIDTH
    n_dt = 2 * SSD_GROUPS * HEADS_PADDED
    w_dtp = _pad_heads(w_in[:, i2:i3])
    wdtc = jnp.pad(w_dtp, ((0, 0), (0, LANES - n_dt))).astype(BF16)
    wdtT = w_dtp.T.astype(BF16)
    bias_p = _pad_heads(jnp.concatenate([dt_bias_fwd, dt_bias_bwd]).astype(F32))
    a_p = _pad_heads(-jnp.exp(jnp.concatenate([a_log_fwd, a_log_bwd]).astype(F32)))
    pad_row = lambda v: jnp.pad(v, (0, LANES - n_dt)).reshape(1, LANES)
    z, xbc, acsc, dtT, acsT, uf, gates = _inproj(
        x2, mod, row(pre_norm_mix), w_in[:, :i1].astype(BF16), w_in[:, i1:i2].astype(BF16),
        wdtc, wdtT, w_in[:, i3:i4].astype(BF16), w_in[:, i4:].astype(BF16),
        pad_row(bias_p), pad_row(a_p), bias_p.reshape(n_dt, 1), a_p.reshape(n_dt, 1), seq)

    xbc3 = _conv(xbc.reshape(bsz, seq, XBC_WIDTH), conv_w, conv_b)

    yf, yb = _ssd(xbc3, dtT, acsT, acsc, row(jnp.repeat(d_skip, SSD_HEAD_DIM)))

    fm = _fourier(uf.reshape(bsz, seq, FOURIER_WIDTH))

    x1 = _mix(yf.reshape(t, SSD_INNER), yb.reshape(t, SSD_INNER), z,
              fm.reshape(t, FOURIER_WIDTH), gates, x2, mod,
              row(ssd_norm), w_branch_ssd.astype(BF16),
              w_branch_fourier.astype(BF16), w_out.astype(BF16), row(post_norm_mix), seq)

    hp, idx_pad, w_pad, rank_pad, counts = _route(x1, mod, row(pre_norm_ffn),
                                                  w_router.astype(F32), router_bias, seq)
    pad_start, first_block, n_block, n_used, n_rows = _dispatch_plan(counts, t)
    dest = _dest(idx_pad, rank_pad, pad_start)[:, :TOP_K]
    xs = _scatter(dest, hp, first_block, n_block, n_used, n_rows)
    y_sorted = _experts(xs, first_block, n_block, n_used, w13_experts, w2_experts)
    out = _final(dest, y_sorted, w_pad, x1, mod, row(pre_norm_ffn), row(post_norm_ffn),
                 w13_shared.astype(BF16), w2_shared.astype(BF16), seq)
    return out.reshape(bsz, seq, d)


def kernel(x, c, w_ada, b_ada, pre_norm_mix, post_norm_mix, pre_norm_ffn, post_norm_ffn, w_in,
           conv_w, conv_b, dt_bias_fwd, dt_bias_bwd, a_log_fwd, a_log_bwd, d_skip, ssd_norm,
           w_branch_ssd, w_branch_fourier, w_out, w_router, router_bias, w13_experts, w2_experts,
           w13_shared, w2_shared):
    for layer in range(w_ada.shape[0]):
        x = _layer(x, c, w_ada[layer], b_ada[layer], pre_norm_mix[layer], post_norm_mix[layer],
                   pre_norm_ffn[layer], post_norm_ffn[layer], w_in[layer], conv_w[layer],
                   conv_b[layer], dt_bias_fwd[layer], dt_bias_bwd[layer], a_log_fwd[layer],
                   a_log_bwd[layer], d_skip[layer], ssd_norm[layer], w_branch_ssd[layer],
                   w_branch_fourier[layer], w_out[layer], w_router[layer], router_bias[layer],
                   w13_experts[layer], w2_experts[layer], w13_shared[layer], w2_shared[layer])
    return x
```

```python
import functools
import math

import numpy as np
import jax
import jax.numpy as jnp
from jax import lax
from jax.experimental import pallas as pl
from jax.experimental.pallas import tpu as pltpu

F32 = jnp.float32
BF16 = jnp.bfloat16
HIGHEST = lax.Precision.HIGHEST

D_MODEL = 1024
SSD_HEADS = 24
SSD_HEAD_DIM = 64
SSD_INNER = SSD_HEADS * SSD_HEAD_DIM
SSD_GROUPS = 4
HEADS_PER_GROUP = SSD_HEADS // SSD_GROUPS
HEADS_PADDED = 8
SSD_STATE = 128
SSD_CONV = 5
SSD_CHUNK = 128
XBC_WIDTH = SSD_INNER + 2 * SSD_GROUPS * SSD_STATE
GROUP_X = HEADS_PER_GROUP * SSD_HEAD_DIM
FOURIER_WIDTH = 512
FOURIER_GROUP_DIM = 128
N_EXPERTS = 256
TOP_K = 8
N_EXPERT_GROUPS = 8
TOPK_GROUPS = 4
EXPERT_HIDDEN = 256
SHARED_HIDDEN = 256
ROUTED_SCALE = 2.5
RMS_EPS = 1e-6

LANES = 128
VMEM_LIMIT = 56 * 1024 * 1024
TOKEN_TILE = 256
INPROJ_TILE = 512
INPROJ_SUBTILE = 256
EXPERT_ROWS = 256
COMBINE_TILE = 128
SCATTER_TILE = 512
DEST_TILE = 1024
EXPERT_IN_SLOTS = 4
EXPERT_OUT_SLOTS = 3
CONV_ROWS = 256


def _cparams(sem):
    return pltpu.CompilerParams(dimension_semantics=sem, vmem_limit_bytes=VMEM_LIMIT)


def _dot(a, b, precision=None):
    return jnp.dot(a, b, preferred_element_type=F32, precision=precision)


def _dot_nt(a, b, precision=None):
    return lax.dot_general(a, b, (((1,), (1,)), ((), ())), preferred_element_type=F32,
                           precision=precision)


def _dot_tn(a, b):
    return lax.dot_general(a, b, (((0,), (0,)), ((), ())), preferred_element_type=F32)


def _sigmoid(x):
    return 1.0 / (1.0 + jnp.exp(-x))


def _silu(x):
    return x * _sigmoid(x)


def _softplus(x):
    return jnp.maximum(x, 0.0) + jnp.log1p(jnp.exp(-jnp.abs(x)))


def _rms(x, g):
    return x * lax.rsqrt(jnp.mean(x * x, axis=-1, keepdims=True) + RMS_EPS) * g


def _ada_kernel(c_ref, w_ref, b_ref, o_ref):
    o_ref[...] = _dot(_silu(c_ref[...]), w_ref[...], HIGHEST) + b_ref[...]


def _ada(c, w_ada, b_ada):
    bsz, d = c.shape
    rows = 8
    cp = jnp.zeros((rows, d), F32).at[:bsz].set(c)
    n = w_ada.shape[1]
    tn = 1536
    out = pl.pallas_call(
        _ada_kernel,
        grid=(n // tn,),
        in_specs=[pl.BlockSpec((rows, d), lambda j: (0, 0)),
                  pl.BlockSpec((d, tn), lambda j: (0, j)),
                  pl.BlockSpec((1, tn), lambda j: (0, j))],
        out_specs=pl.BlockSpec((rows, tn), lambda j: (0, j)),
        out_shape=jax.ShapeDtypeStruct((rows, n), F32),
        compiler_params=_cparams(("arbitrary",)),
        name="adaln",
    )(cp, w_ada, b_ada.reshape(1, n))
    return out[:bsz].reshape(bsz, 6, d)


def _split3(a):
    a1 = a.astype(BF16)
    r1 = a - a1.astype(F32)
    a2 = r1.astype(BF16)
    a3 = (r1 - a2.astype(F32)).astype(BF16)
    return a1, a2, a3


def _inproj_kernel(x_ref, mod_ref, g_ref, wz_ref, wxbc_ref, wdtc_ref, wdtT_ref, wuf_ref, wg_ref,
                   bias_row_ref, a_row_ref, bias_col_ref, a_col_ref,
                   z_ref, xbc_ref, acsc_ref, dtT_ref, acsT_ref, uf_ref, gates_ref):
    m = mod_ref[0]
    tm = INPROJ_SUBTILE
    n_fwd = SSD_GROUPS * HEADS_PADDED
    ii = lax.broadcasted_iota(jnp.int32, (tm, tm), 0)
    jj = lax.broadcasted_iota(jnp.int32, (tm, tm), 1)
    same = (ii // SSD_CHUNK) == (jj // SSD_CHUNK)
    tri_f = (same & (jj <= ii)).astype(BF16)
    tri_b = (same & (jj >= ii)).astype(BF16)

    for s in range(x_ref.shape[0] // tm):
        rows = slice(s * tm, (s + 1) * tm)
        h = _rms(x_ref[rows, :], g_ref[...]) * (1.0 + m[1:2]) + m[0:1]
        hb = h.astype(BF16)
        z_ref[rows, :] = _dot(hb, wz_ref[...]).astype(z_ref.dtype)
        xbc_ref[rows, :] = _dot(hb, wxbc_ref[...]).astype(xbc_ref.dtype)
        uf_ref[rows, :] = _dot(hb, wuf_ref[...]).astype(uf_ref.dtype)
        gates_ref[rows, :] = _dot(hb, wg_ref[...]).astype(gates_ref.dtype)

        dt_c = _softplus(_dot(hb, wdtc_ref[...]) + bias_row_ref[...])
        pieces = _split3(dt_c * a_row_ref[...])
        acs_f = sum(_dot(tri_f, p) for p in pieces)
        acs_b = sum(_dot(tri_b, p) for p in pieces)
        lane = lax.broadcasted_iota(jnp.int32, acs_f.shape, 1)
        acsc_ref[rows, :] = jnp.where(lane < n_fwd, acs_f, acs_b)

        dt_t = _softplus(_dot_nt(wdtT_ref[...], hb) + bias_col_ref[...])
        pieces = _split3(dt_t * a_col_ref[...])
        acs_f = sum(_dot_nt(p, tri_f) for p in pieces)
        acs_b = sum(_dot_nt(p, tri_b) for p in pieces)
        sub = lax.broadcasted_iota(jnp.int32, acs_f.shape, 0)
        dtT_ref[:, rows] = dt_t
        acsT_ref[:, rows] = jnp.where(sub < n_fwd, acs_f, acs_b)


def _inproj(x2, mod, g, wz, wxbc, wdtc, wdtT, wuf, wg, bias_row, a_row, bias_col, a_col, seq):
    t, d = x2.shape
    tm = INPROJ_TILE
    per_b = seq // tm
    full = lambda a: pl.BlockSpec(a.shape, lambda i: (0,) * a.ndim)
    row = lambda n: pl.BlockSpec((tm, n), lambda i: (i, 0))
    nd = wdtT.shape[0]
    colspec = pl.BlockSpec((nd, tm), lambda i: (0, i))
    return pl.pallas_call(
        _inproj_kernel,
        grid=(t // tm,),
        in_specs=[row(d), pl.BlockSpec((1, 6, d), lambda i: (i // per_b, 0, 0)), full(g),
                  full(wz), full(wxbc), full(wdtc), full(wdtT), full(wuf), full(wg),
                  full(bias_row), full(a_row), full(bias_col), full(a_col)],
        out_specs=[row(wz.shape[1]), row(wxbc.shape[1]), row(LANES), colspec, colspec,
                   row(wuf.shape[1]), row(wg.shape[1])],
        out_shape=[jax.ShapeDtypeStruct((t, wz.shape[1]), BF16),
                   jax.ShapeDtypeStruct((t, wxbc.shape[1]), BF16),
                   jax.ShapeDtypeStruct((t, LANES), F32),
                   jax.ShapeDtypeStruct((nd, t), F32),
                   jax.ShapeDtypeStruct((nd, t), F32),
                   jax.ShapeDtypeStruct((t, wuf.shape[1]), BF16),
                   jax.ShapeDtypeStruct((t, wg.shape[1]), BF16)],
        compiler_params=_cparams(("arbitrary",)),
        name="inproj",
    )(x2, mod, g, wz, wxbc, wdtc, wdtT, wuf, wg, bias_row, a_row, bias_col, a_col)


def _conv_kernel(u_ref, w_ref, b_ref, o_ref, pad_ref):
    s = u_ref.shape[1]
    halo = 8
    pad_ref[0:halo, :] = jnp.zeros((halo, LANES), F32)
    pad_ref[halo + s:2 * halo + s, :] = jnp.zeros((halo, LANES), F32)
    pad_ref[halo:halo + s, :] = u_ref[0].astype(F32)
    w = w_ref[...]
    b = b_ref[...]
    half = (SSD_CONV - 1) // 2
    for r in range(s // CONV_ROWS):
        base = r * CONV_ROWS
        acc = b
        for k in range(SSD_CONV):
            lo = base + halo + k - half
            acc = acc + w[k:k + 1, :] * pad_ref[lo:lo + CONV_ROWS, :]
        o_ref[0, base:base + CONV_ROWS, :] = _silu(acc).astype(o_ref.dtype)


def _conv(xbc3, conv_w, conv_b):
    bsz, s, c = xbc3.shape
    return pl.pallas_call(
        _conv_kernel,
        grid=(bsz, c // LANES),
        in_specs=[pl.BlockSpec((1, s, LANES), lambda b, j: (b, 0, j)),
                  pl.BlockSpec((SSD_CONV, LANES), lambda b, j: (0, j)),
                  pl.BlockSpec((1, LANES), lambda b, j: (0, j))],
        out_specs=pl.BlockSpec((1, s, LANES), lambda b, j: (b, 0, j)),
        out_shape=jax.ShapeDtypeStruct((bsz, s, c), BF16),
        scratch_shapes=[pltpu.VMEM((s + 16, LANES), F32)],
        compiler_params=_cparams(("arbitrary", "arbitrary")),
        name="conv",
    )(xbc3, conv_w, conv_b.reshape(1, c))


HEAD_PAIRS = HEADS_PER_GROUP // 2
SSD_GROUPS_PER_STEP = 4


def _ssd_direction(x, bm, cm, dt_r, acs_r, acsc_all, lane_off, s_ref, reverse):
    L, N = bm.shape
    assert L == N == LANES
    ii = lax.broadcasted_iota(jnp.int32, (L, L), 0)
    jj = lax.broadcasted_iota(jnp.int32, (L, L), 1)
    mask = (jj >= ii) if reverse else (jj <= ii)
    lo_half = jj < SSD_HEAD_DIM
    shift = jnp.where(lane_off == 0, 0, LANES - lane_off)
    acs_c = pltpu.roll(acsc_all, shift, 1)
    last = 0 if reverse else L - 1
    tot_r = acs_r[:, last:last + 1]
    w_r = jnp.exp(tot_r - acs_r) * dt_r
    etot = jnp.broadcast_to(jnp.exp(tot_r), (HEADS_PADDED, LANES))

    cbt = _dot_nt(cm.astype(BF16), bm.astype(BF16))
    bt = bm.astype(F32).T
    cm = cm.astype(F32)
    ys = []
    for q in range(HEAD_PAIRS):
        xq = x[:, q * LANES:(q + 1) * LANES]
        sq = s_ref[q]
        x_a = jnp.where(lo_half, xq, jnp.zeros_like(xq)).astype(BF16)
        x_b = jnp.where(lo_half, jnp.zeros_like(xq), xq).astype(BF16)
        s_a = jnp.where(lo_half, sq, 0.0).astype(BF16)
        s_b = jnp.where(lo_half, 0.0, sq).astype(BF16)
        m_parts, c_parts, b_parts = [], [], []
        for h in (2 * q, 2 * q + 1):
            col = jnp.broadcast_to(acs_c[:, h:h + 1], (L, L))
            decay = jnp.exp(jnp.where(mask, col - acs_r[h:h + 1, :], -jnp.inf))
            m_parts.append((cbt * decay * dt_r[h:h + 1, :]).astype(BF16))
            c_parts.append((cm * jnp.exp(col)).astype(BF16))
            b_parts.append((bt * w_r[h:h + 1, :]).astype(BF16))
        x_diag = jnp.concatenate([x_a, x_b], axis=0)
        lhs = jnp.concatenate(m_parts + c_parts, axis=1)
        rhs = jnp.concatenate([x_diag, s_a, s_b], axis=0)
        ys.append(_dot(lhs, rhs))
        dec = jnp.where(lo_half[0:1], etot[2 * q:2 * q + 1], etot[2 * q + 1:2 * q + 2])
        s_ref[q] = sq * dec + _dot(jnp.concatenate(b_parts, axis=1), x_diag)
    return jnp.concatenate(ys, axis=1)


def _ssd_kernel(xf_ref, bf_ref, cf_ref, dtTf_ref, acsTf_ref, acscf_ref,
                xb_ref, bb_ref, cb_ref, dtTb_ref, acsTb_ref, acscb_ref, dskip_ref,
                yf_ref, yb_ref, sf_ref, sb_ref):
    gp = pl.program_id(1)

    @pl.when(pl.program_id(2) == 0)
    def _():
        sf_ref[...] = jnp.zeros_like(sf_ref)
        sb_ref[...] = jnp.zeros_like(sb_ref)

    for i in range(SSD_GROUPS_PER_STEP):
        g = gp * SSD_GROUPS_PER_STEP + i
        xs = slice(i * GROUP_X, (i + 1) * GROUP_X)
        ns = slice(i * SSD_STATE, (i + 1) * SSD_STATE)
        hs = slice(i * HEADS_PADDED, (i + 1) * HEADS_PADDED)
        y_fwd = _ssd_direction(
            xf_ref[0, :, xs], bf_ref[0, :, ns], cf_ref[0, :, ns], dtTf_ref[hs, :], acsTf_ref[hs, :],
            acscf_ref[...], g * HEADS_PADDED, sf_ref.at[i], False)
        y_fwd = y_fwd + dskip_ref[:, xs] * xf_ref[0, :, xs].astype(F32)
        yf_ref[0, :, xs] = y_fwd.astype(yf_ref.dtype)
        yb_ref[0, :, xs] = _ssd_direction(
            xb_ref[0, :, xs], bb_ref[0, :, ns], cb_ref[0, :, ns], dtTb_ref[hs, :], acsTb_ref[hs, :],
            acscb_ref[...], (SSD_GROUPS + g) * HEADS_PADDED, sb_ref.at[i], True).astype(yb_ref.dtype)


def _ssd(xbc3, dtT, acsT, acsc, dskip):
    bsz, s, _ = xbc3.shape
    L = SSD_CHUNK
    nc = s // L
    P = SSD_GROUPS_PER_STEP
    GP = SSD_GROUPS // P
    nb = SSD_INNER // (P * SSD_STATE)
    ncb = nb + GP
    fwd = lambda c: c
    bwd = lambda c: nc - 1 - c

    def specs(cidx, dirn):
        rowspec = pl.BlockSpec((P * HEADS_PADDED, L),
                               lambda b, g, c: (dirn * GP + g, b * nc + cidx(c)))
        return [
            pl.BlockSpec((1, L, P * GROUP_X), lambda b, g, c: (b, cidx(c), g)),
            pl.BlockSpec((1, L, P * SSD_STATE), lambda b, g, c: (b, cidx(c), nb + g)),
            pl.BlockSpec((1, L, P * SSD_STATE), lambda b, g, c: (b, cidx(c), ncb + g)),
            rowspec, rowspec,
            pl.BlockSpec((L, LANES), lambda b, g, c: (b * nc + cidx(c), 0)),
        ]

    out_specs = [pl.BlockSpec((1, L, P * GROUP_X), lambda b, g, c: (b, c, g)),
                 pl.BlockSpec((1, L, P * GROUP_X), lambda b, g, c: (b, nc - 1 - c, g))]
    return pl.pallas_call(
        _ssd_kernel,
        grid=(bsz, GP, nc),
        in_specs=specs(fwd, 0) + specs(bwd, 1)
        + [pl.BlockSpec((1, P * GROUP_X), lambda b, g, c: (0, g))],
        out_specs=out_specs,
        out_shape=[jax.ShapeDtypeStruct((bsz, s, SSD_INNER), BF16)] * 2,
        scratch_shapes=[pltpu.VMEM((P, HEAD_PAIRS, SSD_STATE, LANES), F32)] * 2,
        compiler_params=_cparams(("arbitrary", "arbitrary", "arbitrary")),
        name="ssd",
    )(xbc3, xbc3, xbc3, dtT, acsT, acsc, xbc3, xbc3, xbc3, dtT, acsT, acsc, dskip)


DFT_RADIX = 8
DFT_ROWS = 256


def _dft_tables(seq):
    m_len = seq // DFT_RADIX
    mm = np.arange(m_len)
    a0 = 2.0 * np.pi * np.outer(mm, mm) / m_len
    f0 = np.concatenate([np.cos(a0), -np.sin(a0)], axis=0)
    ang = -2.0 * np.pi * np.outer(np.arange(DFT_RADIX), mm) / seq
    tw = np.zeros((DFT_RADIX, m_len, LANES))
    tw[:, :, 0] = np.cos(ang)
    tw[:, :, 1] = np.sin(ang)
    ch = np.arange(FOURIER_GROUP_DIM)
    ac = 2.0 * np.pi * np.outer(ch, ch) / FOURIER_GROUP_DIM
    scale = 1.0 / math.sqrt(seq * FOURIER_GROUP_DIM)
    fc = np.concatenate([np.cos(ac), np.sin(ac)], axis=0) * scale
    return (jnp.asarray(f0, BF16), jnp.asarray(tw, F32), jnp.asarray(fc, BF16))


def _fourier_kernel(u_ref, f0_ref, tw_ref, fc_ref, o_ref, x32, tbuf):
    m_len = f0_ref.shape[1]
    x32[...] = u_ref[0].astype(F32)

    def sub_dft(r, carry):
        xr = x32[pl.ds(r, m_len, stride=DFT_RADIX), :].astype(BF16)
        tbuf[r] = _dot(f0_ref[...], xr)
        return carry

    lax.fori_loop(0, DFT_RADIX, sub_dft, 0)
    rows = min(DFT_ROWS, m_len)

    def combine(c, carry):
        sl = pl.ds(pl.multiple_of(c * rows, rows), rows)
        si = pl.ds(pl.multiple_of(m_len + c * rows, rows), rows)
        for r in range(1, DFT_RADIX):
            tre = tbuf[r, sl, :]
            tim = tbuf[r, si, :]
            tw = tw_ref[r, sl, :]
            gre = jnp.broadcast_to(tw[:, 0:1], (rows, LANES))
            gim = jnp.broadcast_to(tw[:, 1:2], (rows, LANES))
            tbuf[r, sl, :] = gre * tre - gim * tim
            tbuf[r, si, :] = gre * tim + gim * tre
        for kr in range(DFT_RADIX):
            zr = tbuf[0, sl, :]
            zi = tbuf[0, si, :]
            for r in range(1, DFT_RADIX):
                tre = tbuf[r, sl, :]
                tim = tbuf[r, si, :]
                quarter, odd = divmod((r * kr) % DFT_RADIX, 2)
                for _ in range(quarter):
                    tre, tim = tim, -tre
                if odd:
                    tre, tim = (tre + tim) * math.sqrt(0.5), (tim - tre) * math.sqrt(0.5)
                zr = zr + tre
                zi = zi + tim
            zz = jnp.concatenate([zr, zi], axis=1).astype(BF16)
            out_rows = pl.ds(pl.multiple_of(kr * m_len + c * rows, rows), rows)
            o_ref[0, out_rows, :] = _dot(zz, fc_ref[...])
        return carry

    lax.fori_loop(0, m_len // rows, combine, 0)


def _fourier(uf3):
    bsz, s, w = uf3.shape
    f0, tw, fc = _dft_tables(s)
    m_len = s // DFT_RADIX
    ng = w // FOURIER_GROUP_DIM
    return pl.pallas_call(
        _fourier_kernel,
        grid=(bsz, ng),
        in_specs=[pl.BlockSpec((1, s, LANES), lambda b, j: (b, 0, j)),
                  pl.BlockSpec(f0.shape, lambda b, j: (0, 0), pipeline_mode=pl.Buffered(1)),
                  pl.BlockSpec(tw.shape, lambda b, j: (0, 0, 0), pipeline_mode=pl.Buffered(1)),
                  pl.BlockSpec(fc.shape, lambda b, j: (0, 0), pipeline_mode=pl.Buffered(1))],
        out_specs=pl.BlockSpec((1, s, LANES), lambda b, j: (b, 0, j)),
        out_shape=jax.ShapeDtypeStruct((bsz, s, w), F32),
        scratch_shapes=[pltpu.VMEM((s, LANES), F32),
                        pltpu.VMEM((DFT_RADIX, 2 * m_len, LANES), F32)],
        compiler_params=_cparams(("arbitrary", "arbitrary")),
        name="fourier",
    )(uf3, f0, tw, fc)


def _mix_kernel(yf_ref, yb_ref, z_ref, fm_ref, gates_ref, x_ref, mod_ref,
                ssdn_ref, wbs_ref, wbf_ref, wout_ref, gpost_ref, o_ref):
    m = mod_ref[0]
    y = yf_ref[...].astype(F32) + yb_ref[...].astype(F32)
    v = y * _silu(z_ref[...].astype(F32))
    parts = []
    for g in range(SSD_GROUPS):
        vg = v[:, g * GROUP_X:(g + 1) * GROUP_X]
        parts.append(vg * lax.rsqrt(jnp.mean(vg * vg, axis=-1, keepdims=True) + RMS_EPS))
    vn = jnp.concatenate(parts, axis=1) * ssdn_ref[...]
    y_ssd = _dot(vn.astype(BF16), wbs_ref[...])
    y_fou = _dot(fm_ref[...].astype(BF16), wbf_ref[...])
    gt = _sigmoid(gates_ref[...].astype(F32))
    d = y_ssd.shape[1]
    mixed = gt[:, :d] * y_ssd + gt[:, d:] * y_fou
    mo = _dot(mixed.astype(BF16), wout_ref[...])
    o_ref[...] = x_ref[...] + m[2:3] * _rms(mo, gpost_ref[...])


def _mix(yf, yb, z, fm, gates, x2, mod, ssdn, wbs, wbf, wout, gpost, seq):
    t, d = x2.shape
    tm = TOKEN_TILE
    per_b = seq // tm
    full = lambda a: pl.BlockSpec(a.shape, lambda i: (0,) * a.ndim)
    row = lambda n: pl.BlockSpec((tm, n), lambda i: (i, 0))
    return pl.pallas_call(
        _mix_kernel,
        grid=(t // tm,),
        in_specs=[row(SSD_INNER), row(SSD_INNER), row(SSD_INNER),
                  row(FOURIER_WIDTH), row(2 * d), row(d),
                  pl.BlockSpec((1, 6, d), lambda i: (i // per_b, 0, 0)),
                  full(ssdn), full(wbs), full(wbf), full(wout), full(gpost)],
        out_specs=row(d),
        out_shape=jax.ShapeDtypeStruct((t, d), F32),
        compiler_params=_cparams(("arbitrary",)),
        name="mix",
    )(yf, yb, z, fm, gates, x2, mod, ssdn, wbs, wbf, wout, gpost)


def _route_kernel(x_ref, mod_ref, gpre_ref, wr_ref, rb_ref, hp_ref, idx_ref, w_ref, rank_ref,
                  counts_ref, run_ref):
    @pl.when(pl.program_id(0) == 0)
    def _():
        run_ref[...] = jnp.zeros_like(run_ref)

    m = mod_ref[0]
    h = _rms(x_ref[...], gpre_ref[...]) * (1.0 + m[4:5]) + m[3:4]
    tm, d = h.shape
    half = d // 2
    bits = pltpu.bitcast(h.astype(BF16).astype(F32), jnp.uint32)
    hp_ref[...] = (bits[:, :half] >> 16) | (bits[:, half:] & jnp.uint32(0xFFFF0000))

    h_hi = h.astype(BF16)
    h_lo = (h - h_hi.astype(F32)).astype(BF16)
    w_hi = wr_ref[0]
    logits = _dot(h_hi, w_hi) + (_dot(h_hi, wr_ref[1]) + _dot(h_lo, w_hi))
    scores = _sigmoid(logits)
    biased = scores + rb_ref[...]
    ne = scores.shape[1]
    per_group = ne // N_EXPERT_GROUPS
    lane = lax.broadcasted_iota(jnp.int32, (tm, ne), 1)
    lane_f = lane.astype(F32)
    grp = lane // per_group
    out_lane = lax.broadcasted_iota(jnp.int32, (tm, LANES), 1)
    neg = -jnp.inf

    def argmax_first(v):
        mx = jnp.max(v, axis=-1, keepdims=True)
        ix = jnp.min(jnp.where(v == mx, lane_f, float(ne)), axis=-1, keepdims=True)
        return mx, ix

    gs = []
    gmat = jnp.full((tm, LANES), neg, F32)
    for g in range(N_EXPERT_GROUPS):
        vg = jnp.where(grp == g, biased, neg)
        m1, i1 = argmax_first(vg)
        m2 = jnp.max(jnp.where(lane_f == i1, neg, vg), axis=-1, keepdims=True)
        gs.append(m1 + m2)
        gmat = jnp.where(out_lane == g, gs[g], gmat)
    ahead = jnp.zeros((tm, LANES), F32)
    for o in range(N_EXPERT_GROUPS):
        beats = (gs[o] > gmat) | ((gs[o] == gmat) & (out_lane > o))
        ahead = ahead + beats.astype(F32)
    sel = ((ahead < TOPK_GROUPS) & (out_lane < N_EXPERT_GROUPS)).astype(BF16)
    eg = lax.broadcasted_iota(jnp.int32, (LANES, ne), 0)
    ee = lax.broadcasted_iota(jnp.int32, (LANES, ne), 1)
    allowed = _dot(sel, (eg == ee // per_group).astype(BF16)) > 0.5
    masked = jnp.where(allowed, biased, neg)

    idx_out = jnp.zeros((tm, LANES), F32)
    w_out = jnp.zeros((tm, LANES), F32)
    w_sum = jnp.zeros((tm, 1), F32)
    hits = []
    for k in range(TOP_K):
        _, ik = argmax_first(masked)
        hit = lane_f == ik
        hits.append(hit)
        wk = jnp.sum(jnp.where(hit, scores, 0.0), axis=-1, keepdims=True)
        masked = jnp.where(hit, neg, masked)
        idx_out = jnp.where(out_lane == k, ik, idx_out)
        w_out = jnp.where(out_lane == k, wk, w_out)
        w_sum = w_sum + wk
    idx_ref[...] = idx_out.astype(jnp.int32)
    w_ref[...] = w_out / w_sum * ROUTED_SCALE

    chosen = functools.reduce(jnp.logical_or, hits).astype(F32)
    ti = lax.broadcasted_iota(jnp.int32, (tm, tm), 0)
    tj = lax.broadcasted_iota(jnp.int32, (tm, tm), 1)
    before = _dot((tj < ti).astype(BF16), chosen.astype(BF16)) + run_ref[...]
    rank_out = jnp.zeros((tm, LANES), jnp.int32)
    for k in range(TOP_K):
        rk = jnp.sum(jnp.where(hits[k], before, 0.0), axis=-1, keepdims=True)
        rank_out = jnp.where(out_lane == k, rk.astype(jnp.int32), rank_out)
    rank_ref[...] = rank_out
    run_ref[...] = run_ref[...] + jnp.sum(chosen, axis=0, keepdims=True)
    counts_ref[...] = run_ref[...]


def _route(x1, mod, gpre, w_router, router_bias, seq):
    t, d = x1.shape
    tm = TOKEN_TILE
    per_b = seq // tm
    full = lambda a: pl.BlockSpec(a.shape, lambda i: (0,) * a.ndim)
    row = lambda n: pl.BlockSpec((tm, n), lambda i: (i, 0))
    rb = router_bias.reshape(1, -1)
    w_hi = w_router.astype(BF16)
    w_router = jnp.stack([w_hi, (w_router - w_hi.astype(F32)).astype(BF16)])
    return pl.pallas_call(
        _route_kernel,
        grid=(t // tm,),
        in_specs=[row(d), pl.BlockSpec((1, 6, d), lambda i: (i // per_b, 0, 0)), full(gpre),
                  full(w_router), full(rb)],
        out_specs=[row(d // 2), row(LANES), row(LANES), row(LANES),
                   pl.BlockSpec((1, N_EXPERTS), lambda i: (0, 0))],
        out_shape=[jax.ShapeDtypeStruct((t, d // 2), jnp.uint32),
                   jax.ShapeDtypeStruct((t, LANES), jnp.int32),
                   jax.ShapeDtypeStruct((t, LANES), F32),
                   jax.ShapeDtypeStruct((t, LANES), jnp.int32),
                   jax.ShapeDtypeStruct((1, N_EXPERTS), F32)],
        scratch_shapes=[pltpu.VMEM((1, N_EXPERTS), F32)],
        compiler_params=_cparams(("arbitrary",)),
        name="route",
    )(x1, mod, gpre, w_router, rb)


GATHER_UNROLL = 8


ROW_TILE = 8


def _tile_rows(r):
    return pl.ds(pl.multiple_of(r * ROW_TILE, ROW_TILE), ROW_TILE)


def _gather_rows(idx_ref, n_rows, src_hbm, dst_of, sem):
    def body(q, carry):
        for u in range(GATHER_UNROLL):
            src = idx_ref[0, 0, q * GATHER_UNROLL + u]
            pltpu.make_async_copy(src_hbm.at[_tile_rows(src), :], dst_of(q, u), sem).start(
                priority=u % 2)
        return carry

    lax.fori_loop(0, n_rows // GATHER_UNROLL, body, 0)


def _to_tiled_rows(o_ref, y):
    rows, width = y.shape
    for r in range(rows // ROW_TILE):
        for j in range(width // LANES):
            o_ref[pl.ds(r * ROW_TILE * ROW_TILE + j, ROW_TILE, stride=ROW_TILE), :] = (
                y[r * ROW_TILE:(r + 1) * ROW_TILE, j * LANES:(j + 1) * LANES])


def _from_tiled_rows(load, rows, width):
    cols = []
    for j in range(width // LANES):
        pieces = [load(r * ROW_TILE * ROW_TILE + j, ROW_TILE, ROW_TILE) for r in range(rows // ROW_TILE)]
        cols.append(jnp.concatenate(pieces, axis=0))
    return jnp.concatenate(cols, axis=1)


def _dest_kernel(idx_ref, rank_ref, start_ref, o_ref):
    idx = idx_ref[...]
    rank = rank_ref[...]
    start = start_ref[...]
    tm = idx.shape[0]
    lane = lax.broadcasted_iota(jnp.int32, (tm, start.shape[1]), 1)
    out_lane = lax.broadcasted_iota(jnp.int32, (tm, LANES), 1)
    out = jnp.zeros((tm, LANES), jnp.int32)
    for k in range(TOP_K):
        base = jnp.sum(jnp.where(lane == idx[:, k:k + 1], start, 0.0), axis=-1, keepdims=True)
        out = jnp.where(out_lane == k, base.astype(jnp.int32) + rank[:, k:k + 1], out)
    o_ref[...] = out


def _dest(idx_pad, rank_pad, pad_start):
    t = idx_pad.shape[0]
    tm = DEST_TILE
    row = pl.BlockSpec((tm, LANES), lambda i: (i, 0))
    return pl.pallas_call(
        _dest_kernel,
        grid=(t // tm,),
        in_specs=[row, row, pl.BlockSpec((1, N_EXPERTS), lambda i: (0, 0))],
        out_specs=row,
        out_shape=jax.ShapeDtypeStruct((t, LANES), jnp.int32),
        compiler_params=_cparams(("arbitrary",)),
        name="dest",
    )(idx_pad, rank_pad, pad_start.astype(F32).reshape(1, N_EXPERTS))


def _scatter_kernel(b0_ref, nb_ref, nu_ref, dest_ref, hp_ref, xs_hbm, zbuf, sem, zsem):
    ts = hp_ref.shape[0]
    bm = zbuf.shape[0]
    n_blocks = xs_hbm.shape[0] // bm

    @pl.when(pl.program_id(0) == 0)
    def _():
        zbuf[...] = jnp.zeros_like(zbuf)
        n_used = nu_ref[0]

        def zero_block(g):
            return pltpu.make_async_copy(zbuf, xs_hbm.at[pl.ds(pl.multiple_of(g * bm, bm), bm), :], zsem)

        def tails(action):
            def step(e, carry):
                @pl.when(nb_ref[e] > 0)
                def _():
                    action(zero_block(b0_ref[e] + nb_ref[e] - 1))
                return carry
            lax.fori_loop(0, N_EXPERTS, step, 0)

        def unused(action):
            def step(g, carry):
                action(zero_block(g))
                return carry
            lax.fori_loop(n_used, n_blocks, step, 0)

        tails(lambda c: c.start())
        unused(lambda c: c.start())
        tails(lambda c: c.wait())
        unused(lambda c: c.wait())

    def body(t, carry):
        for k in range(TOP_K):
            dst = dest_ref[0, 0, t * TOP_K + k]
            pltpu.make_async_copy(hp_ref.at[pl.ds(t, 1), :], xs_hbm.at[pl.ds(dst, 1), :], sem).start(
                priority=k % 2)
        return carry

    lax.fori_loop(0, ts, body, 0)
    for k in range(TOP_K):
        pltpu.make_async_copy(hp_ref, xs_hbm.at[pl.ds(0, ts), :], sem).wait()


def _scatter(dest, hp, first_block, n_block, n_used, n_rows):
    t, half = hp.shape
    ts = SCATTER_TILE
    n_steps = t // ts
    dest3 = dest.reshape(n_steps, 1, ts * TOP_K)
    grid_spec = pltpu.PrefetchScalarGridSpec(
        num_scalar_prefetch=3,
        grid=(n_steps,),
        in_specs=[pl.BlockSpec((1, 1, ts * TOP_K), lambda i, b0, nb, nu: (i, 0, 0),
                               memory_space=pltpu.SMEM),
                  pl.BlockSpec((ts, half), lambda i, b0, nb, nu: (i, 0))],
        out_specs=pl.BlockSpec(memory_space=pl.ANY),
        scratch_shapes=[pltpu.VMEM((EXPERT_ROWS, half), jnp.uint32),
                        pltpu.SemaphoreType.DMA(()), pltpu.SemaphoreType.DMA(())],
    )
    return pl.pallas_call(
        _scatter_kernel,
        grid_spec=grid_spec,
        out_shape=jax.ShapeDtypeStruct((n_rows, half), jnp.uint32),
        compiler_params=_cparams(("arbitrary",)),
        name="scatter",
    )(first_block, n_block, n_used, dest3, hp)


def _expert_kernel(b0_ref, nb_ref, nu_ref, xs_hbm, w13_ref, w2_ref, y_hbm,
                   xbuf, ybuf, w13b, w2b, sem_in, sem_out):
    e = pl.program_id(0)
    n_used = nu_ref[0]
    b0 = b0_ref[e]
    nb = nb_ref[e]
    n_in, bm = xbuf.shape[:2]
    n_out, y_rows = ybuf.shape[:2]
    ahead = n_in - 1

    def in_copy(g, slot):
        return pltpu.make_async_copy(xs_hbm.at[pl.ds(pl.multiple_of(g * bm, bm), bm), :],
                                     xbuf.at[slot], sem_in.at[slot])

    def out_copy(g, slot):
        return pltpu.make_async_copy(ybuf.at[slot],
                                     y_hbm.at[pl.ds(pl.multiple_of(g * y_rows, y_rows), y_rows), :],
                                     sem_out.at[slot])

    @pl.when(nb > 0)
    def _():
        w13b[...] = w13_ref[0].astype(BF16)
        w2b[...] = w2_ref[0].astype(BF16)

    @pl.when((nb > 0) & (b0 == 0))
    def _():
        for a in range(ahead):
            @pl.when(a < n_used)
            def _():
                in_copy(a, a).start()

    def block(g, carry):
        in_slot = g % n_in
        out_slot = g % n_out
        in_copy(g, in_slot).wait()

        @pl.when(g + ahead < n_used)
        def _():
            in_copy(g + ahead, (g + ahead) % n_in).start()

        @pl.when(g >= n_out)
        def _():
            out_copy(g - n_out, out_slot).wait()

        w = xbuf[in_slot]
        lo = pltpu.bitcast(w << 16, F32).astype(BF16)
        hi = pltpu.bitcast(w & jnp.uint32(0xFFFF0000), F32).astype(BF16)
        half = lo.shape[1]
        ag = _dot(lo, w13b[:half, :]) + _dot(hi, w13b[half:, :])
        hh = ag.shape[1] // 2
        act = (_silu(ag[:, :hh]) * ag[:, hh:]).astype(BF16)
        _to_tiled_rows(ybuf.at[out_slot], _dot(act, w2b[...]))
        out_copy(g, out_slot).start()
        return carry

    lax.fori_loop(b0, b0 + nb, block, 0)

    @pl.when(e == pl.num_programs(0) - 1)
    def _():
        for back in range(1, n_out + 1):
            @pl.when(n_used >= back)
            def _():
                out_copy(n_used - back, (n_used - back) % n_out).wait()

        n_blocks = y_hbm.shape[0] // y_rows
        ybuf[0] = jnp.zeros((y_rows, LANES), F32)

        def start_zero(g, carry):
            out_copy(g, 0).start()
            return carry

        def wait_zero(g, carry):
            out_copy(g, 0).wait()
            return carry

        lax.fori_loop(n_used, n_blocks, start_zero, 0)
        lax.fori_loop(n_used, n_blocks, wait_zero, 0)


def _experts(xs, first_block, n_block, n_used, w13, w2):
    n_rows, half = xs.shape
    ne, d, h2 = w13.shape
    bm = EXPERT_ROWS
    y_rows = bm * d // LANES
    grid_spec = pltpu.PrefetchScalarGridSpec(
        num_scalar_prefetch=3,
        grid=(ne,),
        in_specs=[
            pl.BlockSpec(memory_space=pl.ANY),
            pl.BlockSpec((1, d, h2), lambda e, b0, nb, nu: (e, 0, 0)),
            pl.BlockSpec((1, h2 // 2, d), lambda e, b0, nb, nu: (e, 0, 0)),
        ],
        out_specs=pl.BlockSpec(memory_space=pl.ANY),
        scratch_shapes=[pltpu.VMEM((EXPERT_IN_SLOTS, bm, half), jnp.uint32),
                        pltpu.VMEM((EXPERT_OUT_SLOTS, y_rows, LANES), F32),
                        pltpu.VMEM((d, h2), BF16),
                        pltpu.VMEM((h2 // 2, d), BF16),
                        pltpu.SemaphoreType.DMA((EXPERT_IN_SLOTS,)),
                        pltpu.SemaphoreType.DMA((EXPERT_OUT_SLOTS,))],
    )
    return pl.pallas_call(
        _expert_kernel,
        grid_spec=grid_spec,
        out_shape=jax.ShapeDtypeStruct((n_rows * d // LANES, LANES), F32),
        compiler_params=_cparams(("arbitrary",)),
        name="experts",
    )(first_block, n_block, n_used, xs, w13, w2)


def _final_kernel(dc_ref, dn_ref, y_hbm, w_ref, x_ref, mod_ref, gpre_ref, gpost_ref, w13s_ref,
                  w2s_ref, o_ref, buf, sem):
    i = pl.program_id(0)
    slot = i % 2
    tm = x_ref.shape[0]

    def issue(d_ref, s):
        _gather_rows(d_ref, tm * TOP_K, y_hbm,
                     lambda q, u: buf.at[s, u, _tile_rows(q), :], sem.at[s])

    @pl.when(i == 0)
    def _():
        issue(dc_ref, 0)

    @pl.when(i + 1 < pl.num_programs(0))
    def _():
        issue(dn_ref, 1 - slot)

    for k in range(TOP_K):
        pltpu.make_async_copy(y_hbm.at[pl.ds(0, tm * ROW_TILE), :], buf.at[slot, k],
                              sem.at[slot]).wait()

    w = w_ref[...]
    x1 = x_ref[...]
    routed = None
    for k in range(TOP_K):
        yk = _from_tiled_rows(lambda a, n, st: buf[slot, k, pl.ds(a, n, stride=st), :], tm, x1.shape[1])
        routed = yk * w[:, k:k + 1] if routed is None else routed + yk * w[:, k:k + 1]

    m = mod_ref[0]
    h = (_rms(x1, gpre_ref[...]) * (1.0 + m[4:5]) + m[3:4]).astype(BF16)
    ag = _dot(h, w13s_ref[...])
    hh = ag.shape[1] // 2
    act = (_silu(ag[:, :hh]) * ag[:, hh:]).astype(BF16)
    ffn = routed + _dot(act, w2s_ref[...])
    o_ref[...] = x1 + m[5:6] * _rms(ffn, gpost_ref[...])


def _final(dest, y_sorted, top_w, x1, mod, gpre, gpost, w13s, w2s, seq):
    t, d = x1.shape
    tm = COMBINE_TILE
    per_b = seq // tm
    n_steps = t // tm
    dest3 = dest.reshape(n_steps, 1, tm * TOP_K)
    full = lambda a: pl.BlockSpec(a.shape, lambda i: (0,) * a.ndim)
    row = lambda n: pl.BlockSpec((tm, n), lambda i: (i, 0))
    return pl.pallas_call(
        _final_kernel,
        grid=(n_steps,),
        in_specs=[
            pl.BlockSpec((1, 1, tm * TOP_K), lambda i: (i, 0, 0), memory_space=pltpu.SMEM),
            pl.BlockSpec((1, 1, tm * TOP_K), lambda i: (jnp.minimum(i + 1, n_steps - 1), 0, 0),
                         memory_space=pltpu.SMEM),
            pl.BlockSpec(memory_space=pl.ANY),
            row(LANES), row(d), pl.BlockSpec((1, 6, d), lambda i: (i // per_b, 0, 0)),
            full(gpre), full(gpost), full(w13s), full(w2s)],
        out_specs=row(d),
        out_shape=jax.ShapeDtypeStruct((t, d), F32),
        scratch_shapes=[pltpu.VMEM((2, TOP_K, tm * ROW_TILE, LANES), F32),
                        pltpu.SemaphoreType.DMA((2,))],
        compiler_params=_cparams(("arbitrary",)),
        name="combine_final",
    )(dest3, dest3, y_sorted, top_w, x1, mod, gpre, gpost, w13s, w2s)


def _pad_heads(v):
    lead = v.shape[:-1]
    v = v.reshape(lead + (2, SSD_GROUPS, HEADS_PER_GROUP))
    v = jnp.pad(v, [(0, 0)] * (len(lead) + 2) + [(0, HEADS_PADDED - HEADS_PER_GROUP)])
    return v.reshape(lead + (2 * SSD_GROUPS * HEADS_PADDED,))


def _dispatch_plan(counts, n_tokens):
    bm = EXPERT_ROWS
    counts = counts.reshape(N_EXPERTS).astype(jnp.int32)
    padded = (counts + bm - 1) // bm * bm
    pad_end = jnp.cumsum(padded)
    pad_start = pad_end - padded
    n_rows = -(-n_tokens * TOP_K // bm) * bm + N_EXPERTS * bm
    first_block = (pad_start // bm).astype(jnp.int32)
    n_block = (padded // bm).astype(jnp.int32)
    n_used = (pad_end[-1] // bm).astype(jnp.int32).reshape(1)
    return pad_start, first_block, n_block, n_used, n_rows


def _layer(x, c, w_ada, b_ada, pre_norm_mix, post_norm_mix, pre_norm_ffn, post_norm_ffn, w_in,
           conv_w, conv_b, dt_bias_fwd, dt_bias_bwd, a_log_fwd, a_log_bwd, d_skip, ssd_norm,
           w_branch_ssd, w_branch_fourier, w_out, w_router, router_bias, w13_experts, w2_experts,
           w13_shared, w2_shared):
    bsz, seq, d = x.shape
    t = bsz * seq
    x2 = x.reshape(t, d)
    row = lambda v: v.reshape(1, -1).astype(F32)

    mod = _ada(c, w_ada, b_ada)

    i1 = SSD_INNER
    i2 = i1 + XBC_WIDTH
    i3 = i2 + 2 * SSD_HEADS
    i4 = i3 + FOURIER_WIDTH
    n_dt = 2 * SSD_GROUPS * HEADS_PADDED
    w_dtp = _pad_heads(w_in[:, i2:i3])
    wdtc = jnp.pad(w_dtp, ((0, 0), (0, LANES - n_dt))).astype(BF16)
    wdtT = w_dtp.T.astype(BF16)
    bias_p = _pad_heads(jnp.concatenate([dt_bias_fwd, dt_bias_bwd]).astype(F32))
    a_p = _pad_heads(-jnp.exp(jnp.concatenate([a_log_fwd, a_log_bwd]).astype(F32)))
    pad_row = lambda v: jnp.pad(v, (0, LANES - n_dt)).reshape(1, LANES)
    z, xbc, acsc, dtT, acsT, uf, gates = _inproj(
        x2, mod, row(pre_norm_mix), w_in[:, :i1].astype(BF16), w_in[:, i1:i2].astype(BF16),
        wdtc, wdtT, w_in[:, i3:i4].astype(BF16), w_in[:, i4:].astype(BF16),
        pad_row(bias_p), pad_row(a_p), bias_p.reshape(n_dt, 1), a_p.reshape(n_dt, 1), seq)

    xbc3 = _conv(xbc.reshape(bsz, seq, XBC_WIDTH), conv_w, conv_b)

    yf, yb = _ssd(xbc3, dtT, acsT, acsc, row(jnp.repeat(d_skip, SSD_HEAD_DIM)))

    fm = _fourier(uf.reshape(bsz, seq, FOURIER_WIDTH))

    x1 = _mix(yf.reshape(t, SSD_INNER), yb.reshape(t, SSD_INNER), z,
              fm.reshape(t, FOURIER_WIDTH), gates, x2, mod,
              row(ssd_norm), w_branch_ssd.astype(BF16),
              w_branch_fourier.astype(BF16), w_out.astype(BF16), row(post_norm_mix), seq)

    hp, idx_pad, w_pad, rank_pad, counts = _route(x1, mod, row(pre_norm_ffn),
                                                  w_router.astype(F32), router_bias, seq)
    pad_start, first_block, n_block, n_used, n_rows = _dispatch_plan(counts, t)
    dest = _dest(idx_pad, rank_pad, pad_start)[:, :TOP_K]
    xs = _scatter(dest, hp, first_block, n_block, n_used, n_rows)
    y_sorted = _experts(xs, first_block, n_block, n_used, w13_experts, w2_experts)
    out = _final(dest, y_sorted, w_pad, x1, mod, row(pre_norm_ffn), row(post_norm_ffn),
                 w13_shared.astype(BF16), w2_shared.astype(BF16), seq)
    return out.reshape(bsz, seq, d)


def kernel(x, c, w_ada, b_ada, pre_norm_mix, post_norm_mix, pre_norm_ffn, post_norm_ffn, w_in,
           conv_w, conv_b, dt_bias_fwd, dt_bias_bwd, a_log_fwd, a_log_bwd, d_skip, ssd_norm,
           w_branch_ssd, w_branch_fourier, w_out, w_router, router_bias, w13_experts, w2_experts,
           w13_shared, w2_shared):
    for layer in range(w_ada.shape[0]):
        x = _layer(x, c, w_ada[layer], b_ada[layer], pre_norm_mix[layer], post_norm_mix[layer],
                   pre_norm_ffn[layer], post_norm_ffn[layer], w_in[layer], conv_w[layer],
                   conv_b[layer], dt_bias_fwd[layer], dt_bias_bwd[layer], a_log_fwd[layer],
                   a_log_bwd[layer], d_skip[layer], ssd_norm[layer], w_branch_ssd[layer],
                   w_branch_fourier[layer], w_out[layer], w_router[layer], router_bias[layer],
                   w13_experts[layer], w2_experts[layer], w13_shared[layer], w2_shared[layer])
    return x
```

```python
import functools
import math

import numpy as np
import jax
import jax.numpy as jnp
from jax import lax
from jax.experimental import pallas as pl
from jax.experimental.pallas import tpu as pltpu

F32 = jnp.float32
BF16 = jnp.bfloat16
HIGHEST = lax.Precision.HIGHEST

D_MODEL = 1024
SSD_HEADS = 24
SSD_HEAD_DIM = 64
SSD_INNER = SSD_HEADS * SSD_HEAD_DIM
SSD_GROUPS = 4
HEADS_PER_GROUP = SSD_HEADS // SSD_GROUPS
HEADS_PADDED = 8
SSD_STATE = 128
SSD_CONV = 5
SSD_CHUNK = 128
XBC_WIDTH = SSD_INNER + 2 * SSD_GROUPS * SSD_STATE
GROUP_X = HEADS_PER_GROUP * SSD_HEAD_DIM
FOURIER_WIDTH = 512
FOURIER_GROUP_DIM = 128
N_EXPERTS = 256
TOP_K = 8
N_EXPERT_GROUPS = 8
TOPK_GROUPS = 4
EXPERT_HIDDEN = 256
SHARED_HIDDEN = 256
ROUTED_SCALE = 2.5
RMS_EPS = 1e-6

LANES = 128
VMEM_LIMIT = 56 * 1024 * 1024
TOKEN_TILE = 256
INPROJ_TILE = 512
INPROJ_SUBTILE = 256
EXPERT_ROWS = 256
COMBINE_TILE = 256
SCATTER_TILE = 512
DEST_TILE = 1024
EXPERT_IN_SLOTS = 4
EXPERT_OUT_SLOTS = 3
CONV_ROWS = 256


def _cparams(sem):
    return pltpu.CompilerParams(dimension_semantics=sem, vmem_limit_bytes=VMEM_LIMIT)


def _dot(a, b, precision=None):
    return jnp.dot(a, b, preferred_element_type=F32, precision=precision)


def _dot_nt(a, b, precision=None):
    return lax.dot_general(a, b, (((1,), (1,)), ((), ())), preferred_element_type=F32,
                           precision=precision)


def _dot_tn(a, b):
    return lax.dot_general(a, b, (((0,), (0,)), ((), ())), preferred_element_type=F32)


def _sigmoid(x):
    return 1.0 / (1.0 + jnp.exp(-x))


def _silu(x):
    return x * _sigmoid(x)


def _softplus(x):
    return jnp.maximum(x, 0.0) + jnp.log1p(jnp.exp(-jnp.abs(x)))


def _rms(x, g):
    return x * lax.rsqrt(jnp.mean(x * x, axis=-1, keepdims=True) + RMS_EPS) * g


def _ada_kernel(c_ref, w_ref, b_ref, o_ref):
    o_ref[...] = _dot(_silu(c_ref[...]), w_ref[...], HIGHEST) + b_ref[...]


def _ada(c, w_ada, b_ada):
    bsz, d = c.shape
    rows = 8
    cp = jnp.zeros((rows, d), F32).at[:bsz].set(c)
    n = w_ada.shape[1]
    tn = 1536
    out = pl.pallas_call(
        _ada_kernel,
        grid=(n // tn,),
        in_specs=[pl.BlockSpec((rows, d), lambda j: (0, 0)),
                  pl.BlockSpec((d, tn), lambda j: (0, j)),
                  pl.BlockSpec((1, tn), lambda j: (0, j))],
        out_specs=pl.BlockSpec((rows, tn), lambda j: (0, j)),
        out_shape=jax.ShapeDtypeStruct((rows, n), F32),
        compiler_params=_cparams(("arbitrary",)),
        name="adaln",
    )(cp, w_ada, b_ada.reshape(1, n))
    return out[:bsz].reshape(bsz, 6, d)


def _split3(a):
    a1 = a.astype(BF16)
    r1 = a - a1.astype(F32)
    a2 = r1.astype(BF16)
    a3 = (r1 - a2.astype(F32)).astype(BF16)
    return a1, a2, a3


def _inproj_kernel(x_ref, mod_ref, g_ref, wz_ref, wxbc_ref, wdtc_ref, wdtT_ref, wuf_ref, wg_ref,
                   bias_row_ref, a_row_ref, bias_col_ref, a_col_ref,
                   z_ref, xbc_ref, acsc_ref, dtT_ref, acsT_ref, uf_ref, gates_ref):
    m = mod_ref[0]
    tm = INPROJ_SUBTILE
    n_fwd = SSD_GROUPS * HEADS_PADDED
    ii = lax.broadcasted_iota(jnp.int32, (tm, tm), 0)
    jj = lax.broadcasted_iota(jnp.int32, (tm, tm), 1)
    same = (ii // SSD_CHUNK) == (jj // SSD_CHUNK)
    tri_f = (same & (jj <= ii)).astype(BF16)
    tri_b = (same & (jj >= ii)).astype(BF16)

    for s in range(x_ref.shape[0] // tm):
        rows = slice(s * tm, (s + 1) * tm)
        h = _rms(x_ref[rows, :], g_ref[...]) * (1.0 + m[1:2]) + m[0:1]
        hb = h.astype(BF16)
        z_ref[rows, :] = _dot(hb, wz_ref[...]).astype(z_ref.dtype)
        xbc_ref[rows, :] = _dot(hb, wxbc_ref[...]).astype(xbc_ref.dtype)
        uf_ref[rows, :] = _dot(hb, wuf_ref[...]).astype(uf_ref.dtype)
        gates_ref[rows, :] = _dot(hb, wg_ref[...]).astype(gates_ref.dtype)

        dt_c = _softplus(_dot(hb, wdtc_ref[...]) + bias_row_ref[...])
        pieces = _split3(dt_c * a_row_ref[...])
        acs_f = sum(_dot(tri_f, p) for p in pieces)
        acs_b = sum(_dot(tri_b, p) for p in pieces)
        lane = lax.broadcasted_iota(jnp.int32, acs_f.shape, 1)
        acsc_ref[rows, :] = jnp.where(lane < n_fwd, acs_f, acs_b)

        dt_t = _softplus(_dot_nt(wdtT_ref[...], hb) + bias_col_ref[...])
        pieces = _split3(dt_t * a_col_ref[...])
        acs_f = sum(_dot_nt(p, tri_f) for p in pieces)
        acs_b = sum(_dot_nt(p, tri_b) for p in pieces)
        sub = lax.broadcasted_iota(jnp.int32, acs_f.shape, 0)
        dtT_ref[:, rows] = dt_t
        acsT_ref[:, rows] = jnp.where(sub < n_fwd, acs_f, acs_b)


def _inproj(x2, mod, g, wz, wxbc, wdtc, wdtT, wuf, wg, bias_row, a_row, bias_col, a_col, seq):
    t, d = x2.shape
    tm = INPROJ_TILE
    per_b = seq // tm
    full = lambda a: pl.BlockSpec(a.shape, lambda i: (0,) * a.ndim)
    row = lambda n: pl.BlockSpec((tm, n), lambda i: (i, 0))
    nd = wdtT.shape[0]
    colspec = pl.BlockSpec((nd, tm), lambda i: (0, i))
    return pl.pallas_call(
        _inproj_kernel,
        grid=(t // tm,),
        in_specs=[row(d), pl.BlockSpec((1, 6, d), lambda i: (i // per_b, 0, 0)), full(g),
                  full(wz), full(wxbc), full(wdtc), full(wdtT), full(wuf), full(wg),
                  full(bias_row), full(a_row), full(bias_col), full(a_col)],
        out_specs=[row(wz.shape[1]), row(wxbc.shape[1]), row(LANES), colspec, colspec,
                   row(wuf.shape[1]), row(wg.shape[1])],
        out_shape=[jax.ShapeDtypeStruct((t, wz.shape[1]), BF16),
                   jax.ShapeDtypeStruct((t, wxbc.shape[1]), BF16),
                   jax.ShapeDtypeStruct((t, LANES), F32),
                   jax.ShapeDtypeStruct((nd, t), F32),
                   jax.ShapeDtypeStruct((nd, t), F32),
                   jax.ShapeDtypeStruct((t, wuf.shape[1]), BF16),
                   jax.ShapeDtypeStruct((t, wg.shape[1]), BF16)],
        compiler_params=_cparams(("arbitrary",)),
        name="inproj",
    )(x2, mod, g, wz, wxbc, wdtc, wdtT, wuf, wg, bias_row, a_row, bias_col, a_col)


def _conv_kernel(u_ref, w_ref, b_ref, o_ref, pad_ref):
    s = u_ref.shape[1]
    halo = 8
    pad_ref[0:halo, :] = jnp.zeros((halo, LANES), F32)
    pad_ref[halo + s:2 * halo + s, :] = jnp.zeros((halo, LANES), F32)
    pad_ref[halo:halo + s, :] = u_ref[0].astype(F32)
    w = w_ref[...]
    b = b_ref[...]
    half = (SSD_CONV - 1) // 2
    for r in range(s // CONV_ROWS):
        base = r * CONV_ROWS
        acc = b
        for k in range(SSD_CONV):
            lo = base + halo + k - half
            acc = acc + w[k:k + 1, :] * pad_ref[lo:lo + CONV_ROWS, :]
        o_ref[0, base:base + CONV_ROWS, :] = _silu(acc).astype(o_ref.dtype)


def _conv(xbc3, conv_w, conv_b):
    bsz, s, c = xbc3.shape
    return pl.pallas_call(
        _conv_kernel,
        grid=(bsz, c // LANES),
        in_specs=[pl.BlockSpec((1, s, LANES), lambda b, j: (b, 0, j)),
                  pl.BlockSpec((SSD_CONV, LANES), lambda b, j: (0, j)),
                  pl.BlockSpec((1, LANES), lambda b, j: (0, j))],
        out_specs=pl.BlockSpec((1, s, LANES), lambda b, j: (b, 0, j)),
        out_shape=jax.ShapeDtypeStruct((bsz, s, c), BF16),
        scratch_shapes=[pltpu.VMEM((s + 16, LANES), F32)],
        compiler_params=_cparams(("arbitrary", "arbitrary")),
        name="conv",
    )(xbc3, conv_w, conv_b.reshape(1, c))


HEAD_PAIRS = HEADS_PER_GROUP // 2
SSD_GROUPS_PER_STEP = 4


def _ssd_direction(x, bm, cm, dt_r, acs_r, acsc_all, lane_off, s_ref, reverse):
    L, N = bm.shape
    assert L == N == LANES
    ii = lax.broadcasted_iota(jnp.int32, (L, L), 0)
    jj = lax.broadcasted_iota(jnp.int32, (L, L), 1)
    mask = (jj >= ii) if reverse else (jj <= ii)
    lo_half = jj < SSD_HEAD_DIM
    shift = jnp.where(lane_off == 0, 0, LANES - lane_off)
    acs_c = pltpu.roll(acsc_all, shift, 1)
    last = 0 if reverse else L - 1
    tot_r = acs_r[:, last:last + 1]
    w_r = jnp.exp(tot_r - acs_r) * dt_r
    etot = jnp.broadcast_to(jnp.exp(tot_r), (HEADS_PADDED, LANES))

    cbt = _dot_nt(cm.astype(BF16), bm.astype(BF16))
    bt = bm.astype(F32).T
    cm = cm.astype(F32)
    ys = []
    for q in range(HEAD_PAIRS):
        xq = x[:, q * LANES:(q + 1) * LANES]
        sq = s_ref[q]
        x_a = jnp.where(lo_half, xq, jnp.zeros_like(xq)).astype(BF16)
        x_b = jnp.where(lo_half, jnp.zeros_like(xq), xq).astype(BF16)
        s_a = jnp.where(lo_half, sq, 0.0).astype(BF16)
        s_b = jnp.where(lo_half, 0.0, sq).astype(BF16)
        m_parts, c_parts, b_parts = [], [], []
        for h in (2 * q, 2 * q + 1):
            col = jnp.broadcast_to(acs_c[:, h:h + 1], (L, L))
            decay = jnp.exp(jnp.where(mask, col - acs_r[h:h + 1, :], -jnp.inf))
            m_parts.append((cbt * decay * dt_r[h:h + 1, :]).astype(BF16))
            c_parts.append((cm * jnp.exp(col)).astype(BF16))
            b_parts.append((bt * w_r[h:h + 1, :]).astype(BF16))
        x_diag = jnp.concatenate([x_a, x_b], axis=0)
        lhs = jnp.concatenate(m_parts + c_parts, axis=1)
        rhs = jnp.concatenate([x_diag, s_a, s_b], axis=0)
        ys.append(_dot(lhs, rhs))
        dec = jnp.where(lo_half[0:1], etot[2 * q:2 * q + 1], etot[2 * q + 1:2 * q + 2])
        s_ref[q] = sq * dec + _dot(jnp.concatenate(b_parts, axis=1), x_diag)
    return jnp.concatenate(ys, axis=1)


def _ssd_kernel(xf_ref, bf_ref, cf_ref, dtTf_ref, acsTf_ref, acscf_ref,
                xb_ref, bb_ref, cb_ref, dtTb_ref, acsTb_ref, acscb_ref, dskip_ref,
                yf_ref, yb_ref, sf_ref, sb_ref):
    gp = pl.program_id(1)

    @pl.when(pl.program_id(2) == 0)
    def _():
        sf_ref[...] = jnp.zeros_like(sf_ref)
        sb_ref[...] = jnp.zeros_like(sb_ref)

    for i in range(SSD_GROUPS_PER_STEP):
        g = gp * SSD_GROUPS_PER_STEP + i
        xs = slice(i * GROUP_X, (i + 1) * GROUP_X)
        ns = slice(i * SSD_STATE, (i + 1) * SSD_STATE)
        hs = slice(i * HEADS_PADDED, (i + 1) * HEADS_PADDED)
        y_fwd = _ssd_direction(
            xf_ref[0, :, xs], bf_ref[0, :, ns], cf_ref[0, :, ns], dtTf_ref[hs, :], acsTf_ref[hs, :],
            acscf_ref[...], g * HEADS_PADDED, sf_ref.at[i], False)
        y_fwd = y_fwd + dskip_ref[:, xs] * xf_ref[0, :, xs].astype(F32)
        yf_ref[0, :, xs] = y_fwd.astype(yf_ref.dtype)
        yb_ref[0, :, xs] = _ssd_direction(
            xb_ref[0, :, xs], bb_ref[0, :, ns], cb_ref[0, :, ns], dtTb_ref[hs, :], acsTb_ref[hs, :],
            acscb_ref[...], (SSD_GROUPS + g) * HEADS_PADDED, sb_ref.at[i], True).astype(yb_ref.dtype)


def _ssd(xbc3, dtT, acsT, acsc, dskip):
    bsz, s, _ = xbc3.shape
    L = SSD_CHUNK
    nc = s // L
    P = SSD_GROUPS_PER_STEP
    GP = SSD_GROUPS // P
    nb = SSD_INNER // (P * SSD_STATE)
    ncb = nb + GP
    fwd = lambda c: c
    bwd = lambda c: nc - 1 - c

    def specs(cidx, dirn):
        rowspec = pl.BlockSpec((P * HEADS_PADDED, L),
                               lambda b, g, c: (dirn * GP + g, b * nc + cidx(c)))
        return [
            pl.BlockSpec((1, L, P * GROUP_X), lambda b, g, c: (b, cidx(c), g)),
            pl.BlockSpec((1, L, P * SSD_STATE), lambda b, g, c: (b, cidx(c), nb + g)),
            pl.BlockSpec((1, L, P * SSD_STATE), lambda b, g, c: (b, cidx(c), ncb + g)),
            rowspec, rowspec,
            pl.BlockSpec((L, LANES), lambda b, g, c: (b * nc + cidx(c), 0)),
        ]

    out_specs = [pl.BlockSpec((1, L, P * GROUP_X), lambda b, g, c: (b, c, g)),
                 pl.BlockSpec((1, L, P * GROUP_X), lambda b, g, c: (b, nc - 1 - c, g))]
    return pl.pallas_call(
        _ssd_kernel,
        grid=(bsz, GP, nc),
        in_specs=specs(fwd, 0) + specs(bwd, 1)
        + [pl.BlockSpec((1, P * GROUP_X), lambda b, g, c: (0, g))],
        out_specs=out_specs,
        out_shape=[jax.ShapeDtypeStruct((bsz, s, SSD_INNER), BF16)] * 2,
        scratch_shapes=[pltpu.VMEM((P, HEAD_PAIRS, SSD_STATE, LANES), F32)] * 2,
        compiler_params=_cparams(("arbitrary", "arbitrary", "arbitrary")),
        name="ssd",
    )(xbc3, xbc3, xbc3, dtT, acsT, acsc, xbc3, xbc3, xbc3, dtT, acsT, acsc, dskip)


DFT_RADIX = 8
DFT_ROWS = 256


def _dft_tables(seq):
    m_len = seq // DFT_RADIX
    mm = np.arange(m_len)
    a0 = 2.0 * np.pi * np.outer(mm, mm) / m_len
    f0 = np.concatenate([np.cos(a0), -np.sin(a0)], axis=0)
    ang = -2.0 * np.pi * np.outer(np.arange(DFT_RADIX), mm) / seq
    tw = np.zeros((DFT_RADIX, m_len, LANES))
    tw[:, :, 0] = np.cos(ang)
    tw[:, :, 1] = np.sin(ang)
    ch = np.arange(FOURIER_GROUP_DIM)
    ac = 2.0 * np.pi * np.outer(ch, ch) / FOURIER_GROUP_DIM
    scale = 1.0 / math.sqrt(seq * FOURIER_GROUP_DIM)
    fc = np.concatenate([np.cos(ac), np.sin(ac)], axis=0) * scale
    return (jnp.asarray(f0, BF16), jnp.asarray(tw, F32), jnp.asarray(fc, BF16))


def _fourier_kernel(u_ref, f0_ref, tw_ref, fc_ref, o_ref, x32, tbuf):
    m_len = f0_ref.shape[1]
    x32[...] = u_ref[0].astype(F32)

    def sub_dft(p, carry):
        xa = x32[pl.ds(2 * p, m_len, stride=DFT_RADIX), :]
        xb = x32[pl.ds(2 * p + 1, m_len, stride=DFT_RADIX), :]
        xr = jnp.concatenate([xa, xb], axis=1).astype(BF16)
        tbuf[p] = _dot(f0_ref[...], xr)
        return carry

    lax.fori_loop(0, DFT_RADIX // 2, sub_dft, 0)
    rows = min(DFT_ROWS, m_len)
    lanes_of = lambda r: slice((r % 2) * LANES, (r % 2 + 1) * LANES)

    def combine(c, carry):
        sl = pl.ds(pl.multiple_of(c * rows, rows), rows)
        si = pl.ds(pl.multiple_of(m_len + c * rows, rows), rows)
        for r in range(1, DFT_RADIX):
            tre = tbuf[r // 2, sl, lanes_of(r)]
            tim = tbuf[r // 2, si, lanes_of(r)]
            tw = tw_ref[r, sl, :]
            gre = jnp.broadcast_to(tw[:, 0:1], (rows, LANES))
            gim = jnp.broadcast_to(tw[:, 1:2], (rows, LANES))
            tbuf[r // 2, sl, lanes_of(r)] = gre * tre - gim * tim
            tbuf[r // 2, si, lanes_of(r)] = gre * tim + gim * tre
        for kr in range(DFT_RADIX):
            zr = tbuf[0, sl, lanes_of(0)]
            zi = tbuf[0, si, lanes_of(0)]
            for r in range(1, DFT_RADIX):
                tre = tbuf[r // 2, sl, lanes_of(r)]
                tim = tbuf[r // 2, si, lanes_of(r)]
                quarter, odd = divmod((r * kr) % DFT_RADIX, 2)
                for _ in range(quarter):
                    tre, tim = tim, -tre
                if odd:
                    tre, tim = (tre + tim) * math.sqrt(0.5), (tim - tre) * math.sqrt(0.5)
                zr = zr + tre
                zi = zi + tim
            zz = jnp.concatenate([zr, zi], axis=1).astype(BF16)
            out_rows = pl.ds(pl.multiple_of(kr * m_len + c * rows, rows), rows)
            o_ref[0, out_rows, :] = _dot(zz, fc_ref[...])
        return carry

    lax.fori_loop(0, m_len // rows, combine, 0)


def _fourier(uf3):
    bsz, s, w = uf3.shape
    f0, tw, fc = _dft_tables(s)
    m_len = s // DFT_RADIX
    ng = w // FOURIER_GROUP_DIM
    return pl.pallas_call(
        _fourier_kernel,
        grid=(bsz, ng),
        in_specs=[pl.BlockSpec((1, s, LANES), lambda b, j: (b, 0, j)),
                  pl.BlockSpec(f0.shape, lambda b, j: (0, 0), pipeline_mode=pl.Buffered(1)),
                  pl.BlockSpec(tw.shape, lambda b, j: (0, 0, 0), pipeline_mode=pl.Buffered(1)),
                  pl.BlockSpec(fc.shape, lambda b, j: (0, 0), pipeline_mode=pl.Buffered(1))],
        out_specs=pl.BlockSpec((1, s, LANES), lambda b, j: (b, 0, j)),
        out_shape=jax.ShapeDtypeStruct((bsz, s, w), F32),
        scratch_shapes=[pltpu.VMEM((s, LANES), F32),
                        pltpu.VMEM((DFT_RADIX // 2, 2 * m_len, 2 * LANES), F32)],
        compiler_params=_cparams(("arbitrary", "arbitrary")),
        name="fourier",
    )(uf3, f0, tw, fc)


def _mix_kernel(yf_ref, yb_ref, z_ref, fm_ref, gates_ref, x_ref, mod_ref,
                ssdn_ref, wbs_ref, wbf_ref, wout_ref, gpost_ref, o_ref):
    m = mod_ref[0]
    y = yf_ref[...].astype(F32) + yb_ref[...].astype(F32)
    v = y * _silu(z_ref[...].astype(F32))
    parts = []
    for g in range(SSD_GROUPS):
        vg = v[:, g * GROUP_X:(g + 1) * GROUP_X]
        parts.append(vg * lax.rsqrt(jnp.mean(vg * vg, axis=-1, keepdims=True) + RMS_EPS))
    vn = jnp.concatenate(parts, axis=1) * ssdn_ref[...]
    y_ssd = _dot(vn.astype(BF16), wbs_ref[...])
    y_fou = _dot(fm_ref[...].astype(BF16), wbf_ref[...])
    gt = _sigmoid(gates_ref[...].astype(F32))
    d = y_ssd.shape[1]
    mixed = gt[:, :d] * y_ssd + gt[:, d:] * y_fou
    mo = _dot(mixed.astype(BF16), wout_ref[...])
    o_ref[...] = x_ref[...] + m[2:3] * _rms(mo, gpost_ref[...])


def _mix(yf, yb, z, fm, gates, x2, mod, ssdn, wbs, wbf, wout, gpost, seq):
    t, d = x2.shape
    tm = TOKEN_TILE
    per_b = seq // tm
    full = lambda a: pl.BlockSpec(a.shape, lambda i: (0,) * a.ndim)
    row = lambda n: pl.BlockSpec((tm, n), lambda i: (i, 0))
    return pl.pallas_call(
        _mix_kernel,
        grid=(t // tm,),
        in_specs=[row(SSD_INNER), row(SSD_INNER), row(SSD_INNER),
                  row(FOURIER_WIDTH), row(2 * d), row(d),
                  pl.BlockSpec((1, 6, d), lambda i: (i // per_b, 0, 0)),
                  full(ssdn), full(wbs), full(wbf), full(wout), full(gpost)],
        out_specs=row(d),
        out_shape=jax.ShapeDtypeStruct((t, d), F32),
        compiler_params=_cparams(("arbitrary",)),
        name="mix",
    )(yf, yb, z, fm, gates, x2, mod, ssdn, wbs, wbf, wout, gpost)


def _route_kernel(x_ref, mod_ref, gpre_ref, wr_ref, rb_ref, hp_ref, idx_ref, w_ref, rank_ref,
                  counts_ref, run_ref):
    @pl.when(pl.program_id(0) == 0)
    def _():
        run_ref[...] = jnp.zeros_like(run_ref)

    m = mod_ref[0]
    h = _rms(x_ref[...], gpre_ref[...]) * (1.0 + m[4:5]) + m[3:4]
    tm, d = h.shape
    half = d // 2
    bits = pltpu.bitcast(h.astype(BF16).astype(F32), jnp.uint32)
    hp_ref[...] = (bits[:, :half] >> 16) | (bits[:, half:] & jnp.uint32(0xFFFF0000))

    h_hi = h.astype(BF16)
    h_lo = (h - h_hi.astype(F32)).astype(BF16)
    w_hi = wr_ref[0]
    logits = _dot(h_hi, w_hi) + (_dot(h_hi, wr_ref[1]) + _dot(h_lo, w_hi))
    scores = _sigmoid(logits)
    biased = scores + rb_ref[...]
    ne = scores.shape[1]
    per_group = ne // N_EXPERT_GROUPS
    lane = lax.broadcasted_iota(jnp.int32, (tm, ne), 1)
    lane_f = lane.astype(F32)
    grp = lane // per_group
    out_lane = lax.broadcasted_iota(jnp.int32, (tm, LANES), 1)
    neg = -jnp.inf

    def argmax_first(v):
        mx = jnp.max(v, axis=-1, keepdims=True)
        ix = jnp.min(jnp.where(v == mx, lane_f, float(ne)), axis=-1, keepdims=True)
        return mx, ix

    gs = []
    gmat = jnp.full((tm, LANES), neg, F32)
    for g in range(N_EXPERT_GROUPS):
        vg = jnp.where(grp == g, biased, neg)
        m1, i1 = argmax_first(vg)
        m2 = jnp.max(jnp.where(lane_f == i1, neg, vg), axis=-1, keepdims=True)
        gs.append(m1 + m2)
        gmat = jnp.where(out_lane == g, gs[g], gmat)
    ahead = jnp.zeros((tm, LANES), F32)
    for o in range(N_EXPERT_GROUPS):
        beats = (gs[o] > gmat) | ((gs[o] == gmat) & (out_lane > o))
        ahead = ahead + beats.astype(F32)
    sel = ((ahead < TOPK_GROUPS) & (out_lane < N_EXPERT_GROUPS)).astype(BF16)
    eg = lax.broadcasted_iota(jnp.int32, (LANES, ne), 0)
    ee = lax.broadcasted_iota(jnp.int32, (LANES, ne), 1)
    allowed = _dot(sel, (eg == ee // per_group).astype(BF16)) > 0.5
    masked = jnp.where(allowed, biased, neg)

    idx_out = jnp.zeros((tm, LANES), F32)
    w_out = jnp.zeros((tm, LANES), F32)
    w_sum = jnp.zeros((tm, 1), F32)
    hits = []
    for k in range(TOP_K):
        _, ik = argmax_first(masked)
        hit = lane_f == ik
        hits.append(hit)
        wk = jnp.sum(jnp.where(hit, scores, 0.0), axis=-1, keepdims=True)
        masked = jnp.where(hit, neg, masked)
        idx_out = jnp.where(out_lane == k, ik, idx_out)
        w_out = jnp.where(out_lane == k, wk, w_out)
        w_sum = w_sum + wk
    idx_ref[...] = idx_out.astype(jnp.int32)
    w_ref[...] = w_out / w_sum * ROUTED_SCALE

    chosen = functools.reduce(jnp.logical_or, hits).astype(F32)
    ti = lax.broadcasted_iota(jnp.int32, (tm, tm), 0)
    tj = lax.broadcasted_iota(jnp.int32, (tm, tm), 1)
    before = _dot((tj < ti).astype(BF16), chosen.astype(BF16)) + run_ref[...]
    rank_out = jnp.zeros((tm, LANES), jnp.int32)
    for k in range(TOP_K):
        rk = jnp.sum(jnp.where(hits[k], before, 0.0), axis=-1, keepdims=True)
        rank_out = jnp.where(out_lane == k, rk.astype(jnp.int32), rank_out)
    rank_ref[...] = rank_out
    run_ref[...] = run_ref[...] + jnp.sum(chosen, axis=0, keepdims=True)
    counts_ref[...] = run_ref[...]


def _route(x1, mod, gpre, w_router, router_bias, seq):
    t, d = x1.shape
    tm = TOKEN_TILE
    per_b = seq // tm
    full = lambda a: pl.BlockSpec(a.shape, lambda i: (0,) * a.ndim)
    row = lambda n: pl.BlockSpec((tm, n), lambda i: (i, 0))
    rb = router_bias.reshape(1, -1)
    w_hi = w_router.astype(BF16)
    w_router = jnp.stack([w_hi, (w_router - w_hi.astype(F32)).astype(BF16)])
    return pl.pallas_call(
        _route_kernel,
        grid=(t // tm,),
        in_specs=[row(d), pl.BlockSpec((1, 6, d), lambda i: (i // per_b, 0, 0)), full(gpre),
                  full(w_router), full(rb)],
        out_specs=[row(d // 2), row(LANES), row(LANES), row(LANES),
                   pl.BlockSpec((1, N_EXPERTS), lambda i: (0, 0))],
        out_shape=[jax.ShapeDtypeStruct((t, d // 2), jnp.uint32),
                   jax.ShapeDtypeStruct((t, LANES), jnp.int32),
                   jax.ShapeDtypeStruct((t, LANES), F32),
                   jax.ShapeDtypeStruct((t, LANES), jnp.int32),
                   jax.ShapeDtypeStruct((1, N_EXPERTS), F32)],
        scratch_shapes=[pltpu.VMEM((1, N_EXPERTS), F32)],
        compiler_params=_cparams(("arbitrary",)),
        name="route",
    )(x1, mod, gpre, w_router, rb)


GATHER_UNROLL = 8


ROW_TILE = 8


def _tile_rows(r):
    return pl.ds(pl.multiple_of(r * ROW_TILE, ROW_TILE), ROW_TILE)


def _gather_rows(idx_ref, n_rows, src_hbm, dst_of, sem):
    def body(q, carry):
        for u in range(GATHER_UNROLL):
            src = idx_ref[0, 0, q * GATHER_UNROLL + u]
            pltpu.make_async_copy(src_hbm.at[_tile_rows(src), :], dst_of(q, u), sem).start(
                priority=u % 2)
        return carry

    lax.fori_loop(0, n_rows // GATHER_UNROLL, body, 0)


def _to_tiled_rows(o_ref, y):
    rows, width = y.shape
    for r in range(rows // ROW_TILE):
        for j in range(width // LANES):
            o_ref[pl.ds(r * ROW_TILE * ROW_TILE + j, ROW_TILE, stride=ROW_TILE), :] = (
                y[r * ROW_TILE:(r + 1) * ROW_TILE, j * LANES:(j + 1) * LANES])


def _from_tiled_rows(load, rows, width):
    cols = []
    for j in range(width // LANES):
        pieces = [load(r * ROW_TILE * ROW_TILE + j, ROW_TILE, ROW_TILE) for r in range(rows // ROW_TILE)]
        cols.append(jnp.concatenate(pieces, axis=0))
    return jnp.concatenate(cols, axis=1)


def _dest_kernel(idx_ref, rank_ref, start_ref, o_ref):
    idx = idx_ref[...]
    rank = rank_ref[...]
    start = start_ref[...]
    tm = idx.shape[0]
    lane = lax.broadcasted_iota(jnp.int32, (tm, start.shape[1]), 1)
    out_lane = lax.broadcasted_iota(jnp.int32, (tm, LANES), 1)
    out = jnp.zeros((tm, LANES), jnp.int32)
    for k in range(TOP_K):
        base = jnp.sum(jnp.where(lane == idx[:, k:k + 1], start, 0.0), axis=-1, keepdims=True)
        out = jnp.where(out_lane == k, base.astype(jnp.int32) + rank[:, k:k + 1], out)
    o_ref[...] = out


def _dest(idx_pad, rank_pad, pad_start):
    t = idx_pad.shape[0]
    tm = DEST_TILE
    row = pl.BlockSpec((tm, LANES), lambda i: (i, 0))
    return pl.pallas_call(
        _dest_kernel,
        grid=(t // tm,),
        in_specs=[row, row, pl.BlockSpec((1, N_EXPERTS), lambda i: (0, 0))],
        out_specs=row,
        out_shape=jax.ShapeDtypeStruct((t, LANES), jnp.int32),
        compiler_params=_cparams(("arbitrary",)),
        name="dest",
    )(idx_pad, rank_pad, pad_start.astype(F32).reshape(1, N_EXPERTS))


def _scatter_kernel(b0_ref, nb_ref, nu_ref, dest_ref, hp_ref, xs_hbm, zbuf, sem, zsem):
    ts = hp_ref.shape[0]
    bm = zbuf.shape[0]
    n_blocks = xs_hbm.shape[0] // bm

    @pl.when(pl.program_id(0) == 0)
    def _():
        zbuf[...] = jnp.zeros_like(zbuf)
        n_used = nu_ref[0]

        def zero_block(g):
            return pltpu.make_async_copy(zbuf, xs_hbm.at[pl.ds(pl.multiple_of(g * bm, bm), bm), :], zsem)

        def tails(action):
            def step(e, carry):
                @pl.when(nb_ref[e] > 0)
                def _():
                    action(zero_block(b0_ref[e] + nb_ref[e] - 1))
                return carry
            lax.fori_loop(0, N_EXPERTS, step, 0)

        def unused(action):
            def step(g, carry):
                action(zero_block(g))
                return carry
            lax.fori_loop(n_used, n_blocks, step, 0)

        tails(lambda c: c.start())
        unused(lambda c: c.start())
        tails(lambda c: c.wait())
        unused(lambda c: c.wait())

    def body(t, carry):
        for k in range(TOP_K):
            dst = dest_ref[0, 0, t * TOP_K + k]
            pltpu.make_async_copy(hp_ref.at[pl.ds(t, 1), :], xs_hbm.at[pl.ds(dst, 1), :], sem).start(
                priority=k % 2)
        return carry

    lax.fori_loop(0, ts, body, 0)
    for k in range(TOP_K):
        pltpu.make_async_copy(hp_ref, xs_hbm.at[pl.ds(0, ts), :], sem).wait()


def _scatter(dest, hp, first_block, n_block, n_used, n_rows):
    t, half = hp.shape
    ts = SCATTER_TILE
    n_steps = t // ts
    dest3 = dest.reshape(n_steps, 1, ts * TOP_K)
    grid_spec = pltpu.PrefetchScalarGridSpec(
        num_scalar_prefetch=3,
        grid=(n_steps,),
        in_specs=[pl.BlockSpec((1, 1, ts * TOP_K), lambda i, b0, nb, nu: (i, 0, 0),
                               memory_space=pltpu.SMEM),
                  pl.BlockSpec((ts, half), lambda i, b0, nb, nu: (i, 0))],
        out_specs=pl.BlockSpec(memory_space=pl.ANY),
        scratch_shapes=[pltpu.VMEM((EXPERT_ROWS, half), jnp.uint32),
                        pltpu.SemaphoreType.DMA(()), pltpu.SemaphoreType.DMA(())],
    )
    return pl.pallas_call(
        _scatter_kernel,
        grid_spec=grid_spec,
        out_shape=jax.ShapeDtypeStruct((n_rows, half), jnp.uint32),
        compiler_params=_cparams(("arbitrary",)),
        name="scatter",
    )(first_block, n_block, n_used, dest3, hp)


def _expert_kernel(b0_ref, nb_ref, nu_ref, xs_hbm, w13_ref, w2_ref, y_hbm,
                   xbuf, ybuf, w13b, w2b, sem_in, sem_out):
    e = pl.program_id(0)
    n_used = nu_ref[0]
    b0 = b0_ref[e]
    nb = nb_ref[e]
    n_in, bm = xbuf.shape[:2]
    n_out, y_rows = ybuf.shape[:2]
    ahead = n_in - 1

    def in_copy(g, slot):
        return pltpu.make_async_copy(xs_hbm.at[pl.ds(pl.multiple_of(g * bm, bm), bm), :],
                                     xbuf.at[slot], sem_in.at[slot])

    def out_copy(g, slot):
        return pltpu.make_async_copy(ybuf.at[slot],
                                     y_hbm.at[pl.ds(pl.multiple_of(g * y_rows, y_rows), y_rows), :],
                                     sem_out.at[slot])

    @pl.when(nb > 0)
    def _():
        w13b[...] = w13_ref[0].astype(BF16)
        w2b[...] = w2_ref[0].astype(BF16)

    @pl.when((nb > 0) & (b0 == 0))
    def _():
        for a in range(ahead):
            @pl.when(a < n_used)
            def _():
                in_copy(a, a).start()

    def block(g, carry):
        in_slot = g % n_in
        out_slot = g % n_out
        in_copy(g, in_slot).wait()

        @pl.when(g + ahead < n_used)
        def _():
            in_copy(g + ahead, (g + ahead) % n_in).start()

        @pl.when(g >= n_out)
        def _():
            out_copy(g - n_out, out_slot).wait()

        w = xbuf[in_slot]
        lo = pltpu.bitcast(w << 16, F32).astype(BF16)
        hi = pltpu.bitcast(w & jnp.uint32(0xFFFF0000), F32).astype(BF16)
        half = lo.shape[1]
        ag = _dot(lo, w13b[:half, :]) + _dot(hi, w13b[half:, :])
        hh = ag.shape[1] // 2
        act = (_silu(ag[:, :hh]) * ag[:, hh:]).astype(BF16)
        _to_tiled_rows(ybuf.at[out_slot], _dot(act, w2b[...]))
        out_copy(g, out_slot).start()
        return carry

    lax.fori_loop(b0, b0 + nb, block, 0)

    @pl.when(e == pl.num_programs(0) - 1)
    def _():
        for back in range(1, n_out + 1):
            @pl.when(n_used >= back)
            def _():
                out_copy(n_used - back, (n_used - back) % n_out).wait()

        n_blocks = y_hbm.shape[0] // y_rows
        ybuf[0] = jnp.zeros((y_rows, LANES), F32)

        def start_zero(g, carry):
            out_copy(g, 0).start()
            return carry

        def wait_zero(g, carry):
            out_copy(g, 0).wait()
            return carry

        lax.fori_loop(n_used, n_blocks, start_zero, 0)
        lax.fori_loop(n_used, n_blocks, wait_zero, 0)


def _experts(xs, first_block, n_block, n_used, w13, w2):
    n_rows, half = xs.shape
    ne, d, h2 = w13.shape
    bm = EXPERT_ROWS
    y_rows = bm * d // LANES
    grid_spec = pltpu.PrefetchScalarGridSpec(
        num_scalar_prefetch=3,
        grid=(ne,),
        in_specs=[
            pl.BlockSpec(memory_space=pl.ANY),
            pl.BlockSpec((1, d, h2), lambda e, b0, nb, nu: (e, 0, 0)),
            pl.BlockSpec((1, h2 // 2, d), lambda e, b0, nb, nu: (e, 0, 0)),
        ],
        out_specs=pl.BlockSpec(memory_space=pl.ANY),
        scratch_shapes=[pltpu.VMEM((EXPERT_IN_SLOTS, bm, half), jnp.uint32),
                        pltpu.VMEM((EXPERT_OUT_SLOTS, y_rows, LANES), F32),
                        pltpu.VMEM((d, h2), BF16),
                        pltpu.VMEM((h2 // 2, d), BF16),
                        pltpu.SemaphoreType.DMA((EXPERT_IN_SLOTS,)),
                        pltpu.SemaphoreType.DMA((EXPERT_OUT_SLOTS,))],
    )
    return pl.pallas_call(
        _expert_kernel,
        grid_spec=grid_spec,
        out_shape=jax.ShapeDtypeStruct((n_rows * d // LANES, LANES), F32),
        compiler_params=_cparams(("arbitrary",)),
        name="experts",
    )(first_block, n_block, n_used, xs, w13, w2)


def _final_kernel(dc_ref, dn_ref, y_hbm, w_ref, x_ref, mod_ref, gpre_ref, gpost_ref, w13s_ref,
                  w2s_ref, o_ref, buf, sem):
    i = pl.program_id(0)
    slot = i % 2
    tm = x_ref.shape[0]

    def issue(d_ref, s):
        _gather_rows(d_ref, tm * TOP_K, y_hbm,
                     lambda q, u: buf.at[s, u, _tile_rows(q), :], sem.at[s])

    @pl.when(i == 0)
    def _():
        issue(dc_ref, 0)

    @pl.when(i + 1 < pl.num_programs(0))
    def _():
        issue(dn_ref, 1 - slot)

    for k in range(TOP_K):
        pltpu.make_async_copy(y_hbm.at[pl.ds(0, tm * ROW_TILE), :], buf.at[slot, k],
                              sem.at[slot]).wait()

    w = w_ref[...]
    x1 = x_ref[...]
    routed = None
    for k in range(TOP_K):
        yk = _from_tiled_rows(lambda a, n, st: buf[slot, k, pl.ds(a, n, stride=st), :], tm, x1.shape[1])
        routed = yk * w[:, k:k + 1] if routed is None else routed + yk * w[:, k:k + 1]

    m = mod_ref[0]
    h = (_rms(x1, gpre_ref[...]) * (1.0 + m[4:5]) + m[3:4]).astype(BF16)
    ag = _dot(h, w13s_ref[...])
    hh = ag.shape[1] // 2
    act = (_silu(ag[:, :hh]) * ag[:, hh:]).astype(BF16)
    ffn = routed + _dot(act, w2s_ref[...])
    o_ref[...] = x1 + m[5:6] * _rms(ffn, gpost_ref[...])


def _final(dest, y_sorted, top_w, x1, mod, gpre, gpost, w13s, w2s, seq):
    t, d = x1.shape
    tm = COMBINE_TILE
    per_b = seq // tm
    n_steps = t // tm
    dest3 = dest.reshape(n_steps, 1, tm * TOP_K)
    full = lambda a: pl.BlockSpec(a.shape, lambda i: (0,) * a.ndim)
    row = lambda n: pl.BlockSpec((tm, n), lambda i: (i, 0))
    return pl.pallas_call(
        _final_kernel,
        grid=(n_steps,),
        in_specs=[
            pl.BlockSpec((1, 1, tm * TOP_K), lambda i: (i, 0, 0), memory_space=pltpu.SMEM),
            pl.BlockSpec((1, 1, tm * TOP_K), lambda i: (jnp.minimum(i + 1, n_steps - 1), 0, 0),
                         memory_space=pltpu.SMEM),
            pl.BlockSpec(memory_space=pl.ANY),
            row(LANES), row(d), pl.BlockSpec((1, 6, d), lambda i: (i // per_b, 0, 0)),
            full(gpre), full(gpost), full(w13s), full(w2s)],
        out_specs=row(d),
        out_shape=jax.ShapeDtypeStruct((t, d), F32),
        scratch_shapes=[pltpu.VMEM((2, TOP_K, tm * ROW_TILE, LANES), F32),
                        pltpu.SemaphoreType.DMA((2,))],
        compiler_params=_cparams(("arbitrary",)),
        name="combine_final",
    )(dest3, dest3, y_sorted, top_w, x1, mod, gpre, gpost, w13s, w2s)


def _pad_heads(v):
    lead = v.shape[:-1]
    v = v.reshape(lead + (2, SSD_GROUPS, HEADS_PER_GROUP))
    v = jnp.pad(v, [(0, 0)] * (len(lead) + 2) + [(0, HEADS_PADDED - HEADS_PER_GROUP)])
    return v.reshape(lead + (2 * SSD_GROUPS * HEADS_PADDED,))


def _dispatch_plan(counts, n_tokens):
    bm = EXPERT_ROWS
    counts = counts.reshape(N_EXPERTS).astype(jnp.int32)
    padded = (counts + bm - 1) // bm * bm
    pad_end = jnp.cumsum(padded)
    pad_start = pad_end - padded
    n_rows = -(-n_tokens * TOP_K // bm) * bm + N_EXPERTS * bm
    first_block = (pad_start // bm).astype(jnp.int32)
    n_block = (padded // bm).astype(jnp.int32)
    n_used = (pad_end[-1] // bm).astype(jnp.int32).reshape(1)
    return pad_start, first_block, n_block, n_used, n_rows


def _layer(x, c, w_ada, b_ada, pre_norm_mix, post_norm_mix, pre_norm_ffn, post_norm_ffn, w_in,
           conv_w, conv_b, dt_bias_fwd, dt_bias_bwd, a_log_fwd, a_log_bwd, d_skip, ssd_norm,
           w_branch_ssd, w_branch_fourier, w_out, w_router, router_bias, w13_experts, w2_experts,
           w13_shared, w2_shared):
    bsz, seq, d = x.shape
    t = bsz * seq
    x2 = x.reshape(t, d)
    row = lambda v: v.reshape(1, -1).astype(F32)

    mod = _ada(c, w_ada, b_ada)

    i1 = SSD_INNER
    i2 = i1 + XBC_WIDTH
    i3 = i2 + 2 * SSD_HEADS
    i4 = i3 + FOURIER_WIDTH
    n_dt = 2 * SSD_GROUPS * HEADS_PADDED
    w_dtp = _pad_heads(w_in[:, i2:i3])
    wdtc = jnp.pad(w_dtp, ((0, 0), (0, LANES - n_dt))).astype(BF16)
    wdtT = w_dtp.T.astype(BF16)
    bias_p = _pad_heads(jnp.concatenate([dt_bias_fwd, dt_bias_bwd]).astype(F32))
    a_p = _pad_heads(-jnp.exp(jnp.concatenate([a_log_fwd, a_log_bwd]).astype(F32)))
    pad_row = lambda v: jnp.pad(v, (0, LANES - n_dt)).reshape(1, LANES)
    z, xbc, acsc, dtT, acsT, uf, gates = _inproj(
        x2, mod, row(pre_norm_mix), w_in[:, :i1].astype(BF16), w_in[:, i1:i2].astype(BF16),
        wdtc, wdtT, w_in[:, i3:i4].astype(BF16), w_in[:, i4:].astype(BF16),
        pad_row(bias_p), pad_row(a_p), bias_p.reshape(n_dt, 1), a_p.reshape(n_dt, 1), seq)

    xbc3 = _conv(xbc.reshape(bsz, seq, XBC_WIDTH), conv_w, conv_b)

    yf, yb = _ssd(xbc3, dtT, acsT, acsc, row(jnp.repeat(d_skip, SSD_HEAD_DIM)))

    fm = _fourier(uf.reshape(bsz, seq, FOURIER_WIDTH))

    x1 = _mix(yf.reshape(t, SSD_INNER), yb.reshape(t, SSD_INNER), z,
              fm.reshape(t, FOURIER_WIDTH), gates, x2, mod,
              row(ssd_norm), w_branch_ssd.astype(BF16),
              w_branch_fourier.astype(BF16), w_out.astype(BF16), row(post_norm_mix), seq)

    hp, idx_pad, w_pad, rank_pad, counts = _route(x1, mod, row(pre_norm_ffn),
                                                  w_router.astype(F32), router_bias, seq)
    pad_start, first_block, n_block, n_used, n_rows = _dispatch_plan(counts, t)
    dest = _dest(idx_pad, rank_pad, pad_start)[:, :TOP_K]
    xs = _scatter(dest, hp, first_block, n_block, n_used, n_rows)
    y_sorted = _experts(xs, first_block, n_block, n_used, w13_experts, w2_experts)
    out = _final(dest, y_sorted, w_pad, x1, mod, row(pre_norm_ffn), row(post_norm_ffn),
                 w13_shared.astype(BF16), w2_shared.astype(BF16), seq)
    return out.reshape(bsz, seq, d)


def kernel(x, c, w_ada, b_ada, pre_norm_mix, post_norm_mix, pre_norm_ffn, post_norm_ffn, w_in,
           conv_w, conv_b, dt_bias_fwd, dt_bias_bwd, a_log_fwd, a_log_bwd, d_skip, ssd_norm,
           w_branch_ssd, w_branch_fourier, w_out, w_router, router_bias, w13_experts, w2_experts,
           w13_shared, w2_shared):
    for layer in range(w_ada.shape[0]):
        x = _layer(x, c, w_ada[layer], b_ada[layer], pre_norm_mix[layer], post_norm_mix[layer],
                   pre_norm_ffn[layer], post_norm_ffn[layer], w_in[layer], conv_w[layer],
                   conv_b[layer], dt_bias_fwd[layer], dt_bias_bwd[layer], a_log_fwd[layer],
                   a_log_bwd[layer], d_skip[layer], ssd_norm[layer], w_branch_ssd[layer],
                   w_branch_fourier[layer], w_out[layer], w_router[layer], router_bias[layer],
                   w13_experts[layer], w2_experts[layer], w13_shared[layer], w2_shared[layer])
    return x
```

```python
import functools
import math

import numpy as np
import jax
import jax.numpy as jnp
from jax import lax
from jax.experimental import pallas as pl
from jax.experimental.pallas import tpu as pltpu

F32 = jnp.float32
BF16 = jnp.bfloat16
HIGHEST = lax.Precision.HIGHEST

D_MODEL = 1024
SSD_HEADS = 24
SSD_HEAD_DIM = 64
SSD_INNER = SSD_HEADS * SSD_HEAD_DIM
SSD_GROUPS = 4
HEADS_PER_GROUP = SSD_HEADS // SSD_GROUPS
HEADS_PADDED = 8
SSD_STATE = 128
SSD_CONV = 5
SSD_CHUNK = 128
XBC_WIDTH = SSD_INNER + 2 * SSD_GROUPS * SSD_STATE
GROUP_X = HEADS_PER_GROUP * SSD_HEAD_DIM
FOURIER_WIDTH = 512
FOURIER_GROUP_DIM = 128
N_EXPERTS = 256
TOP_K = 8
N_EXPERT_GROUPS = 8
TOPK_GROUPS = 4
EXPERT_HIDDEN = 256
SHARED_HIDDEN = 256
ROUTED_SCALE = 2.5
RMS_EPS = 1e-6
LOG2_E = math.log2(math.e)

LANES = 128
VMEM_LIMIT = 56 * 1024 * 1024
TOKEN_TILE = 256
INPROJ_TILE = 512
INPROJ_SUBTILE = 256
EXPERT_ROWS = 256
COMBINE_TILE = 256
SCATTER_TILE = 512
DEST_TILE = 1024
EXPERT_IN_SLOTS = 6
EXPERT_OUT_SLOTS = 4
CONV_ROWS = 256


def _cparams(sem):
    return pltpu.CompilerParams(dimension_semantics=sem, vmem_limit_bytes=VMEM_LIMIT)


def _dot(a, b, precision=None):
    return jnp.dot(a, b, preferred_element_type=F32, precision=precision)


def _dot_nt(a, b, precision=None):
    return lax.dot_general(a, b, (((1,), (1,)), ((), ())), preferred_element_type=F32,
                           precision=precision)


def _dot_tn(a, b):
    return lax.dot_general(a, b, (((0,), (0,)), ((), ())), preferred_element_type=F32)


def _sigmoid(x):
    return 1.0 / (1.0 + jnp.exp(-x))


def _silu(x):
    return x * _sigmoid(x)


def _softplus(x):
    return jnp.maximum(x, 0.0) + jnp.log1p(jnp.exp(-jnp.abs(x)))


def _rms(x, g):
    return x * lax.rsqrt(jnp.mean(x * x, axis=-1, keepdims=True) + RMS_EPS) * g


def _ada_kernel(c_ref, w_ref, b_ref, o_ref):
    o_ref[...] = _dot(_silu(c_ref[...]), w_ref[...], HIGHEST) + b_ref[...]


def _ada(c, w_ada, b_ada):
    bsz, d = c.shape
    rows = 8
    cp = jnp.zeros((rows, d), F32).at[:bsz].set(c)
    n = w_ada.shape[1]
    tn = 1536
    out = pl.pallas_call(
        _ada_kernel,
        grid=(n // tn,),
        in_specs=[pl.BlockSpec((rows, d), lambda j: (0, 0)),
                  pl.BlockSpec((d, tn), lambda j: (0, j)),
                  pl.BlockSpec((1, tn), lambda j: (0, j))],
        out_specs=pl.BlockSpec((rows, tn), lambda j: (0, j)),
        out_shape=jax.ShapeDtypeStruct((rows, n), F32),
        compiler_params=_cparams(("arbitrary",)),
        name="adaln",
    )(cp, w_ada, b_ada.reshape(1, n))
    return out[:bsz].reshape(bsz, 6, d)


def _split3(a):
    a1 = a.astype(BF16)
    r1 = a - a1.astype(F32)
    a2 = r1.astype(BF16)
    a3 = (r1 - a2.astype(F32)).astype(BF16)
    return a1, a2, a3


def _inproj_kernel(x_ref, mod_ref, g_ref, wz_ref, wxbc_ref, wdtc_ref, wdtT_ref, wuf_ref, wg_ref,
                   bias_row_ref, a_row_ref, bias_col_ref, a_col_ref,
                   z_ref, xbc_ref, acsc_ref, dtT_ref, acsT_ref, uf_ref, gates_ref):
    m = mod_ref[0]
    tm = INPROJ_SUBTILE
    n_fwd = SSD_GROUPS * HEADS_PADDED
    ii = lax.broadcasted_iota(jnp.int32, (tm, tm), 0)
    jj = lax.broadcasted_iota(jnp.int32, (tm, tm), 1)
    same = (ii // SSD_CHUNK) == (jj // SSD_CHUNK)
    tri_f = (same & (jj <= ii)).astype(BF16)
    tri_b = (same & (jj >= ii)).astype(BF16)

    for s in range(x_ref.shape[0] // tm):
        rows = slice(s * tm, (s + 1) * tm)
        h = _rms(x_ref[rows, :], g_ref[...]) * (1.0 + m[1:2]) + m[0:1]
        hb = h.astype(BF16)
        z_ref[rows, :] = _dot(hb, wz_ref[...]).astype(z_ref.dtype)
        xbc_ref[rows, :] = _dot(hb, wxbc_ref[...]).astype(xbc_ref.dtype)
        uf_ref[rows, :] = _dot(hb, wuf_ref[...]).astype(uf_ref.dtype)
        gates_ref[rows, :] = _dot(hb, wg_ref[...]).astype(gates_ref.dtype)

        dt_c = _softplus(_dot(hb, wdtc_ref[...]) + bias_row_ref[...])
        pieces = _split3(dt_c * a_row_ref[...])
        acs_f = sum(_dot(tri_f, p) for p in pieces)
        acs_b = sum(_dot(tri_b, p) for p in pieces)
        lane = lax.broadcasted_iota(jnp.int32, acs_f.shape, 1)
        acsc_ref[rows, :] = jnp.where(lane < n_fwd, acs_f, acs_b) * LOG2_E

        dt_t = _softplus(_dot_nt(wdtT_ref[...], hb) + bias_col_ref[...])
        pieces = _split3(dt_t * a_col_ref[...])
        acs_f = sum(_dot_nt(p, tri_f) for p in pieces)
        acs_b = sum(_dot_nt(p, tri_b) for p in pieces)
        sub = lax.broadcasted_iota(jnp.int32, acs_f.shape, 0)
        dtT_ref[:, rows] = dt_t
        acsT_ref[:, rows] = jnp.where(sub < n_fwd, acs_f, acs_b) * LOG2_E


def _inproj(x2, mod, g, wz, wxbc, wdtc, wdtT, wuf, wg, bias_row, a_row, bias_col, a_col, seq):
    t, d = x2.shape
    tm = INPROJ_TILE
    per_b = seq // tm
    full = lambda a: pl.BlockSpec(a.shape, lambda i: (0,) * a.ndim)
    row = lambda n: pl.BlockSpec((tm, n), lambda i: (i, 0))
    nd = wdtT.shape[0]
    colspec = pl.BlockSpec((nd, tm), lambda i: (0, i))
    return pl.pallas_call(
        _inproj_kernel,
        grid=(t // tm,),
        in_specs=[row(d), pl.BlockSpec((1, 6, d), lambda i: (i // per_b, 0, 0)), full(g),
                  full(wz), full(wxbc), full(wdtc), full(wdtT), full(wuf), full(wg),
                  full(bias_row), full(a_row), full(bias_col), full(a_col)],
        out_specs=[row(wz.shape[1]), row(wxbc.shape[1]), row(LANES), colspec, colspec,
                   row(wuf.shape[1]), row(wg.shape[1])],
        out_shape=[jax.ShapeDtypeStruct((t, wz.shape[1]), BF16),
                   jax.ShapeDtypeStruct((t, wxbc.shape[1]), BF16),
                   jax.ShapeDtypeStruct((t, LANES), F32),
                   jax.ShapeDtypeStruct((nd, t), F32),
                   jax.ShapeDtypeStruct((nd, t), F32),
                   jax.ShapeDtypeStruct((t, wuf.shape[1]), BF16),
                   jax.ShapeDtypeStruct((t, wg.shape[1]), BF16)],
        compiler_params=_cparams(("arbitrary",)),
        name="inproj",
    )(x2, mod, g, wz, wxbc, wdtc, wdtT, wuf, wg, bias_row, a_row, bias_col, a_col)


def _conv_kernel(u_ref, w_ref, b_ref, o_ref, pad_ref):
    s = u_ref.shape[1]
    halo = 8
    pad_ref[0:halo, :] = jnp.zeros((halo, LANES), F32)
    pad_ref[halo + s:2 * halo + s, :] = jnp.zeros((halo, LANES), F32)
    pad_ref[halo:halo + s, :] = u_ref[0].astype(F32)
    w = w_ref[...]
    b = b_ref[...]
    half = (SSD_CONV - 1) // 2
    for r in range(s // CONV_ROWS):
        base = r * CONV_ROWS
        acc = b
        for k in range(SSD_CONV):
            lo = base + halo + k - half
            acc = acc + w[k:k + 1, :] * pad_ref[lo:lo + CONV_ROWS, :]
        o_ref[0, base:base + CONV_ROWS, :] = _silu(acc).astype(o_ref.dtype)


def _conv(xbc3, conv_w, conv_b):
    bsz, s, c = xbc3.shape
    return pl.pallas_call(
        _conv_kernel,
        grid=(bsz, c // LANES),
        in_specs=[pl.BlockSpec((1, s, LANES), lambda b, j: (b, 0, j)),
                  pl.BlockSpec((SSD_CONV, LANES), lambda b, j: (0, j)),
                  pl.BlockSpec((1, LANES), lambda b, j: (0, j))],
        out_specs=pl.BlockSpec((1, s, LANES), lambda b, j: (b, 0, j)),
        out_shape=jax.ShapeDtypeStruct((bsz, s, c), BF16),
        scratch_shapes=[pltpu.VMEM((s + 16, LANES), F32)],
        compiler_params=_cparams(("arbitrary", "arbitrary")),
        name="conv",
    )(xbc3, conv_w, conv_b.reshape(1, c))


HEAD_PAIRS = HEADS_PER_GROUP // 2
SSD_GROUPS_PER_STEP = 4


def _ssd_direction(x, bm, cm, dt_r, acs_r, acsc_all, lane_off, s_ref, reverse):
    L, N = bm.shape
    assert L == N == LANES
    ii = lax.broadcasted_iota(jnp.int32, (L, L), 0)
    jj = lax.broadcasted_iota(jnp.int32, (L, L), 1)
    mask = (jj >= ii) if reverse else (jj <= ii)
    lo_half = jj < SSD_HEAD_DIM
    shift = jnp.where(lane_off == 0, 0, LANES - lane_off)
    acs_c = pltpu.roll(acsc_all, shift, 1)
    last = 0 if reverse else L - 1
    tot_r = acs_r[:, last:last + 1]
    w_r = jnp.exp2(tot_r - acs_r) * dt_r
    src_r = acs_r - jnp.log2(dt_r)
    etot = jnp.broadcast_to(jnp.exp2(tot_r), (HEADS_PADDED, LANES))

    cm16 = cm.astype(BF16)
    cbt = _dot_nt(cm16, bm.astype(BF16))
    bt = bm.astype(F32).T
    s_all = [s_ref[q] for q in range(HEAD_PAIRS)]
    cs_all = _dot(cm16, jnp.concatenate(s_all, axis=1).astype(BF16))
    ys = []
    for q in range(HEAD_PAIRS):
        xq = x[:, q * LANES:(q + 1) * LANES]
        x_a = jnp.where(lo_half, xq, jnp.zeros_like(xq)).astype(BF16)
        x_b = jnp.where(lo_half, jnp.zeros_like(xq), xq).astype(BF16)
        m_parts, b_parts, e_cols = [], [], []
        for h in (2 * q, 2 * q + 1):
            col = jnp.broadcast_to(acs_c[:, h:h + 1], (L, L))
            decay_dt = jnp.exp2(jnp.where(mask, col - src_r[h:h + 1, :], -jnp.inf))
            m_parts.append((cbt * decay_dt).astype(BF16))
            e_cols.append(jnp.exp2(col))
            b_parts.append((bt * w_r[h:h + 1, :]).astype(BF16))
        x_diag = jnp.concatenate([x_a, x_b], axis=0)
        y_off = cs_all[:, q * LANES:(q + 1) * LANES] * jnp.where(lo_half, e_cols[0], e_cols[1])
        ys.append(_dot(jnp.concatenate(m_parts, axis=1), x_diag) + y_off)
        dec = jnp.where(lo_half[0:1], etot[2 * q:2 * q + 1], etot[2 * q + 1:2 * q + 2])
        s_ref[q] = s_all[q] * dec + _dot(jnp.concatenate(b_parts, axis=1), x_diag)
    return jnp.concatenate(ys, axis=1)


def _ssd_kernel(xf_ref, bf_ref, cf_ref, dtTf_ref, acsTf_ref, acscf_ref,
                xb_ref, bb_ref, cb_ref, dtTb_ref, acsTb_ref, acscb_ref, dskip_ref,
                yf_ref, yb_ref, sf_ref, sb_ref):
    gp = pl.program_id(1)

    @pl.when(pl.program_id(2) == 0)
    def _():
        sf_ref[...] = jnp.zeros_like(sf_ref)
        sb_ref[...] = jnp.zeros_like(sb_ref)

    for i in range(SSD_GROUPS_PER_STEP):
        g = gp * SSD_GROUPS_PER_STEP + i
        xs = slice(i * GROUP_X, (i + 1) * GROUP_X)
        ns = slice(i * SSD_STATE, (i + 1) * SSD_STATE)
        hs = slice(i * HEADS_PADDED, (i + 1) * HEADS_PADDED)
        y_fwd = _ssd_direction(
            xf_ref[0, :, xs], bf_ref[0, :, ns], cf_ref[0, :, ns], dtTf_ref[hs, :], acsTf_ref[hs, :],
            acscf_ref[...], g * HEADS_PADDED, sf_ref.at[i], False)
        y_fwd = y_fwd + dskip_ref[:, xs] * xf_ref[0, :, xs].astype(F32)
        yf_ref[0, :, xs] = y_fwd.astype(yf_ref.dtype)
        yb_ref[0, :, xs] = _ssd_direction(
            xb_ref[0, :, xs], bb_ref[0, :, ns], cb_ref[0, :, ns], dtTb_ref[hs, :], acsTb_ref[hs, :],
            acscb_ref[...], (SSD_GROUPS + g) * HEADS_PADDED, sb_ref.at[i], True).astype(yb_ref.dtype)


def _ssd(xbc3, dtT, acsT, acsc, dskip):
    bsz, s, _ = xbc3.shape
    L = SSD_CHUNK
    nc = s // L
    P = SSD_GROUPS_PER_STEP
    GP = SSD_GROUPS // P
    nb = SSD_INNER // (P * SSD_STATE)
    ncb = nb + GP
    fwd = lambda c: c
    bwd = lambda c: nc - 1 - c

    def specs(cidx, dirn):
        rowspec = pl.BlockSpec((P * HEADS_PADDED, L),
                               lambda b, g, c: (dirn * GP + g, b * nc + cidx(c)))
        return [
            pl.BlockSpec((1, L, P * GROUP_X), lambda b, g, c: (b, cidx(c), g)),
            pl.BlockSpec((1, L, P * SSD_STATE), lambda b, g, c: (b, cidx(c), nb + g)),
            pl.BlockSpec((1, L, P * SSD_STATE), lambda b, g, c: (b, cidx(c), ncb + g)),
            rowspec, rowspec,
            pl.BlockSpec((L, LANES), lambda b, g, c: (b * nc + cidx(c), 0)),
        ]

    out_specs = [pl.BlockSpec((1, L, P * GROUP_X), lambda b, g, c: (b, c, g)),
                 pl.BlockSpec((1, L, P * GROUP_X), lambda b, g, c: (b, nc - 1 - c, g))]
    return pl.pallas_call(
        _ssd_kernel,
        grid=(bsz, GP, nc),
        in_specs=specs(fwd, 0) + specs(bwd, 1)
        + [pl.BlockSpec((1, P * GROUP_X), lambda b, g, c: (0, g))],
        out_specs=out_specs,
        out_shape=[jax.ShapeDtypeStruct((bsz, s, SSD_INNER), BF16)] * 2,
        scratch_shapes=[pltpu.VMEM((P, HEAD_PAIRS, SSD_STATE, LANES), F32)] * 2,
        compiler_params=_cparams(("arbitrary", "arbitrary", "arbitrary")),
        name="ssd",
    )(xbc3, xbc3, xbc3, dtT, acsT, acsc, xbc3, xbc3, xbc3, dtT, acsT, acsc, dskip)


DFT_RADIX = 8
DFT_ROWS = 256


def _dft_tables(seq):
    m_len = seq // DFT_RADIX
    mm = np.arange(m_len)
    a0 = 2.0 * np.pi * np.outer(mm, mm) / m_len
    f0 = np.concatenate([np.cos(a0), -np.sin(a0)], axis=0)
    ang = -2.0 * np.pi * np.outer(np.arange(DFT_RADIX), mm) / seq
    tw = np.zeros((DFT_RADIX, m_len, LANES))
    tw[:, :, 0] = np.cos(ang)
    tw[:, :, 1] = np.sin(ang)
    ch = np.arange(FOURIER_GROUP_DIM)
    ac = 2.0 * np.pi * np.outer(ch, ch) / FOURIER_GROUP_DIM
    scale = 1.0 / math.sqrt(seq * FOURIER_GROUP_DIM)
    fc = np.concatenate([np.cos(ac), np.sin(ac)], axis=0) * scale
    return (jnp.asarray(f0, BF16), jnp.asarray(tw, F32), jnp.asarray(fc, BF16))


def _fourier_kernel(u_ref, f0_ref, tw_ref, fc_ref, o_ref, x32, tbuf):
    m_len = f0_ref.shape[1]
    x32[...] = u_ref[0].astype(F32)

    def sub_dft(p, carry):
        xa = x32[pl.ds(2 * p, m_len, stride=DFT_RADIX), :]
        xb = x32[pl.ds(2 * p + 1, m_len, stride=DFT_RADIX), :]
        xr = jnp.concatenate([xa, xb], axis=1).astype(BF16)
        tbuf[p] = _dot(f0_ref[...], xr)
        return carry

    lax.fori_loop(0, DFT_RADIX // 2, sub_dft, 0)
    rows = min(DFT_ROWS, m_len)
    lanes_of = lambda r: slice((r % 2) * LANES, (r % 2 + 1) * LANES)

    def combine(c, carry):
        sl = pl.ds(pl.multiple_of(c * rows, rows), rows)
        si = pl.ds(pl.multiple_of(m_len + c * rows, rows), rows)
        for r in range(1, DFT_RADIX):
            tre = tbuf[r // 2, sl, lanes_of(r)]
            tim = tbuf[r // 2, si, lanes_of(r)]
            tw = tw_ref[r, sl, :]
            gre = jnp.broadcast_to(tw[:, 0:1], (rows, LANES))
            gim = jnp.broadcast_to(tw[:, 1:2], (rows, LANES))
            tbuf[r // 2, sl, lanes_of(r)] = gre * tre - gim * tim
            tbuf[r // 2, si, lanes_of(r)] = gre * tim + gim * tre
        for kr in range(DFT_RADIX):
            zr = tbuf[0, sl, lanes_of(0)]
            zi = tbuf[0, si, lanes_of(0)]
            for r in range(1, DFT_RADIX):
                tre = tbuf[r // 2, sl, lanes_of(r)]
                tim = tbuf[r // 2, si, lanes_of(r)]
                quarter, odd = divmod((r * kr) % DFT_RADIX, 2)
                for _ in range(quarter):
                    tre, tim = tim, -tre
                if odd:
                    tre, tim = (tre + tim) * math.sqrt(0.5), (tim - tre) * math.sqrt(0.5)
                zr = zr + tre
                zi = zi + tim
            zz = jnp.concatenate([zr, zi], axis=1).astype(BF16)
            out_rows = pl.ds(pl.multiple_of(kr * m_len + c * rows, rows), rows)
            o_ref[0, out_rows, :] = _dot(zz, fc_ref[...])
        return carry

    lax.fori_loop(0, m_len // rows, combine, 0)


def _fourier(uf3):
    bsz, s, w = uf3.shape
    f0, tw, fc = _dft_tables(s)
    m_len = s // DFT_RADIX
    ng = w // FOURIER_GROUP_DIM
    return pl.pallas_call(
        _fourier_kernel,
        grid=(bsz, ng),
        in_specs=[pl.BlockSpec((1, s, LANES), lambda b, j: (b, 0, j)),
                  pl.BlockSpec(f0.shape, lambda b, j: (0, 0), pipeline_mode=pl.Buffered(1)),
                  pl.BlockSpec(tw.shape, lambda b, j: (0, 0, 0), pipeline_mode=pl.Buffered(1)),
                  pl.BlockSpec(fc.shape, lambda b, j: (0, 0), pipeline_mode=pl.Buffered(1))],
        out_specs=pl.BlockSpec((1, s, LANES), lambda b, j: (b, 0, j)),
        out_shape=jax.ShapeDtypeStruct((bsz, s, w), F32),
        scratch_shapes=[pltpu.VMEM((s, LANES), F32),
                        pltpu.VMEM((DFT_RADIX // 2, 2 * m_len, 2 * LANES), F32)],
        compiler_params=_cparams(("arbitrary", "arbitrary")),
        name="fourier",
    )(uf3, f0, tw, fc)


def _mix_kernel(yf_ref, yb_ref, z_ref, fm_ref, gates_ref, x_ref, mod_ref,
                ssdn_ref, wbs_ref, wbf_ref, wout_ref, gpost_ref, o_ref):
    m = mod_ref[0]
    y = yf_ref[...].astype(F32) + yb_ref[...].astype(F32)
    v = y * _silu(z_ref[...].astype(F32))
    parts = []
    for g in range(SSD_GROUPS):
        vg = v[:, g * GROUP_X:(g + 1) * GROUP_X]
        parts.append(vg * lax.rsqrt(jnp.mean(vg * vg, axis=-1, keepdims=True) + RMS_EPS))
    vn = jnp.concatenate(parts, axis=1) * ssdn_ref[...]
    y_ssd = _dot(vn.astype(BF16), wbs_ref[...])
    y_fou = _dot(fm_ref[...].astype(BF16), wbf_ref[...])
    gt = _sigmoid(gates_ref[...].astype(F32))
    d = y_ssd.shape[1]
    mixed = gt[:, :d] * y_ssd + gt[:, d:] * y_fou
    mo = _dot(mixed.astype(BF16), wout_ref[...])
    o_ref[...] = x_ref[...] + m[2:3] * _rms(mo, gpost_ref[...])


def _mix(yf, yb, z, fm, gates, x2, mod, ssdn, wbs, wbf, wout, gpost, seq):
    t, d = x2.shape
    tm = TOKEN_TILE
    per_b = seq // tm
    full = lambda a: pl.BlockSpec(a.shape, lambda i: (0,) * a.ndim)
    row = lambda n: pl.BlockSpec((tm, n), lambda i: (i, 0))
    return pl.pallas_call(
        _mix_kernel,
        grid=(t // tm,),
        in_specs=[row(SSD_INNER), row(SSD_INNER), row(SSD_INNER),
                  row(FOURIER_WIDTH), row(2 * d), row(d),
                  pl.BlockSpec((1, 6, d), lambda i: (i // per_b, 0, 0)),
                  full(ssdn), full(wbs), full(wbf), full(wout), full(gpost)],
        out_specs=row(d),
        out_shape=jax.ShapeDtypeStruct((t, d), F32),
        compiler_params=_cparams(("arbitrary",)),
        name="mix",
    )(yf, yb, z, fm, gates, x2, mod, ssdn, wbs, wbf, wout, gpost)


def _route_kernel(x_ref, mod_ref, gpre_ref, wr_ref, rb_ref, hp_ref, idx_ref, w_ref, rank_ref,
                  counts_ref, run_ref):
    @pl.when(pl.program_id(0) == 0)
    def _():
        run_ref[...] = jnp.zeros_like(run_ref)

    m = mod_ref[0]
    h = _rms(x_ref[...], gpre_ref[...]) * (1.0 + m[4:5]) + m[3:4]
    tm, d = h.shape
    half = d // 2
    bits = pltpu.bitcast(h.astype(BF16).astype(F32), jnp.uint32)
    hp_ref[...] = (bits[:, :half] >> 16) | (bits[:, half:] & jnp.uint32(0xFFFF0000))

    h_hi = h.astype(BF16)
    h_lo = (h - h_hi.astype(F32)).astype(BF16)
    w_hi = wr_ref[0]
    logits = _dot(h_hi, w_hi) + (_dot(h_hi, wr_ref[1]) + _dot(h_lo, w_hi))
    scores = _sigmoid(logits)
    biased = scores + rb_ref[...]
    ne = scores.shape[1]
    per_group = ne // N_EXPERT_GROUPS
    lane = lax.broadcasted_iota(jnp.int32, (tm, ne), 1)
    lane_f = lane.astype(F32)
    grp = lane // per_group
    out_lane = lax.broadcasted_iota(jnp.int32, (tm, LANES), 1)
    neg = -jnp.inf

    def argmax_first(v):
        mx = jnp.max(v, axis=-1, keepdims=True)
        ix = jnp.min(jnp.where(v == mx, lane_f, float(ne)), axis=-1, keepdims=True)
        return mx, ix

    gs = []
    gmat = jnp.full((tm, LANES), neg, F32)
    for g in range(N_EXPERT_GROUPS):
        vg = jnp.where(grp == g, biased, neg)
        m1, i1 = argmax_first(vg)
        m2 = jnp.max(jnp.where(lane_f == i1, neg, vg), axis=-1, keepdims=True)
        gs.append(m1 + m2)
        gmat = jnp.where(out_lane == g, gs[g], gmat)
    ahead = jnp.zeros((tm, LANES), F32)
    for o in range(N_EXPERT_GROUPS):
        beats = (gs[o] > gmat) | ((gs[o] == gmat) & (out_lane > o))
        ahead = ahead + beats.astype(F32)
    sel = ((ahead < TOPK_GROUPS) & (out_lane < N_EXPERT_GROUPS)).astype(BF16)
    eg = lax.broadcasted_iota(jnp.int32, (LANES, ne), 0)
    ee = lax.broadcasted_iota(jnp.int32, (LANES, ne), 1)
    allowed = _dot(sel, (eg == ee // per_group).astype(BF16)) > 0.5
    masked = jnp.where(allowed, biased, neg)

    idx_out = jnp.zeros((tm, LANES), F32)
    w_out = jnp.zeros((tm, LANES), F32)
    w_sum = jnp.zeros((tm, 1), F32)
    hits = []
    for k in range(TOP_K):
        _, ik = argmax_first(masked)
        hit = lane_f == ik
        hits.append(hit)
        wk = jnp.sum(jnp.where(hit, scores, 0.0), axis=-1, keepdims=True)
        masked = jnp.where(hit, neg, masked)
        idx_out = jnp.where(out_lane == k, ik, idx_out)
        w_out = jnp.where(out_lane == k, wk, w_out)
        w_sum = w_sum + wk
    idx_ref[...] = idx_out.astype(jnp.int32)
    w_ref[...] = w_out / w_sum * ROUTED_SCALE

    chosen = functools.reduce(jnp.logical_or, hits).astype(F32)
    ti = lax.broadcasted_iota(jnp.int32, (tm, tm), 0)
    tj = lax.broadcasted_iota(jnp.int32, (tm, tm), 1)
    before = _dot((tj < ti).astype(BF16), chosen.astype(BF16)) + run_ref[...]
    rank_out = jnp.zeros((tm, LANES), jnp.int32)
    for k in range(TOP_K):
        rk = jnp.sum(jnp.where(hits[k], before, 0.0), axis=-1, keepdims=True)
        rank_out = jnp.where(out_lane == k, rk.astype(jnp.int32), rank_out)
    rank_ref[...] = rank_out
    run_ref[...] = run_ref[...] + jnp.sum(chosen, axis=0, keepdims=True)
    counts_ref[...] = run_ref[...]


def _route(x1, mod, gpre, w_router, router_bias, seq):
    t, d = x1.shape
    tm = TOKEN_TILE
    per_b = seq // tm
    full = lambda a: pl.BlockSpec(a.shape, lambda i: (0,) * a.ndim)
    row = lambda n: pl.BlockSpec((tm, n), lambda i: (i, 0))
    rb = router_bias.reshape(1, -1)
    w_hi = w_router.astype(BF16)
    w_router = jnp.stack([w_hi, (w_router - w_hi.astype(F32)).astype(BF16)])
    return pl.pallas_call(
        _route_kernel,
        grid=(t // tm,),
        in_specs=[row(d), pl.BlockSpec((1, 6, d), lambda i: (i // per_b, 0, 0)), full(gpre),
                  full(w_router), full(rb)],
        out_specs=[row(d // 2), row(LANES), row(LANES), row(LANES),
                   pl.BlockSpec((1, N_EXPERTS), lambda i: (0, 0))],
        out_shape=[jax.ShapeDtypeStruct((t, d // 2), jnp.uint32),
                   jax.ShapeDtypeStruct((t, LANES), jnp.int32),
                   jax.ShapeDtypeStruct((t, LANES), F32),
                   jax.ShapeDtypeStruct((t, LANES), jnp.int32),
                   jax.ShapeDtypeStruct((1, N_EXPERTS), F32)],
        scratch_shapes=[pltpu.VMEM((1, N_EXPERTS), F32)],
        compiler_params=_cparams(("arbitrary",)),
        name="route",
    )(x1, mod, gpre, w_router, rb)


GATHER_UNROLL = 8


ROW_TILE = 8


def _tile_rows(r):
    return pl.ds(pl.multiple_of(r * ROW_TILE, ROW_TILE), ROW_TILE)


def _gather_rows(idx_ref, n_rows, src_hbm, dst_of, sem):
    def body(q, carry):
        for u in range(GATHER_UNROLL):
            src = idx_ref[0, 0, q * GATHER_UNROLL + u]
            pltpu.make_async_copy(src_hbm.at[_tile_rows(src), :], dst_of(q, u), sem).start(
                priority=u % 2)
        return carry

    lax.fori_loop(0, n_rows // GATHER_UNROLL, body, 0)


def _to_tiled_rows(o_ref, y):
    rows, width = y.shape
    for r in range(rows // ROW_TILE):
        for j in range(width // LANES):
            o_ref[pl.ds(r * ROW_TILE * ROW_TILE + j, ROW_TILE, stride=ROW_TILE), :] = (
                y[r * ROW_TILE:(r + 1) * ROW_TILE, j * LANES:(j + 1) * LANES])


def _from_tiled_rows(load, rows, width):
    cols = []
    for j in range(width // LANES):
        pieces = [load(r * ROW_TILE * ROW_TILE + j, ROW_TILE, ROW_TILE) for r in range(rows // ROW_TILE)]
        cols.append(jnp.concatenate(pieces, axis=0))
    return jnp.concatenate(cols, axis=1)


def _dest_kernel(idx_ref, rank_ref, start_ref, o_ref):
    idx = idx_ref[...]
    rank = rank_ref[...]
    start = start_ref[...]
    tm = idx.shape[0]
    lane = lax.broadcasted_iota(jnp.int32, (tm, start.shape[1]), 1)
    out_lane = lax.broadcasted_iota(jnp.int32, (tm, LANES), 1)
    out = jnp.zeros((tm, LANES), jnp.int32)
    for k in range(TOP_K):
        base = jnp.sum(jnp.where(lane == idx[:, k:k + 1], start, 0.0), axis=-1, keepdims=True)
        out = jnp.where(out_lane == k, base.astype(jnp.int32) + rank[:, k:k + 1], out)
    o_ref[...] = out


def _dest(idx_pad, rank_pad, pad_start):
    t = idx_pad.shape[0]
    tm = DEST_TILE
    row = pl.BlockSpec((tm, LANES), lambda i: (i, 0))
    return pl.pallas_call(
        _dest_kernel,
        grid=(t // tm,),
        in_specs=[row, row, pl.BlockSpec((1, N_EXPERTS), lambda i: (0, 0))],
        out_specs=row,
        out_shape=jax.ShapeDtypeStruct((t, LANES), jnp.int32),
        compiler_params=_cparams(("arbitrary",)),
        name="dest",
    )(idx_pad, rank_pad, pad_start.astype(F32).reshape(1, N_EXPERTS))


def _scatter_kernel(b0_ref, nb_ref, nu_ref, dest_ref, hp_ref, xs_hbm, zbuf, sem, zsem):
    ts = hp_ref.shape[0]
    bm = zbuf.shape[0]
    n_blocks = xs_hbm.shape[0] // bm

    @pl.when(pl.program_id(0) == 0)
    def _():
        zbuf[...] = jnp.zeros_like(zbuf)
        n_used = nu_ref[0]

        def zero_block(g):
            return pltpu.make_async_copy(zbuf, xs_hbm.at[pl.ds(pl.multiple_of(g * bm, bm), bm), :], zsem)

        def tails(action):
            def step(e, carry):
                @pl.when(nb_ref[e] > 0)
                def _():
                    action(zero_block(b0_ref[e] + nb_ref[e] - 1))
                return carry
            lax.fori_loop(0, N_EXPERTS, step, 0)

        def unused(action):
            def step(g, carry):
                action(zero_block(g))
                return carry
            lax.fori_loop(n_used, n_blocks, step, 0)

        tails(lambda c: c.start())
        unused(lambda c: c.start())
        tails(lambda c: c.wait())
        unused(lambda c: c.wait())

    def body(t, carry):
        for k in range(TOP_K):
            dst = dest_ref[0, 0, t * TOP_K + k]
            pltpu.make_async_copy(hp_ref.at[pl.ds(t, 1), :], xs_hbm.at[pl.ds(dst, 1), :], sem).start(
                priority=k % 2)
        return carry

    lax.fori_loop(0, ts, body, 0)
    for k in range(TOP_K):
        pltpu.make_async_copy(hp_ref, xs_hbm.at[pl.ds(0, ts), :], sem).wait()


def _scatter(dest, hp, first_block, n_block, n_used, n_rows):
    t, half = hp.shape
    ts = SCATTER_TILE
    n_steps = t // ts
    dest3 = dest.reshape(n_steps, 1, ts * TOP_K)
    grid_spec = pltpu.PrefetchScalarGridSpec(
        num_scalar_prefetch=3,
        grid=(n_steps,),
        in_specs=[pl.BlockSpec((1, 1, ts * TOP_K), lambda i, b0, nb, nu: (i, 0, 0),
                               memory_space=pltpu.SMEM),
                  pl.BlockSpec((ts, half), lambda i, b0, nb, nu: (i, 0))],
        out_specs=pl.BlockSpec(memory_space=pl.ANY),
        scratch_shapes=[pltpu.VMEM((EXPERT_ROWS, half), jnp.uint32),
                        pltpu.SemaphoreType.DMA(()), pltpu.SemaphoreType.DMA(())],
    )
    return pl.pallas_call(
        _scatter_kernel,
        grid_spec=grid_spec,
        out_shape=jax.ShapeDtypeStruct((n_rows, half), jnp.uint32),
        compiler_params=_cparams(("arbitrary",)),
        name="scatter",
    )(first_block, n_block, n_used, dest3, hp)


def _expert_kernel(b0_ref, nb_ref, nu_ref, xs_hbm, w13_ref, w2_ref, y_hbm,
                   xbuf, ybuf, w13b, w2b, sem_in, sem_out):
    e = pl.program_id(0)
    n_used = nu_ref[0]
    b0 = b0_ref[e]
    nb = nb_ref[e]
    n_in, bm = xbuf.shape[:2]
    n_out, y_rows = ybuf.shape[:2]
    ahead = n_in - 1

    def in_copy(g, slot):
        return pltpu.make_async_copy(xs_hbm.at[pl.ds(pl.multiple_of(g * bm, bm), bm), :],
                                     xbuf.at[slot], sem_in.at[slot])

    def out_copy(g, slot):
        return pltpu.make_async_copy(ybuf.at[slot],
                                     y_hbm.at[pl.ds(pl.multiple_of(g * y_rows, y_rows), y_rows), :],
                                     sem_out.at[slot])

    @pl.when(nb > 0)
    def _():
        w13b[...] = w13_ref[0].astype(BF16)
        w2b[...] = w2_ref[0].astype(BF16)

    @pl.when((nb > 0) & (b0 == 0))
    def _():
        for a in range(ahead):
            @pl.when(a < n_used)
            def _():
                in_copy(a, a).start()

    def block(g, carry):
        in_slot = g % n_in
        out_slot = g % n_out
        in_copy(g, in_slot).wait()

        @pl.when(g + ahead < n_used)
        def _():
            in_copy(g + ahead, (g + ahead) % n_in).start()

        @pl.when(g >= n_out)
        def _():
            out_copy(g - n_out, out_slot).wait()

        w = xbuf[in_slot]
        lo = pltpu.bitcast(w << 16, F32).astype(BF16)
        hi = pltpu.bitcast(w & jnp.uint32(0xFFFF0000), F32).astype(BF16)
        half = lo.shape[1]
        ag = _dot(lo, w13b[:half, :]) + _dot(hi, w13b[half:, :])
        hh = ag.shape[1] // 2
        act = (_silu(ag[:, :hh]) * ag[:, hh:]).astype(BF16)
        _to_tiled_rows(ybuf.at[out_slot], _dot(act, w2b[...]))
        out_copy(g, out_slot).start()
        return carry

    lax.fori_loop(b0, b0 + nb, block, 0)

    @pl.when(e == pl.num_programs(0) - 1)
    def _():
        for back in range(1, n_out + 1):
            @pl.when(n_used >= back)
            def _():
                out_copy(n_used - back, (n_used - back) % n_out).wait()

        n_blocks = y_hbm.shape[0] // y_rows
        ybuf[0] = jnp.zeros((y_rows, LANES), F32)

        def start_zero(g, carry):
            out_copy(g, 0).start()
            return carry

        def wait_zero(g, carry):
            out_copy(g, 0).wait()
            return carry

        lax.fori_loop(n_used, n_blocks, start_zero, 0)
        lax.fori_loop(n_used, n_blocks, wait_zero, 0)


def _experts(xs, first_block, n_block, n_used, w13, w2):
    n_rows, half = xs.shape
    ne, d, h2 = w13.shape
    bm = EXPERT_ROWS
    y_rows = bm * d // LANES
    grid_spec = pltpu.PrefetchScalarGridSpec(
        num_scalar_prefetch=3,
        grid=(ne,),
        in_specs=[
            pl.BlockSpec(memory_space=pl.ANY),
            pl.BlockSpec((1, d, h2), lambda e, b0, nb, nu: (e, 0, 0)),
            pl.BlockSpec((1, h2 // 2, d), lambda e, b0, nb, nu: (e, 0, 0)),
        ],
        out_specs=pl.BlockSpec(memory_space=pl.ANY),
        scratch_shapes=[pltpu.VMEM((EXPERT_IN_SLOTS, bm, half), jnp.uint32),
                        pltpu.VMEM((EXPERT_OUT_SLOTS, y_rows, LANES), F32),
                        pltpu.VMEM((d, h2), BF16),
                        pltpu.VMEM((h2 // 2, d), BF16),
                        pltpu.SemaphoreType.DMA((EXPERT_IN_SLOTS,)),
                        pltpu.SemaphoreType.DMA((EXPERT_OUT_SLOTS,))],
    )
    return pl.pallas_call(
        _expert_kernel,
        grid_spec=grid_spec,
        out_shape=jax.ShapeDtypeStruct((n_rows * d // LANES, LANES), F32),
        compiler_params=_cparams(("arbitrary",)),
        name="experts",
    )(first_block, n_block, n_used, xs, w13, w2)


def _final_kernel(dc_ref, dn_ref, y_hbm, w_ref, x_ref, mod_ref, gpre_ref, gpost_ref, w13s_ref,
                  w2s_ref, o_ref, buf, sem):
    i = pl.program_id(0)
    slot = i % 2
    tm = x_ref.shape[0]

    def issue(d_ref, s):
        _gather_rows(d_ref, tm * TOP_K, y_hbm,
                     lambda q, u: buf.at[s, u, _tile_rows(q), :], sem.at[s])

    @pl.when(i == 0)
    def _():
        issue(dc_ref, 0)

    @pl.when(i + 1 < pl.num_programs(0))
    def _():
        issue(dn_ref, 1 - slot)

    for k in range(TOP_K):
        pltpu.make_async_copy(y_hbm.at[pl.ds(0, tm * ROW_TILE), :], buf.at[slot, k],
                              sem.at[slot]).wait()

    w = w_ref[...]
    x1 = x_ref[...]
    routed = None
    for k in range(TOP_K):
        yk = _from_tiled_rows(lambda a, n, st: buf[slot, k, pl.ds(a, n, stride=st), :], tm, x1.shape[1])
        routed = yk * w[:, k:k + 1] if routed is None else routed + yk * w[:, k:k + 1]

    m = mod_ref[0]
    h = (_rms(x1, gpre_ref[...]) * (1.0 + m[4:5]) + m[3:4]).astype(BF16)
    ag = _dot(h, w13s_ref[...])
    hh = ag.shape[1] // 2
    act = (_silu(ag[:, :hh]) * ag[:, hh:]).astype(BF16)
    ffn = routed + _dot(act, w2s_ref[...])
    o_ref[...] = x1 + m[5:6] * _rms(ffn, gpost_ref[...])


def _final(dest, y_sorted, top_w, x1, mod, gpre, gpost, w13s, w2s, seq):
    t, d = x1.shape
    tm = COMBINE_TILE
    per_b = seq // tm
    n_steps = t // tm
    dest3 = dest.reshape(n_steps, 1, tm * TOP_K)
    full = lambda a: pl.BlockSpec(a.shape, lambda i: (0,) * a.ndim)
    row = lambda n: pl.BlockSpec((tm, n), lambda i: (i, 0))
    return pl.pallas_call(
        _final_kernel,
        grid=(n_steps,),
        in_specs=[
            pl.BlockSpec((1, 1, tm * TOP_K), lambda i: (i, 0, 0), memory_space=pltpu.SMEM),
            pl.BlockSpec((1, 1, tm * TOP_K), lambda i: (jnp.minimum(i + 1, n_steps - 1), 0, 0),
                         memory_space=pltpu.SMEM),
            pl.BlockSpec(memory_space=pl.ANY),
            row(LANES), row(d), pl.BlockSpec((1, 6, d), lambda i: (i // per_b, 0, 0)),
            full(gpre), full(gpost), full(w13s), full(w2s)],
        out_specs=row(d),
        out_shape=jax.ShapeDtypeStruct((t, d), F32),
        scratch_shapes=[pltpu.VMEM((2, TOP_K, tm * ROW_TILE, LANES), F32),
                        pltpu.SemaphoreType.DMA((2,))],
        compiler_params=_cparams(("arbitrary",)),
        name="combine_final",
    )(dest3, dest3, y_sorted, top_w, x1, mod, gpre, gpost, w13s, w2s)


def _pad_heads(v):
    lead = v.shape[:-1]
    v = v.reshape(lead + (2, SSD_GROUPS, HEADS_PER_GROUP))
    v = jnp.pad(v, [(0, 0)] * (len(lead) + 2) + [(0, HEADS_PADDED - HEADS_PER_GROUP)])
    return v.reshape(lead + (2 * SSD_GROUPS * HEADS_PADDED,))


def _dispatch_plan(counts, n_tokens):
    bm = EXPERT_ROWS
    counts = counts.reshape(N_EXPERTS).astype(jnp.int32)
    padded = (counts + bm - 1) // bm * bm
    pad_end = jnp.cumsum(padded)
    pad_start = pad_end - padded
    n_rows = -(-n_tokens * TOP_K // bm) * bm + N_EXPERTS * bm
    first_block = (pad_start // bm).astype(jnp.int32)
    n_block = (padded // bm).astype(jnp.int32)
    n_used = (pad_end[-1] // bm).astype(jnp.int32).reshape(1)
    return pad_start, first_block, n_block, n_used, n_rows


def _layer(x, c, w_ada, b_ada, pre_norm_mix, post_norm_mix, pre_norm_ffn, post_norm_ffn, w_in,
           conv_w, conv_b, dt_bias_fwd, dt_bias_bwd, a_log_fwd, a_log_bwd, d_skip, ssd_norm,
           w_branch_ssd, w_branch_fourier, w_out, w_router, router_bias, w13_experts, w2_experts,
           w13_shared, w2_shared):
    bsz, seq, d = x.shape
    t = bsz * seq
    x2 = x.reshape(t, d)
    row = lambda v: v.reshape(1, -1).astype(F32)

    mod = _ada(c, w_ada, b_ada)

    i1 = SSD_INNER
    i2 = i1 + XBC_WIDTH
    i3 = i2 + 2 * SSD_HEADS
    i4 = i3 + FOURIER_WIDTH
    n_dt = 2 * SSD_GROUPS * HEADS_PADDED
    w_dtp = _pad_heads(w_in[:, i2:i3])
    wdtc = jnp.pad(w_dtp, ((0, 0), (0, LANES - n_dt))).astype(BF16)
    wdtT = w_dtp.T.astype(BF16)
    bias_p = _pad_heads(jnp.concatenate([dt_bias_fwd, dt_bias_bwd]).astype(F32))
    a_p = _pad_heads(-jnp.exp(jnp.concatenate([a_log_fwd, a_log_bwd]).astype(F32)))
    pad_row = lambda v: jnp.pad(v, (0, LANES - n_dt)).reshape(1, LANES)
    z, xbc, acsc, dtT, acsT, uf, gates = _inproj(
        x2, mod, row(pre_norm_mix), w_in[:, :i1].astype(BF16), w_in[:, i1:i2].astype(BF16),
        wdtc, wdtT, w_in[:, i3:i4].astype(BF16), w_in[:, i4:].astype(BF16),
        pad_row(bias_p), pad_row(a_p), bias_p.reshape(n_dt, 1), a_p.reshape(n_dt, 1), seq)

    xbc3 = _conv(xbc.reshape(bsz, seq, XBC_WIDTH), conv_w, conv_b)

    yf, yb = _ssd(xbc3, dtT, acsT, acsc, row(jnp.repeat(d_skip, SSD_HEAD_DIM)))

    fm = _fourier(uf.reshape(bsz, seq, FOURIER_WIDTH))

    x1 = _mix(yf.reshape(t, SSD_INNER), yb.reshape(t, SSD_INNER), z,
              fm.reshape(t, FOURIER_WIDTH), gates, x2, mod,
              row(ssd_norm), w_branch_ssd.astype(BF16),
              w_branch_fourier.astype(BF16), w_out.astype(BF16), row(post_norm_mix), seq)

    hp, idx_pad, w_pad, rank_pad, counts = _route(x1, mod, row(pre_norm_ffn),
                                                  w_router.astype(F32), router_bias, seq)
    pad_start, first_block, n_block, n_used, n_rows = _dispatch_plan(counts, t)
    dest = _dest(idx_pad, rank_pad, pad_start)[:, :TOP_K]
    xs = _scatter(dest, hp, first_block, n_block, n_used, n_rows)
    y_sorted = _experts(xs, first_block, n_block, n_used, w13_experts, w2_experts)
    out = _final(dest, y_sorted, w_pad, x1, mod, row(pre_norm_ffn), row(post_norm_ffn),
                 w13_shared.astype(BF16), w2_shared.astype(BF16), seq)
    return out.reshape(bsz, seq, d)


def kernel(x, c, w_ada, b_ada, pre_norm_mix, post_norm_mix, pre_norm_ffn, post_norm_ffn, w_in,
           conv_w, conv_b, dt_bias_fwd, dt_bias_bwd, a_log_fwd, a_log_bwd, d_skip, ssd_norm,
           w_branch_ssd, w_branch_fourier, w_out, w_router, router_bias, w13_experts, w2_experts,
           w13_shared, w2_shared):
    for layer in range(w_ada.shape[0]):
        x = _layer(x, c, w_ada[layer], b_ada[layer], pre_norm_mix[layer], post_norm_mix[layer],
                   pre_norm_ffn[layer], post_norm_ffn[layer], w_in[layer], conv_w[layer],
                   conv_b[layer], dt_bias_fwd[layer], dt_bias_bwd[layer], a_log_fwd[layer],
                   a_log_bwd[layer], d_skip[layer], ssd_norm[layer], w_branch_ssd[layer],
                   w_branch_fourier[layer], w_out[layer], w_router[layer], router_bias[layer],
                   w13_experts[layer], w2_experts[layer], w13_shared[layer], w2_shared[layer])
    return x
```

```python
import functools
import math

import numpy as np
import jax
import jax.numpy as jnp
from jax import lax
from jax.experimental import pallas as pl
from jax.experimental.pallas import tpu as pltpu

F32 = jnp.float32
BF16 = jnp.bfloat16
HIGHEST = lax.Precision.HIGHEST

D_MODEL = 1024
SSD_HEADS = 24
SSD_HEAD_DIM = 64
SSD_INNER = SSD_HEADS * SSD_HEAD_DIM
SSD_GROUPS = 4
HEADS_PER_GROUP = SSD_HEADS // SSD_GROUPS
HEADS_PADDED = 8
SSD_STATE = 128
SSD_CONV = 5
SSD_CHUNK = 128
XBC_WIDTH = SSD_INNER + 2 * SSD_GROUPS * SSD_STATE
GROUP_X = HEADS_PER_GROUP * SSD_HEAD_DIM
FOURIER_WIDTH = 512
FOURIER_GROUP_DIM = 128
N_EXPERTS = 256
TOP_K = 8
N_EXPERT_GROUPS = 8
TOPK_GROUPS = 4
EXPERT_HIDDEN = 256
SHARED_HIDDEN = 256
ROUTED_SCALE = 2.5
RMS_EPS = 1e-6
LOG2_E = math.log2(math.e)

LANES = 128
VMEM_LIMIT = 56 * 1024 * 1024
TOKEN_TILE = 256
INPROJ_TILE = 512
INPROJ_SUBTILE = 256
EXPERT_ROWS = 256
COMBINE_TILE = 256
SCATTER_TILE = 512
DEST_TILE = 1024
EXPERT_IN_SLOTS = 6
EXPERT_OUT_SLOTS = 4
CONV_ROWS = 256


def _cparams(sem):
    return pltpu.CompilerParams(dimension_semantics=sem, vmem_limit_bytes=VMEM_LIMIT)


def _dot(a, b, precision=None):
    return jnp.dot(a, b, preferred_element_type=F32, precision=precision)


def _dot_nt(a, b, precision=None):
    return lax.dot_general(a, b, (((1,), (1,)), ((), ())), preferred_element_type=F32,
                           precision=precision)


def _dot_tn(a, b):
    return lax.dot_general(a, b, (((0,), (0,)), ((), ())), preferred_element_type=F32)


def _sigmoid(x):
    return 1.0 / (1.0 + jnp.exp(-x))


def _silu(x):
    return x * _sigmoid(x)


def _softplus(x):
    return jnp.maximum(x, 0.0) + jnp.log1p(jnp.exp(-jnp.abs(x)))


def _rms(x, g):
    return x * lax.rsqrt(jnp.mean(x * x, axis=-1, keepdims=True) + RMS_EPS) * g


def _ada_kernel(c_ref, w_ref, b_ref, o_ref):
    o_ref[...] = _dot(_silu(c_ref[...]), w_ref[...], HIGHEST) + b_ref[...]


def _ada(c, w_ada, b_ada):
    bsz, d = c.shape
    rows = 8
    cp = jnp.zeros((rows, d), F32).at[:bsz].set(c)
    n = w_ada.shape[1]
    tn = 1536
    out = pl.pallas_call(
        _ada_kernel,
        grid=(n // tn,),
        in_specs=[pl.BlockSpec((rows, d), lambda j: (0, 0)),
                  pl.BlockSpec((d, tn), lambda j: (0, j)),
                  pl.BlockSpec((1, tn), lambda j: (0, j))],
        out_specs=pl.BlockSpec((rows, tn), lambda j: (0, j)),
        out_shape=jax.ShapeDtypeStruct((rows, n), F32),
        compiler_params=_cparams(("arbitrary",)),
        name="adaln",
    )(cp, w_ada, b_ada.reshape(1, n))
    return out[:bsz].reshape(bsz, 6, d)


def _split3(a):
    a1 = a.astype(BF16)
    r1 = a - a1.astype(F32)
    a2 = r1.astype(BF16)
    a3 = (r1 - a2.astype(F32)).astype(BF16)
    return a1, a2, a3


def _inproj_kernel(x_ref, mod_ref, g_ref, wz_ref, wxbc_ref, wdtc_ref, wdtT_ref, wuf_ref, wg_ref,
                   bias_row_ref, a_row_ref, bias_col_ref, a_col_ref,
                   z_ref, xbc_ref, acsc_ref, dtT_ref, acsT_ref, uf_ref, gates_ref):
    m = mod_ref[0]
    tm = INPROJ_SUBTILE
    n_fwd = SSD_GROUPS * HEADS_PADDED
    ii = lax.broadcasted_iota(jnp.int32, (tm, tm), 0)
    jj = lax.broadcasted_iota(jnp.int32, (tm, tm), 1)
    same = (ii // SSD_CHUNK) == (jj // SSD_CHUNK)
    tri_f = (same & (jj <= ii)).astype(BF16)
    tri_b = (same & (jj >= ii)).astype(BF16)

    for s in range(x_ref.shape[0] // tm):
        rows = slice(s * tm, (s + 1) * tm)
        h = _rms(x_ref[rows, :], g_ref[...]) * (1.0 + m[1:2]) + m[0:1]
        hb = h.astype(BF16)
        z_ref[rows, :] = _dot(hb, wz_ref[...]).astype(z_ref.dtype)
        xbc_ref[rows, :] = _dot(hb, wxbc_ref[...]).astype(xbc_ref.dtype)
        uf_ref[rows, :] = _dot(hb, wuf_ref[...]).astype(uf_ref.dtype)
        gates_ref[rows, :] = _dot(hb, wg_ref[...]).astype(gates_ref.dtype)

        dt_c = _softplus(_dot(hb, wdtc_ref[...]) + bias_row_ref[...])
        pieces = _split3(dt_c * a_row_ref[...])
        acs_f = sum(_dot(tri_f, p) for p in pieces)
        acs_b = sum(_dot(tri_b, p) for p in pieces)
        lane = lax.broadcasted_iota(jnp.int32, acs_f.shape, 1)
        acsc_ref[rows, :] = jnp.where(lane < n_fwd, acs_f, acs_b) * LOG2_E

        dt_t = _softplus(_dot_nt(wdtT_ref[...], hb) + bias_col_ref[...])
        pieces = _split3(dt_t * a_col_ref[...])
        acs_f = sum(_dot_nt(p, tri_f) for p in pieces)
        acs_b = sum(_dot_nt(p, tri_b) for p in pieces)
        sub = lax.broadcasted_iota(jnp.int32, acs_f.shape, 0)
        dtT_ref[:, rows] = dt_t
        acsT_ref[:, rows] = jnp.where(sub < n_fwd, acs_f, acs_b) * LOG2_E


def _inproj(x2, mod, g, wz, wxbc, wdtc, wdtT, wuf, wg, bias_row, a_row, bias_col, a_col, seq):
    t, d = x2.shape
    tm = INPROJ_TILE
    per_b = seq // tm
    full = lambda a: pl.BlockSpec(a.shape, lambda i: (0,) * a.ndim)
    row = lambda n: pl.BlockSpec((tm, n), lambda i: (i, 0))
    nd = wdtT.shape[0]
    colspec = pl.BlockSpec((nd, tm), lambda i: (0, i))
    return pl.pallas_call(
        _inproj_kernel,
        grid=(t // tm,),
        in_specs=[row(d), pl.BlockSpec((1, 6, d), lambda i: (i // per_b, 0, 0)), full(g),
                  full(wz), full(wxbc), full(wdtc), full(wdtT), full(wuf), full(wg),
                  full(bias_row), full(a_row), full(bias_col), full(a_col)],
        out_specs=[row(wz.shape[1]), row(wxbc.shape[1]), row(LANES), colspec, colspec,
                   row(wuf.shape[1]), row(wg.shape[1])],
        out_shape=[jax.ShapeDtypeStruct((t, wz.shape[1]), BF16),
                   jax.ShapeDtypeStruct((t, wxbc.shape[1]), BF16),
                   jax.ShapeDtypeStruct((t, LANES), F32),
                   jax.ShapeDtypeStruct((nd, t), F32),
                   jax.ShapeDtypeStruct((nd, t), F32),
                   jax.ShapeDtypeStruct((t, wuf.shape[1]), BF16),
                   jax.ShapeDtypeStruct((t, wg.shape[1]), BF16)],
        compiler_params=_cparams(("arbitrary",)),
        name="inproj",
    )(x2, mod, g, wz, wxbc, wdtc, wdtT, wuf, wg, bias_row, a_row, bias_col, a_col)


def _conv_kernel(u_ref, w_ref, b_ref, o_ref, pad_ref):
    s = u_ref.shape[1]
    halo = 8
    pad_ref[0:halo, :] = jnp.zeros((halo, LANES), F32)
    pad_ref[halo + s:2 * halo + s, :] = jnp.zeros((halo, LANES), F32)
    pad_ref[halo:halo + s, :] = u_ref[0].astype(F32)
    w = w_ref[...]
    b = b_ref[...]
    half = (SSD_CONV - 1) // 2
    for r in range(s // CONV_ROWS):
        base = r * CONV_ROWS
        acc = b
        for k in range(SSD_CONV):
            lo = base + halo + k - half
            acc = acc + w[k:k + 1, :] * pad_ref[lo:lo + CONV_ROWS, :]
        o_ref[0, base:base + CONV_ROWS, :] = _silu(acc).astype(o_ref.dtype)


def _conv(xbc3, conv_w, conv_b):
    bsz, s, c = xbc3.shape
    return pl.pallas_call(
        _conv_kernel,
        grid=(bsz, c // LANES),
        in_specs=[pl.BlockSpec((1, s, LANES), lambda b, j: (b, 0, j)),
                  pl.BlockSpec((SSD_CONV, LANES), lambda b, j: (0, j)),
                  pl.BlockSpec((1, LANES), lambda b, j: (0, j))],
        out_specs=pl.BlockSpec((1, s, LANES), lambda b, j: (b, 0, j)),
        out_shape=jax.ShapeDtypeStruct((bsz, s, c), BF16),
        scratch_shapes=[pltpu.VMEM((s + 16, LANES), F32)],
        compiler_params=_cparams(("arbitrary", "arbitrary")),
        name="conv",
    )(xbc3, conv_w, conv_b.reshape(1, c))


HEAD_PAIRS = HEADS_PER_GROUP // 2
SSD_GROUPS_PER_STEP = 4


def _ssd_direction(x, bm, cm, dt_r, acs_r, acsc_all, lane_off, s_ref, reverse):
    L, N = bm.shape
    assert L == N == LANES
    ii = lax.broadcasted_iota(jnp.int32, (L, L), 0)
    jj = lax.broadcasted_iota(jnp.int32, (L, L), 1)
    mask = (jj >= ii) if reverse else (jj <= ii)
    lo_half = jj < SSD_HEAD_DIM
    shift = jnp.where(lane_off == 0, 0, LANES - lane_off)
    acs_c = pltpu.roll(acsc_all, shift, 1)
    last = 0 if reverse else L - 1
    tot_r = acs_r[:, last:last + 1]
    w_r = jnp.exp2(tot_r - acs_r) * dt_r
    src_r = acs_r - jnp.log2(dt_r)
    etot = jnp.broadcast_to(jnp.exp2(tot_r), (HEADS_PADDED, LANES))

    cm16 = cm.astype(BF16)
    cbt = _dot_nt(cm16, bm.astype(BF16))
    bt = bm.astype(F32).T
    s_all = [s_ref[q] for q in range(HEAD_PAIRS)]
    cs_all = _dot(cm16, jnp.concatenate(s_all, axis=1).astype(BF16))
    ys = []
    for q in range(HEAD_PAIRS):
        xq = x[:, q * LANES:(q + 1) * LANES]
        x_a = jnp.where(lo_half, xq, jnp.zeros_like(xq)).astype(BF16)
        x_b = jnp.where(lo_half, jnp.zeros_like(xq), xq).astype(BF16)
        m_parts, b_parts, e_cols = [], [], []
        for h in (2 * q, 2 * q + 1):
            col = jnp.broadcast_to(acs_c[:, h:h + 1], (L, L))
            decay_dt = jnp.exp2(jnp.where(mask, col - src_r[h:h + 1, :], -jnp.inf))
            m_parts.append((cbt * decay_dt).astype(BF16))
            e_cols.append(jnp.exp2(col))
            b_parts.append((bt * w_r[h:h + 1, :]).astype(BF16))
        x_diag = jnp.concatenate([x_a, x_b], axis=0)
        y_off = cs_all[:, q * LANES:(q + 1) * LANES] * jnp.where(lo_half, e_cols[0], e_cols[1])
        ys.append(_dot(jnp.concatenate(m_parts, axis=1), x_diag) + y_off)
        dec = jnp.where(lo_half[0:1], etot[2 * q:2 * q + 1], etot[2 * q + 1:2 * q + 2])
        s_ref[q] = s_all[q] * dec + _dot(jnp.concatenate(b_parts, axis=1), x_diag)
    return jnp.concatenate(ys, axis=1)


def _ssd_kernel(xf_ref, bf_ref, cf_ref, dtTf_ref, acsTf_ref, acscf_ref,
                xb_ref, bb_ref, cb_ref, dtTb_ref, acsTb_ref, acscb_ref, dskip_ref,
                yf_ref, yb_ref, sf_ref, sb_ref):
    gp = pl.program_id(1)

    @pl.when(pl.program_id(2) == 0)
    def _():
        sf_ref[...] = jnp.zeros_like(sf_ref)
        sb_ref[...] = jnp.zeros_like(sb_ref)

    for i in range(SSD_GROUPS_PER_STEP):
        g = gp * SSD_GROUPS_PER_STEP + i
        xs = slice(i * GROUP_X, (i + 1) * GROUP_X)
        ns = slice(i * SSD_STATE, (i + 1) * SSD_STATE)
        hs = slice(i * HEADS_PADDED, (i + 1) * HEADS_PADDED)
        y_fwd = _ssd_direction(
            xf_ref[0, :, xs], bf_ref[0, :, ns], cf_ref[0, :, ns], dtTf_ref[hs, :], acsTf_ref[hs, :],
            acscf_ref[...], g * HEADS_PADDED, sf_ref.at[i], False)
        y_fwd = y_fwd + dskip_ref[:, xs] * xf_ref[0, :, xs].astype(F32)
        yf_ref[0, :, xs] = y_fwd.astype(yf_ref.dtype)
        yb_ref[0, :, xs] = _ssd_direction(
            xb_ref[0, :, xs], bb_ref[0, :, ns], cb_ref[0, :, ns], dtTb_ref[hs, :], acsTb_ref[hs, :],
            acscb_ref[...], (SSD_GROUPS + g) * HEADS_PADDED, sb_ref.at[i], True).astype(yb_ref.dtype)


def _ssd(xbc3, dtT, acsT, acsc, dskip):
    bsz, s, _ = xbc3.shape
    L = SSD_CHUNK
    nc = s // L
    P = SSD_GROUPS_PER_STEP
    GP = SSD_GROUPS // P
    nb = SSD_INNER // (P * SSD_STATE)
    ncb = nb + GP
    fwd = lambda c: c
    bwd = lambda c: nc - 1 - c

    def specs(cidx, dirn):
        rowspec = pl.BlockSpec((P * HEADS_PADDED, L),
                               lambda b, g, c: (dirn * GP + g, b * nc + cidx(c)))
        return [
            pl.BlockSpec((1, L, P * GROUP_X), lambda b, g, c: (b, cidx(c), g)),
            pl.BlockSpec((1, L, P * SSD_STATE), lambda b, g, c: (b, cidx(c), nb + g)),
            pl.BlockSpec((1, L, P * SSD_STATE), lambda b, g, c: (b, cidx(c), ncb + g)),
            rowspec, rowspec,
            pl.BlockSpec((L, LANES), lambda b, g, c: (b * nc + cidx(c), 0)),
        ]

    out_specs = [pl.BlockSpec((1, L, P * GROUP_X), lambda b, g, c: (b, c, g)),
                 pl.BlockSpec((1, L, P * GROUP_X), lambda b, g, c: (b, nc - 1 - c, g))]
    return pl.pallas_call(
        _ssd_kernel,
        grid=(bsz, GP, nc),
        in_specs=specs(fwd, 0) + specs(bwd, 1)
        + [pl.BlockSpec((1, P * GROUP_X), lambda b, g, c: (0, g))],
        out_specs=out_specs,
        out_shape=[jax.ShapeDtypeStruct((bsz, s, SSD_INNER), BF16)] * 2,
        scratch_shapes=[pltpu.VMEM((P, HEAD_PAIRS, SSD_STATE, LANES), F32)] * 2,
        compiler_params=_cparams(("arbitrary", "arbitrary", "arbitrary")),
        name="ssd",
    )(xbc3, xbc3, xbc3, dtT, acsT, acsc, xbc3, xbc3, xbc3, dtT, acsT, acsc, dskip)


DFT_RADIX = 8
DFT_ROWS = 256


def _dft_tables(seq):
    m_len = seq // DFT_RADIX
    mm = np.arange(m_len)
    a0 = 2.0 * np.pi * np.outer(mm, mm) / m_len
    f0 = np.concatenate([np.cos(a0), -np.sin(a0)], axis=0)
    ang = -2.0 * np.pi * np.outer(np.arange(DFT_RADIX), mm) / seq
    tw = np.zeros((DFT_RADIX, m_len, LANES))
    tw[:, :, 0] = np.cos(ang)
    tw[:, :, 1] = np.sin(ang)
    ch = np.arange(FOURIER_GROUP_DIM)
    ac = 2.0 * np.pi * np.outer(ch, ch) / FOURIER_GROUP_DIM
    scale = 1.0 / math.sqrt(seq * FOURIER_GROUP_DIM)
    fc = np.concatenate([np.cos(ac), np.sin(ac)], axis=0) * scale
    return (jnp.asarray(f0, BF16), jnp.asarray(tw, F32), jnp.asarray(fc, BF16))


def _fourier_kernel(u_ref, f0_ref, tw_ref, fc_ref, o_ref, x32, tbuf):
    m_len = f0_ref.shape[1]
    x32[...] = u_ref[0].astype(F32)

    def sub_dft(p, carry):
        xa = x32[pl.ds(2 * p, m_len, stride=DFT_RADIX), :]
        xb = x32[pl.ds(2 * p + 1, m_len, stride=DFT_RADIX), :]
        xr = jnp.concatenate([xa, xb], axis=1).astype(BF16)
        tbuf[p] = _dot(f0_ref[...], xr)
        return carry

    lax.fori_loop(0, DFT_RADIX // 2, sub_dft, 0)
    rows = min(DFT_ROWS, m_len)
    lanes_of = lambda r: slice((r % 2) * LANES, (r % 2 + 1) * LANES)

    def combine(c, carry):
        sl = pl.ds(pl.multiple_of(c * rows, rows), rows)
        si = pl.ds(pl.multiple_of(m_len + c * rows, rows), rows)
        for r in range(1, DFT_RADIX):
            tre = tbuf[r // 2, sl, lanes_of(r)]
            tim = tbuf[r // 2, si, lanes_of(r)]
            tw = tw_ref[r, sl, :]
            gre = jnp.broadcast_to(tw[:, 0:1], (rows, LANES))
            gim = jnp.broadcast_to(tw[:, 1:2], (rows, LANES))
            tbuf[r // 2, sl, lanes_of(r)] = gre * tre - gim * tim
            tbuf[r // 2, si, lanes_of(r)] = gre * tim + gim * tre
        for kr in range(DFT_RADIX):
            zr = tbuf[0, sl, lanes_of(0)]
            zi = tbuf[0, si, lanes_of(0)]
            for r in range(1, DFT_RADIX):
                tre = tbuf[r // 2, sl, lanes_of(r)]
                tim = tbuf[r // 2, si, lanes_of(r)]
                quarter, odd = divmod((r * kr) % DFT_RADIX, 2)
                for _ in range(quarter):
                    tre, tim = tim, -tre
                if odd:
                    tre, tim = (tre + tim) * math.sqrt(0.5), (tim - tre) * math.sqrt(0.5)
                zr = zr + tre
                zi = zi + tim
            zz = jnp.concatenate([zr, zi], axis=1).astype(BF16)
            out_rows = pl.ds(pl.multiple_of(kr * m_len + c * rows, rows), rows)
            o_ref[0, out_rows, :] = _dot(zz, fc_ref[...])
        return carry

    lax.fori_loop(0, m_len // rows, combine, 0)


def _fourier(uf3):
    bsz, s, w = uf3.shape
    f0, tw, fc = _dft_tables(s)
    m_len = s // DFT_RADIX
    ng = w // FOURIER_GROUP_DIM
    return pl.pallas_call(
        _fourier_kernel,
        grid=(bsz, ng),
        in_specs=[pl.BlockSpec((1, s, LANES), lambda b, j: (b, 0, j)),
                  pl.BlockSpec(f0.shape, lambda b, j: (0, 0), pipeline_mode=pl.Buffered(1)),
                  pl.BlockSpec(tw.shape, lambda b, j: (0, 0, 0), pipeline_mode=pl.Buffered(1)),
                  pl.BlockSpec(fc.shape, lambda b, j: (0, 0), pipeline_mode=pl.Buffered(1))],
        out_specs=pl.BlockSpec((1, s, LANES), lambda b, j: (b, 0, j)),
        out_shape=jax.ShapeDtypeStruct((bsz, s, w), F32),
        scratch_shapes=[pltpu.VMEM((s, LANES), F32),
                        pltpu.VMEM((DFT_RADIX // 2, 2 * m_len, 2 * LANES), F32)],
        compiler_params=_cparams(("arbitrary", "arbitrary")),
        name="fourier",
    )(uf3, f0, tw, fc)


def _mix_kernel(yf_ref, yb_ref, z_ref, fm_ref, gates_ref, x_ref, mod_ref,
                ssdn_ref, wbs_ref, wbf_ref, wout_ref, gpost_ref, o_ref):
    m = mod_ref[0]
    y = yf_ref[...].astype(F32) + yb_ref[...].astype(F32)
    v = y * _silu(z_ref[...].astype(F32))
    parts = []
    for g in range(SSD_GROUPS):
        vg = v[:, g * GROUP_X:(g + 1) * GROUP_X]
        parts.append(vg * lax.rsqrt(jnp.mean(vg * vg, axis=-1, keepdims=True) + RMS_EPS))
    vn = jnp.concatenate(parts, axis=1) * ssdn_ref[...]
    y_ssd = _dot(vn.astype(BF16), wbs_ref[...])
    y_fou = _dot(fm_ref[...].astype(BF16), wbf_ref[...])
    gt = _sigmoid(gates_ref[...].astype(F32))
    d = y_ssd.shape[1]
    mixed = gt[:, :d] * y_ssd + gt[:, d:] * y_fou
    mo = _dot(mixed.astype(BF16), wout_ref[...])
    o_ref[...] = x_ref[...] + m[2:3] * _rms(mo, gpost_ref[...])


def _mix(yf, yb, z, fm, gates, x2, mod, ssdn, wbs, wbf, wout, gpost, seq):
    t, d = x2.shape
    tm = TOKEN_TILE
    per_b = seq // tm
    full = lambda a: pl.BlockSpec(a.shape, lambda i: (0,) * a.ndim)
    row = lambda n: pl.BlockSpec((tm, n), lambda i: (i, 0))
    return pl.pallas_call(
        _mix_kernel,
        grid=(t // tm,),
        in_specs=[row(SSD_INNER), row(SSD_INNER), row(SSD_INNER),
                  row(FOURIER_WIDTH), row(2 * d), row(d),
                  pl.BlockSpec((1, 6, d), lambda i: (i // per_b, 0, 0)),
                  full(ssdn), full(wbs), full(wbf), full(wout), full(gpost)],
        out_specs=row(d),
        out_shape=jax.ShapeDtypeStruct((t, d), F32),
        compiler_params=_cparams(("arbitrary",)),
        name="mix",
    )(yf, yb, z, fm, gates, x2, mod, ssdn, wbs, wbf, wout, gpost)


def _route_kernel(x_ref, mod_ref, gpre_ref, wr_ref, rb_ref, hp_ref, idx_ref, w_ref, rank_ref,
                  counts_ref, run_ref):
    @pl.when(pl.program_id(0) == 0)
    def _():
        run_ref[...] = jnp.zeros_like(run_ref)

    m = mod_ref[0]
    h = _rms(x_ref[...], gpre_ref[...]) * (1.0 + m[4:5]) + m[3:4]
    tm, d = h.shape
    half = d // 2
    bits = pltpu.bitcast(h.astype(BF16).astype(F32), jnp.uint32)
    hp_ref[...] = (bits[:, :half] >> 16) | (bits[:, half:] & jnp.uint32(0xFFFF0000))

    h_hi = h.astype(BF16)
    h_lo = (h - h_hi.astype(F32)).astype(BF16)
    w_hi = wr_ref[0]
    logits = _dot(h_hi, w_hi) + (_dot(h_hi, wr_ref[1]) + _dot(h_lo, w_hi))
    scores = _sigmoid(logits)
    biased = scores + rb_ref[...]
    ne = scores.shape[1]
    per_group = ne // N_EXPERT_GROUPS
    lane = lax.broadcasted_iota(jnp.int32, (tm, ne), 1)
    lane_f = lane.astype(F32)
    grp = lane // per_group
    out_lane = lax.broadcasted_iota(jnp.int32, (tm, LANES), 1)
    neg = -jnp.inf

    def argmax_first(v):
        mx = jnp.max(v, axis=-1, keepdims=True)
        ix = jnp.min(jnp.where(v == mx, lane_f, float(ne)), axis=-1, keepdims=True)
        return mx, ix

    gs = []
    gmat = jnp.full((tm, LANES), neg, F32)
    for g in range(N_EXPERT_GROUPS):
        vg = jnp.where(grp == g, biased, neg)
        m1, i1 = argmax_first(vg)
        m2 = jnp.max(jnp.where(lane_f == i1, neg, vg), axis=-1, keepdims=True)
        gs.append(m1 + m2)
        gmat = jnp.where(out_lane == g, gs[g], gmat)
    ahead = jnp.zeros((tm, LANES), F32)
    for o in range(N_EXPERT_GROUPS):
        beats = (gs[o] > gmat) | ((gs[o] == gmat) & (out_lane > o))
        ahead = ahead + beats.astype(F32)
    sel = ((ahead < TOPK_GROUPS) & (out_lane < N_EXPERT_GROUPS)).astype(BF16)
    eg = lax.broadcasted_iota(jnp.int32, (LANES, ne), 0)
    ee = lax.broadcasted_iota(jnp.int32, (LANES, ne), 1)
    allowed = _dot(sel, (eg == ee // per_group).astype(BF16)) > 0.5
    masked = jnp.where(allowed, biased, neg)

    idx_out = jnp.zeros((tm, LANES), F32)
    w_out = jnp.zeros((tm, LANES), F32)
    w_sum = jnp.zeros((tm, 1), F32)
    hits = []
    for k in range(TOP_K):
        _, ik = argmax_first(masked)
        hit = lane_f == ik
        hits.append(hit)
        wk = jnp.sum(jnp.where(hit, scores, 0.0), axis=-1, keepdims=True)
        masked = jnp.where(hit, neg, masked)
        idx_out = jnp.where(out_lane == k, ik, idx_out)
        w_out = jnp.where(out_lane == k, wk, w_out)
        w_sum = w_sum + wk
    idx_ref[...] = idx_out.astype(jnp.int32)
    w_ref[...] = w_out / w_sum * ROUTED_SCALE

    chosen = functools.reduce(jnp.logical_or, hits).astype(F32)
    ti = lax.broadcasted_iota(jnp.int32, (tm, tm), 0)
    tj = lax.broadcasted_iota(jnp.int32, (tm, tm), 1)
    before = _dot((tj < ti).astype(BF16), chosen.astype(BF16)) + run_ref[...]
    rank_out = jnp.zeros((tm, LANES), jnp.int32)
    for k in range(TOP_K):
        rk = jnp.sum(jnp.where(hits[k], before, 0.0), axis=-1, keepdims=True)
        rank_out = jnp.where(out_lane == k, rk.astype(jnp.int32), rank_out)
    rank_ref[...] = rank_out
    run_ref[...] = run_ref[...] + jnp.sum(chosen, axis=0, keepdims=True)
    counts_ref[...] = run_ref[...]


def _route(x1, mod, gpre, w_router, router_bias, seq):
    t, d = x1.shape
    tm = TOKEN_TILE
    per_b = seq // tm
    full = lambda a: pl.BlockSpec(a.shape, lambda i: (0,) * a.ndim)
    row = lambda n: pl.BlockSpec((tm, n), lambda i: (i, 0))
    rb = router_bias.reshape(1, -1)
    w_hi = w_router.astype(BF16)
    w_router = jnp.stack([w_hi, (w_router - w_hi.astype(F32)).astype(BF16)])
    return pl.pallas_call(
        _route_kernel,
        grid=(t // tm,),
        in_specs=[row(d), pl.BlockSpec((1, 6, d), lambda i: (i // per_b, 0, 0)), full(gpre),
                  full(w_router), full(rb)],
        out_specs=[row(d // 2), row(LANES), row(LANES), row(LANES),
                   pl.BlockSpec((1, N_EXPERTS), lambda i: (0, 0))],
        out_shape=[jax.ShapeDtypeStruct((t, d // 2), jnp.uint32),
                   jax.ShapeDtypeStruct((t, LANES), jnp.int32),
                   jax.ShapeDtypeStruct((t, LANES), F32),
                   jax.ShapeDtypeStruct((t, LANES), jnp.int32),
                   jax.ShapeDtypeStruct((1, N_EXPERTS), F32)],
        scratch_shapes=[pltpu.VMEM((1, N_EXPERTS), F32)],
        compiler_params=_cparams(("arbitrary",)),
        name="route",
    )(x1, mod, gpre, w_router, rb)


GATHER_UNROLL = 8


ROW_TILE = 8


def _tile_rows(r):
    return pl.ds(pl.multiple_of(r * ROW_TILE, ROW_TILE), ROW_TILE)


def _gather_rows(idx_ref, n_rows, src_hbm, dst_of, sem):
    def body(q, carry):
        for u in range(GATHER_UNROLL):
            src = idx_ref[0, 0, q * GATHER_UNROLL + u]
            pltpu.make_async_copy(src_hbm.at[_tile_rows(src), :], dst_of(q, u), sem).start(
                priority=u % 2)
        return carry

    lax.fori_loop(0, n_rows // GATHER_UNROLL, body, 0)


def _to_tiled_rows(o_ref, y):
    rows, width = y.shape
    for r in range(rows // ROW_TILE):
        for j in range(width // LANES):
            o_ref[pl.ds(r * ROW_TILE * ROW_TILE + j, ROW_TILE, stride=ROW_TILE), :] = (
                y[r * ROW_TILE:(r + 1) * ROW_TILE, j * LANES:(j + 1) * LANES])


def _from_tiled_rows(load, rows, width):
    cols = []
    for j in range(width // LANES):
        pieces = [load(r * ROW_TILE * ROW_TILE + j, ROW_TILE, ROW_TILE) for r in range(rows // ROW_TILE)]
        cols.append(jnp.concatenate(pieces, axis=0))
    return jnp.concatenate(cols, axis=1)


def _dest_kernel(idx_ref, rank_ref, start_ref, o_ref):
    idx = idx_ref[...]
    rank = rank_ref[...]
    start = start_ref[...]
    tm = idx.shape[0]
    lane = lax.broadcasted_iota(jnp.int32, (tm, start.shape[1]), 1)
    out_lane = lax.broadcasted_iota(jnp.int32, (tm, LANES), 1)
    out = jnp.zeros((tm, LANES), jnp.int32)
    for k in range(TOP_K):
        base = jnp.sum(jnp.where(lane == idx[:, k:k + 1], start, 0.0), axis=-1, keepdims=True)
        out = jnp.where(out_lane == k, base.astype(jnp.int32) + rank[:, k:k + 1], out)
    o_ref[...] = out


def _dest(idx_pad, rank_pad, pad_start):
    t = idx_pad.shape[0]
    tm = DEST_TILE
    row = pl.BlockSpec((tm, LANES), lambda i: (i, 0))
    return pl.pallas_call(
        _dest_kernel,
        grid=(t // tm,),
        in_specs=[row, row, pl.BlockSpec((1, N_EXPERTS), lambda i: (0, 0))],
        out_specs=row,
        out_shape=jax.ShapeDtypeStruct((t, LANES), jnp.int32),
        compiler_params=_cparams(("arbitrary",)),
        name="dest",
    )(idx_pad, rank_pad, pad_start.astype(F32).reshape(1, N_EXPERTS))


def _scatter_kernel(b0_ref, nb_ref, nu_ref, dest_ref, hp_ref, xs_hbm, zbuf, sem, zsem):
    ts = hp_ref.shape[0]
    bm = zbuf.shape[0]
    n_blocks = xs_hbm.shape[0] // bm

    @pl.when(pl.program_id(0) == 0)
    def _():
        zbuf[...] = jnp.zeros_like(zbuf)
        n_used = nu_ref[0]

        def zero_block(g):
            return pltpu.make_async_copy(zbuf, xs_hbm.at[pl.ds(pl.multiple_of(g * bm, bm), bm), :], zsem)

        def tails(action):
            def step(e, carry):
                @pl.when(nb_ref[e] > 0)
                def _():
                    action(zero_block(b0_ref[e] + nb_ref[e] - 1))
                return carry
            lax.fori_loop(0, N_EXPERTS, step, 0)

        def unused(action):
            def step(g, carry):
                action(zero_block(g))
                return carry
            lax.fori_loop(n_used, n_blocks, step, 0)

        tails(lambda c: c.start())
        unused(lambda c: c.start())
        tails(lambda c: c.wait())
        unused(lambda c: c.wait())

    def body(t, carry):
        for k in range(TOP_K):
            dst = dest_ref[0, 0, t * TOP_K + k]
            pltpu.make_async_copy(hp_ref.at[pl.ds(t, 1), :], xs_hbm.at[pl.ds(dst, 1), :], sem).start(
                priority=k % 2)
        return carry

    lax.fori_loop(0, ts, body, 0)
    for k in range(TOP_K):
        pltpu.make_async_copy(hp_ref, xs_hbm.at[pl.ds(0, ts), :], sem).wait()


def _scatter(dest, hp, first_block, n_block, n_used, n_rows):
    t, half = hp.shape
    ts = SCATTER_TILE
    n_steps = t // ts
    dest3 = dest.reshape(n_steps, 1, ts * TOP_K)
    grid_spec = pltpu.PrefetchScalarGridSpec(
        num_scalar_prefetch=3,
        grid=(n_steps,),
        in_specs=[pl.BlockSpec((1, 1, ts * TOP_K), lambda i, b0, nb, nu: (i, 0, 0),
                               memory_space=pltpu.SMEM),
                  pl.BlockSpec((ts, half), lambda i, b0, nb, nu: (i, 0))],
        out_specs=pl.BlockSpec(memory_space=pl.ANY),
        scratch_shapes=[pltpu.VMEM((EXPERT_ROWS, half), jnp.uint32),
                        pltpu.SemaphoreType.DMA(()), pltpu.SemaphoreType.DMA(())],
    )
    return pl.pallas_call(
        _scatter_kernel,
        grid_spec=grid_spec,
        out_shape=jax.ShapeDtypeStruct((n_rows, half), jnp.uint32),
        compiler_params=_cparams(("arbitrary",)),
        name="scatter",
    )(first_block, n_block, n_used, dest3, hp)


def _expert_kernel(b0_ref, nb_ref, nu_ref, xs_hbm, w13_ref, w2_ref, y_hbm,
                   xbuf, ybuf, w13b, w2b, sem_in, sem_out):
    e = pl.program_id(0)
    n_used = nu_ref[0]
    b0 = b0_ref[e]
    nb = nb_ref[e]
    n_in, bm = xbuf.shape[:2]
    n_out, y_rows = ybuf.shape[:2]
    ahead = n_in - 1

    def in_copy(g, slot):
        return pltpu.make_async_copy(xs_hbm.at[pl.ds(pl.multiple_of(g * bm, bm), bm), :],
                                     xbuf.at[slot], sem_in.at[slot])

    def out_copy(g, slot):
        return pltpu.make_async_copy(ybuf.at[slot],
                                     y_hbm.at[pl.ds(pl.multiple_of(g * y_rows, y_rows), y_rows), :],
                                     sem_out.at[slot])

    @pl.when(nb > 0)
    def _():
        w13b[...] = w13_ref[0].astype(BF16)
        w2b[...] = w2_ref[0].astype(BF16)

    @pl.when((nb > 0) & (b0 == 0))
    def _():
        for a in range(ahead):
            @pl.when(a < n_used)
            def _():
                in_copy(a, a).start()

    def run_blocks(g, count):
        blocks = [g + j for j in range(count)]
        for b in blocks:
            in_copy(b, b % n_in).wait()
        for b in blocks:
            @pl.when(b >= n_out)
            def _():
                out_copy(b - n_out, b % n_out).wait()

        w = jnp.concatenate([xbuf[b % n_in] for b in blocks], axis=0)
        lo = pltpu.bitcast(w << 16, F32).astype(BF16)
        hi = pltpu.bitcast(w & jnp.uint32(0xFFFF0000), F32).astype(BF16)
        half = lo.shape[1]
        ag = _dot(lo, w13b[:half, :]) + _dot(hi, w13b[half:, :])
        hh = ag.shape[1] // 2
        act = (_silu(ag[:, :hh]) * ag[:, hh:]).astype(BF16)
        y = _dot(act, w2b[...])
        for j, b in enumerate(blocks):
            _to_tiled_rows(ybuf.at[b % n_out], y[j * bm:(j + 1) * bm])
            out_copy(b, b % n_out).start()
        for b in blocks:
            @pl.when(b + ahead < n_used)
            def _():
                in_copy(b + ahead, (b + ahead) % n_in).start()

    def pair(p, carry):
        run_blocks(b0 + 2 * p, 2)
        return carry

    lax.fori_loop(0, nb // 2, pair, 0)

    @pl.when(nb % 2 == 1)
    def _():
        run_blocks(b0 + nb - 1, 1)

    @pl.when(e == pl.num_programs(0) - 1)
    def _():
        for back in range(1, n_out + 1):
            @pl.when(n_used >= back)
            def _():
                out_copy(n_used - back, (n_used - back) % n_out).wait()

        n_blocks = y_hbm.shape[0] // y_rows
        ybuf[0] = jnp.zeros((y_rows, LANES), F32)

        def start_zero(g, carry):
            out_copy(g, 0).start()
            return carry

        def wait_zero(g, carry):
            out_copy(g, 0).wait()
            return carry

        lax.fori_loop(n_used, n_blocks, start_zero, 0)
        lax.fori_loop(n_used, n_blocks, wait_zero, 0)


def _experts(xs, first_block, n_block, n_used, w13, w2):
    n_rows, half = xs.shape
    ne, d, h2 = w13.shape
    bm = EXPERT_ROWS
    y_rows = bm * d // LANES
    grid_spec = pltpu.PrefetchScalarGridSpec(
        num_scalar_prefetch=3,
        grid=(ne,),
        in_specs=[
            pl.BlockSpec(memory_space=pl.ANY),
            pl.BlockSpec((1, d, h2), lambda e, b0, nb, nu: (e, 0, 0)),
            pl.BlockSpec((1, h2 // 2, d), lambda e, b0, nb, nu: (e, 0, 0)),
        ],
        out_specs=pl.BlockSpec(memory_space=pl.ANY),
        scratch_shapes=[pltpu.VMEM((EXPERT_IN_SLOTS, bm, half), jnp.uint32),
                        pltpu.VMEM((EXPERT_OUT_SLOTS, y_rows, LANES), F32),
                        pltpu.VMEM((d, h2), BF16),
                        pltpu.VMEM((h2 // 2, d), BF16),
                        pltpu.SemaphoreType.DMA((EXPERT_IN_SLOTS,)),
                        pltpu.SemaphoreType.DMA((EXPERT_OUT_SLOTS,))],
    )
    return pl.pallas_call(
        _expert_kernel,
        grid_spec=grid_spec,
        out_shape=jax.ShapeDtypeStruct((n_rows * d // LANES, LANES), F32),
        compiler_params=_cparams(("arbitrary",)),
        name="experts",
    )(first_block, n_block, n_used, xs, w13, w2)


def _final_kernel(dc_ref, dn_ref, y_hbm, w_ref, x_ref, mod_ref, gpre_ref, gpost_ref, w13s_ref,
                  w2s_ref, o_ref, buf, sem):
    i = pl.program_id(0)
    slot = i % 2
    tm = x_ref.shape[0]

    def issue(d_ref, s):
        _gather_rows(d_ref, tm * TOP_K, y_hbm,
                     lambda q, u: buf.at[s, u, _tile_rows(q), :], sem.at[s])

    @pl.when(i == 0)
    def _():
        issue(dc_ref, 0)

    @pl.when(i + 1 < pl.num_programs(0))
    def _():
        issue(dn_ref, 1 - slot)

    for k in range(TOP_K):
        pltpu.make_async_copy(y_hbm.at[pl.ds(0, tm * ROW_TILE), :], buf.at[slot, k],
                              sem.at[slot]).wait()

    w = w_ref[...]
    x1 = x_ref[...]
    routed = None
    for k in range(TOP_K):
        yk = _from_tiled_rows(lambda a, n, st: buf[slot, k, pl.ds(a, n, stride=st), :], tm, x1.shape[1])
        routed = yk * w[:, k:k + 1] if routed is None else routed + yk * w[:, k:k + 1]

    m = mod_ref[0]
    h = (_rms(x1, gpre_ref[...]) * (1.0 + m[4:5]) + m[3:4]).astype(BF16)
    ag = _dot(h, w13s_ref[...])
    hh = ag.shape[1] // 2
    act = (_silu(ag[:, :hh]) * ag[:, hh:]).astype(BF16)
    ffn = routed + _dot(act, w2s_ref[...])
    o_ref[...] = x1 + m[5:6] * _rms(ffn, gpost_ref[...])


def _final(dest, y_sorted, top_w, x1, mod, gpre, gpost, w13s, w2s, seq):
    t, d = x1.shape
    tm = COMBINE_TILE
    per_b = seq // tm
    n_steps = t // tm
    dest3 = dest.reshape(n_steps, 1, tm * TOP_K)
    full = lambda a: pl.BlockSpec(a.shape, lambda i: (0,) * a.ndim)
    row = lambda n: pl.BlockSpec((tm, n), lambda i: (i, 0))
    return pl.pallas_call(
        _final_kernel,
        grid=(n_steps,),
        in_specs=[
            pl.BlockSpec((1, 1, tm * TOP_K), lambda i: (i, 0, 0), memory_space=pltpu.SMEM),
            pl.BlockSpec((1, 1, tm * TOP_K), lambda i: (jnp.minimum(i + 1, n_steps - 1), 0, 0),
                         memory_space=pltpu.SMEM),
            pl.BlockSpec(memory_space=pl.ANY),
            row(LANES), row(d), pl.BlockSpec((1, 6, d), lambda i: (i // per_b, 0, 0)),
            full(gpre), full(gpost), full(w13s), full(w2s)],
        out_specs=row(d),
        out_shape=jax.ShapeDtypeStruct((t, d), F32),
        scratch_shapes=[pltpu.VMEM((2, TOP_K, tm * ROW_TILE, LANES), F32),
                        pltpu.SemaphoreType.DMA((2,))],
        compiler_params=_cparams(("arbitrary",)),
        name="combine_final",
    )(dest3, dest3, y_sorted, top_w, x1, mod, gpre, gpost, w13s, w2s)


def _pad_heads(v):
    lead = v.shape[:-1]
    v = v.reshape(lead + (2, SSD_GROUPS, HEADS_PER_GROUP))
    v = jnp.pad(v, [(0, 0)] * (len(lead) + 2) + [(0, HEADS_PADDED - HEADS_PER_GROUP)])
    return v.reshape(lead + (2 * SSD_GROUPS * HEADS_PADDED,))


def _dispatch_plan(counts, n_tokens):
    bm = EXPERT_ROWS
    counts = counts.reshape(N_EXPERTS).astype(jnp.int32)
    padded = (counts + bm - 1) // bm * bm
    pad_end = jnp.cumsum(padded)
    pad_start = pad_end - padded
    n_rows = -(-n_tokens * TOP_K // bm) * bm + N_EXPERTS * bm
    first_block = (pad_start // bm).astype(jnp.int32)
    n_block = (padded // bm).astype(jnp.int32)
    n_used = (pad_end[-1] // bm).astype(jnp.int32).reshape(1)
    return pad_start, first_block, n_block, n_used, n_rows


def _layer(x, c, w_ada, b_ada, pre_norm_mix, post_norm_mix, pre_norm_ffn, post_norm_ffn, w_in,
           conv_w, conv_b, dt_bias_fwd, dt_bias_bwd, a_log_fwd, a_log_bwd, d_skip, ssd_norm,
           w_branch_ssd, w_branch_fourier, w_out, w_router, router_bias, w13_experts, w2_experts,
           w13_shared, w2_shared):
    bsz, seq, d = x.shape
    t = bsz * seq
    x2 = x.reshape(t, d)
    row = lambda v: v.reshape(1, -1).astype(F32)

    mod = _ada(c, w_ada, b_ada)

    i1 = SSD_INNER
    i2 = i1 + XBC_WIDTH
    i3 = i2 + 2 * SSD_HEADS
    i4 = i3 + FOURIER_WIDTH
    n_dt = 2 * SSD_GROUPS * HEADS_PADDED
    w_dtp = _pad_heads(w_in[:, i2:i3])
    wdtc = jnp.pad(w_dtp, ((0, 0), (0, LANES - n_dt))).astype(BF16)
    wdtT = w_dtp.T.astype(BF16)
    bias_p = _pad_heads(jnp.concatenate([dt_bias_fwd, dt_bias_bwd]).astype(F32))
    a_p = _pad_heads(-jnp.exp(jnp.concatenate([a_log_fwd, a_log_bwd]).astype(F32)))
    pad_row = lambda v: jnp.pad(v, (0, LANES - n_dt)).reshape(1, LANES)
    z, xbc, acsc, dtT, acsT, uf, gates = _inproj(
        x2, mod, row(pre_norm_mix), w_in[:, :i1].astype(BF16), w_in[:, i1:i2].astype(BF16),
        wdtc, wdtT, w_in[:, i3:i4].astype(BF16), w_in[:, i4:].astype(BF16),
        pad_row(bias_p), pad_row(a_p), bias_p.reshape(n_dt, 1), a_p.reshape(n_dt, 1), seq)

    xbc3 = _conv(xbc.reshape(bsz, seq, XBC_WIDTH), conv_w, conv_b)

    yf, yb = _ssd(xbc3, dtT, acsT, acsc, row(jnp.repeat(d_skip, SSD_HEAD_DIM)))

    fm = _fourier(uf.reshape(bsz, seq, FOURIER_WIDTH))

    x1 = _mix(yf.reshape(t, SSD_INNER), yb.reshape(t, SSD_INNER), z,
              fm.reshape(t, FOURIER_WIDTH), gates, x2, mod,
              row(ssd_norm), w_branch_ssd.astype(BF16),
              w_branch_fourier.astype(BF16), w_out.astype(BF16), row(post_norm_mix), seq)

    hp, idx_pad, w_pad, rank_pad, counts = _route(x1, mod, row(pre_norm_ffn),
                                                  w_router.astype(F32), router_bias, seq)
    pad_start, first_block, n_block, n_used, n_rows = _dispatch_plan(counts, t)
    dest = _dest(idx_pad, rank_pad, pad_start)[:, :TOP_K]
    xs = _scatter(dest, hp, first_block, n_block, n_used, n_rows)
    y_sorted = _experts(xs, first_block, n_block, n_used, w13_experts, w2_experts)
    out = _final(dest, y_sorted, w_pad, x1, mod, row(pre_norm_ffn), row(post_norm_ffn),
                 w13_shared.astype(BF16), w2_shared.astype(BF16), seq)
    return out.reshape(bsz, seq, d)


def kernel(x, c, w_ada, b_ada, pre_norm_mix, post_norm_mix, pre_norm_ffn, post_norm_ffn, w_in,
           conv_w, conv_b, dt_bias_fwd, dt_bias_bwd, a_log_fwd, a_log_bwd, d_skip, ssd_norm,
           w_branch_ssd, w_branch_fourier, w_out, w_router, router_bias, w13_experts, w2_experts,
           w13_shared, w2_shared):
    for layer in range(w_ada.shape[0]):
        x = _layer(x, c, w_ada[layer], b_ada[layer], pre_norm_mix[layer], post_norm_mix[layer],
                   pre_norm_ffn[layer], post_norm_ffn[layer], w_in[layer], conv_w[layer],
                   conv_b[layer], dt_bias_fwd[layer], dt_bias_bwd[layer], a_log_fwd[layer],
                   a_log_bwd[layer], d_skip[layer], ssd_norm[layer], w_branch_ssd[layer],
                   w_branch_fourier[layer], w_out[layer], w_router[layer], router_bias[layer],
                   w13_experts[layer], w2_experts[layer], w13_shared[layer], w2_shared[layer])
    return x
```

```python
import functools
import math

import numpy as np
import jax
import jax.numpy as jnp
from jax import lax
from jax.experimental import pallas as pl
from jax.experimental.pallas import tpu as pltpu

F32 = jnp.float32
BF16 = jnp.bfloat16
HIGHEST = lax.Precision.HIGHEST

D_MODEL = 1024
SSD_HEADS = 24
SSD_HEAD_DIM = 64
SSD_INNER = SSD_HEADS * SSD_HEAD_DIM
SSD_GROUPS = 4
HEADS_PER_GROUP = SSD_HEADS // SSD_GROUPS
HEADS_PADDED = 8
SSD_STATE = 128
SSD_CONV = 5
SSD_CHUNK = 128
XBC_WIDTH = SSD_INNER + 2 * SSD_GROUPS * SSD_STATE
GROUP_X = HEADS_PER_GROUP * SSD_HEAD_DIM
FOURIER_WIDTH = 512
FOURIER_GROUP_DIM = 128
N_EXPERTS = 256
TOP_K = 8
N_EXPERT_GROUPS = 8
TOPK_GROUPS = 4
EXPERT_HIDDEN = 256
SHARED_HIDDEN = 256
ROUTED_SCALE = 2.5
RMS_EPS = 1e-6
LOG2_E = math.log2(math.e)

LANES = 128
VMEM_LIMIT = 56 * 1024 * 1024
TOKEN_TILE = 256
INPROJ_TILE = 512
INPROJ_SUBTILE = 256
EXPERT_ROWS = 256
COMBINE_TILE = 256
SCATTER_TILE = 512
DEST_TILE = 1024
EXPERT_IN_SLOTS = 6
EXPERT_OUT_SLOTS = 4
CONV_ROWS = 256


def _cparams(sem):
    return pltpu.CompilerParams(dimension_semantics=sem, vmem_limit_bytes=VMEM_LIMIT)


def _dot(a, b, precision=None):
    return jnp.dot(a, b, preferred_element_type=F32, precision=precision)


def _dot_nt(a, b, precision=None):
    return lax.dot_general(a, b, (((1,), (1,)), ((), ())), preferred_element_type=F32,
                           precision=precision)


def _dot_tn(a, b):
    return lax.dot_general(a, b, (((0,), (0,)), ((), ())), preferred_element_type=F32)


def _sigmoid(x):
    return 1.0 / (1.0 + jnp.exp(-x))


def _silu(x):
    return x * _sigmoid(x)


def _softplus(x):
    return jnp.maximum(x, 0.0) + jnp.log1p(jnp.exp(-jnp.abs(x)))


def _rms(x, g):
    return x * lax.rsqrt(jnp.mean(x * x, axis=-1, keepdims=True) + RMS_EPS) * g


def _ada_kernel(c_ref, w_ref, b_ref, o_ref):
    o_ref[...] = _dot(_silu(c_ref[...]), w_ref[...], HIGHEST) + b_ref[...]


def _ada(c, w_ada, b_ada):
    bsz, d = c.shape
    rows = 8
    cp = jnp.zeros((rows, d), F32).at[:bsz].set(c)
    n = w_ada.shape[1]
    tn = 1536
    out = pl.pallas_call(
        _ada_kernel,
        grid=(n // tn,),
        in_specs=[pl.BlockSpec((rows, d), lambda j: (0, 0)),
                  pl.BlockSpec((d, tn), lambda j: (0, j)),
                  pl.BlockSpec((1, tn), lambda j: (0, j))],
        out_specs=pl.BlockSpec((rows, tn), lambda j: (0, j)),
        out_shape=jax.ShapeDtypeStruct((rows, n), F32),
        compiler_params=_cparams(("arbitrary",)),
        name="adaln",
    )(cp, w_ada, b_ada.reshape(1, n))
    return out[:bsz].reshape(bsz, 6, d)


def _split3(a):
    a1 = a.astype(BF16)
    r1 = a - a1.astype(F32)
    a2 = r1.astype(BF16)
    a3 = (r1 - a2.astype(F32)).astype(BF16)
    return a1, a2, a3


def _inproj_kernel(x_ref, mod_ref, g_ref, wz_ref, wxbc_ref, wdtc_ref, wdtT_ref, wuf_ref, wg_ref,
                   bias_row_ref, a_row_ref, bias_col_ref, a_col_ref,
                   z_ref, xbc_ref, acsc_ref, dtT_ref, acsT_ref, uf_ref, gates_ref):
    m = mod_ref[0]
    tm = INPROJ_SUBTILE
    n_fwd = SSD_GROUPS * HEADS_PADDED
    ii = lax.broadcasted_iota(jnp.int32, (tm, tm), 0)
    jj = lax.broadcasted_iota(jnp.int32, (tm, tm), 1)
    same = (ii // SSD_CHUNK) == (jj // SSD_CHUNK)
    tri_f = (same & (jj <= ii)).astype(BF16)
    tri_b = (same & (jj >= ii)).astype(BF16)

    for s in range(x_ref.shape[0] // tm):
        rows = slice(s * tm, (s + 1) * tm)
        h = _rms(x_ref[rows, :], g_ref[...]) * (1.0 + m[1:2]) + m[0:1]
        hb = h.astype(BF16)
        z_ref[rows, :] = _dot(hb, wz_ref[...]).astype(z_ref.dtype)
        xbc_ref[rows, :] = _dot(hb, wxbc_ref[...]).astype(xbc_ref.dtype)
        uf_ref[rows, :] = _dot(hb, wuf_ref[...]).astype(uf_ref.dtype)
        gates_ref[rows, :] = _dot(hb, wg_ref[...]).astype(gates_ref.dtype)

        dt_c = _softplus(_dot(hb, wdtc_ref[...]) + bias_row_ref[...])
        pieces = _split3(dt_c * a_row_ref[...])
        acs_f = sum(_dot(tri_f, p) for p in pieces)
        acs_b = sum(_dot(tri_b, p) for p in pieces)
        lane = lax.broadcasted_iota(jnp.int32, acs_f.shape, 1)
        acsc_ref[rows, :] = jnp.where(lane < n_fwd, acs_f, acs_b) * LOG2_E

        dt_t = _softplus(_dot_nt(wdtT_ref[...], hb) + bias_col_ref[...])
        pieces = _split3(dt_t * a_col_ref[...])
        acs_f = sum(_dot_nt(p, tri_f) for p in pieces)
        acs_b = sum(_dot_nt(p, tri_b) for p in pieces)
        sub = lax.broadcasted_iota(jnp.int32, acs_f.shape, 0)
        dtT_ref[:, rows] = dt_t
        acsT_ref[:, rows] = jnp.where(sub < n_fwd, acs_f, acs_b) * LOG2_E


def _inproj(x2, mod, g, wz, wxbc, wdtc, wdtT, wuf, wg, bias_row, a_row, bias_col, a_col, seq):
    t, d = x2.shape
    tm = INPROJ_TILE
    per_b = seq // tm
    full = lambda a: pl.BlockSpec(a.shape, lambda i: (0,) * a.ndim)
    row = lambda n: pl.BlockSpec((tm, n), lambda i: (i, 0))
    nd = wdtT.shape[0]
    colspec = pl.BlockSpec((nd, tm), lambda i: (0, i))
    return pl.pallas_call(
        _inproj_kernel,
        grid=(t // tm,),
        in_specs=[row(d), pl.BlockSpec((1, 6, d), lambda i: (i // per_b, 0, 0)), full(g),
                  full(wz), full(wxbc), full(wdtc), full(wdtT), full(wuf), full(wg),
                  full(bias_row), full(a_row), full(bias_col), full(a_col)],
        out_specs=[row(wz.shape[1]), row(wxbc.shape[1]), row(LANES), colspec, colspec,
                   row(wuf.shape[1]), row(wg.shape[1])],
        out_shape=[jax.ShapeDtypeStruct((t, wz.shape[1]), BF16),
                   jax.ShapeDtypeStruct((t, wxbc.shape[1]), BF16),
                   jax.ShapeDtypeStruct((t, LANES), F32),
                   jax.ShapeDtypeStruct((nd, t), F32),
                   jax.ShapeDtypeStruct((nd, t), F32),
                   jax.ShapeDtypeStruct((t, wuf.shape[1]), BF16),
                   jax.ShapeDtypeStruct((t, wg.shape[1]), BF16)],
        compiler_params=_cparams(("arbitrary",)),
        name="inproj",
    )(x2, mod, g, wz, wxbc, wdtc, wdtT, wuf, wg, bias_row, a_row, bias_col, a_col)


def _conv_kernel(u_ref, w_ref, b_ref, o_ref, pad_ref):
    s = u_ref.shape[1]
    halo = 8
    pad_ref[0:halo, :] = jnp.zeros((halo, LANES), F32)
    pad_ref[halo + s:2 * halo + s, :] = jnp.zeros((halo, LANES), F32)
    pad_ref[halo:halo + s, :] = u_ref[0].astype(F32)
    w = w_ref[...]
    b = b_ref[...]
    half = (SSD_CONV - 1) // 2
    for r in range(s // CONV_ROWS):
        base = r * CONV_ROWS
        acc = b
        for k in range(SSD_CONV):
            lo = base + halo + k - half
            acc = acc + w[k:k + 1, :] * pad_ref[lo:lo + CONV_ROWS, :]
        o_ref[0, base:base + CONV_ROWS, :] = _silu(acc).astype(o_ref.dtype)


def _conv(xbc3, conv_w, conv_b):
    bsz, s, c = xbc3.shape
    return pl.pallas_call(
        _conv_kernel,
        grid=(bsz, c // LANES),
        in_specs=[pl.BlockSpec((1, s, LANES), lambda b, j: (b, 0, j)),
                  pl.BlockSpec((SSD_CONV, LANES), lambda b, j: (0, j)),
                  pl.BlockSpec((1, LANES), lambda b, j: (0, j))],
        out_specs=pl.BlockSpec((1, s, LANES), lambda b, j: (b, 0, j)),
        out_shape=jax.ShapeDtypeStruct((bsz, s, c), BF16),
        scratch_shapes=[pltpu.VMEM((s + 16, LANES), F32)],
        compiler_params=_cparams(("arbitrary", "arbitrary")),
        name="conv",
    )(xbc3, conv_w, conv_b.reshape(1, c))


HEAD_PAIRS = HEADS_PER_GROUP // 2
SSD_GROUPS_PER_STEP = 4


def _ssd_direction(x, bm, cm, dt_r, acs_r, acsc_all, lane_off, s_ref, reverse):
    L, N = bm.shape
    assert L == N == LANES
    ii = lax.broadcasted_iota(jnp.int32, (L, L), 0)
    jj = lax.broadcasted_iota(jnp.int32, (L, L), 1)
    mask = (jj >= ii) if reverse else (jj <= ii)
    lo_half = jj < SSD_HEAD_DIM
    shift = jnp.where(lane_off == 0, 0, LANES - lane_off)
    acs_c = pltpu.roll(acsc_all, shift, 1)
    last = 0 if reverse else L - 1
    tot_r = acs_r[:, last:last + 1]
    w_r = jnp.exp2(tot_r - acs_r) * dt_r
    src_r = acs_r - jnp.log2(dt_r)
    etot = jnp.broadcast_to(jnp.exp2(tot_r), (HEADS_PADDED, LANES))

    cm16 = cm.astype(BF16)
    cbt = _dot_nt(cm16, bm.astype(BF16))
    bt = bm.astype(F32).T
    s_all = [s_ref[q] for q in range(HEAD_PAIRS)]
    cs_all = _dot(cm16, jnp.concatenate(s_all, axis=1).astype(BF16))
    ys = []
    for q in range(HEAD_PAIRS):
        xq = x[:, q * LANES:(q + 1) * LANES]
        x_a = jnp.where(lo_half, xq, jnp.zeros_like(xq)).astype(BF16)
        x_b = jnp.where(lo_half, jnp.zeros_like(xq), xq).astype(BF16)
        m_parts, b_parts, e_cols = [], [], []
        for h in (2 * q, 2 * q + 1):
            col = jnp.broadcast_to(acs_c[:, h:h + 1], (L, L))
            decay_dt = jnp.exp2(jnp.where(mask, col - src_r[h:h + 1, :], -jnp.inf))
            m_parts.append((cbt * decay_dt).astype(BF16))
            e_cols.append(jnp.exp2(col))
            b_parts.append((bt * w_r[h:h + 1, :]).astype(BF16))
        x_diag = jnp.concatenate([x_a, x_b], axis=0)
        y_off = cs_all[:, q * LANES:(q + 1) * LANES] * jnp.where(lo_half, e_cols[0], e_cols[1])
        ys.append(_dot(jnp.concatenate(m_parts, axis=1), x_diag) + y_off)
        dec = jnp.where(lo_half[0:1], etot[2 * q:2 * q + 1], etot[2 * q + 1:2 * q + 2])
        s_ref[q] = s_all[q] * dec + _dot(jnp.concatenate(b_parts, axis=1), x_diag)
    return jnp.concatenate(ys, axis=1)


def _ssd_kernel(xf_ref, bf_ref, cf_ref, dtTf_ref, acsTf_ref, acscf_ref,
                xb_ref, bb_ref, cb_ref, dtTb_ref, acsTb_ref, acscb_ref, dskip_ref,
                yf_ref, yb_ref, sf_ref, sb_ref):
    gp = pl.program_id(1)

    @pl.when(pl.program_id(2) == 0)
    def _():
        sf_ref[...] = jnp.zeros_like(sf_ref)
        sb_ref[...] = jnp.zeros_like(sb_ref)

    for i in range(SSD_GROUPS_PER_STEP):
        g = gp * SSD_GROUPS_PER_STEP + i
        xs = slice(i * GROUP_X, (i + 1) * GROUP_X)
        ns = slice(i * SSD_STATE, (i + 1) * SSD_STATE)
        hs = slice(i * HEADS_PADDED, (i + 1) * HEADS_PADDED)
        y_fwd = _ssd_direction(
            xf_ref[0, :, xs], bf_ref[0, :, ns], cf_ref[0, :, ns], dtTf_ref[hs, :], acsTf_ref[hs, :],
            acscf_ref[...], g * HEADS_PADDED, sf_ref.at[i], False)
        y_fwd = y_fwd + dskip_ref[:, xs] * xf_ref[0, :, xs].astype(F32)
        yf_ref[0, :, xs] = y_fwd.astype(yf_ref.dtype)
        yb_ref[0, :, xs] = _ssd_direction(
            xb_ref[0, :, xs], bb_ref[0, :, ns], cb_ref[0, :, ns], dtTb_ref[hs, :], acsTb_ref[hs, :],
            acscb_ref[...], (SSD_GROUPS + g) * HEADS_PADDED, sb_ref.at[i], True).astype(yb_ref.dtype)


def _ssd(xbc3, dtT, acsT, acsc, dskip):
    bsz, s, _ = xbc3.shape
    L = SSD_CHUNK
    nc = s // L
    P = SSD_GROUPS_PER_STEP
    GP = SSD_GROUPS // P
    nb = SSD_INNER // (P * SSD_STATE)
    ncb = nb + GP
    fwd = lambda c: c
    bwd = lambda c: nc - 1 - c

    def specs(cidx, dirn):
        rowspec = pl.BlockSpec((P * HEADS_PADDED, L),
                               lambda b, g, c: (dirn * GP + g, b * nc + cidx(c)))
        return [
            pl.BlockSpec((1, L, P * GROUP_X), lambda b, g, c: (b, cidx(c), g)),
            pl.BlockSpec((1, L, P * SSD_STATE), lambda b, g, c: (b, cidx(c), nb + g)),
            pl.BlockSpec((1, L, P * SSD_STATE), lambda b, g, c: (b, cidx(c), ncb + g)),
            rowspec, rowspec,
            pl.BlockSpec((L, LANES), lambda b, g, c: (b * nc + cidx(c), 0)),
        ]

    out_specs = [pl.BlockSpec((1, L, P * GROUP_X), lambda b, g, c: (b, c, g)),
                 pl.BlockSpec((1, L, P * GROUP_X), lambda b, g, c: (b, nc - 1 - c, g))]
    return pl.pallas_call(
        _ssd_kernel,
        grid=(bsz, GP, nc),
        in_specs=specs(fwd, 0) + specs(bwd, 1)
        + [pl.BlockSpec((1, P * GROUP_X), lambda b, g, c: (0, g))],
        out_specs=out_specs,
        out_shape=[jax.ShapeDtypeStruct((bsz, s, SSD_INNER), BF16)] * 2,
        scratch_shapes=[pltpu.VMEM((P, HEAD_PAIRS, SSD_STATE, LANES), F32)] * 2,
        compiler_params=_cparams(("arbitrary", "arbitrary", "arbitrary")),
        name="ssd",
    )(xbc3, xbc3, xbc3, dtT, acsT, acsc, xbc3, xbc3, xbc3, dtT, acsT, acsc, dskip)


DFT_RADIX = 8
DFT_ROWS = 256


def _dft_tables(seq):
    m_len = seq // DFT_RADIX
    mm = np.arange(m_len)
    a0 = 2.0 * np.pi * np.outer(mm, mm) / m_len
    f0 = np.concatenate([np.cos(a0), -np.sin(a0)], axis=0)
    ang = -2.0 * np.pi * np.outer(np.arange(DFT_RADIX), mm) / seq
    tw = np.zeros((DFT_RADIX, m_len, LANES))
    tw[:, :, 0] = np.cos(ang)
    tw[:, :, 1] = np.sin(ang)
    ch = np.arange(FOURIER_GROUP_DIM)
    ac = 2.0 * np.pi * np.outer(ch, ch) / FOURIER_GROUP_DIM
    scale = 1.0 / math.sqrt(seq * FOURIER_GROUP_DIM)
    fc = np.concatenate([np.cos(ac), np.sin(ac)], axis=0) * scale
    return (jnp.asarray(f0, BF16), jnp.asarray(tw, F32), jnp.asarray(fc, BF16))


def _fourier_kernel(u_ref, f0_ref, tw_ref, fc_ref, o_ref, x32, tbuf):
    m_len = f0_ref.shape[1]
    x32[...] = u_ref[0].astype(F32)

    def sub_dft(p, carry):
        xa = x32[pl.ds(2 * p, m_len, stride=DFT_RADIX), :]
        xb = x32[pl.ds(2 * p + 1, m_len, stride=DFT_RADIX), :]
        xr = jnp.concatenate([xa, xb], axis=1).astype(BF16)
        tbuf[p] = _dot(f0_ref[...], xr)
        return carry

    lax.fori_loop(0, DFT_RADIX // 2, sub_dft, 0)
    rows = min(DFT_ROWS, m_len)
    lanes_of = lambda r: slice((r % 2) * LANES, (r % 2 + 1) * LANES)

    def combine(c, carry):
        sl = pl.ds(pl.multiple_of(c * rows, rows), rows)
        si = pl.ds(pl.multiple_of(m_len + c * rows, rows), rows)
        for r in range(1, DFT_RADIX):
            tre = tbuf[r // 2, sl, lanes_of(r)]
            tim = tbuf[r // 2, si, lanes_of(r)]
            tw = tw_ref[r, sl, :]
            gre = jnp.broadcast_to(tw[:, 0:1], (rows, LANES))
            gim = jnp.broadcast_to(tw[:, 1:2], (rows, LANES))
            tbuf[r // 2, sl, lanes_of(r)] = gre * tre - gim * tim
            tbuf[r // 2, si, lanes_of(r)] = gre * tim + gim * tre
        for kr in range(DFT_RADIX):
            zr = tbuf[0, sl, lanes_of(0)]
            zi = tbuf[0, si, lanes_of(0)]
            for r in range(1, DFT_RADIX):
                tre = tbuf[r // 2, sl, lanes_of(r)]
                tim = tbuf[r // 2, si, lanes_of(r)]
                quarter, odd = divmod((r * kr) % DFT_RADIX, 2)
                for _ in range(quarter):
                    tre, tim = tim, -tre
                if odd:
                    tre, tim = (tre + tim) * math.sqrt(0.5), (tim - tre) * math.sqrt(0.5)
                zr = zr + tre
                zi = zi + tim
            zz = jnp.concatenate([zr, zi], axis=1).astype(BF16)
            out_rows = pl.ds(pl.multiple_of(kr * m_len + c * rows, rows), rows)
            o_ref[0, out_rows, :] = _dot(zz, fc_ref[...])
        return carry

    lax.fori_loop(0, m_len // rows, combine, 0)


def _fourier(uf3):
    bsz, s, w = uf3.shape
    f0, tw, fc = _dft_tables(s)
    m_len = s // DFT_RADIX
    ng = w // FOURIER_GROUP_DIM
    return pl.pallas_call(
        _fourier_kernel,
        grid=(bsz, ng),
        in_specs=[pl.BlockSpec((1, s, LANES), lambda b, j: (b, 0, j)),
                  pl.BlockSpec(f0.shape, lambda b, j: (0, 0), pipeline_mode=pl.Buffered(1)),
                  pl.BlockSpec(tw.shape, lambda b, j: (0, 0, 0), pipeline_mode=pl.Buffered(1)),
                  pl.BlockSpec(fc.shape, lambda b, j: (0, 0), pipeline_mode=pl.Buffered(1))],
        out_specs=pl.BlockSpec((1, s, LANES), lambda b, j: (b, 0, j)),
        out_shape=jax.ShapeDtypeStruct((bsz, s, w), F32),
        scratch_shapes=[pltpu.VMEM((s, LANES), F32),
                        pltpu.VMEM((DFT_RADIX // 2, 2 * m_len, 2 * LANES), F32)],
        compiler_params=_cparams(("arbitrary", "arbitrary")),
        name="fourier",
    )(uf3, f0, tw, fc)


def _mix_kernel(yf_ref, yb_ref, z_ref, fm_ref, gates_ref, x_ref, mod_ref,
                ssdn_ref, wbs_ref, wbf_ref, wout_ref, gpost_ref, o_ref):
    m = mod_ref[0]
    y = yf_ref[...].astype(F32) + yb_ref[...].astype(F32)
    v = y * _silu(z_ref[...].astype(F32))
    parts = []
    for g in range(SSD_GROUPS):
        vg = v[:, g * GROUP_X:(g + 1) * GROUP_X]
        parts.append(vg * lax.rsqrt(jnp.mean(vg * vg, axis=-1, keepdims=True) + RMS_EPS))
    vn = jnp.concatenate(parts, axis=1) * ssdn_ref[...]
    y_ssd = _dot(vn.astype(BF16), wbs_ref[...])
    y_fou = _dot(fm_ref[...].astype(BF16), wbf_ref[...])
    gt = _sigmoid(gates_ref[...].astype(F32))
    d = y_ssd.shape[1]
    mixed = gt[:, :d] * y_ssd + gt[:, d:] * y_fou
    mo = _dot(mixed.astype(BF16), wout_ref[...])
    o_ref[...] = x_ref[...] + m[2:3] * _rms(mo, gpost_ref[...])


def _mix(yf, yb, z, fm, gates, x2, mod, ssdn, wbs, wbf, wout, gpost, seq):
    t, d = x2.shape
    tm = TOKEN_TILE
    per_b = seq // tm
    full = lambda a: pl.BlockSpec(a.shape, lambda i: (0,) * a.ndim)
    row = lambda n: pl.BlockSpec((tm, n), lambda i: (i, 0))
    return pl.pallas_call(
        _mix_kernel,
        grid=(t // tm,),
        in_specs=[row(SSD_INNER), row(SSD_INNER), row(SSD_INNER),
                  row(FOURIER_WIDTH), row(2 * d), row(d),
                  pl.BlockSpec((1, 6, d), lambda i: (i // per_b, 0, 0)),
                  full(ssdn), full(wbs), full(wbf), full(wout), full(gpost)],
        out_specs=row(d),
        out_shape=jax.ShapeDtypeStruct((t, d), F32),
        compiler_params=_cparams(("arbitrary",)),
        name="mix",
    )(yf, yb, z, fm, gates, x2, mod, ssdn, wbs, wbf, wout, gpost)


def _route_kernel(x_ref, mod_ref, gpre_ref, wr_ref, rb_ref, hp_ref, idx_ref, w_ref, rank_ref,
                  counts_ref, run_ref):
    @pl.when(pl.program_id(0) == 0)
    def _():
        run_ref[...] = jnp.zeros_like(run_ref)

    m = mod_ref[0]
    h = _rms(x_ref[...], gpre_ref[...]) * (1.0 + m[4:5]) + m[3:4]
    tm, d = h.shape
    half = d // 2
    bits = pltpu.bitcast(h.astype(BF16).astype(F32), jnp.uint32)
    hp_ref[...] = (bits[:, :half] >> 16) | (bits[:, half:] & jnp.uint32(0xFFFF0000))

    h_hi = h.astype(BF16)
    h_lo = (h - h_hi.astype(F32)).astype(BF16)
    w_hi = wr_ref[0]
    logits = _dot(h_hi, w_hi) + (_dot(h_hi, wr_ref[1]) + _dot(h_lo, w_hi))
    scores = _sigmoid(logits)
    biased = scores + rb_ref[...]
    ne = scores.shape[1]
    per_group = ne // N_EXPERT_GROUPS
    lane = lax.broadcasted_iota(jnp.int32, (tm, ne), 1)
    lane_f = lane.astype(F32)
    grp = lane // per_group
    out_lane = lax.broadcasted_iota(jnp.int32, (tm, LANES), 1)
    neg = -jnp.inf

    def argmax_first(v):
        mx = jnp.max(v, axis=-1, keepdims=True)
        ix = jnp.min(jnp.where(v == mx, lane_f, float(ne)), axis=-1, keepdims=True)
        return mx, ix

    gs = []
    gmat = jnp.full((tm, LANES), neg, F32)
    for g in range(N_EXPERT_GROUPS):
        vg = jnp.where(grp == g, biased, neg)
        m1, i1 = argmax_first(vg)
        m2 = jnp.max(jnp.where(lane_f == i1, neg, vg), axis=-1, keepdims=True)
        gs.append(m1 + m2)
        gmat = jnp.where(out_lane == g, gs[g], gmat)
    ahead = jnp.zeros((tm, LANES), F32)
    for o in range(N_EXPERT_GROUPS):
        beats = (gs[o] > gmat) | ((gs[o] == gmat) & (out_lane > o))
        ahead = ahead + beats.astype(F32)
    sel = ((ahead < TOPK_GROUPS) & (out_lane < N_EXPERT_GROUPS)).astype(BF16)
    eg = lax.broadcasted_iota(jnp.int32, (LANES, ne), 0)
    ee = lax.broadcasted_iota(jnp.int32, (LANES, ne), 1)
    allowed = _dot(sel, (eg == ee // per_group).astype(BF16)) > 0.5
    masked = jnp.where(allowed, biased, neg)

    idx_out = jnp.zeros((tm, LANES), F32)
    w_out = jnp.zeros((tm, LANES), F32)
    w_sum = jnp.zeros((tm, 1), F32)
    hits = []
    for k in range(TOP_K):
        _, ik = argmax_first(masked)
        hit = lane_f == ik
        hits.append(hit)
        wk = jnp.sum(jnp.where(hit, scores, 0.0), axis=-1, keepdims=True)
        masked = jnp.where(hit, neg, masked)
        idx_out = jnp.where(out_lane == k, ik, idx_out)
        w_out = jnp.where(out_lane == k, wk, w_out)
        w_sum = w_sum + wk
    idx_ref[...] = idx_out.astype(jnp.int32)
    w_ref[...] = w_out / w_sum * ROUTED_SCALE

    chosen = functools.reduce(jnp.logical_or, hits).astype(F32)
    ti = lax.broadcasted_iota(jnp.int32, (tm, tm), 0)
    tj = lax.broadcasted_iota(jnp.int32, (tm, tm), 1)
    before = _dot((tj < ti).astype(BF16), chosen.astype(BF16)) + run_ref[...]
    rank_out = jnp.zeros((tm, LANES), jnp.int32)
    for k in range(TOP_K):
        rk = jnp.sum(jnp.where(hits[k], before, 0.0), axis=-1, keepdims=True)
        rank_out = jnp.where(out_lane == k, rk.astype(jnp.int32), rank_out)
    rank_ref[...] = rank_out
    run_ref[...] = run_ref[...] + jnp.sum(chosen, axis=0, keepdims=True)
    counts_ref[...] = run_ref[...]


def _route(x1, mod, gpre, w_router, router_bias, seq):
    t, d = x1.shape
    tm = TOKEN_TILE
    per_b = seq // tm
    full = lambda a: pl.BlockSpec(a.shape, lambda i: (0,) * a.ndim)
    row = lambda n: pl.BlockSpec((tm, n), lambda i: (i, 0))
    rb = router_bias.reshape(1, -1)
    w_hi = w_router.astype(BF16)
    w_router = jnp.stack([w_hi, (w_router - w_hi.astype(F32)).astype(BF16)])
    return pl.pallas_call(
        _route_kernel,
        grid=(t // tm,),
        in_specs=[row(d), pl.BlockSpec((1, 6, d), lambda i: (i // per_b, 0, 0)), full(gpre),
                  full(w_router), full(rb)],
        out_specs=[row(d // 2), row(LANES), row(LANES), row(LANES),
                   pl.BlockSpec((1, N_EXPERTS), lambda i: (0, 0))],
        out_shape=[jax.ShapeDtypeStruct((t, d // 2), jnp.uint32),
                   jax.ShapeDtypeStruct((t, LANES), jnp.int32),
                   jax.ShapeDtypeStruct((t, LANES), F32),
                   jax.ShapeDtypeStruct((t, LANES), jnp.int32),
                   jax.ShapeDtypeStruct((1, N_EXPERTS), F32)],
        scratch_shapes=[pltpu.VMEM((1, N_EXPERTS), F32)],
        compiler_params=_cparams(("arbitrary",)),
        name="route",
    )(x1, mod, gpre, w_router, rb)


def _mix_route_kernel(yf_ref, yb_ref, z_ref, fm_ref, gates_ref, x_ref, mod_ref, ssdn_ref, wbs_ref,
                      wbf_ref, wout_ref, gpost_ref, gpre_ref, wr_ref, rb_ref,
                      x1_ref, hp_ref, idx_ref, w_ref, rank_ref, counts_ref, run_ref):
    _mix_kernel(yf_ref, yb_ref, z_ref, fm_ref, gates_ref, x_ref, mod_ref, ssdn_ref, wbs_ref,
                wbf_ref, wout_ref, gpost_ref, x1_ref)
    _route_kernel(x1_ref, mod_ref, gpre_ref, wr_ref, rb_ref, hp_ref, idx_ref, w_ref, rank_ref,
                  counts_ref, run_ref)


def _mix_route(yf, yb, z, fm, gates, x2, mod, ssdn, wbs, wbf, wout, gpost, gpre, w_router,
               router_bias, seq):
    t, d = x2.shape
    tm = TOKEN_TILE
    per_b = seq // tm
    full = lambda a: pl.BlockSpec(a.shape, lambda i: (0,) * a.ndim)
    row = lambda n: pl.BlockSpec((tm, n), lambda i: (i, 0))
    rb = router_bias.reshape(1, -1)
    w_hi = w_router.astype(BF16)
    w_router = jnp.stack([w_hi, (w_router - w_hi.astype(F32)).astype(BF16)])
    return pl.pallas_call(
        _mix_route_kernel,
        grid=(t // tm,),
        in_specs=[row(SSD_INNER), row(SSD_INNER), row(SSD_INNER),
                  row(FOURIER_WIDTH), row(2 * d), row(d),
                  pl.BlockSpec((1, 6, d), lambda i: (i // per_b, 0, 0)),
                  full(ssdn), full(wbs), full(wbf), full(wout), full(gpost),
                  full(gpre), full(w_router), full(rb)],
        out_specs=[row(d), row(d // 2), row(LANES), row(LANES), row(LANES),
                   pl.BlockSpec((1, N_EXPERTS), lambda i: (0, 0))],
        out_shape=[jax.ShapeDtypeStruct((t, d), F32),
                   jax.ShapeDtypeStruct((t, d // 2), jnp.uint32),
                   jax.ShapeDtypeStruct((t, LANES), jnp.int32),
                   jax.ShapeDtypeStruct((t, LANES), F32),
                   jax.ShapeDtypeStruct((t, LANES), jnp.int32),
                   jax.ShapeDtypeStruct((1, N_EXPERTS), F32)],
        scratch_shapes=[pltpu.VMEM((1, N_EXPERTS), F32)],
        compiler_params=_cparams(("arbitrary",)),
        name="mix_route",
    )(yf, yb, z, fm, gates, x2, mod, ssdn, wbs, wbf, wout, gpost, gpre, w_router, rb)


GATHER_UNROLL = 8


ROW_TILE = 8


def _tile_rows(r):
    return pl.ds(pl.multiple_of(r * ROW_TILE, ROW_TILE), ROW_TILE)


def _gather_rows(idx_ref, n_rows, src_hbm, dst_of, sem):
    def body(q, carry):
        for u in range(GATHER_UNROLL):
            src = idx_ref[0, 0, q * GATHER_UNROLL + u]
            pltpu.make_async_copy(src_hbm.at[_tile_rows(src), :], dst_of(q, u), sem).start(
                priority=u % 2)
        return carry

    lax.fori_loop(0, n_rows // GATHER_UNROLL, body, 0)


def _to_tiled_rows(o_ref, y):
    rows, width = y.shape
    for r in range(rows // ROW_TILE):
        for j in range(width // LANES):
            o_ref[pl.ds(r * ROW_TILE * ROW_TILE + j, ROW_TILE, stride=ROW_TILE), :] = (
                y[r * ROW_TILE:(r + 1) * ROW_TILE, j * LANES:(j + 1) * LANES])


def _from_tiled_rows(load, rows, width):
    cols = []
    for j in range(width // LANES):
        pieces = [load(r * ROW_TILE * ROW_TILE + j, ROW_TILE, ROW_TILE) for r in range(rows // ROW_TILE)]
        cols.append(jnp.concatenate(pieces, axis=0))
    return jnp.concatenate(cols, axis=1)


def _dest_kernel(idx_ref, rank_ref, start_ref, o_ref):
    idx = idx_ref[...]
    rank = rank_ref[...]
    start = start_ref[...]
    tm = idx.shape[0]
    lane = lax.broadcasted_iota(jnp.int32, (tm, start.shape[1]), 1)
    out_lane = lax.broadcasted_iota(jnp.int32, (tm, LANES), 1)
    out = jnp.zeros((tm, LANES), jnp.int32)
    for k in range(TOP_K):
        base = jnp.sum(jnp.where(lane == idx[:, k:k + 1], start, 0.0), axis=-1, keepdims=True)
        out = jnp.where(out_lane == k, base.astype(jnp.int32) + rank[:, k:k + 1], out)
    o_ref[...] = out


def _dest(idx_pad, rank_pad, pad_start):
    t = idx_pad.shape[0]
    tm = DEST_TILE
    row = pl.BlockSpec((tm, LANES), lambda i: (i, 0))
    return pl.pallas_call(
        _dest_kernel,
        grid=(t // tm,),
        in_specs=[row, row, pl.BlockSpec((1, N_EXPERTS), lambda i: (0, 0))],
        out_specs=row,
        out_shape=jax.ShapeDtypeStruct((t, LANES), jnp.int32),
        compiler_params=_cparams(("arbitrary",)),
        name="dest",
    )(idx_pad, rank_pad, pad_start.astype(F32).reshape(1, N_EXPERTS))


def _scatter_kernel(b0_ref, nb_ref, nu_ref, dest_ref, hp_ref, xs_hbm, zbuf, sem, zsem):
    ts = hp_ref.shape[0]
    bm = zbuf.shape[0]
    n_blocks = xs_hbm.shape[0] // bm

    @pl.when(pl.program_id(0) == 0)
    def _():
        zbuf[...] = jnp.zeros_like(zbuf)
        n_used = nu_ref[0]

        def zero_block(g):
            return pltpu.make_async_copy(zbuf, xs_hbm.at[pl.ds(pl.multiple_of(g * bm, bm), bm), :], zsem)

        def tails(action):
            def step(e, carry):
                @pl.when(nb_ref[e] > 0)
                def _():
                    action(zero_block(b0_ref[e] + nb_ref[e] - 1))
                return carry
            lax.fori_loop(0, N_EXPERTS, step, 0)

        def unused(action):
            def step(g, carry):
                action(zero_block(g))
                return carry
            lax.fori_loop(n_used, n_blocks, step, 0)

        tails(lambda c: c.start())
        unused(lambda c: c.start())
        tails(lambda c: c.wait())
        unused(lambda c: c.wait())

    def body(t, carry):
        for k in range(TOP_K):
            dst = dest_ref[0, 0, t * TOP_K + k]
            pltpu.make_async_copy(hp_ref.at[pl.ds(t, 1), :], xs_hbm.at[pl.ds(dst, 1), :], sem).start(
                priority=k % 2)
        return carry

    lax.fori_loop(0, ts, body, 0)
    for k in range(TOP_K):
        pltpu.make_async_copy(hp_ref, xs_hbm.at[pl.ds(0, ts), :], sem).wait()


def _scatter(dest, hp, first_block, n_block, n_used, n_rows):
    t, half = hp.shape
    ts = SCATTER_TILE
    n_steps = t // ts
    dest3 = dest.reshape(n_steps, 1, ts * TOP_K)
    grid_spec = pltpu.PrefetchScalarGridSpec(
        num_scalar_prefetch=3,
        grid=(n_steps,),
        in_specs=[pl.BlockSpec((1, 1, ts * TOP_K), lambda i, b0, nb, nu: (i, 0, 0),
                               memory_space=pltpu.SMEM),
                  pl.BlockSpec((ts, half), lambda i, b0, nb, nu: (i, 0))],
        out_specs=pl.BlockSpec(memory_space=pl.ANY),
        scratch_shapes=[pltpu.VMEM((EXPERT_ROWS, half), jnp.uint32),
                        pltpu.SemaphoreType.DMA(()), pltpu.SemaphoreType.DMA(())],
    )
    return pl.pallas_call(
        _scatter_kernel,
        grid_spec=grid_spec,
        out_shape=jax.ShapeDtypeStruct((n_rows, half), jnp.uint32),
        compiler_params=_cparams(("arbitrary",)),
        name="scatter",
    )(first_block, n_block, n_used, dest3, hp)


def _expert_kernel(b0_ref, nb_ref, nu_ref, xs_hbm, w13_ref, w2_ref, y_hbm,
                   xbuf, ybuf, w13b, w2b, sem_in, sem_out):
    e = pl.program_id(0)
    n_used = nu_ref[0]
    b0 = b0_ref[e]
    nb = nb_ref[e]
    n_in, bm = xbuf.shape[:2]
    n_out, y_rows = ybuf.shape[:2]
    ahead = n_in - 1

    def in_copy(g, slot):
        return pltpu.make_async_copy(xs_hbm.at[pl.ds(pl.multiple_of(g * bm, bm), bm), :],
                                     xbuf.at[slot], sem_in.at[slot])

    def out_copy(g, slot):
        return pltpu.make_async_copy(ybuf.at[slot],
                                     y_hbm.at[pl.ds(pl.multiple_of(g * y_rows, y_rows), y_rows), :],
                                     sem_out.at[slot])

    @pl.when(nb > 0)
    def _():
        w13b[...] = w13_ref[0].astype(BF16)
        w2b[...] = w2_ref[0].astype(BF16)

    @pl.when((nb > 0) & (b0 == 0))
    def _():
        for a in range(ahead):
            @pl.when(a < n_used)
            def _():
                in_copy(a, a).start()

    def run_blocks(g, count):
        blocks = [g + j for j in range(count)]
        for b in blocks:
            in_copy(b, b % n_in).wait()
        for b in blocks:
            @pl.when(b >= n_out)
            def _():
                out_copy(b - n_out, b % n_out).wait()

        w = jnp.concatenate([xbuf[b % n_in] for b in blocks], axis=0)
        lo = pltpu.bitcast(w << 16, F32).astype(BF16)
        hi = pltpu.bitcast(w & jnp.uint32(0xFFFF0000), F32).astype(BF16)
        half = lo.shape[1]
        ag = _dot(lo, w13b[:half, :]) + _dot(hi, w13b[half:, :])
        hh = ag.shape[1] // 2
        act = (_silu(ag[:, :hh]) * ag[:, hh:]).astype(BF16)
        y = _dot(act, w2b[...])
        for j, b in enumerate(blocks):
            _to_tiled_rows(ybuf.at[b % n_out], y[j * bm:(j + 1) * bm])
            out_copy(b, b % n_out).start()
        for b in blocks:
            @pl.when(b + ahead < n_used)
            def _():
                in_copy(b + ahead, (b + ahead) % n_in).start()

    def pair(p, carry):
        run_blocks(b0 + 2 * p, 2)
        return carry

    lax.fori_loop(0, nb // 2, pair, 0)

    @pl.when(nb % 2 == 1)
    def _():
        run_blocks(b0 + nb - 1, 1)

    @pl.when(e == pl.num_programs(0) - 1)
    def _():
        for back in range(1, n_out + 1):
            @pl.when(n_used >= back)
            def _():
                out_copy(n_used - back, (n_used - back) % n_out).wait()

        n_blocks = y_hbm.shape[0] // y_rows
        ybuf[0] = jnp.zeros((y_rows, LANES), F32)

        def start_zero(g, carry):
            out_copy(g, 0).start()
            return carry

        def wait_zero(g, carry):
            out_copy(g, 0).wait()
            return carry

        lax.fori_loop(n_used, n_blocks, start_zero, 0)
        lax.fori_loop(n_used, n_blocks, wait_zero, 0)


def _experts(xs, first_block, n_block, n_used, w13, w2):
    n_rows, half = xs.shape
    ne, d, h2 = w13.shape
    bm = EXPERT_ROWS
    y_rows = bm * d // LANES
    grid_spec = pltpu.PrefetchScalarGridSpec(
        num_scalar_prefetch=3,
        grid=(ne,),
        in_specs=[
            pl.BlockSpec(memory_space=pl.ANY),
            pl.BlockSpec((1, d, h2), lambda e, b0, nb, nu: (e, 0, 0)),
            pl.BlockSpec((1, h2 // 2, d), lambda e, b0, nb, nu: (e, 0, 0)),
        ],
        out_specs=pl.BlockSpec(memory_space=pl.ANY),
        scratch_shapes=[pltpu.VMEM((EXPERT_IN_SLOTS, bm, half), jnp.uint32),
                        pltpu.VMEM((EXPERT_OUT_SLOTS, y_rows, LANES), F32),
                        pltpu.VMEM((d, h2), BF16),
                        pltpu.VMEM((h2 // 2, d), BF16),
                        pltpu.SemaphoreType.DMA((EXPERT_IN_SLOTS,)),
                        pltpu.SemaphoreType.DMA((EXPERT_OUT_SLOTS,))],
    )
    return pl.pallas_call(
        _expert_kernel,
        grid_spec=grid_spec,
        out_shape=jax.ShapeDtypeStruct((n_rows * d // LANES, LANES), F32),
        compiler_params=_cparams(("arbitrary",)),
        name="experts",
    )(first_block, n_block, n_used, xs, w13, w2)


def _final_kernel(dc_ref, dn_ref, y_hbm, w_ref, x_ref, mod_ref, gpre_ref, gpost_ref, w13s_ref,
                  w2s_ref, o_ref, buf, sem):
    i = pl.program_id(0)
    slot = i % 2
    tm = x_ref.shape[0]

    def issue(d_ref, s):
        _gather_rows(d_ref, tm * TOP_K, y_hbm,
                     lambda q, u: buf.at[s, u, _tile_rows(q), :], sem.at[s])

    @pl.when(i == 0)
    def _():
        issue(dc_ref, 0)

    @pl.when(i + 1 < pl.num_programs(0))
    def _():
        issue(dn_ref, 1 - slot)

    for k in range(TOP_K):
        pltpu.make_async_copy(y_hbm.at[pl.ds(0, tm * ROW_TILE), :], buf.at[slot, k],
                              sem.at[slot]).wait()

    w = w_ref[...]
    x1 = x_ref[...]
    routed = None
    for k in range(TOP_K):
        yk = _from_tiled_rows(lambda a, n, st: buf[slot, k, pl.ds(a, n, stride=st), :], tm, x1.shape[1])
        routed = yk * w[:, k:k + 1] if routed is None else routed + yk * w[:, k:k + 1]

    m = mod_ref[0]
    h = (_rms(x1, gpre_ref[...]) * (1.0 + m[4:5]) + m[3:4]).astype(BF16)
    ag = _dot(h, w13s_ref[...])
    hh = ag.shape[1] // 2
    act = (_silu(ag[:, :hh]) * ag[:, hh:]).astype(BF16)
    ffn = routed + _dot(act, w2s_ref[...])
    o_ref[...] = x1 + m[5:6] * _rms(ffn, gpost_ref[...])


def _final(dest, y_sorted, top_w, x1, mod, gpre, gpost, w13s, w2s, seq):
    t, d = x1.shape
    tm = COMBINE_TILE
    per_b = seq // tm
    n_steps = t // tm
    dest3 = dest.reshape(n_steps, 1, tm * TOP_K)
    full = lambda a: pl.BlockSpec(a.shape, lambda i: (0,) * a.ndim)
    row = lambda n: pl.BlockSpec((tm, n), lambda i: (i, 0))
    return pl.pallas_call(
        _final_kernel,
        grid=(n_steps,),
        in_specs=[
            pl.BlockSpec((1, 1, tm * TOP_K), lambda i: (i, 0, 0), memory_space=pltpu.SMEM),
            pl.BlockSpec((1, 1, tm * TOP_K), lambda i: (jnp.minimum(i + 1, n_steps - 1), 0, 0),
                         memory_space=pltpu.SMEM),
            pl.BlockSpec(memory_space=pl.ANY),
            row(LANES), row(d), pl.BlockSpec((1, 6, d), lambda i: (i // per_b, 0, 0)),
            full(gpre), full(gpost), full(w13s), full(w2s)],
        out_specs=row(d),
        out_shape=jax.ShapeDtypeStruct((t, d), F32),
        scratch_shapes=[pltpu.VMEM((2, TOP_K, tm * ROW_TILE, LANES), F32),
                        pltpu.SemaphoreType.DMA((2,))],
        compiler_params=_cparams(("arbitrary",)),
        name="combine_final",
    )(dest3, dest3, y_sorted, top_w, x1, mod, gpre, gpost, w13s, w2s)


def _pad_heads(v):
    lead = v.shape[:-1]
    v = v.reshape(lead + (2, SSD_GROUPS, HEADS_PER_GROUP))
    v = jnp.pad(v, [(0, 0)] * (len(lead) + 2) + [(0, HEADS_PADDED - HEADS_PER_GROUP)])
    return v.reshape(lead + (2 * SSD_GROUPS * HEADS_PADDED,))


def _dispatch_plan(counts, n_tokens):
    bm = EXPERT_ROWS
    counts = counts.reshape(N_EXPERTS).astype(jnp.int32)
    padded = (counts + bm - 1) // bm * bm
    pad_end = jnp.cumsum(padded)
    pad_start = pad_end - padded
    n_rows = -(-n_tokens * TOP_K // bm) * bm + N_EXPERTS * bm
    first_block = (pad_start // bm).astype(jnp.int32)
    n_block = (padded // bm).astype(jnp.int32)
    n_used = (pad_end[-1] // bm).astype(jnp.int32).reshape(1)
    return pad_start, first_block, n_block, n_used, n_rows


def _layer(x, c, w_ada, b_ada, pre_norm_mix, post_norm_mix, pre_norm_ffn, post_norm_ffn, w_in,
           conv_w, conv_b, dt_bias_fwd, dt_bias_bwd, a_log_fwd, a_log_bwd, d_skip, ssd_norm,
           w_branch_ssd, w_branch_fourier, w_out, w_router, router_bias, w13_experts, w2_experts,
           w13_shared, w2_shared):
    bsz, seq, d = x.shape
    t = bsz * seq
    x2 = x.reshape(t, d)
    row = lambda v: v.reshape(1, -1).astype(F32)

    mod = _ada(c, w_ada, b_ada)

    i1 = SSD_INNER
    i2 = i1 + XBC_WIDTH
    i3 = i2 + 2 * SSD_HEADS
    i4 = i3 + FOURIER_WIDTH
    n_dt = 2 * SSD_GROUPS * HEADS_PADDED
    w_dtp = _pad_heads(w_in[:, i2:i3])
    wdtc = jnp.pad(w_dtp, ((0, 0), (0, LANES - n_dt))).astype(BF16)
    wdtT = w_dtp.T.astype(BF16)
    bias_p = _pad_heads(jnp.concatenate([dt_bias_fwd, dt_bias_bwd]).astype(F32))
    a_p = _pad_heads(-jnp.exp(jnp.concatenate([a_log_fwd, a_log_bwd]).astype(F32)))
    pad_row = lambda v: jnp.pad(v, (0, LANES - n_dt)).reshape(1, LANES)
    z, xbc, acsc, dtT, acsT, uf, gates = _inproj(
        x2, mod, row(pre_norm_mix), w_in[:, :i1].astype(BF16), w_in[:, i1:i2].astype(BF16),
        wdtc, wdtT, w_in[:, i3:i4].astype(BF16), w_in[:, i4:].astype(BF16),
        pad_row(bias_p), pad_row(a_p), bias_p.reshape(n_dt, 1), a_p.reshape(n_dt, 1), seq)

    xbc3 = _conv(xbc.reshape(bsz, seq, XBC_WIDTH), conv_w, conv_b)

    yf, yb = _ssd(xbc3, dtT, acsT, acsc, row(jnp.repeat(d_skip, SSD_HEAD_DIM)))

    fm = _fourier(uf.reshape(bsz, seq, FOURIER_WIDTH))

    x1, hp, idx_pad, w_pad, rank_pad, counts = _mix_route(
        yf.reshape(t, SSD_INNER), yb.reshape(t, SSD_INNER), z, fm.reshape(t, FOURIER_WIDTH), gates,
        x2, mod, row(ssd_norm), w_branch_ssd.astype(BF16), w_branch_fourier.astype(BF16),
        w_out.astype(BF16), row(post_norm_mix), row(pre_norm_ffn), w_router.astype(F32),
        router_bias, seq)
    pad_start, first_block, n_block, n_used, n_rows = _dispatch_plan(counts, t)
    dest = _dest(idx_pad, rank_pad, pad_start)[:, :TOP_K]
    xs = _scatter(dest, hp, first_block, n_block, n_used, n_rows)
    y_sorted = _experts(xs, first_block, n_block, n_used, w13_experts, w2_experts)
    out = _final(dest, y_sorted, w_pad, x1, mod, row(pre_norm_ffn), row(post_norm_ffn),
                 w13_shared.astype(BF16), w2_shared.astype(BF16), seq)
    return out.reshape(bsz, seq, d)


def kernel(x, c, w_ada, b_ada, pre_norm_mix, post_norm_mix, pre_norm_ffn, post_norm_ffn, w_in,
           conv_w, conv_b, dt_bias_fwd, dt_bias_bwd, a_log_fwd, a_log_bwd, d_skip, ssd_norm,
           w_branch_ssd, w_branch_fourier, w_out, w_router, router_bias, w13_experts, w2_experts,
           w13_shared, w2_shared):
    for layer in range(w_ada.shape[0]):
        x = _layer(x, c, w_ada[layer], b_ada[layer], pre_norm_mix[layer], post_norm_mix[layer],
                   pre_norm_ffn[layer], post_norm_ffn[layer], w_in[layer], conv_w[layer],
                   conv_b[layer], dt_bias_fwd[layer], dt_bias_bwd[layer], a_log_fwd[layer],
                   a_log_bwd[layer], d_skip[layer], ssd_norm[layer], w_branch_ssd[layer],
                   w_branch_fourier[layer], w_out[layer], w_router[layer], router_bias[layer],
                   w13_experts[layer], w2_experts[layer], w13_shared[layer], w2_shared[layer])
    return x
```
